```python
import math
import jax, jax.numpy as jnp
from jax import lax
import numpy as np

D_MODEL = 2048
BATCH = 4
SEQ = 2048
DEPTH = 1

N_DN_HEADS = 6
DN_HEAD_DIM = 128
DN_WIDTH = N_DN_HEADS * DN_HEAD_DIM
DN_QKV_WIDTH = 3 * DN_WIDTH
DN_CONV = 4
DN_CHUNK = 64
DT_MIN = 0.001
DT_MAX = 0.1
N_SWA_HEADS = 6
N_SWA_KV = 2
SWA_HEAD_DIM = 128
SWA_WIDTH = N_SWA_HEADS * SWA_HEAD_DIM
SWA_KV_WIDTH = N_SWA_KV * SWA_HEAD_DIM
WINDOW = 128
SWA_BLOCK = 128
MEM_LEN = 256
N_MEM_HEADS = 4
MEM_HEAD_DIM = 128
MEM_WIDTH = N_MEM_HEADS * MEM_HEAD_DIM
N_BUCKETS = 32
MAX_DISTANCE = 128
N_BRANCHES = 3
D_FF = 4 * D_MODEL
EPS = 1e-6

IN_SIZES = [DN_QKV_WIDTH, DN_WIDTH, N_DN_HEADS, N_DN_HEADS,
            SWA_WIDTH, SWA_KV_WIDTH, SWA_KV_WIDTH, MEM_WIDTH, N_BRANCHES * D_MODEL]
IN_WIDTH = sum(IN_SIZES)
IN_SPLITS = [int(v) for v in np.cumsum(IN_SIZES[:-1])]

kernel_name = "hybrid_gdn_swa_memxattn_layer"


def rms_norm(x, w, eps=EPS):
    xf = x.astype(jnp.float32)
    y = xf * lax.rsqrt(jnp.mean(xf * xf, axis=-1, keepdims=True) + eps)
    return (y * w.astype(jnp.float32)).astype(x.dtype)


def l2_norm(x, eps=EPS):
    xf = x.astype(jnp.float32)
    return xf * lax.rsqrt(jnp.sum(xf * xf, axis=-1, keepdims=True) + eps)


def causal_depthwise_conv(x, w):
    k = w.shape[0]
    return lax.conv_general_dilated(
        x, w[:, None, :].astype(x.dtype), window_strides=(1,), padding=[(k - 1, 0)],
        dimension_numbers=("NWC", "WIO", "NWC"), feature_group_count=x.shape[-1])


def t5_bucket(dist):
    n = jnp.maximum(dist, 0)
    max_exact = N_BUCKETS // 2
    nf = jnp.maximum(n, 1).astype(jnp.float32)
    large = max_exact + (jnp.log(nf / max_exact) / math.log(MAX_DISTANCE / max_exact)
                         * (N_BUCKETS - max_exact)).astype(jnp.int32)
    large = jnp.minimum(large, N_BUCKETS - 1)
    return jnp.where(n < max_exact, n, large)


def gated_delta_rule(q, k, v, beta, g):
    B, S, H, DK = q.shape
    DV = v.shape[-1]
    C = DN_CHUNK
    NC = S // C

    def chunks(t):
        return t.reshape(B, NC, C, H, -1).transpose(0, 3, 1, 2, 4)

    q, k, v = chunks(q), chunks(k), chunks(v)
    beta = beta.reshape(B, NC, C, H).transpose(0, 3, 1, 2)
    G = jnp.cumsum(g.reshape(B, NC, C, H).transpose(0, 3, 1, 2), axis=-1)
    idx = jnp.arange(C)
    strict = idx[:, None] > idx[None, :]
    incl = idx[:, None] >= idx[None, :]
    dG = G[..., :, None] - G[..., None, :]
    decay_strict = jnp.exp(jnp.where(strict, dG, -jnp.inf))
    decay_incl = jnp.exp(jnp.where(incl, dG, -jnp.inf))
    kk = jnp.einsum("bhnid,bhnjd->bhnij", k, k)
    a_mat = jnp.eye(C, dtype=jnp.float32) + beta[..., None] * kk * decay_strict
    rhs = jnp.concatenate([v * beta[..., None], k * (beta * jnp.exp(G))[..., None]], axis=-1)
    sol = lax.linalg.triangular_solve(a_mat, rhs, left_side=True, lower=True, unit_diagonal=True)
    u, w = sol[..., :DV], sol[..., DV:]
    p = jnp.einsum("bhnid,bhnjd->bhnij", q, k) * decay_incl
    qg = q * jnp.exp(G)[..., None]
    kd = k * jnp.exp(G[..., -1:] - G)[..., None]
    gc = jnp.exp(G[..., -1])
    xs = (jnp.moveaxis(u, 2, 0), jnp.moveaxis(w, 2, 0), jnp.moveaxis(p, 2, 0),
          jnp.moveaxis(qg, 2, 0), jnp.moveaxis(kd, 2, 0), jnp.moveaxis(gc, 2, 0))

    def step(state, inp):
        u_c, w_c, p_c, qg_c, kd_c, gc_c = inp
        delta = u_c - jnp.einsum("bhcd,bhdv->bhcv", w_c, state)
        o_c = jnp.einsum("bhcd,bhdv->bhcv", qg_c, state) + jnp.einsum("bhij,bhjv->bhiv", p_c, delta)
        state = gc_c[..., None, None] * state + jnp.einsum("bhcd,bhcv->bhdv", kd_c, delta)
        return state, o_c

    s0 = jnp.zeros((B, H, DK, DV), jnp.float32)
    _, o = lax.scan(step, s0, xs)
    return o.transpose(1, 0, 3, 2, 4).reshape(B, S, H, DV)


def sliding_window_attention(q, k, v, sinks, rel_bias):
    B, S, HQ, D = q.shape
    HKV = k.shape[2]
    G = HQ // HKV
    L = SWA_BLOCK
    NB = S // L
    qb = q.reshape(B, NB, L, HKV, G, D)

    def windows(t):
        tp = jnp.pad(t, ((0, 0), (L, 0), (0, 0), (0, 0))).reshape(B, NB + 1, L, HKV, D)
        return jnp.concatenate([tp[:, :-1], tp[:, 1:]], axis=2)

    kw, vw = windows(k), windows(v)
    logits = jnp.einsum("bnqhgd,bnkhd->bnhgqk", qb, kw).astype(jnp.float32) * (D ** -0.5)
    qi = jnp.arange(L)[:, None]
    kj = jnp.arange(2 * L)[None, :]
    dist = qi - kj + L
    bias = rel_bias[t5_bucket(dist)].astype(jnp.float32)
    bias = bias.transpose(2, 0, 1).reshape(HKV, G, L, 2 * L)
    key_pos = jnp.arange(NB)[:, None, None] * L + kj[None] - L
    valid = (dist >= 0) & (dist < WINDOW) & (key_pos >= 0)
    logits = jnp.where(valid[None, :, None, None], logits + bias, -jnp.inf)
    sink = jnp.broadcast_to(sinks.astype(jnp.float32).reshape(HKV, G, 1, 1), logits.shape[:-1] + (1,))
    probs = jax.nn.softmax(jnp.concatenate([logits, sink], axis=-1), axis=-1)[..., :-1]
    out = jnp.einsum("bnhgqk,bnkhd->bnqhgd", probs.astype(v.dtype), vw)
    return out.reshape(B, S, HQ * D)


def memory_cross_attention(q, k, v):
    B, S, H, D = q.shape
    logits = jnp.einsum("bshd,bmhd->bhsm", q, k).astype(jnp.float32) * (D ** -0.5)
    probs = jax.nn.softmax(logits, axis=-1)
    return jnp.einsum("bhsm,bmhd->bshd", probs.astype(v.dtype), v).reshape(B, S, H * D)


def setup_inputs(seed: int = 0) -> dict:
    key = jax.random.key(seed)
    ks = jax.random.split(key, 24)
    f32 = jnp.float32
    L = DEPTH

    def nrm(k, shape, scale):
        return jax.random.normal(k, shape, f32) * scale

    def gain(k, shape):
        return 1.0 + 0.02 * jax.random.normal(k, shape, f32)

    dt = jnp.exp(jax.random.uniform(ks[23], (L, N_DN_HEADS), f32, math.log(DT_MIN), math.log(DT_MAX)))
    return {
        "x": nrm(ks[0], (BATCH, SEQ, D_MODEL), 1.0),
        "mem": nrm(ks[1], (BATCH, MEM_LEN, D_MODEL), 1.0),
        "attn_norm_w": gain(ks[2], (L, D_MODEL)),
        "w_in": nrm(ks[3], (L, D_MODEL, IN_WIDTH), D_MODEL ** -0.5),
        "dn_conv_w": nrm(ks[4], (L, DN_CONV, DN_QKV_WIDTH), DN_CONV ** -0.5),
        "dn_A_log": jnp.log(jax.random.uniform(ks[5], (L, N_DN_HEADS), f32, 1.0, 16.0)),
        "dn_dt_bias": dt + jnp.log(-jnp.expm1(-dt)),
        "dn_out_norm_w": gain(ks[6], (L, DN_HEAD_DIM)),
        "swa_q_norm_w": gain(ks[7], (L, SWA_HEAD_DIM)),
        "swa_k_norm_w": gain(ks[8], (L, SWA_HEAD_DIM)),
        "swa_sinks": nrm(ks[9], (L, N_SWA_HEADS), 0.5),
        "rel_bias": nrm(ks[10], (N_BUCKETS, N_SWA_HEADS), 0.5),
        "mem_norm_w": gain(ks[11], (L, D_MODEL)),
        "w_mem_kv": nrm(ks[12], (L, D_MODEL, 2 * MEM_WIDTH), D_MODEL ** -0.5),
        "xq_norm_w": gain(ks[13], (L, MEM_HEAD_DIM)),
        "xk_norm_w": gain(ks[14], (L, MEM_HEAD_DIM)),
        "p_dn": nrm(ks[15], (L, DN_WIDTH, D_MODEL), DN_WIDTH ** -0.5),
        "p_swa": nrm(ks[16], (L, SWA_WIDTH, D_MODEL), SWA_WIDTH ** -0.5),
        "p_mem": nrm(ks[17], (L, MEM_WIDTH, D_MODEL), MEM_WIDTH ** -0.5),
        "w_out": nrm(ks[18], (L, D_MODEL, D_MODEL), D_MODEL ** -0.5),
        "mlp_norm_w": gain(ks[19], (L, D_MODEL)),
        "w_mlp_up": nrm(ks[20], (L, D_MODEL, D_FF), D_MODEL ** -0.5),
        "w_mlp_down": nrm(ks[21], (L, D_FF, D_MODEL), D_FF ** -0.5),
    }


def reference(x, mem, attn_norm_w, w_in, dn_conv_w, dn_A_log, dn_dt_bias, dn_out_norm_w,
              swa_q_norm_w, swa_k_norm_w, swa_sinks, rel_bias, mem_norm_w, w_mem_kv,
              xq_norm_w, xk_norm_w, p_dn, p_swa, p_mem, w_out, mlp_norm_w, w_mlp_up, w_mlp_down):
    B, S, _ = x.shape
    M = mem.shape[1]
    f32 = jnp.float32
    for l in range(DEPTH):
        h = rms_norm(x, attn_norm_w[l])
        proj = h @ w_in[l]
        dn_qkv, dn_z, dn_b, dn_a, sq, sk, sv, mq, gate_logits = jnp.split(proj, IN_SPLITS, axis=-1)

        qkv = jax.nn.silu(causal_depthwise_conv(dn_qkv, dn_conv_w[l]))
        q_dn, k_dn, v_dn = jnp.split(qkv, 3, axis=-1)
        q_dn = l2_norm(q_dn.reshape(B, S, N_DN_HEADS, DN_HEAD_DIM)) * (DN_HEAD_DIM ** -0.5)
        k_dn = l2_norm(k_dn.reshape(B, S, N_DN_HEADS, DN_HEAD_DIM))
        v_dn = v_dn.reshape(B, S, N_DN_HEADS, DN_HEAD_DIM).astype(f32)
        beta = jax.nn.sigmoid(dn_b.astype(f32))
        g = -jnp.exp(dn_A_log[l].astype(f32)) * jax.nn.softplus(dn_a.astype(f32) + dn_dt_bias[l].astype(f32))
        o_dn = gated_delta_rule(q_dn, k_dn, v_dn, beta, g)
        z = dn_z.reshape(B, S, N_DN_HEADS, DN_HEAD_DIM).astype(f32)
        o_dn = (rms_norm(o_dn, dn_out_norm_w[l]) * jax.nn.silu(z)).astype(x.dtype).reshape(B, S, DN_WIDTH)

        q_s = rms_norm(sq.reshape(B, S, N_SWA_HEADS, SWA_HEAD_DIM), swa_q_norm_w[l])
        k_s = rms_norm(sk.reshape(B, S, N_SWA_KV, SWA_HEAD_DIM), swa_k_norm_w[l])
        v_s = sv.reshape(B, S, N_SWA_KV, SWA_HEAD_DIM)
        o_swa = sliding_window_attention(q_s, k_s, v_s, swa_sinks[l], rel_bias)

        mkv = rms_norm(mem, mem_norm_w[l]) @ w_mem_kv[l]
        mk, mv = jnp.split(mkv, 2, axis=-1)
        q_m = rms_norm(mq.reshape(B, S, N_MEM_HEADS, MEM_HEAD_DIM), xq_norm_w[l])
        k_m = rms_norm(mk.reshape(B, M, N_MEM_HEADS, MEM_HEAD_DIM), xk_norm_w[l])
        v_m = mv.reshape(B, M, N_MEM_HEADS, MEM_HEAD_DIM)
        o_mem = memory_cross_attention(q_m, k_m, v_m)

        gates = jax.nn.sigmoid(gate_logits.astype(f32)).astype(x.dtype).reshape(B, S, N_BRANCHES, D_MODEL)
        merged = (gates[:, :, 0] * (o_dn @ p_dn[l])
                  + gates[:, :, 1] * (o_swa @ p_swa[l])
                  + gates[:, :, 2] * (o_mem @ p_mem[l]))
        x = x + merged @ w_out[l]

        h2 = rms_norm(x, mlp_norm_w[l])
        x = x + jnp.square(jax.nn.relu(h2 @ w_mlp_up[l])) @ w_mlp_down[l]
    return x
```

```python
import functools
import math

import jax
import jax.numpy as jnp
from jax import lax
from jax.experimental import pallas as pl
from jax.experimental.pallas import tpu as pltpu

F32 = jnp.float32
BF16 = jnp.bfloat16

EPS = 1e-6
LANES = 128
V7X_VMEM_BYTES = 64 * 1024 * 1024

DN_CHUNK = 64
DN_GROUP = 256
SWA_WINDOW = 128
SWA_BLOCK = 128
N_BUCKETS = 32
MAX_DISTANCE = 128
BA_WIDTH = 256

_NT = (((1,), (1,)), ((), ()))
_TN = (((0,), (0,)), ((), ()))


def _vmem_limit(pipelined_bytes, resident_bytes):
    want = 2 * pipelined_bytes + resident_bytes + (4 << 20)
    return int(min(want, V7X_VMEM_BYTES - (8 << 20)))


def _nbytes(shape, dtype):
    return math.prod(shape) * jnp.dtype(dtype).itemsize


def _silu(v):
    return v * jax.nn.sigmoid(v)


def _rms(v, w):
    return (v * lax.rsqrt(jnp.mean(v * v, axis=-1, keepdims=True) + EPS)) * w


def _dot(a, b):
    return jnp.dot(a, b, preferred_element_type=F32)


def _norm_matmul_kernel(x_ref, nw_ref, w_ref, o_ref, h_ref):
    @pl.when(pl.program_id(1) == 0)
    def _():
        h_ref[...] = _rms(x_ref[...], nw_ref[...]).astype(BF16)

    o_ref[...] = _dot(h_ref[...], w_ref[...]).astype(o_ref.dtype)


def _norm_matmul(x, nw, w, tm, tn):
    m, k = x.shape
    n = w.shape[1]
    pipelined = _nbytes((tm, k), F32) + _nbytes((k, tn), BF16) + _nbytes((tm, tn), F32)
    resident = _nbytes((tm, k), BF16) + _nbytes((tm, k), F32)
    return pl.pallas_call(
        _norm_matmul_kernel,
        grid=(m // tm, n // tn),
        in_specs=[
            pl.BlockSpec((tm, k), lambda i, j: (i, 0)),
            pl.BlockSpec((1, k), lambda i, j: (0, 0)),
            pl.BlockSpec((k, tn), lambda i, j: (0, j)),
        ],
        out_specs=pl.BlockSpec((tm, tn), lambda i, j: (i, j)),
        out_shape=jax.ShapeDtypeStruct((m, n), F32),
        scratch_shapes=[pltpu.VMEM((tm, k), BF16)],
        compiler_params=pltpu.CompilerParams(
            dimension_semantics=("parallel", "arbitrary"),
            vmem_limit_bytes=_vmem_limit(pipelined, resident)),
        name="norm_matmul",
    )(x, nw, w)


def _dn_kernel(qkv_ref, z_ref, ba_ref, cw_ref, gp_ref, onw_ref, o_ref, ext_ref, state_ref, *, heads, dim):
    gt = DN_GROUP
    c = DN_CHUNK
    hd = heads * dim

    @pl.when(pl.program_id(1) == 0)
    def _():
        ext_ref[0:8, :] = jnp.zeros((8, 3 * hd), F32)
        state_ref[...] = jnp.zeros_like(state_ref)

    ext_ref[8:8 + gt, :] = qkv_ref[...]
    conv = cw_ref[3:4, :] * ext_ref[8:8 + gt, :]
    for s in (1, 2, 3):
        conv = conv + cw_ref[3 - s:4 - s, :] * ext_ref[8 - s:8 - s + gt, :]
    ext_ref[0:8, :] = ext_ref[gt:gt + 8, :]
    act = _silu(conv)

    ba = ba_ref[...]
    beta_all = jax.nn.sigmoid(ba)
    xa = ba + gp_ref[1:2, :]
    softplus = jnp.maximum(xa, 0.0) + jnp.log1p(jnp.exp(-jnp.abs(xa)))
    g_all = -jnp.exp(gp_ref[0:1, :]) * softplus
    row_in_chunk = lax.broadcasted_iota(jnp.int32, (gt, LANES), 0) & (c - 1)
    gcum = g_all
    s = 1
    while s < c:
        gcum = gcum + jnp.where(row_in_chunk >= s, pltpu.roll(gcum, s, 0), 0.0)
        s *= 2
    exp_g = jnp.exp(gcum)
    kdec_parts, gc_rows = [], []
    for j in range(gt // c):
        g_last = gcum[c * j + c - 1:c * j + c, :]
        kdec_parts.append(jnp.exp(g_last - gcum[c * j:c * (j + 1), :]))
        gc_rows.append(jnp.exp(g_last))
    kdec = jnp.concatenate(kdec_parts, axis=0)
    gcum_t = gcum.T

    ri = lax.broadcasted_iota(jnp.int32, (gt, gt), 0)
    ci = lax.broadcasted_iota(jnp.int32, (gt, gt), 1)
    same_chunk = (ri // c) == (ci // c)
    strict = same_chunk & (ri > ci)
    incl = same_chunk & (ri >= ci)

    for h in range(heads):
        qh = act[:, h * dim:(h + 1) * dim]
        kh = act[:, hd + h * dim:hd + (h + 1) * dim]
        vh = act[:, 2 * hd + h * dim:2 * hd + (h + 1) * dim]
        qn = qh * lax.rsqrt(jnp.sum(qh * qh, axis=-1, keepdims=True) + EPS) * (dim ** -0.5)
        kn = kh * lax.rsqrt(jnp.sum(kh * kh, axis=-1, keepdims=True) + EPS)
        la = heads + h
        beta = beta_all[:, h:h + 1]
        eg = exp_g[:, la:la + 1]
        decay = jnp.exp(jnp.where(incl, gcum[:, la:la + 1] - gcum_t[la:la + 1, :], -jnp.inf))

        kb = kn.astype(BF16)
        kk = lax.dot_general(kb, kb, _NT, preferred_element_type=F32)
        nmat = jnp.where(strict, (beta * kk) * decay, 0.0)

        sol = jnp.concatenate([vh * beta, kn * (beta * eg)], axis=1)
        pw = nmat.astype(BF16)
        sol = sol - _dot(pw, sol.astype(BF16))
        order = 2
        while order < c:
            pw = _dot(pw, pw).astype(BF16)
            sol = sol + _dot(pw, sol.astype(BF16))
            order *= 2
        u = sol[:, :dim]
        w = sol[:, dim:].astype(BF16)

        pmat = lax.dot_general(qn.astype(BF16), kb, _NT, preferred_element_type=F32) * decay
        qg = (qn * eg).astype(BF16)
        kd = (kn * kdec[:, la:la + 1]).astype(BF16)

        state = state_ref[h]
        o_parts = []
        for j in range(gt // c):
            r = slice(c * j, c * (j + 1))
            sb = state.astype(BF16)
            delta = u[r] - _dot(w[r], sb)
            db = delta.astype(BF16)
            o_parts.append(_dot(qg[r], sb) + _dot(pmat[r, r].astype(BF16), db))
            state = gc_rows[j][:, la:la + 1] * state + lax.dot_general(kd[r], db, _TN, preferred_element_type=F32)
        state_ref[h] = state

        o = jnp.concatenate(o_parts, axis=0)
        zh = z_ref[:, h * dim:(h + 1) * dim]
        o_ref[:, h * dim:(h + 1) * dim] = (_rms(o, onw_ref[...]) * _silu(zh)).astype(o_ref.dtype)


def _deltanet(proj3, conv_w, gate_params, out_norm_w, heads, dim, col_z, col_ba):
    b, s, _ = proj3.shape
    hd = heads * dim
    gt = DN_GROUP
    pipelined = (_nbytes((gt, 3 * hd), F32) + _nbytes((gt, hd), F32) + _nbytes((gt, LANES), F32)
                 + _nbytes((gt, hd), BF16))
    resident = _nbytes((gt + 8, 3 * hd), F32) * 4 + _nbytes((heads, dim, dim), F32) + (8 << 20)
    return pl.pallas_call(
        functools.partial(_dn_kernel, heads=heads, dim=dim),
        grid=(b, s // gt),
        in_specs=[
            pl.BlockSpec((None, gt, 3 * hd), lambda i, g: (i, g, 0)),
            pl.BlockSpec((None, gt, hd), lambda i, g: (i, g, col_z // hd)),
            pl.BlockSpec((None, gt, LANES), lambda i, g: (i, g, col_ba // LANES)),
            pl.BlockSpec((4, 3 * hd), lambda i, g: (0, 0)),
            pl.BlockSpec((8, LANES), lambda i, g: (0, 0)),
            pl.BlockSpec((1, dim), lambda i, g: (0, 0)),
        ],
        out_specs=pl.BlockSpec((None, gt, hd), lambda i, g: (i, g, 0)),
        out_shape=jax.ShapeDtypeStruct((b, s, hd), BF16),
        scratch_shapes=[pltpu.VMEM((gt + 8, 3 * hd), F32), pltpu.VMEM((heads, dim, dim), F32)],
        compiler_params=pltpu.CompilerParams(
            dimension_semantics=("parallel", "arbitrary"),
            vmem_limit_bytes=_vmem_limit(pipelined, resident)),
        name="deltanet",
    )(proj3, proj3, proj3, conv_w, gate_params, out_norm_w)


def _swa_kernel(rb_ref, sink_ref, q_ref, kc_ref, kp_ref, vc_ref, vp_ref, qw_ref, kw_ref, o_ref, bias_ref,
                *, q_heads, kv_heads, dim):
    blk = SWA_BLOCK
    n = pl.program_id(1)
    qi = lax.broadcasted_iota(jnp.int32, (blk, 2 * blk), 0)
    kj = lax.broadcasted_iota(jnp.int32, (blk, 2 * blk), 1)
    dist = qi - kj + blk

    @pl.when((pl.program_id(0) == 0) & (n == 0))
    def _():
        max_exact = N_BUCKETS // 2
        nn = jnp.maximum(dist, 0)
        nf = jnp.maximum(nn, 1).astype(F32)
        large = max_exact + (jnp.log(nf / max_exact) / math.log(MAX_DISTANCE / max_exact)
                             * (N_BUCKETS - max_exact)).astype(jnp.int32)
        bucket = jnp.where(nn < max_exact, nn, jnp.minimum(large, N_BUCKETS - 1))
        for h in range(q_heads):
            acc = jnp.zeros((blk, 2 * blk), F32)
            for bk in range(N_BUCKETS):
                acc = jnp.where(bucket == bk, rb_ref[bk * q_heads + h], acc)
            bias_ref[h] = acc

    valid = (dist >= 0) & (dist < SWA_WINDOW) & ((kj >= blk) | (n > 0))
    group = q_heads // kv_heads
    for j in range(kv_heads):
        cols = slice(j * dim, (j + 1) * dim)
        kwin = jnp.concatenate([kp_ref[:, cols], kc_ref[:, cols]], axis=0)
        kwin = _rms(kwin, kw_ref[...]).astype(BF16)
        vwin = jnp.concatenate([vp_ref[:, cols], vc_ref[:, cols]], axis=0).astype(BF16)
        qcat = jnp.concatenate(
            [_rms(q_ref[:, (j * group + i) * dim:(j * group + i + 1) * dim], qw_ref[...]) for i in range(group)],
            axis=0).astype(BF16)
        logits = lax.dot_general(qcat, kwin, _NT, preferred_element_type=F32) * (dim ** -0.5)
        for i in range(group):
            h = j * group + i
            lg = jnp.where(valid, logits[i * blk:(i + 1) * blk] + bias_ref[h], -jnp.inf)
            sink = sink_ref[h]
            mx = jnp.maximum(jnp.max(lg, axis=-1, keepdims=True), sink)
            e = jnp.exp(lg - mx)
            den = jnp.sum(e, axis=-1, keepdims=True) + jnp.exp(sink - mx)
            pv = _dot(e.astype(BF16), vwin)
            o_ref[:, h * dim:(h + 1) * dim] = (pv / den).astype(o_ref.dtype)


def _swa(proj3, rel_bias_flat, sinks, q_norm_w, k_norm_w, q_heads, kv_heads, dim, col_q, col_k, col_v):
    b, s, _ = proj3.shape
    blk = SWA_BLOCK
    qw, kvw = q_heads * dim, kv_heads * dim
    smem = pl.BlockSpec(memory_space=pltpu.SMEM)
    pipelined = _nbytes((blk, qw), F32) + 4 * _nbytes((blk, kvw), F32) + _nbytes((blk, qw), BF16)
    resident = _nbytes((q_heads, blk, 2 * blk), F32) + (8 << 20)
    return pl.pallas_call(
        functools.partial(_swa_kernel, q_heads=q_heads, kv_heads=kv_heads, dim=dim),
        grid=(b, s // blk),
        in_specs=[
            smem, smem,
            pl.BlockSpec((None, blk, qw), lambda i, n: (i, n, col_q // qw)),
            pl.BlockSpec((None, blk, kvw), lambda i, n: (i, n, col_k // kvw)),
            pl.BlockSpec((None, blk, kvw), lambda i, n: (i, jnp.maximum(n - 1, 0), col_k // kvw)),
            pl.BlockSpec((None, blk, kvw), lambda i, n: (i, n, col_v // kvw)),
            pl.BlockSpec((None, blk, kvw), lambda i, n: (i, jnp.maximum(n - 1, 0), col_v // kvw)),
            pl.BlockSpec((1, dim), lambda i, n: (0, 0)),
            pl.BlockSpec((1, dim), lambda i, n: (0, 0)),
        ],
        out_specs=pl.BlockSpec((None, blk, qw), lambda i, n: (i, n, 0)),
        out_shape=jax.ShapeDtypeStruct((b, s, qw), BF16),
        scratch_shapes=[pltpu.VMEM((q_heads, blk, 2 * blk), F32)],
        compiler_params=pltpu.CompilerParams(
            dimension_semantics=("arbitrary", "arbitrary"),
            vmem_limit_bytes=_vmem_limit(pipelined, resident)),
        name="swa",
    )(rel_bias_flat, sinks, proj3, proj3, proj3, proj3, proj3, q_norm_w, k_norm_w)


def _memattn_kernel(q_ref, k_ref, v_ref, qw_ref, kw_ref, o_ref, *, heads, dim):
    for h in range(heads):
        cols = slice(h * dim, (h + 1) * dim)
        qn = _rms(q_ref[:, cols], qw_ref[...]).astype(BF16)
        kn = _rms(k_ref[:, cols], kw_ref[...]).astype(BF16)
        lg = lax.dot_general(qn, kn, _NT, preferred_element_type=F32) * (dim ** -0.5)
        e = jnp.exp(lg - jnp.max(lg, axis=-1, keepdims=True))
        den = jnp.sum(e, axis=-1, keepdims=True)
        pv = _dot(e.astype(BF16), v_ref[:, cols].astype(BF16))
        o_ref[:, cols] = (pv / den).astype(o_ref.dtype)


def _memattn(proj3, mkv3, q_norm_w, k_norm_w, heads, dim, col_q, tq):
    b, s, _ = proj3.shape
    m = mkv3.shape[1]
    w = heads * dim
    pipelined = _nbytes((tq, w), F32) + 2 * _nbytes((m, w), F32) + _nbytes((tq, w), BF16)
    return pl.pallas_call(
        functools.partial(_memattn_kernel, heads=heads, dim=dim),
        grid=(b, s // tq),
        in_specs=[
            pl.BlockSpec((None, tq, w), lambda i, t: (i, t, col_q // w)),
            pl.BlockSpec((None, m, w), lambda i, t: (i, 0, 0)),
            pl.BlockSpec((None, m, w), lambda i, t: (i, 0, 1)),
            pl.BlockSpec((1, dim), lambda i, t: (0, 0)),
            pl.BlockSpec((1, dim), lambda i, t: (0, 0)),
        ],
        out_specs=pl.BlockSpec((None, tq, w), lambda i, t: (i, t, 0)),
        out_shape=jax.ShapeDtypeStruct((b, s, w), BF16),
        compiler_params=pltpu.CompilerParams(
            dimension_semantics=("parallel", "parallel"),
            vmem_limit_bytes=_vmem_limit(pipelined, 8 << 20)),
        name="memattn",
    )(proj3, mkv3, mkv3, q_norm_w, k_norm_w)


def _merge_kernel(od_ref, os_ref, om_ref, pd_ref, ps_ref, pm_ref, gd_ref, gs_ref, gm_ref, o_ref):
    merged = (jax.nn.sigmoid(gd_ref[...]) * _dot(od_ref[...], pd_ref[...])
              + jax.nn.sigmoid(gs_ref[...]) * _dot(os_ref[...], ps_ref[...])
              + jax.nn.sigmoid(gm_ref[...]) * _dot(om_ref[...], pm_ref[...]))
    o_ref[...] = merged.astype(o_ref.dtype)


def _merge(o_dn, o_swa, o_mem, p_dn, p_swa, p_mem, proj, col_gates, tm, tn):
    t = o_dn.shape[0]
    d = p_dn.shape[1]
    wd, ws, wm = o_dn.shape[1], o_swa.shape[1], o_mem.shape[1]
    g0 = col_gates // tn
    gstep = d // tn
    pipelined = ((_nbytes((tm, wd), BF16) + _nbytes((tm, ws), BF16) + _nbytes((tm, wm), BF16))
                 + (_nbytes((wd, tn), BF16) + _nbytes((ws, tn), BF16) + _nbytes((wm, tn), BF16))
                 + 3 * _nbytes((tm, tn), F32) + _nbytes((tm, tn), BF16))
    return pl.pallas_call(
        _merge_kernel,
        grid=(d // tn, t // tm),
        in_specs=[
            pl.BlockSpec((tm, wd), lambda j, i: (i, 0)),
            pl.BlockSpec((tm, ws), lambda j, i: (i, 0)),
            pl.BlockSpec((tm, wm), lambda j, i: (i, 0)),
            pl.BlockSpec((wd, tn), lambda j, i: (0, j)),
            pl.BlockSpec((ws, tn), lambda j, i: (0, j)),
            pl.BlockSpec((wm, tn), lambda j, i: (0, j)),
            pl.BlockSpec((tm, tn), lambda j, i: (i, g0 + j)),
            pl.BlockSpec((tm, tn), lambda j, i: (i, g0 + gstep + j)),
            pl.BlockSpec((tm, tn), lambda j, i: (i, g0 + 2 * gstep + j)),
        ],
        out_specs=pl.BlockSpec((tm, tn), lambda j, i: (i, j)),
        out_shape=jax.ShapeDtypeStruct((t, d), BF16),
        compiler_params=pltpu.CompilerParams(
            dimension_semantics=("parallel", "parallel"),
            vmem_limit_bytes=_vmem_limit(pipelined, 3 * _nbytes((tm, tn), F32))),
        name="merge",
    )(o_dn, o_swa, o_mem, p_dn, p_swa, p_mem, proj, proj, proj)


def _outproj_kernel(x_ref, m_ref, w_ref, nw_ref, x1_ref, h_ref):
    x1 = x_ref[...] + _dot(m_ref[...], w_ref[...])
    x1_ref[...] = x1
    h_ref[...] = _rms(x1, nw_ref[...]).astype(h_ref.dtype)


def _outproj(x, merged, w_out, norm_w, tm):
    t, d = x.shape
    pipelined = 2 * _nbytes((tm, d), F32) + 2 * _nbytes((tm, d), BF16) + _nbytes((d, d), BF16)
    return pl.pallas_call(
        _outproj_kernel,
        grid=(t // tm,),
        in_specs=[
            pl.BlockSpec((tm, d), lambda i: (i, 0)),
            pl.BlockSpec((tm, d), lambda i: (i, 0)),
            pl.BlockSpec((d, d), lambda i: (0, 0)),
            pl.BlockSpec((1, d), lambda i: (0, 0)),
        ],
        out_specs=[pl.BlockSpec((tm, d), lambda i: (i, 0)), pl.BlockSpec((tm, d), lambda i: (i, 0))],
        out_shape=[jax.ShapeDtypeStruct((t, d), F32), jax.ShapeDtypeStruct((t, d), BF16)],
        compiler_params=pltpu.CompilerParams(
            dimension_semantics=("parallel",),
            vmem_limit_bytes=_vmem_limit(pipelined, 2 * _nbytes((tm, d), F32))),
        name="outproj",
    )(x, merged, w_out, norm_w)


def _mlp_kernel(h_ref, x1_ref, wu_ref, wd_ref, o_ref):
    @pl.when(pl.program_id(1) == 0)
    def _():
        o_ref[...] = x1_ref[...]

    a = jnp.maximum(_dot(h_ref[...], wu_ref[...]), 0.0)
    o_ref[...] += _dot((a * a).astype(BF16), wd_ref[...])


def _mlp(h2, x1, w_up, w_down, tm, tf):
    t, d = x1.shape
    f = w_up.shape[1]
    pipelined = (_nbytes((tm, d), BF16) + 2 * _nbytes((tm, d), F32)
                 + _nbytes((d, tf), BF16) + _nbytes((tf, d), BF16))
    resident = 2 * _nbytes((tm, tf), F32)
    return pl.pallas_call(
        _mlp_kernel,
        grid=(t // tm, f // tf),
        in_specs=[
            pl.BlockSpec((tm, d), lambda i, k: (i, 0)),
            pl.BlockSpec((tm, d), lambda i, k: (i, 0)),
            pl.BlockSpec((d, tf), lambda i, k: (0, k)),
            pl.BlockSpec((tf, d), lambda i, k: (k, 0)),
        ],
        out_specs=pl.BlockSpec((tm, d), lambda i, k: (i, 0)),
        out_shape=jax.ShapeDtypeStruct((t, d), F32),
        compiler_params=pltpu.CompilerParams(
            dimension_semantics=("parallel", "arbitrary"),
            vmem_limit_bytes=_vmem_limit(pipelined, resident)),
        name="mlp",
    )(h2, x1, w_up, w_down)


def _layer(x, mem, attn_norm_w, w_in, dn_conv_w, dn_a_log, dn_dt_bias, dn_out_norm_w, swa_q_norm_w,
           swa_k_norm_w, swa_sinks, rel_bias, mem_norm_w, w_mem_kv, xq_norm_w, xk_norm_w,
           p_dn, p_swa, p_mem, w_out, mlp_norm_w, w_mlp_up, w_mlp_down):
    b, s, d = x.shape
    m = mem.shape[1]
    t = b * s

    dn_heads = dn_a_log.shape[0]
    dn_dim = dn_out_norm_w.shape[0]
    dn_w = dn_heads * dn_dim
    swa_heads = swa_sinks.shape[0]
    swa_dim = swa_q_norm_w.shape[0]
    swa_w = swa_heads * swa_dim
    mem_dim = xq_norm_w.shape[0]
    mem_w = p_mem.shape[0]
    mem_heads = mem_w // mem_dim
    swa_kv_w = (w_in.shape[1] - 4 * dn_w - 2 * dn_heads - swa_w - mem_w - 3 * d) // 2
    swa_kv = swa_kv_w // swa_dim

    sizes = [3 * dn_w, dn_w, dn_heads, dn_heads, swa_w, swa_kv_w, swa_kv_w, mem_w, 3 * d]
    splits = [sum(sizes[:i + 1]) for i in range(len(sizes) - 1)]
    w_qkv, w_z, w_b, w_a, w_sq, w_sk, w_sv, w_mq, w_g = jnp.split(w_in, splits, axis=1)
    w_ba = jnp.concatenate([w_b, w_a, jnp.zeros((d, BA_WIDTH - 2 * dn_heads), w_in.dtype)], axis=1)
    groups = [("qkv", w_qkv), ("z", w_z), ("sq", w_sq), ("sk", w_sk), ("sv", w_sv), ("ba", w_ba),
              ("mq", w_mq), ("gates", w_g)]
    col, off = {}, 0
    for name, wg in groups:
        col[name] = off
        off += wg.shape[1]
    w_in_r = jnp.concatenate([wg for _, wg in groups], axis=1).astype(BF16)

    proj = _norm_matmul(x.reshape(t, d), attn_norm_w.reshape(1, d), w_in_r, tm=1024, tn=1024)
    proj3 = proj.reshape(b, s, -1)
    mkv = _norm_matmul(mem.reshape(b * m, d), mem_norm_w.reshape(1, d), w_mem_kv.astype(BF16), tm=512, tn=512)
    mkv3 = mkv.reshape(b, m, -1)

    gate_params = jnp.zeros((8, LANES), F32)
    gate_params = gate_params.at[0, dn_heads:2 * dn_heads].set(dn_a_log)
    gate_params = gate_params.at[1, dn_heads:2 * dn_heads].set(dn_dt_bias)
    o_dn = _deltanet(proj3, dn_conv_w, gate_params, dn_out_norm_w.reshape(1, dn_dim),
                     dn_heads, dn_dim, col["z"], col["ba"])
    o_swa = _swa(proj3, rel_bias.reshape(-1), swa_sinks, swa_q_norm_w.reshape(1, swa_dim),
                 swa_k_norm_w.reshape(1, swa_dim), swa_heads, swa_kv, swa_dim, col["sq"], col["sk"], col["sv"])
    o_mem = _memattn(proj3, mkv3, xq_norm_w.reshape(1, mem_dim), xk_norm_w.reshape(1, mem_dim),
                     mem_heads, mem_dim, col["mq"], tq=512)

    merged = _merge(o_dn.reshape(t, dn_w), o_swa.reshape(t, swa_w), o_mem.reshape(t, mem_w),
                    p_dn.astype(BF16), p_swa.astype(BF16), p_mem.astype(BF16), proj, col["gates"],
                    tm=512, tn=1024)
    x1, h2 = _outproj(x.reshape(t, d), merged, w_out.astype(BF16), mlp_norm_w.reshape(1, d), tm=256)
    out = _mlp(h2, x1, w_mlp_up.astype(BF16), w_mlp_down.astype(BF16), tm=512, tf=1024)
    return out.reshape(b, s, d)


def kernel(x, mem, attn_norm_w, w_in, dn_conv_w, dn_A_log, dn_dt_bias, dn_out_norm_w, swa_q_norm_w,
           swa_k_norm_w, swa_sinks, rel_bias, mem_norm_w, w_mem_kv, xq_norm_w, xk_norm_w, p_dn, p_swa,
           p_mem, w_out, mlp_norm_w, w_mlp_up, w_mlp_down):
    depth = w_in.shape[0]
    for l in range(depth):
        x = _layer(x, mem, attn_norm_w[l], w_in[l], dn_conv_w[l], dn_A_log[l], dn_dt_bias[l],
                   dn_out_norm_w[l], swa_q_norm_w[l], swa_k_norm_w[l], swa_sinks[l], rel_bias,
                   mem_norm_w[l], w_mem_kv[l], xq_norm_w[l], xk_norm_w[l], p_dn[l], p_swa[l], p_mem[l],
                   w_out[l], mlp_norm_w[l], w_mlp_up[l], w_mlp_down[l])
    return x
```

```python
import functools
import math

import jax
import jax.numpy as jnp
from jax import lax
from jax.experimental import pallas as pl
from jax.experimental.pallas import tpu as pltpu

F32 = jnp.float32
BF16 = jnp.bfloat16

EPS = 1e-6
LANES = 128
V7X_VMEM_BYTES = 64 * 1024 * 1024

DN_CHUNK = 64
DN_GROUP = 256
SWA_WINDOW = 128
SWA_BLOCK = 128
N_BUCKETS = 32
MAX_DISTANCE = 128
BA_WIDTH = 256

_NT = (((1,), (1,)), ((), ()))
_TN = (((0,), (0,)), ((), ()))


def _vmem_limit(pipelined_bytes, resident_bytes):
    want = 2 * pipelined_bytes + resident_bytes + (4 << 20)
    return int(min(want, V7X_VMEM_BYTES - (8 << 20)))


def _nbytes(shape, dtype):
    return math.prod(shape) * jnp.dtype(dtype).itemsize


def _silu(v):
    return v * jax.nn.sigmoid(v)


def _rms(v, w):
    return (v * lax.rsqrt(jnp.mean(v * v, axis=-1, keepdims=True) + EPS)) * w


def _dot(a, b):
    return jnp.dot(a, b, preferred_element_type=F32)


RELAYOUT_TILE = 2 * LANES
_COPY, _SHIFT, _HEAD = 0, 1, 2


def _relayout_kernel(blk_ref, mode_ref, a_ref, b_ref, o_ref, *, shift):
    mode = mode_ref[pl.program_id(0)]
    rows = a_ref.shape[0]
    lane = lax.broadcasted_iota(jnp.int32, (rows, LANES), 1)

    @pl.when(mode == _COPY)
    def _():
        o_ref[...] = a_ref[...].astype(o_ref.dtype)

    @pl.when(mode == _SHIFT)
    def _():
        r0 = pltpu.roll(a_ref[:, :LANES], LANES - shift, 1)
        r1 = pltpu.roll(a_ref[:, LANES:], LANES - shift, 1)
        r2 = pltpu.roll(b_ref[...], LANES - shift, 1)
        keep = lane < LANES - shift
        o_ref[:, :LANES] = jnp.where(keep, r0, r1).astype(o_ref.dtype)
        o_ref[:, LANES:] = jnp.where(keep, r1, r2).astype(o_ref.dtype)

    @pl.when(mode == _HEAD)
    def _():
        o_ref[:, :LANES] = jnp.where(lane < shift, a_ref[:, :LANES], 0.0).astype(o_ref.dtype)
        o_ref[:, LANES:] = jnp.zeros((rows, LANES), o_ref.dtype)


def _relayout_w_in(w_in, regions, n_out, shift):
    k = w_in.shape[0]
    tw = RELAYOUT_TILE
    blk, mode = [], []
    for out_start, out_end, src_start, m in regions:
        assert out_start % tw == 0 and out_end % tw == 0
        assert src_start % tw == (shift if m == _SHIFT else 0)
        for o in range(out_start, out_end, tw):
            blk.append((src_start + o - out_start) // tw)
            mode.append(m)
    assert len(blk) == n_out // tw
    blk = jnp.asarray(blk, jnp.int32)
    mode = jnp.asarray(mode, jnp.int32)
    pipelined = _nbytes((k, tw), F32) + _nbytes((k, LANES), F32) + _nbytes((k, tw), BF16)
    grid_spec = pltpu.PrefetchScalarGridSpec(
        num_scalar_prefetch=2,
        grid=(n_out // tw,),
        in_specs=[
            pl.BlockSpec((k, tw), lambda j, blk, mode: (0, blk[j])),
            pl.BlockSpec((k, LANES), lambda j, blk, mode: (0, 2 * blk[j] + 2)),
        ],
        out_specs=pl.BlockSpec((k, tw), lambda j, blk, mode: (0, j)),
    )
    return pl.pallas_call(
        functools.partial(_relayout_kernel, shift=shift),
        grid_spec=grid_spec,
        out_shape=jax.ShapeDtypeStruct((k, n_out), BF16),
        compiler_params=pltpu.CompilerParams(
            dimension_semantics=("parallel",),
            vmem_limit_bytes=_vmem_limit(pipelined, 4 * _nbytes((k, tw), F32))),
        name="relayout_w_in",
    )(blk, mode, w_in, w_in)


def _norm_matmul_kernel(x_ref, nw_ref, w_ref, o_ref, h_ref):
    @pl.when(pl.program_id(1) == 0)
    def _():
        h_ref[...] = _rms(x_ref[...], nw_ref[...]).astype(BF16)

    o_ref[...] = _dot(h_ref[...], w_ref[...]).astype(o_ref.dtype)


def _norm_matmul(x, nw, w, tm, tn):
    m, k = x.shape
    n = w.shape[1]
    pipelined = _nbytes((tm, k), F32) + _nbytes((k, tn), BF16) + _nbytes((tm, tn), F32)
    resident = _nbytes((tm, k), BF16) + _nbytes((tm, k), F32)
    return pl.pallas_call(
        _norm_matmul_kernel,
        grid=(m // tm, n // tn),
        in_specs=[
            pl.BlockSpec((tm, k), lambda i, j: (i, 0)),
            pl.BlockSpec((1, k), lambda i, j: (0, 0)),
            pl.BlockSpec((k, tn), lambda i, j: (0, j)),
        ],
        out_specs=pl.BlockSpec((tm, tn), lambda i, j: (i, j)),
        out_shape=jax.ShapeDtypeStruct((m, n), F32),
        scratch_shapes=[pltpu.VMEM((tm, k), BF16)],
        compiler_params=pltpu.CompilerParams(
            dimension_semantics=("parallel", "arbitrary"),
            vmem_limit_bytes=_vmem_limit(pipelined, resident)),
        name="norm_matmul",
    )(x, nw, w)


def _dn_kernel(qkv_ref, z_ref, ba_ref, cw_ref, gp_ref, onw_ref, o_ref, ext_ref, state_ref, *, heads, dim):
    gt = DN_GROUP
    c = DN_CHUNK
    hd = heads * dim

    @pl.when(pl.program_id(1) == 0)
    def _():
        ext_ref[0:8, :] = jnp.zeros((8, 3 * hd), F32)
        state_ref[...] = jnp.zeros_like(state_ref)

    ext_ref[8:8 + gt, :] = qkv_ref[...]
    conv = cw_ref[3:4, :] * ext_ref[8:8 + gt, :]
    for s in (1, 2, 3):
        conv = conv + cw_ref[3 - s:4 - s, :] * ext_ref[8 - s:8 - s + gt, :]
    ext_ref[0:8, :] = ext_ref[gt:gt + 8, :]
    act = _silu(conv)

    ba = ba_ref[...]
    beta_all = jax.nn.sigmoid(ba)
    xa = ba + gp_ref[1:2, :]
    softplus = jnp.maximum(xa, 0.0) + jnp.log1p(jnp.exp(-jnp.abs(xa)))
    g_all = -jnp.exp(gp_ref[0:1, :]) * softplus
    row_in_chunk = lax.broadcasted_iota(jnp.int32, (gt, LANES), 0) & (c - 1)
    gcum = g_all
    s = 1
    while s < c:
        gcum = gcum + jnp.where(row_in_chunk >= s, pltpu.roll(gcum, s, 0), 0.0)
        s *= 2
    exp_g = jnp.exp(gcum)
    kdec_parts, gc_rows = [], []
    for j in range(gt // c):
        g_last = gcum[c * j + c - 1:c * j + c, :]
        kdec_parts.append(jnp.exp(g_last - gcum[c * j:c * (j + 1), :]))
        gc_rows.append(jnp.exp(g_last))
    kdec = jnp.concatenate(kdec_parts, axis=0)
    gcum_t = gcum.T

    ri = lax.broadcasted_iota(jnp.int32, (gt, gt), 0)
    ci = lax.broadcasted_iota(jnp.int32, (gt, gt), 1)
    same_chunk = (ri // c) == (ci // c)
    strict = same_chunk & (ri > ci)
    incl = same_chunk & (ri >= ci)

    for h in range(heads):
        qh = act[:, h * dim:(h + 1) * dim]
        kh = act[:, hd + h * dim:hd + (h + 1) * dim]
        vh = act[:, 2 * hd + h * dim:2 * hd + (h + 1) * dim]
        qn = qh * lax.rsqrt(jnp.sum(qh * qh, axis=-1, keepdims=True) + EPS) * (dim ** -0.5)
        kn = kh * lax.rsqrt(jnp.sum(kh * kh, axis=-1, keepdims=True) + EPS)
        la = heads + h
        beta = beta_all[:, h:h + 1]
        eg = exp_g[:, la:la + 1]
        decay = jnp.exp(jnp.where(incl, gcum[:, la:la + 1] - gcum_t[la:la + 1, :], -jnp.inf))

        kb = kn.astype(BF16)
        kk = lax.dot_general(kb, kb, _NT, preferred_element_type=F32)
        nmat = jnp.where(strict, (beta * kk) * decay, 0.0)

        sol = jnp.concatenate([vh * beta, kn * (beta * eg)], axis=1)
        pw = nmat.astype(BF16)
        sol = sol - _dot(pw, sol.astype(BF16))
        order = 2
        while order < c:
            pw = _dot(pw, pw).astype(BF16)
            sol = sol + _dot(pw, sol.astype(BF16))
            order *= 2
        u = sol[:, :dim]
        w = sol[:, dim:].astype(BF16)

        pmat = lax.dot_general(qn.astype(BF16), kb, _NT, preferred_element_type=F32) * decay
        qg = (qn * eg).astype(BF16)
        kd = (kn * kdec[:, la:la + 1]).astype(BF16)

        state = state_ref[h]
        o_parts = []
        for j in range(gt // c):
            r = slice(c * j, c * (j + 1))
            sb = state.astype(BF16)
            delta = u[r] - _dot(w[r], sb)
            db = delta.astype(BF16)
            o_parts.append(_dot(qg[r], sb) + _dot(pmat[r, r].astype(BF16), db))
            state = gc_rows[j][:, la:la + 1] * state + lax.dot_general(kd[r], db, _TN, preferred_element_type=F32)
        state_ref[h] = state

        o = jnp.concatenate(o_parts, axis=0)
        zh = z_ref[:, h * dim:(h + 1) * dim]
        o_ref[:, h * dim:(h + 1) * dim] = (_rms(o, onw_ref[...]) * _silu(zh)).astype(o_ref.dtype)


def _deltanet(proj3, conv_w, gate_params, out_norm_w, heads, dim, col_z, col_ba):
    b, s, _ = proj3.shape
    hd = heads * dim
    gt = DN_GROUP
    pipelined = (_nbytes((gt, 3 * hd), F32) + _nbytes((gt, hd), F32) + _nbytes((gt, LANES), F32)
                 + _nbytes((gt, hd), BF16))
    resident = _nbytes((gt + 8, 3 * hd), F32) * 4 + _nbytes((heads, dim, dim), F32) + (8 << 20)
    return pl.pallas_call(
        functools.partial(_dn_kernel, heads=heads, dim=dim),
        grid=(b, s // gt),
        in_specs=[
            pl.BlockSpec((None, gt, 3 * hd), lambda i, g: (i, g, 0)),
            pl.BlockSpec((None, gt, hd), lambda i, g: (i, g, col_z // hd)),
            pl.BlockSpec((None, gt, LANES), lambda i, g: (i, g, col_ba // LANES)),
            pl.BlockSpec((4, 3 * hd), lambda i, g: (0, 0)),
            pl.BlockSpec((8, LANES), lambda i, g: (0, 0)),
            pl.BlockSpec((1, dim), lambda i, g: (0, 0)),
        ],
        out_specs=pl.BlockSpec((None, gt, hd), lambda i, g: (i, g, 0)),
        out_shape=jax.ShapeDtypeStruct((b, s, hd), BF16),
        scratch_shapes=[pltpu.VMEM((gt + 8, 3 * hd), F32), pltpu.VMEM((heads, dim, dim), F32)],
        compiler_params=pltpu.CompilerParams(
            dimension_semantics=("parallel", "arbitrary"),
            vmem_limit_bytes=_vmem_limit(pipelined, resident)),
        name="deltanet",
    )(proj3, proj3, proj3, conv_w, gate_params, out_norm_w)


def _swa_kernel(rb_ref, sink_ref, q_ref, kc_ref, kp_ref, vc_ref, vp_ref, qw_ref, kw_ref, o_ref, bias_ref,
                *, q_heads, kv_heads, dim):
    blk = SWA_BLOCK
    n = pl.program_id(1)
    qi = lax.broadcasted_iota(jnp.int32, (blk, 2 * blk), 0)
    kj = lax.broadcasted_iota(jnp.int32, (blk, 2 * blk), 1)
    dist = qi - kj + blk

    @pl.when((pl.program_id(0) == 0) & (n == 0))
    def _():
        max_exact = N_BUCKETS // 2
        nn = jnp.maximum(dist, 0)
        nf = jnp.maximum(nn, 1).astype(F32)
        large = max_exact + (jnp.log(nf / max_exact) / math.log(MAX_DISTANCE / max_exact)
                             * (N_BUCKETS - max_exact)).astype(jnp.int32)
        bucket = jnp.where(nn < max_exact, nn, jnp.minimum(large, N_BUCKETS - 1))
        for h in range(q_heads):
            acc = jnp.zeros((blk, 2 * blk), F32)
            for bk in range(N_BUCKETS):
                acc = jnp.where(bucket == bk, rb_ref[bk * q_heads + h], acc)
            bias_ref[h] = acc

    valid = (dist >= 0) & (dist < SWA_WINDOW) & ((kj >= blk) | (n > 0))
    group = q_heads // kv_heads
    for j in range(kv_heads):
        cols = slice(j * dim, (j + 1) * dim)
        kwin = jnp.concatenate([kp_ref[:, cols], kc_ref[:, cols]], axis=0)
        kwin = _rms(kwin, kw_ref[...]).astype(BF16)
        vwin = jnp.concatenate([vp_ref[:, cols], vc_ref[:, cols]], axis=0).astype(BF16)
        qcat = jnp.concatenate(
            [_rms(q_ref[:, (j * group + i) * dim:(j * group + i + 1) * dim], qw_ref[...]) for i in range(group)],
            axis=0).astype(BF16)
        logits = lax.dot_general(qcat, kwin, _NT, preferred_element_type=F32) * (dim ** -0.5)
        for i in range(group):
            h = j * group + i
            lg = jnp.where(valid, logits[i * blk:(i + 1) * blk] + bias_ref[h], -jnp.inf)
            sink = sink_ref[h]
            mx = jnp.maximum(jnp.max(lg, axis=-1, keepdims=True), sink)
            e = jnp.exp(lg - mx)
            den = jnp.sum(e, axis=-1, keepdims=True) + jnp.exp(sink - mx)
            pv = _dot(e.astype(BF16), vwin)
            o_ref[:, h * dim:(h + 1) * dim] = (pv / den).astype(o_ref.dtype)


def _swa(proj3, rel_bias_flat, sinks, q_norm_w, k_norm_w, q_heads, kv_heads, dim, col_q, col_k, col_v):
    b, s, _ = proj3.shape
    blk = SWA_BLOCK
    qw, kvw = q_heads * dim, kv_heads * dim
    smem = pl.BlockSpec(memory_space=pltpu.SMEM)
    pipelined = _nbytes((blk, qw), F32) + 4 * _nbytes((blk, kvw), F32) + _nbytes((blk, qw), BF16)
    resident = _nbytes((q_heads, blk, 2 * blk), F32) + (8 << 20)
    return pl.pallas_call(
        functools.partial(_swa_kernel, q_heads=q_heads, kv_heads=kv_heads, dim=dim),
        grid=(b, s // blk),
        in_specs=[
            smem, smem,
            pl.BlockSpec((None, blk, qw), lambda i, n: (i, n, col_q // qw)),
            pl.BlockSpec((None, blk, kvw), lambda i, n: (i, n, col_k // kvw)),
            pl.BlockSpec((None, blk, kvw), lambda i, n: (i, jnp.maximum(n - 1, 0), col_k // kvw)),
            pl.BlockSpec((None, blk, kvw), lambda i, n: (i, n, col_v // kvw)),
            pl.BlockSpec((None, blk, kvw), lambda i, n: (i, jnp.maximum(n - 1, 0), col_v // kvw)),
            pl.BlockSpec((1, dim), lambda i, n: (0, 0)),
            pl.BlockSpec((1, dim), lambda i, n: (0, 0)),
        ],
        out_specs=pl.BlockSpec((None, blk, qw), lambda i, n: (i, n, 0)),
        out_shape=jax.ShapeDtypeStruct((b, s, qw), BF16),
        scratch_shapes=[pltpu.VMEM((q_heads, blk, 2 * blk), F32)],
        compiler_params=pltpu.CompilerParams(
            dimension_semantics=("arbitrary", "arbitrary"),
            vmem_limit_bytes=_vmem_limit(pipelined, resident)),
        name="swa",
    )(rel_bias_flat, sinks, proj3, proj3, proj3, proj3, proj3, q_norm_w, k_norm_w)


def _memattn_kernel(q_ref, k_ref, v_ref, qw_ref, kw_ref, o_ref, *, heads, dim):
    for h in range(heads):
        cols = slice(h * dim, (h + 1) * dim)
        qn = _rms(q_ref[:, cols], qw_ref[...]).astype(BF16)
        kn = _rms(k_ref[:, cols], kw_ref[...]).astype(BF16)
        lg = lax.dot_general(qn, kn, _NT, preferred_element_type=F32) * (dim ** -0.5)
        e = jnp.exp(lg - jnp.max(lg, axis=-1, keepdims=True))
        den = jnp.sum(e, axis=-1, keepdims=True)
        pv = _dot(e.astype(BF16), v_ref[:, cols].astype(BF16))
        o_ref[:, cols] = (pv / den).astype(o_ref.dtype)


def _memattn(proj3, mkv3, q_norm_w, k_norm_w, heads, dim, col_q, tq):
    b, s, _ = proj3.shape
    m = mkv3.shape[1]
    w = heads * dim
    pipelined = _nbytes((tq, w), F32) + 2 * _nbytes((m, w), F32) + _nbytes((tq, w), BF16)
    return pl.pallas_call(
        functools.partial(_memattn_kernel, heads=heads, dim=dim),
        grid=(b, s // tq),
        in_specs=[
            pl.BlockSpec((None, tq, w), lambda i, t: (i, t, col_q // w)),
            pl.BlockSpec((None, m, w), lambda i, t: (i, 0, 0)),
            pl.BlockSpec((None, m, w), lambda i, t: (i, 0, 1)),
            pl.BlockSpec((1, dim), lambda i, t: (0, 0)),
            pl.BlockSpec((1, dim), lambda i, t: (0, 0)),
        ],
        out_specs=pl.BlockSpec((None, tq, w), lambda i, t: (i, t, 0)),
        out_shape=jax.ShapeDtypeStruct((b, s, w), BF16),
        compiler_params=pltpu.CompilerParams(
            dimension_semantics=("parallel", "parallel"),
            vmem_limit_bytes=_vmem_limit(pipelined, 8 << 20)),
        name="memattn",
    )(proj3, mkv3, mkv3, q_norm_w, k_norm_w)


def _merge_kernel(od_ref, os_ref, om_ref, pd_ref, ps_ref, pm_ref, gd_ref, gs_ref, gm_ref, o_ref):
    merged = (jax.nn.sigmoid(gd_ref[...]) * _dot(od_ref[...], pd_ref[...])
              + jax.nn.sigmoid(gs_ref[...]) * _dot(os_ref[...], ps_ref[...])
              + jax.nn.sigmoid(gm_ref[...]) * _dot(om_ref[...], pm_ref[...]))
    o_ref[...] = merged.astype(o_ref.dtype)


def _merge(o_dn, o_swa, o_mem, p_dn, p_swa, p_mem, proj, col_gates, tm, tn):
    t = o_dn.shape[0]
    d = p_dn.shape[1]
    wd, ws, wm = o_dn.shape[1], o_swa.shape[1], o_mem.shape[1]
    g0 = col_gates // tn
    gstep = d // tn
    pipelined = ((_nbytes((tm, wd), BF16) + _nbytes((tm, ws), BF16) + _nbytes((tm, wm), BF16))
                 + (_nbytes((wd, tn), BF16) + _nbytes((ws, tn), BF16) + _nbytes((wm, tn), BF16))
                 + 3 * _nbytes((tm, tn), F32) + _nbytes((tm, tn), BF16))
    return pl.pallas_call(
        _merge_kernel,
        grid=(d // tn, t // tm),
        in_specs=[
            pl.BlockSpec((tm, wd), lambda j, i: (i, 0)),
            pl.BlockSpec((tm, ws), lambda j, i: (i, 0)),
            pl.BlockSpec((tm, wm), lambda j, i: (i, 0)),
            pl.BlockSpec((wd, tn), lambda j, i: (0, j)),
            pl.BlockSpec((ws, tn), lambda j, i: (0, j)),
            pl.BlockSpec((wm, tn), lambda j, i: (0, j)),
            pl.BlockSpec((tm, tn), lambda j, i: (i, g0 + j)),
            pl.BlockSpec((tm, tn), lambda j, i: (i, g0 + gstep + j)),
            pl.BlockSpec((tm, tn), lambda j, i: (i, g0 + 2 * gstep + j)),
        ],
        out_specs=pl.BlockSpec((tm, tn), lambda j, i: (i, j)),
        out_shape=jax.ShapeDtypeStruct((t, d), BF16),
        compiler_params=pltpu.CompilerParams(
            dimension_semantics=("parallel", "parallel"),
            vmem_limit_bytes=_vmem_limit(pipelined, 3 * _nbytes((tm, tn), F32))),
        name="merge",
    )(o_dn, o_swa, o_mem, p_dn, p_swa, p_mem, proj, proj, proj)


def _outproj_kernel(x_ref, m_ref, w_ref, nw_ref, x1_ref, h_ref):
    x1 = x_ref[...] + _dot(m_ref[...], w_ref[...])
    x1_ref[...] = x1
    h_ref[...] = _rms(x1, nw_ref[...]).astype(h_ref.dtype)


def _outproj(x, merged, w_out, norm_w, tm):
    t, d = x.shape
    pipelined = 2 * _nbytes((tm, d), F32) + 2 * _nbytes((tm, d), BF16) + _nbytes((d, d), BF16)
    return pl.pallas_call(
        _outproj_kernel,
        grid=(t // tm,),
        in_specs=[
            pl.BlockSpec((tm, d), lambda i: (i, 0)),
            pl.BlockSpec((tm, d), lambda i: (i, 0)),
            pl.BlockSpec((d, d), lambda i: (0, 0)),
            pl.BlockSpec((1, d), lambda i: (0, 0)),
        ],
        out_specs=[pl.BlockSpec((tm, d), lambda i: (i, 0)), pl.BlockSpec((tm, d), lambda i: (i, 0))],
        out_shape=[jax.ShapeDtypeStruct((t, d), F32), jax.ShapeDtypeStruct((t, d), BF16)],
        compiler_params=pltpu.CompilerParams(
            dimension_semantics=("parallel",),
            vmem_limit_bytes=_vmem_limit(pipelined, 2 * _nbytes((tm, d), F32))),
        name="outproj",
    )(x, merged, w_out, norm_w)


def _mlp_kernel(h_ref, x1_ref, wu_ref, wd_ref, o_ref):
    @pl.when(pl.program_id(1) == 0)
    def _():
        o_ref[...] = x1_ref[...]

    a = jnp.maximum(_dot(h_ref[...], wu_ref[...]), 0.0)
    o_ref[...] += _dot((a * a).astype(BF16), wd_ref[...])


def _mlp(h2, x1, w_up, w_down, tm, tf):
    t, d = x1.shape
    f = w_up.shape[1]
    pipelined = (_nbytes((tm, d), BF16) + 2 * _nbytes((tm, d), F32)
                 + _nbytes((d, tf), BF16) + _nbytes((tf, d), BF16))
    resident = 2 * _nbytes((tm, tf), F32)
    return pl.pallas_call(
        _mlp_kernel,
        grid=(t // tm, f // tf),
        in_specs=[
            pl.BlockSpec((tm, d), lambda i, k: (i, 0)),
            pl.BlockSpec((tm, d), lambda i, k: (i, 0)),
            pl.BlockSpec((d, tf), lambda i, k: (0, k)),
            pl.BlockSpec((tf, d), lambda i, k: (k, 0)),
        ],
        out_specs=pl.BlockSpec((tm, d), lambda i, k: (i, 0)),
        out_shape=jax.ShapeDtypeStruct((t, d), F32),
        compiler_params=pltpu.CompilerParams(
            dimension_semantics=("parallel", "arbitrary"),
            vmem_limit_bytes=_vmem_limit(pipelined, resident)),
        name="mlp",
    )(h2, x1, w_up, w_down)


def _layer(x, mem, attn_norm_w, w_in, dn_conv_w, dn_a_log, dn_dt_bias, dn_out_norm_w, swa_q_norm_w,
           swa_k_norm_w, swa_sinks, rel_bias, mem_norm_w, w_mem_kv, xq_norm_w, xk_norm_w,
           p_dn, p_swa, p_mem, w_out, mlp_norm_w, w_mlp_up, w_mlp_down):
    b, s, d = x.shape
    m = mem.shape[1]
    t = b * s

    dn_heads = dn_a_log.shape[0]
    dn_dim = dn_out_norm_w.shape[0]
    dn_w = dn_heads * dn_dim
    swa_heads = swa_sinks.shape[0]
    swa_dim = swa_q_norm_w.shape[0]
    swa_w = swa_heads * swa_dim
    mem_dim = xq_norm_w.shape[0]
    mem_w = p_mem.shape[0]
    mem_heads = mem_w // mem_dim
    swa_kv_w = (w_in.shape[1] - 4 * dn_w - 2 * dn_heads - swa_w - mem_w - 3 * d) // 2
    swa_kv = swa_kv_w // swa_dim

    n_ba = 2 * dn_heads
    src_ba = 4 * dn_w
    src_swa = src_ba + n_ba
    src_mq = src_swa + swa_w + 2 * swa_kv_w
    col = {"qkv": 0, "z": 3 * dn_w, "sq": src_ba}
    col["sk"] = col["sq"] + swa_w
    col["sv"] = col["sk"] + swa_kv_w
    col["ba"] = col["sv"] + swa_kv_w
    col["mq"] = col["ba"] + BA_WIDTH
    col["gates"] = col["mq"] + mem_w
    n_out = col["gates"] + 3 * d
    regions = [(0, col["sq"], 0, _COPY),
               (col["sq"], col["ba"], src_swa, _SHIFT),
               (col["ba"], col["mq"], src_ba, _HEAD),
               (col["mq"], n_out, src_mq, _SHIFT)]
    w_in_r = _relayout_w_in(w_in, regions, n_out, n_ba)

    proj = _norm_matmul(x.reshape(t, d), attn_norm_w.reshape(1, d), w_in_r, tm=1024, tn=1024)
    proj3 = proj.reshape(b, s, -1)
    mkv = _norm_matmul(mem.reshape(b * m, d), mem_norm_w.reshape(1, d), w_mem_kv.astype(BF16), tm=512, tn=512)
    mkv3 = mkv.reshape(b, m, -1)

    gate_params = jnp.zeros((8, LANES), F32)
    gate_params = gate_params.at[0, dn_heads:2 * dn_heads].set(dn_a_log)
    gate_params = gate_params.at[1, dn_heads:2 * dn_heads].set(dn_dt_bias)
    o_dn = _deltanet(proj3, dn_conv_w, gate_params, dn_out_norm_w.reshape(1, dn_dim),
                     dn_heads, dn_dim, col["z"], col["ba"])
    o_swa = _swa(proj3, rel_bias.reshape(-1), swa_sinks, swa_q_norm_w.reshape(1, swa_dim),
                 swa_k_norm_w.reshape(1, swa_dim), swa_heads, swa_kv, swa_dim, col["sq"], col["sk"], col["sv"])
    o_mem = _memattn(proj3, mkv3, xq_norm_w.reshape(1, mem_dim), xk_norm_w.reshape(1, mem_dim),
                     mem_heads, mem_dim, col["mq"], tq=512)

    merged = _merge(o_dn.reshape(t, dn_w), o_swa.reshape(t, swa_w), o_mem.reshape(t, mem_w),
                    p_dn.astype(BF16), p_swa.astype(BF16), p_mem.astype(BF16), proj, col["gates"],
                    tm=512, tn=1024)
    x1, h2 = _outproj(x.reshape(t, d), merged, w_out.astype(BF16), mlp_norm_w.reshape(1, d), tm=256)
    out = _mlp(h2, x1, w_mlp_up.astype(BF16), w_mlp_down.astype(BF16), tm=512, tf=1024)
    return out.reshape(b, s, d)


def kernel(x, mem, attn_norm_w, w_in, dn_conv_w, dn_A_log, dn_dt_bias, dn_out_norm_w, swa_q_norm_w,
           swa_k_norm_w, swa_sinks, rel_bias, mem_norm_w, w_mem_kv, xq_norm_w, xk_norm_w, p_dn, p_swa,
           p_mem, w_out, mlp_norm_w, w_mlp_up, w_mlp_down):
    depth = w_in.shape[0]
    for l in range(depth):
        x = _layer(x, mem, attn_norm_w[l], w_in[l], dn_conv_w[l], dn_A_log[l], dn_dt_bias[l],
                   dn_out_norm_w[l], swa_q_norm_w[l], swa_k_norm_w[l], swa_sinks[l], rel_bias,
                   mem_norm_w[l], w_mem_kv[l], xq_norm_w[l], xk_norm_w[l], p_dn[l], p_swa[l], p_mem[l],
                   w_out[l], mlp_norm_w[l], w_mlp_up[l], w_mlp_down[l])
    return x
```

```python
import functools
import math

import jax
import jax.numpy as jnp
from jax import lax
from jax.experimental import pallas as pl
from jax.experimental.pallas import tpu as pltpu

F32 = jnp.float32
BF16 = jnp.bfloat16

EPS = 1e-6
LANES = 128
SUBLANES = 8
V7X_VMEM_BYTES = 64 * 1024 * 1024

DN_CHUNK = 64
DN_GROUP = 256
SWA_WINDOW = 128
SWA_BLOCK = 128
N_BUCKETS = 32
MAX_DISTANCE = 128
BA_WIDTH = 256

_NT = (((1,), (1,)), ((), ()))
_TN = (((0,), (0,)), ((), ()))


def _vmem_limit(pipelined_bytes, resident_bytes):
    want = 2 * pipelined_bytes + resident_bytes + (4 << 20)
    return int(min(want, V7X_VMEM_BYTES - (8 << 20)))


def _nbytes(shape, dtype):
    return math.prod(shape) * jnp.dtype(dtype).itemsize


def _silu(v):
    return v * jax.nn.sigmoid(v)


def _rms(v, w):
    return (v * lax.rsqrt(jnp.mean(v * v, axis=-1, keepdims=True) + EPS)) * w


def _dot(a, b):
    return jnp.dot(a, b, preferred_element_type=F32)


def _dot_nt(a, b):
    return lax.dot_general(a, b, _NT, preferred_element_type=F32)


def _norm_matmul_kernel(x_ref, nw_ref, w_ref, o_ref, h_ref, *, w_transposed):
    @pl.when(pl.program_id(1) == 0)
    def _():
        h_ref[...] = _rms(x_ref[...], nw_ref[...]).astype(BF16)

    mm = _dot_nt if w_transposed else _dot
    o_ref[...] = mm(h_ref[...], w_ref[...]).astype(o_ref.dtype)


def _norm_matmul(x, nw, w, tm, tn, w_transposed):
    m, k = x.shape
    n = w.shape[0] if w_transposed else w.shape[1]
    w_spec = (pl.BlockSpec((tn, k), lambda i, j: (j, 0)) if w_transposed
              else pl.BlockSpec((k, tn), lambda i, j: (0, j)))
    pipelined = _nbytes((tm, k), F32) + _nbytes((k, tn), BF16) + _nbytes((tm, tn), F32)
    resident = _nbytes((tm, k), BF16) + _nbytes((tm, k), F32)
    return pl.pallas_call(
        functools.partial(_norm_matmul_kernel, w_transposed=w_transposed),
        grid=(m // tm, n // tn),
        in_specs=[
            pl.BlockSpec((tm, k), lambda i, j: (i, 0)),
            pl.BlockSpec((1, k), lambda i, j: (0, 0)),
            w_spec,
        ],
        out_specs=pl.BlockSpec((tm, tn), lambda i, j: (i, j)),
        out_shape=jax.ShapeDtypeStruct((m, n), F32),
        scratch_shapes=[pltpu.VMEM((tm, k), BF16)],
        compiler_params=pltpu.CompilerParams(
            dimension_semantics=("parallel", "arbitrary"),
            vmem_limit_bytes=_vmem_limit(pipelined, resident)),
        name="norm_matmul",
    )(x, nw, w)


def _dn_kernel(qkv_ref, z_ref, ba_ref, cw_ref, gp_ref, onw_ref, o_ref, ext_ref, state_ref, *, heads, dim):
    gt = DN_GROUP
    c = DN_CHUNK
    hd = heads * dim
    hs = range(heads)
    chunks = range(gt // c)

    @pl.when(pl.program_id(1) == 0)
    def _():
        ext_ref[0:SUBLANES, :] = jnp.zeros((SUBLANES, 3 * hd), F32)
        state_ref[...] = jnp.zeros_like(state_ref)

    ext_ref[SUBLANES:SUBLANES + gt, :] = qkv_ref[...]
    xe = ext_ref[...]
    conv = cw_ref[3:4, :] * xe[SUBLANES:]
    for s in (1, 2, 3):
        conv = conv + cw_ref[3 - s:4 - s, :] * pltpu.roll(xe, s, 0)[SUBLANES:]
    ext_ref[0:SUBLANES, :] = xe[gt:gt + SUBLANES]
    act = _silu(conv)

    ba = ba_ref[...]
    beta_all = jax.nn.sigmoid(ba)
    xa = ba + gp_ref[1:2, :]
    softplus = jnp.maximum(xa, 0.0) + jnp.log1p(jnp.exp(-jnp.abs(xa)))
    g_all = -jnp.exp(gp_ref[0:1, :]) * softplus
    row_in_chunk = lax.broadcasted_iota(jnp.int32, (gt, LANES), 0) & (c - 1)
    gcum = g_all
    s = 1
    while s < c:
        gcum = gcum + jnp.where(row_in_chunk >= s, pltpu.roll(gcum, s, 0), 0.0)
        s *= 2
    exp_g = jnp.exp(gcum)
    kdec_parts, gc_rows = [], []
    for j in chunks:
        g_last = gcum[c * j + c - 1:c * j + c, :]
        kdec_parts.append(jnp.exp(g_last - gcum[c * j:c * (j + 1), :]))
        gc_rows.append(jnp.exp(g_last))
    kdec = jnp.concatenate(kdec_parts, axis=0)
    gcum_t = gcum.T

    ri = lax.broadcasted_iota(jnp.int32, (gt, gt), 0)
    ci = lax.broadcasted_iota(jnp.int32, (gt, gt), 1)
    same_chunk = (ri // c) == (ci // c)
    strict = same_chunk & (ri > ci)
    incl = same_chunk & (ri >= ci)

    qn, kn, kb, beta, eg = [], [], [], [], []
    for h in hs:
        qh = act[:, h * dim:(h + 1) * dim]
        kh = act[:, hd + h * dim:hd + (h + 1) * dim]
        qn.append(qh * lax.rsqrt(jnp.sum(qh * qh, axis=-1, keepdims=True) + EPS) * (dim ** -0.5))
        kn.append(kh * lax.rsqrt(jnp.sum(kh * kh, axis=-1, keepdims=True) + EPS))
        kb.append(kn[h].astype(BF16))
        beta.append(beta_all[:, h:h + 1])
        eg.append(exp_g[:, heads + h:heads + h + 1])

    decay = [jnp.exp(jnp.where(incl, gcum[:, heads + h:heads + h + 1] - gcum_t[heads + h:heads + h + 1, :],
                               -jnp.inf)) for h in hs]
    kk = [_dot_nt(kb[h], kb[h]) for h in hs]
    pw = [jnp.where(strict, (beta[h] * kk[h]) * decay[h], 0.0).astype(BF16) for h in hs]

    sol = [jnp.concatenate([act[:, 2 * hd + h * dim:2 * hd + (h + 1) * dim] * beta[h],
                            kn[h] * (beta[h] * eg[h])], axis=1) for h in hs]
    sol = [sol[h] - _dot(pw[h], sol[h].astype(BF16)) for h in hs]
    order = 2
    while order < c:
        pw = [_dot(pw[h], pw[h]).astype(BF16) for h in hs]
        sol = [sol[h] + _dot(pw[h], sol[h].astype(BF16)) for h in hs]
        order *= 2

    pmat = [(_dot_nt(qn[h].astype(BF16), kb[h]) * decay[h]).astype(BF16) for h in hs]
    wq = [[jnp.concatenate([sol[h][c * j:c * (j + 1), dim:], (qn[h] * eg[h])[c * j:c * (j + 1)]],
                           axis=0).astype(BF16) for j in chunks] for h in hs]
    kd = [(kn[h] * kdec[:, heads + h:heads + h + 1]).astype(BF16) for h in hs]

    state = [state_ref[h] for h in hs]
    o_parts = [[] for _ in hs]
    for j in chunks:
        r = slice(c * j, c * (j + 1))
        ws = [_dot(wq[h][j], state[h].astype(BF16)) for h in hs]
        db = [(sol[h][r, :dim] - ws[h][:c]).astype(BF16) for h in hs]
        for h in hs:
            o_parts[h].append(ws[h][c:] + _dot(pmat[h][r, r], db[h]))
        state = [gc_rows[j][:, heads + h:heads + h + 1] * state[h]
                 + lax.dot_general(kd[h][r], db[h], _TN, preferred_element_type=F32) for h in hs]
    for h in hs:
        state_ref[h] = state[h]

    for h in hs:
        o = jnp.concatenate(o_parts[h], axis=0)
        zh = z_ref[:, h * dim:(h + 1) * dim]
        o_ref[:, h * dim:(h + 1) * dim] = (_rms(o, onw_ref[...]) * _silu(zh)).astype(o_ref.dtype)


def _deltanet(proj3, conv_w, gate_params, out_norm_w, heads, dim, col_z, col_ba):
    b, s, _ = proj3.shape
    hd = heads * dim
    gt = DN_GROUP
    pipelined = (_nbytes((gt, 3 * hd), F32) + _nbytes((gt, hd), F32) + _nbytes((gt, LANES), F32)
                 + _nbytes((gt, hd), BF16))
    resident = (_nbytes((gt + SUBLANES, 3 * hd), F32) * 4 + _nbytes((heads, dim, dim), F32)
                + heads * 6 * _nbytes((gt, gt), F32))
    return pl.pallas_call(
        functools.partial(_dn_kernel, heads=heads, dim=dim),
        grid=(b, s // gt),
        in_specs=[
            pl.BlockSpec((None, gt, 3 * hd), lambda i, g: (i, g, 0)),
            pl.BlockSpec((None, gt, hd), lambda i, g: (i, g, col_z // hd)),
            pl.BlockSpec((None, gt, LANES), lambda i, g: (i, g, col_ba // LANES)),
            pl.BlockSpec((4, 3 * hd), lambda i, g: (0, 0)),
            pl.BlockSpec((SUBLANES, LANES), lambda i, g: (0, 0)),
            pl.BlockSpec((1, dim), lambda i, g: (0, 0)),
        ],
        out_specs=pl.BlockSpec((None, gt, hd), lambda i, g: (i, g, 0)),
        out_shape=jax.ShapeDtypeStruct((b, s, hd), BF16),
        scratch_shapes=[pltpu.VMEM((gt + SUBLANES, 3 * hd), F32), pltpu.VMEM((heads, dim, dim), F32)],
        compiler_params=pltpu.CompilerParams(
            dimension_semantics=("parallel", "arbitrary"),
            vmem_limit_bytes=_vmem_limit(pipelined, resident)),
        name="deltanet",
    )(proj3, proj3, proj3, conv_w, gate_params, out_norm_w)


def _swa_kernel(rb_ref, sink_ref, q_ref, kc_ref, kp_ref, vc_ref, vp_ref, qw_ref, kw_ref, o_ref, bias_ref,
                *, q_heads, kv_heads, dim):
    blk = SWA_BLOCK
    n = pl.program_id(1)
    qi = lax.broadcasted_iota(jnp.int32, (blk, 2 * blk), 0)
    kj = lax.broadcasted_iota(jnp.int32, (blk, 2 * blk), 1)
    dist = qi - kj + blk

    @pl.when((pl.program_id(0) == 0) & (n == 0))
    def _():
        max_exact = N_BUCKETS // 2
        nn = jnp.maximum(dist, 0)
        nf = jnp.maximum(nn, 1).astype(F32)
        large = max_exact + (jnp.log(nf / max_exact) / math.log(MAX_DISTANCE / max_exact)
                             * (N_BUCKETS - max_exact)).astype(jnp.int32)
        bucket = jnp.where(nn < max_exact, nn, jnp.minimum(large, N_BUCKETS - 1))
        for h in range(q_heads):
            acc = jnp.zeros((blk, 2 * blk), F32)
            for bk in range(N_BUCKETS):
                acc = jnp.where(bucket == bk, rb_ref[bk * q_heads + h], acc)
            bias_ref[h] = acc

    valid = (dist >= 0) & (dist < SWA_WINDOW) & ((kj >= blk) | (n > 0))
    group = q_heads // kv_heads
    for j in range(kv_heads):
        cols = slice(j * dim, (j + 1) * dim)
        kwin = jnp.concatenate([kp_ref[:, cols], kc_ref[:, cols]], axis=0)
        kwin = _rms(kwin, kw_ref[...]).astype(BF16)
        vwin = jnp.concatenate([vp_ref[:, cols], vc_ref[:, cols]], axis=0).astype(BF16)
        qcat = jnp.concatenate(
            [_rms(q_ref[:, (j * group + i) * dim:(j * group + i + 1) * dim], qw_ref[...]) for i in range(group)],
            axis=0).astype(BF16)
        logits = _dot_nt(qcat, kwin) * (dim ** -0.5)
        for i in range(group):
            h = j * group + i
            lg = jnp.where(valid, logits[i * blk:(i + 1) * blk] + bias_ref[h], -jnp.inf)
            sink = sink_ref[h]
            mx = jnp.maximum(jnp.max(lg, axis=-1, keepdims=True), sink)
            e = jnp.exp(lg - mx)
            den = jnp.sum(e, axis=-1, keepdims=True) + jnp.exp(sink - mx)
            pv = _dot(e.astype(BF16), vwin)
            o_ref[:, h * dim:(h + 1) * dim] = (pv / den).astype(o_ref.dtype)


def _swa(proj3, rel_bias_flat, sinks, q_norm_w, k_norm_w, q_heads, kv_heads, dim, col_q, col_k, col_v):
    b, s, _ = proj3.shape
    blk = SWA_BLOCK
    qw, kvw = q_heads * dim, kv_heads * dim
    smem = pl.BlockSpec(memory_space=pltpu.SMEM)
    pipelined = _nbytes((blk, qw), F32) + 4 * _nbytes((blk, kvw), F32) + _nbytes((blk, qw), BF16)
    resident = _nbytes((q_heads, blk, 2 * blk), F32) + (8 << 20)
    return pl.pallas_call(
        functools.partial(_swa_kernel, q_heads=q_heads, kv_heads=kv_heads, dim=dim),
        grid=(b, s // blk),
        in_specs=[
            smem, smem,
            pl.BlockSpec((None, blk, qw), lambda i, n: (i, n, col_q // qw)),
            pl.BlockSpec((None, blk, kvw), lambda i, n: (i, n, col_k // kvw)),
            pl.BlockSpec((None, blk, kvw), lambda i, n: (i, jnp.maximum(n - 1, 0), col_k // kvw)),
            pl.BlockSpec((None, blk, kvw), lambda i, n: (i, n, col_v // kvw)),
            pl.BlockSpec((None, blk, kvw), lambda i, n: (i, jnp.maximum(n - 1, 0), col_v // kvw)),
            pl.BlockSpec((1, dim), lambda i, n: (0, 0)),
            pl.BlockSpec((1, dim), lambda i, n: (0, 0)),
        ],
        out_specs=pl.BlockSpec((None, blk, qw), lambda i, n: (i, n, 0)),
        out_shape=jax.ShapeDtypeStruct((b, s, qw), BF16),
        scratch_shapes=[pltpu.VMEM((q_heads, blk, 2 * blk), F32)],
        compiler_params=pltpu.CompilerParams(
            dimension_semantics=("arbitrary", "arbitrary"),
            vmem_limit_bytes=_vmem_limit(pipelined, resident)),
        name="swa",
    )(rel_bias_flat, sinks, proj3, proj3, proj3, proj3, proj3, q_norm_w, k_norm_w)


def _memattn_kernel(q_ref, k_ref, v_ref, qw_ref, kw_ref, o_ref, *, heads, dim):
    for h in range(heads):
        cols = slice(h * dim, (h + 1) * dim)
        qn = _rms(q_ref[:, cols], qw_ref[...]).astype(BF16)
        kn = _rms(k_ref[:, cols], kw_ref[...]).astype(BF16)
        lg = _dot_nt(qn, kn) * (dim ** -0.5)
        e = jnp.exp(lg - jnp.max(lg, axis=-1, keepdims=True))
        den = jnp.sum(e, axis=-1, keepdims=True)
        pv = _dot(e.astype(BF16), v_ref[:, cols].astype(BF16))
        o_ref[:, cols] = (pv / den).astype(o_ref.dtype)


def _memattn(proj3, mkv3, q_norm_w, k_norm_w, heads, dim, col_q, tq):
    b, s, _ = proj3.shape
    m = mkv3.shape[1]
    w = heads * dim
    pipelined = _nbytes((tq, w), F32) + 2 * _nbytes((m, w), F32) + _nbytes((tq, w), BF16)
    return pl.pallas_call(
        functools.partial(_memattn_kernel, heads=heads, dim=dim),
        grid=(b, s // tq),
        in_specs=[
            pl.BlockSpec((None, tq, w), lambda i, t: (i, t, col_q // w)),
            pl.BlockSpec((None, m, w), lambda i, t: (i, 0, 0)),
            pl.BlockSpec((None, m, w), lambda i, t: (i, 0, 1)),
            pl.BlockSpec((1, dim), lambda i, t: (0, 0)),
            pl.BlockSpec((1, dim), lambda i, t: (0, 0)),
        ],
        out_specs=pl.BlockSpec((None, tq, w), lambda i, t: (i, t, 0)),
        out_shape=jax.ShapeDtypeStruct((b, s, w), BF16),
        compiler_params=pltpu.CompilerParams(
            dimension_semantics=("parallel", "parallel"),
            vmem_limit_bytes=_vmem_limit(pipelined, 8 << 20)),
        name="memattn",
    )(proj3, mkv3, mkv3, q_norm_w, k_norm_w)


def _merge_kernel(od_ref, os_ref, om_ref, pd_ref, ps_ref, pm_ref, gd_ref, gs_ref, gm_ref, o_ref):
    merged = (jax.nn.sigmoid(gd_ref[...]) * _dot(od_ref[...], pd_ref[...])
              + jax.nn.sigmoid(gs_ref[...]) * _dot(os_ref[...], ps_ref[...])
              + jax.nn.sigmoid(gm_ref[...]) * _dot(om_ref[...], pm_ref[...]))
    o_ref[...] = merged.astype(o_ref.dtype)


def _merge(o_dn, o_swa, o_mem, p_dn, p_swa, p_mem, proj, col_gates, tm, tn):
    t = o_dn.shape[0]
    d = p_dn.shape[1]
    wd, ws, wm = o_dn.shape[1], o_swa.shape[1], o_mem.shape[1]
    g0 = col_gates // tn
    gstep = d // tn
    pipelined = ((_nbytes((tm, wd), BF16) + _nbytes((tm, ws), BF16) + _nbytes((tm, wm), BF16))
                 + (_nbytes((wd, tn), BF16) + _nbytes((ws, tn), BF16) + _nbytes((wm, tn), BF16))
                 + 3 * _nbytes((tm, tn), F32) + _nbytes((tm, tn), BF16))
    return pl.pallas_call(
        _merge_kernel,
        grid=(d // tn, t // tm),
        in_specs=[
            pl.BlockSpec((tm, wd), lambda j, i: (i, 0)),
            pl.BlockSpec((tm, ws), lambda j, i: (i, 0)),
            pl.BlockSpec((tm, wm), lambda j, i: (i, 0)),
            pl.BlockSpec((wd, tn), lambda j, i: (0, j)),
            pl.BlockSpec((ws, tn), lambda j, i: (0, j)),
            pl.BlockSpec((wm, tn), lambda j, i: (0, j)),
            pl.BlockSpec((tm, tn), lambda j, i: (i, g0 + j)),
            pl.BlockSpec((tm, tn), lambda j, i: (i, g0 + gstep + j)),
            pl.BlockSpec((tm, tn), lambda j, i: (i, g0 + 2 * gstep + j)),
        ],
        out_specs=pl.BlockSpec((tm, tn), lambda j, i: (i, j)),
        out_shape=jax.ShapeDtypeStruct((t, d), BF16),
        compiler_params=pltpu.CompilerParams(
            dimension_semantics=("parallel", "parallel"),
            vmem_limit_bytes=_vmem_limit(pipelined, 3 * _nbytes((tm, tn), F32))),
        name="merge",
    )(o_dn, o_swa, o_mem, p_dn, p_swa, p_mem, proj, proj, proj)


def _outproj_kernel(x_ref, m_ref, w_ref, nw_ref, x1_ref, h_ref):
    x1 = x_ref[...] + _dot(m_ref[...], w_ref[...])
    x1_ref[...] = x1
    h_ref[...] = _rms(x1, nw_ref[...]).astype(h_ref.dtype)


def _outproj(x, merged, w_out, norm_w, tm):
    t, d = x.shape
    pipelined = 2 * _nbytes((tm, d), F32) + 2 * _nbytes((tm, d), BF16) + _nbytes((d, d), BF16)
    return pl.pallas_call(
        _outproj_kernel,
        grid=(t // tm,),
        in_specs=[
            pl.BlockSpec((tm, d), lambda i: (i, 0)),
            pl.BlockSpec((tm, d), lambda i: (i, 0)),
            pl.BlockSpec((d, d), lambda i: (0, 0)),
            pl.BlockSpec((1, d), lambda i: (0, 0)),
        ],
        out_specs=[pl.BlockSpec((tm, d), lambda i: (i, 0)), pl.BlockSpec((tm, d), lambda i: (i, 0))],
        out_shape=[jax.ShapeDtypeStruct((t, d), F32), jax.ShapeDtypeStruct((t, d), BF16)],
        compiler_params=pltpu.CompilerParams(
            dimension_semantics=("parallel",),
            vmem_limit_bytes=_vmem_limit(pipelined, 2 * _nbytes((tm, d), F32))),
        name="outproj",
    )(x, merged, w_out, norm_w)


def _mlp_kernel(h_ref, x1_ref, wu_ref, wd_ref, o_ref):
    @pl.when(pl.program_id(1) == 0)
    def _():
        o_ref[...] = x1_ref[...]

    a = jnp.maximum(_dot(h_ref[...], wu_ref[...]), 0.0)
    o_ref[...] += _dot((a * a).astype(BF16), wd_ref[...])


def _mlp(h2, x1, w_up, w_down, tm, tf):
    t, d = x1.shape
    f = w_up.shape[1]
    pipelined = (_nbytes((tm, d), BF16) + 2 * _nbytes((tm, d), F32)
                 + _nbytes((d, tf), BF16) + _nbytes((tf, d), BF16))
    resident = 2 * _nbytes((tm, tf), F32)
    return pl.pallas_call(
        _mlp_kernel,
        grid=(t // tm, f // tf),
        in_specs=[
            pl.BlockSpec((tm, d), lambda i, k: (i, 0)),
            pl.BlockSpec((tm, d), lambda i, k: (i, 0)),
            pl.BlockSpec((d, tf), lambda i, k: (0, k)),
            pl.BlockSpec((tf, d), lambda i, k: (k, 0)),
        ],
        out_specs=pl.BlockSpec((tm, d), lambda i, k: (i, 0)),
        out_shape=jax.ShapeDtypeStruct((t, d), F32),
        compiler_params=pltpu.CompilerParams(
            dimension_semantics=("parallel", "arbitrary"),
            vmem_limit_bytes=_vmem_limit(pipelined, resident)),
        name="mlp",
    )(h2, x1, w_up, w_down)


def _layer(x, mem, attn_norm_w, w_in, dn_conv_w, dn_a_log, dn_dt_bias, dn_out_norm_w, swa_q_norm_w,
           swa_k_norm_w, swa_sinks, rel_bias, mem_norm_w, w_mem_kv, xq_norm_w, xk_norm_w,
           p_dn, p_swa, p_mem, w_out, mlp_norm_w, w_mlp_up, w_mlp_down):
    b, s, d = x.shape
    m = mem.shape[1]
    t = b * s

    dn_heads = dn_a_log.shape[0]
    dn_dim = dn_out_norm_w.shape[0]
    dn_w = dn_heads * dn_dim
    swa_heads = swa_sinks.shape[0]
    swa_dim = swa_q_norm_w.shape[0]
    swa_w = swa_heads * swa_dim
    mem_dim = xq_norm_w.shape[0]
    mem_w = p_mem.shape[0]
    mem_heads = mem_w // mem_dim
    swa_kv_w = (w_in.shape[1] - 4 * dn_w - 2 * dn_heads - swa_w - mem_w - 3 * d) // 2
    swa_kv = swa_kv_w // swa_dim

    n_ba = 2 * dn_heads
    src_ba = 4 * dn_w
    src_swa = src_ba + n_ba
    src_mq = src_swa + swa_w + 2 * swa_kv_w
    w_t = w_in.T
    w_in_rt = jnp.concatenate(
        [w_t[:src_ba], w_t[src_swa:src_mq], w_t[src_ba:src_swa],
         jnp.zeros((BA_WIDTH - n_ba, d), w_in.dtype), w_t[src_mq:]], axis=0).astype(BF16)
    col = {"qkv": 0, "z": 3 * dn_w, "sq": src_ba}
    col["sk"] = col["sq"] + swa_w
    col["sv"] = col["sk"] + swa_kv_w
    col["ba"] = col["sv"] + swa_kv_w
    col["mq"] = col["ba"] + BA_WIDTH
    col["gates"] = col["mq"] + mem_w

    proj = _norm_matmul(x.reshape(t, d), attn_norm_w.reshape(1, d), w_in_rt, tm=1024, tn=1024,
                        w_transposed=True)
    proj3 = proj.reshape(b, s, -1)
    mkv = _norm_matmul(mem.reshape(b * m, d), mem_norm_w.reshape(1, d), w_mem_kv.astype(BF16), tm=512, tn=512,
                       w_transposed=False)
    mkv3 = mkv.reshape(b, m, -1)

    gate_params = jnp.zeros((SUBLANES, LANES), F32)
    gate_params = gate_params.at[0, dn_heads:2 * dn_heads].set(dn_a_log)
    gate_params = gate_params.at[1, dn_heads:2 * dn_heads].set(dn_dt_bias)
    o_dn = _deltanet(proj3, dn_conv_w, gate_params, dn_out_norm_w.reshape(1, dn_dim),
                     dn_heads, dn_dim, col["z"], col["ba"])
    o_swa = _swa(proj3, rel_bias.reshape(-1), swa_sinks, swa_q_norm_w.reshape(1, swa_dim),
                 swa_k_norm_w.reshape(1, swa_dim), swa_heads, swa_kv, swa_dim, col["sq"], col["sk"], col["sv"])
    o_mem = _memattn(proj3, mkv3, xq_norm_w.reshape(1, mem_dim), xk_norm_w.reshape(1, mem_dim),
                     mem_heads, mem_dim, col["mq"], tq=512)

    merged = _merge(o_dn.reshape(t, dn_w), o_swa.reshape(t, swa_w), o_mem.reshape(t, mem_w),
                    p_dn.astype(BF16), p_swa.astype(BF16), p_mem.astype(BF16), proj, col["gates"],
                    tm=512, tn=1024)
    x1, h2 = _outproj(x.reshape(t, d), merged, w_out.astype(BF16), mlp_norm_w.reshape(1, d), tm=256)
    out = _mlp(h2, x1, w_mlp_up.astype(BF16), w_mlp_down.astype(BF16), tm=512, tf=1024)
    return out.reshape(b, s, d)


def kernel(x, mem, attn_norm_w, w_in, dn_conv_w, dn_A_log, dn_dt_bias, dn_out_norm_w, swa_q_norm_w,
           swa_k_norm_w, swa_sinks, rel_bias, mem_norm_w, w_mem_kv, xq_norm_w, xk_norm_w, p_dn, p_swa,
           p_mem, w_out, mlp_norm_w, w_mlp_up, w_mlp_down):
    depth = w_in.shape[0]
    for l in range(depth):
        x = _layer(x, mem, attn_norm_w[l], w_in[l], dn_conv_w[l], dn_A_log[l], dn_dt_bias[l],
                   dn_out_norm_w[l], swa_q_norm_w[l], swa_k_norm_w[l], swa_sinks[l], rel_bias,
                   mem_norm_w[l], w_mem_kv[l], xq_norm_w[l], xk_norm_w[l], p_dn[l], p_swa[l], p_mem[l],
                   w_out[l], mlp_norm_w[l], w_mlp_up[l], w_mlp_down[l])
    return x
```

```python
import functools
import math

import jax
import jax.numpy as jnp
from jax import lax
from jax.experimental import pallas as pl
from jax.experimental.pallas import tpu as pltpu

F32 = jnp.float32
BF16 = jnp.bfloat16

EPS = 1e-6
LANES = 128
SUBLANES = 8
V7X_VMEM_BYTES = 64 * 1024 * 1024

DN_CHUNK = 64
DN_GROUP = 256
SWA_WINDOW = 128
SWA_BLOCK = 128
N_BUCKETS = 32
MAX_DISTANCE = 128

_NT = (((1,), (1,)), ((), ()))
_TN = (((0,), (0,)), ((), ()))


def _vmem_limit(pipelined_bytes, resident_bytes):
    want = 2 * pipelined_bytes + resident_bytes + (4 << 20)
    return int(min(want, V7X_VMEM_BYTES - (8 << 20)))


def _nbytes(shape, dtype):
    return math.prod(shape) * jnp.dtype(dtype).itemsize


def _silu(v):
    return v * jax.nn.sigmoid(v)


def _rms(v, w):
    return (v * lax.rsqrt(jnp.mean(v * v, axis=-1, keepdims=True) + EPS)) * w


def _dot(a, b):
    return jnp.dot(a, b, preferred_element_type=F32)


def _dot_nt(a, b):
    return lax.dot_general(a, b, _NT, preferred_element_type=F32)


def _inproj_kernel(x_ref, nw_ref, wa_ref, wb_ref, wc_ref, o_ref, ba_ref, h_ref, *, n_a):
    j = pl.program_id(1)

    @pl.when(j == 0)
    def _():
        h_ref[...] = _rms(x_ref[...], nw_ref[...]).astype(BF16)
        ba_ref[...] = _dot_nt(h_ref[...], wc_ref[...])

    @pl.when(j < n_a)
    def _():
        o_ref[...] = _dot_nt(h_ref[...], wa_ref[...])

    @pl.when(j >= n_a)
    def _():
        o_ref[...] = _dot_nt(h_ref[...], wb_ref[...])


def _inproj(x, nw, w_t, w_b, w_c, rows_a, tm, tn):
    m, k = x.shape
    n = rows_a + w_b.shape[0]
    n_a = rows_a // tn
    assert rows_a % tn == 0
    pipelined = (_nbytes((tm, k), F32) + 2 * _nbytes((tn, k), BF16) + _nbytes((tm, tn), F32)
                 + _nbytes((tm, LANES), F32))
    resident = _nbytes((tm, k), BF16) + _nbytes((tm, k), F32) + _nbytes((LANES, k), BF16)
    return pl.pallas_call(
        functools.partial(_inproj_kernel, n_a=n_a),
        grid=(m // tm, pl.cdiv(n, tn)),
        in_specs=[
            pl.BlockSpec((tm, k), lambda i, j: (i, 0)),
            pl.BlockSpec((1, k), lambda i, j: (0, 0)),
            pl.BlockSpec((tn, k), lambda i, j: (jnp.minimum(j, n_a - 1), 0)),
            pl.BlockSpec((tn, k), lambda i, j: (jnp.maximum(j - n_a, 0), 0)),
            pl.BlockSpec((LANES, k), lambda i, j: (0, 0)),
        ],
        out_specs=[pl.BlockSpec((tm, tn), lambda i, j: (i, j)),
                   pl.BlockSpec((tm, LANES), lambda i, j: (i, 0))],
        out_shape=[jax.ShapeDtypeStruct((m, n), F32), jax.ShapeDtypeStruct((m, LANES), F32)],
        scratch_shapes=[pltpu.VMEM((tm, k), BF16)],
        compiler_params=pltpu.CompilerParams(
            dimension_semantics=("parallel", "arbitrary"),
            vmem_limit_bytes=_vmem_limit(pipelined, resident)),
        name="inproj",
    )(x, nw, w_t, w_b, w_c)


def _norm_matmul_kernel(x_ref, nw_ref, w_ref, o_ref, h_ref):
    @pl.when(pl.program_id(1) == 0)
    def _():
        h_ref[...] = _rms(x_ref[...], nw_ref[...]).astype(BF16)

    o_ref[...] = _dot(h_ref[...], w_ref[...]).astype(o_ref.dtype)


def _norm_matmul(x, nw, w, tm, tn):
    m, k = x.shape
    n = w.shape[1]
    pipelined = _nbytes((tm, k), F32) + _nbytes((k, tn), BF16) + _nbytes((tm, tn), F32)
    resident = _nbytes((tm, k), BF16) + _nbytes((tm, k), F32)
    return pl.pallas_call(
        _norm_matmul_kernel,
        grid=(m // tm, n // tn),
        in_specs=[
            pl.BlockSpec((tm, k), lambda i, j: (i, 0)),
            pl.BlockSpec((1, k), lambda i, j: (0, 0)),
            pl.BlockSpec((k, tn), lambda i, j: (0, j)),
        ],
        out_specs=pl.BlockSpec((tm, tn), lambda i, j: (i, j)),
        out_shape=jax.ShapeDtypeStruct((m, n), F32),
        scratch_shapes=[pltpu.VMEM((tm, k), BF16)],
        compiler_params=pltpu.CompilerParams(
            dimension_semantics=("parallel", "arbitrary"),
            vmem_limit_bytes=_vmem_limit(pipelined, resident)),
        name="norm_matmul",
    )(x, nw, w)


def _dn_kernel(qkv_ref, z_ref, ba_ref, cw_ref, gp_ref, onw_ref, o_ref, ext_ref, state_ref, *, heads, dim):
    gt = DN_GROUP
    c = DN_CHUNK
    hd = heads * dim
    hs = range(heads)
    chunks = range(gt // c)

    @pl.when(pl.program_id(1) == 0)
    def _():
        ext_ref[0:SUBLANES, :] = jnp.zeros((SUBLANES, 3 * hd), F32)
        state_ref[...] = jnp.zeros_like(state_ref)

    ext_ref[SUBLANES:SUBLANES + gt, :] = qkv_ref[...]
    xe = ext_ref[...]
    conv = cw_ref[3:4, :] * xe[SUBLANES:]
    for s in (1, 2, 3):
        conv = conv + cw_ref[3 - s:4 - s, :] * pltpu.roll(xe, s, 0)[SUBLANES:]
    ext_ref[0:SUBLANES, :] = xe[gt:gt + SUBLANES]
    act = _silu(conv)

    ba = ba_ref[...]
    beta_all = jax.nn.sigmoid(ba)
    xa = ba + gp_ref[1:2, :]
    softplus = jnp.maximum(xa, 0.0) + jnp.log1p(jnp.exp(-jnp.abs(xa)))
    g_all = -jnp.exp(gp_ref[0:1, :]) * softplus
    row_in_chunk = lax.broadcasted_iota(jnp.int32, (gt, LANES), 0) & (c - 1)
    gcum = g_all
    s = 1
    while s < c:
        gcum = gcum + jnp.where(row_in_chunk >= s, pltpu.roll(gcum, s, 0), 0.0)
        s *= 2
    exp_g = jnp.exp(gcum)
    kdec_parts, gc_rows = [], []
    for j in chunks:
        g_last = gcum[c * j + c - 1:c * j + c, :]
        kdec_parts.append(jnp.exp(g_last - gcum[c * j:c * (j + 1), :]))
        gc_rows.append(jnp.exp(g_last))
    kdec = jnp.concatenate(kdec_parts, axis=0)
    gcum_t = gcum.T

    ri = lax.broadcasted_iota(jnp.int32, (gt, gt), 0)
    ci = lax.broadcasted_iota(jnp.int32, (gt, gt), 1)
    same_chunk = (ri // c) == (ci // c)
    strict = same_chunk & (ri > ci)
    incl = same_chunk & (ri >= ci)

    qn, kn, kb, beta, eg = [], [], [], [], []
    for h in hs:
        qh = act[:, h * dim:(h + 1) * dim]
        kh = act[:, hd + h * dim:hd + (h + 1) * dim]
        qn.append(qh * lax.rsqrt(jnp.sum(qh * qh, axis=-1, keepdims=True) + EPS) * (dim ** -0.5))
        kn.append(kh * lax.rsqrt(jnp.sum(kh * kh, axis=-1, keepdims=True) + EPS))
        kb.append(kn[h].astype(BF16))
        beta.append(beta_all[:, h:h + 1])
        eg.append(exp_g[:, heads + h:heads + h + 1])

    decay = [jnp.exp(jnp.where(incl, gcum[:, heads + h:heads + h + 1] - gcum_t[heads + h:heads + h + 1, :],
                               -jnp.inf)) for h in hs]
    kk = [_dot_nt(kb[h], kb[h]) for h in hs]
    pw = [jnp.where(strict, (beta[h] * kk[h]) * decay[h], 0.0).astype(BF16) for h in hs]

    sol = [jnp.concatenate([act[:, 2 * hd + h * dim:2 * hd + (h + 1) * dim] * beta[h],
                            kn[h] * (beta[h] * eg[h])], axis=1) for h in hs]
    sol = [sol[h] - _dot(pw[h], sol[h].astype(BF16)) for h in hs]
    order = 2
    while order < c:
        pw = [_dot(pw[h], pw[h]).astype(BF16) for h in hs]
        sol = [sol[h] + _dot(pw[h], sol[h].astype(BF16)) for h in hs]
        order *= 2

    pmat = [(_dot_nt(qn[h].astype(BF16), kb[h]) * decay[h]).astype(BF16) for h in hs]
    wq = [[jnp.concatenate([sol[h][c * j:c * (j + 1), dim:], (qn[h] * eg[h])[c * j:c * (j + 1)]],
                           axis=0).astype(BF16) for j in chunks] for h in hs]
    kd = [(kn[h] * kdec[:, heads + h:heads + h + 1]).astype(BF16) for h in hs]

    state = [state_ref[h] for h in hs]
    o_parts = [[] for _ in hs]
    for j in chunks:
        r = slice(c * j, c * (j + 1))
        ws = [_dot(wq[h][j], state[h].astype(BF16)) for h in hs]
        db = [(sol[h][r, :dim] - ws[h][:c]).astype(BF16) for h in hs]
        for h in hs:
            o_parts[h].append(ws[h][c:] + _dot(pmat[h][r, r], db[h]))
        state = [gc_rows[j][:, heads + h:heads + h + 1] * state[h]
                 + lax.dot_general(kd[h][r], db[h], _TN, preferred_element_type=F32) for h in hs]
    for h in hs:
        state_ref[h] = state[h]

    for h in hs:
        o = jnp.concatenate(o_parts[h], axis=0)
        zh = z_ref[:, h * dim:(h + 1) * dim]
        o_ref[:, h * dim:(h + 1) * dim] = (_rms(o, onw_ref[...]) * _silu(zh)).astype(o_ref.dtype)


def _deltanet(proj3, ba3, conv_w, gate_params, out_norm_w, heads, dim, col_z):
    b, s, _ = proj3.shape
    hd = heads * dim
    gt = DN_GROUP
    pipelined = (_nbytes((gt, 3 * hd), F32) + _nbytes((gt, hd), F32) + _nbytes((gt, LANES), F32)
                 + _nbytes((gt, hd), BF16))
    resident = (_nbytes((gt + SUBLANES, 3 * hd), F32) * 4 + _nbytes((heads, dim, dim), F32)
                + heads * 6 * _nbytes((gt, gt), F32))
    return pl.pallas_call(
        functools.partial(_dn_kernel, heads=heads, dim=dim),
        grid=(b, s // gt),
        in_specs=[
            pl.BlockSpec((None, gt, 3 * hd), lambda i, g: (i, g, 0)),
            pl.BlockSpec((None, gt, hd), lambda i, g: (i, g, col_z // hd)),
            pl.BlockSpec((None, gt, LANES), lambda i, g: (i, g, 0)),
            pl.BlockSpec((4, 3 * hd), lambda i, g: (0, 0)),
            pl.BlockSpec((SUBLANES, LANES), lambda i, g: (0, 0)),
            pl.BlockSpec((1, dim), lambda i, g: (0, 0)),
        ],
        out_specs=pl.BlockSpec((None, gt, hd), lambda i, g: (i, g, 0)),
        out_shape=jax.ShapeDtypeStruct((b, s, hd), BF16),
        scratch_shapes=[pltpu.VMEM((gt + SUBLANES, 3 * hd), F32), pltpu.VMEM((heads, dim, dim), F32)],
        compiler_params=pltpu.CompilerParams(
            dimension_semantics=("parallel", "arbitrary"),
            vmem_limit_bytes=_vmem_limit(pipelined, resident)),
        name="deltanet",
    )(proj3, proj3, ba3, conv_w, gate_params, out_norm_w)


def _swa_kernel(rb_ref, sink_ref, q_ref, kc_ref, kp_ref, vc_ref, vp_ref, qw_ref, kw_ref, o_ref, bias_ref,
                *, q_heads, kv_heads, dim):
    blk = SWA_BLOCK
    n = pl.program_id(1)
    qi = lax.broadcasted_iota(jnp.int32, (blk, 2 * blk), 0)
    kj = lax.broadcasted_iota(jnp.int32, (blk, 2 * blk), 1)
    dist = qi - kj + blk

    @pl.when((pl.program_id(0) == 0) & (n == 0))
    def _():
        max_exact = N_BUCKETS // 2
        nn = jnp.maximum(dist, 0)
        nf = jnp.maximum(nn, 1).astype(F32)
        large = max_exact + (jnp.log(nf / max_exact) / math.log(MAX_DISTANCE / max_exact)
                             * (N_BUCKETS - max_exact)).astype(jnp.int32)
        bucket = jnp.where(nn < max_exact, nn, jnp.minimum(large, N_BUCKETS - 1))
        for h in range(q_heads):
            acc = jnp.zeros((blk, 2 * blk), F32)
            for bk in range(N_BUCKETS):
                acc = jnp.where(bucket == bk, rb_ref[bk * q_heads + h], acc)
            bias_ref[h] = acc

    valid = (dist >= 0) & (dist < SWA_WINDOW) & ((kj >= blk) | (n > 0))
    group = q_heads // kv_heads
    for j in range(kv_heads):
        cols = slice(j * dim, (j + 1) * dim)
        kwin = jnp.concatenate([kp_ref[:, cols], kc_ref[:, cols]], axis=0)
        kwin = _rms(kwin, kw_ref[...]).astype(BF16)
        vwin = jnp.concatenate([vp_ref[:, cols], vc_ref[:, cols]], axis=0).astype(BF16)
        qcat = jnp.concatenate(
            [_rms(q_ref[:, (j * group + i) * dim:(j * group + i + 1) * dim], qw_ref[...]) for i in range(group)],
            axis=0).astype(BF16)
        logits = _dot_nt(qcat, kwin) * (dim ** -0.5)
        for i in range(group):
            h = j * group + i
            lg = jnp.where(valid, logits[i * blk:(i + 1) * blk] + bias_ref[h], -jnp.inf)
            sink = sink_ref[h]
            mx = jnp.maximum(jnp.max(lg, axis=-1, keepdims=True), sink)
            e = jnp.exp(lg - mx)
            den = jnp.sum(e, axis=-1, keepdims=True) + jnp.exp(sink - mx)
            pv = _dot(e.astype(BF16), vwin)
            o_ref[:, h * dim:(h + 1) * dim] = (pv / den).astype(o_ref.dtype)


def _swa(proj3, rel_bias_flat, sinks, q_norm_w, k_norm_w, q_heads, kv_heads, dim, col_q, col_k, col_v):
    b, s, _ = proj3.shape
    blk = SWA_BLOCK
    qw, kvw = q_heads * dim, kv_heads * dim
    smem = pl.BlockSpec(memory_space=pltpu.SMEM)
    pipelined = _nbytes((blk, qw), F32) + 4 * _nbytes((blk, kvw), F32) + _nbytes((blk, qw), BF16)
    resident = _nbytes((q_heads, blk, 2 * blk), F32) + (8 << 20)
    return pl.pallas_call(
        functools.partial(_swa_kernel, q_heads=q_heads, kv_heads=kv_heads, dim=dim),
        grid=(b, s // blk),
        in_specs=[
            smem, smem,
            pl.BlockSpec((None, blk, qw), lambda i, n: (i, n, col_q // qw)),
            pl.BlockSpec((None, blk, kvw), lambda i, n: (i, n, col_k // kvw)),
            pl.BlockSpec((None, blk, kvw), lambda i, n: (i, jnp.maximum(n - 1, 0), col_k // kvw)),
            pl.BlockSpec((None, blk, kvw), lambda i, n: (i, n, col_v // kvw)),
            pl.BlockSpec((None, blk, kvw), lambda i, n: (i, jnp.maximum(n - 1, 0), col_v // kvw)),
            pl.BlockSpec((1, dim), lambda i, n: (0, 0)),
            pl.BlockSpec((1, dim), lambda i, n: (0, 0)),
        ],
        out_specs=pl.BlockSpec((None, blk, qw), lambda i, n: (i, n, 0)),
        out_shape=jax.ShapeDtypeStruct((b, s, qw), BF16),
        scratch_shapes=[pltpu.VMEM((q_heads, blk, 2 * blk), F32)],
        compiler_params=pltpu.CompilerParams(
            dimension_semantics=("arbitrary", "arbitrary"),
            vmem_limit_bytes=_vmem_limit(pipelined, resident)),
        name="swa",
    )(rel_bias_flat, sinks, proj3, proj3, proj3, proj3, proj3, q_norm_w, k_norm_w)


def _memattn_kernel(q_ref, k_ref, v_ref, qw_ref, kw_ref, o_ref, *, heads, dim):
    for h in range(heads):
        cols = slice(h * dim, (h + 1) * dim)
        qn = _rms(q_ref[:, cols], qw_ref[...]).astype(BF16)
        kn = _rms(k_ref[:, cols], kw_ref[...]).astype(BF16)
        lg = _dot_nt(qn, kn) * (dim ** -0.5)
        e = jnp.exp(lg - jnp.max(lg, axis=-1, keepdims=True))
        den = jnp.sum(e, axis=-1, keepdims=True)
        pv = _dot(e.astype(BF16), v_ref[:, cols].astype(BF16))
        o_ref[:, cols] = (pv / den).astype(o_ref.dtype)


def _memattn(proj3, mkv3, q_norm_w, k_norm_w, heads, dim, col_q, tq):
    b, s, _ = proj3.shape
    m = mkv3.shape[1]
    w = heads * dim
    pipelined = _nbytes((tq, w), F32) + 2 * _nbytes((m, w), F32) + _nbytes((tq, w), BF16)
    return pl.pallas_call(
        functools.partial(_memattn_kernel, heads=heads, dim=dim),
        grid=(b, s // tq),
        in_specs=[
            pl.BlockSpec((pl.Element(tq), pl.Element(w)), lambda i, t: (pl.multiple_of(i * s + t * tq, tq), col_q)),
            pl.BlockSpec((None, m, w), lambda i, t: (i, 0, 0)),
            pl.BlockSpec((None, m, w), lambda i, t: (i, 0, 1)),
            pl.BlockSpec((1, dim), lambda i, t: (0, 0)),
            pl.BlockSpec((1, dim), lambda i, t: (0, 0)),
        ],
        out_specs=pl.BlockSpec((None, tq, w), lambda i, t: (i, t, 0)),
        out_shape=jax.ShapeDtypeStruct((b, s, w), BF16),
        compiler_params=pltpu.CompilerParams(
            dimension_semantics=("parallel", "parallel"),
            vmem_limit_bytes=_vmem_limit(pipelined, 8 << 20)),
        name="memattn",
    )(proj3.reshape(b * s, -1), mkv3, mkv3, q_norm_w, k_norm_w)


def _merge_kernel(od_ref, os_ref, om_ref, pd_ref, ps_ref, pm_ref, gd_ref, gs_ref, gm_ref, o_ref):
    merged = (jax.nn.sigmoid(gd_ref[...]) * _dot(od_ref[...], pd_ref[...])
              + jax.nn.sigmoid(gs_ref[...]) * _dot(os_ref[...], ps_ref[...])
              + jax.nn.sigmoid(gm_ref[...]) * _dot(om_ref[...], pm_ref[...]))
    o_ref[...] = merged.astype(o_ref.dtype)


def _merge(o_dn, o_swa, o_mem, p_dn, p_swa, p_mem, proj, col_gates, tm, tn):
    t = o_dn.shape[0]
    d = p_dn.shape[1]
    wd, ws, wm = o_dn.shape[1], o_swa.shape[1], o_mem.shape[1]
    assert col_gates % LANES == 0

    def gate_index(branch, j, i):
        return (pl.multiple_of(i * tm, tm), pl.multiple_of(col_gates + branch * d + j * tn, LANES))

    pipelined = ((_nbytes((tm, wd), BF16) + _nbytes((tm, ws), BF16) + _nbytes((tm, wm), BF16))
                 + (_nbytes((wd, tn), BF16) + _nbytes((ws, tn), BF16) + _nbytes((wm, tn), BF16))
                 + 3 * _nbytes((tm, tn), F32) + _nbytes((tm, tn), BF16))
    return pl.pallas_call(
        _merge_kernel,
        grid=(d // tn, t // tm),
        in_specs=[
            pl.BlockSpec((tm, wd), lambda j, i: (i, 0)),
            pl.BlockSpec((tm, ws), lambda j, i: (i, 0)),
            pl.BlockSpec((tm, wm), lambda j, i: (i, 0)),
            pl.BlockSpec((wd, tn), lambda j, i: (0, j)),
            pl.BlockSpec((ws, tn), lambda j, i: (0, j)),
            pl.BlockSpec((wm, tn), lambda j, i: (0, j)),
            pl.BlockSpec((pl.Element(tm), pl.Element(tn)), functools.partial(gate_index, 0)),
            pl.BlockSpec((pl.Element(tm), pl.Element(tn)), functools.partial(gate_index, 1)),
            pl.BlockSpec((pl.Element(tm), pl.Element(tn)), functools.partial(gate_index, 2)),
        ],
        out_specs=pl.BlockSpec((tm, tn), lambda j, i: (i, j)),
        out_shape=jax.ShapeDtypeStruct((t, d), BF16),
        compiler_params=pltpu.CompilerParams(
            dimension_semantics=("parallel", "parallel"),
            vmem_limit_bytes=_vmem_limit(pipelined, 3 * _nbytes((tm, tn), F32))),
        name="merge",
    )(o_dn, o_swa, o_mem, p_dn, p_swa, p_mem, proj, proj, proj)


def _outproj_kernel(x_ref, m_ref, w_ref, nw_ref, x1_ref, h_ref):
    x1 = x_ref[...] + _dot(m_ref[...], w_ref[...])
    x1_ref[...] = x1
    h_ref[...] = _rms(x1, nw_ref[...]).astype(h_ref.dtype)


def _outproj(x, merged, w_out, norm_w, tm):
    t, d = x.shape
    pipelined = 2 * _nbytes((tm, d), F32) + 2 * _nbytes((tm, d), BF16) + _nbytes((d, d), BF16)
    return pl.pallas_call(
        _outproj_kernel,
        grid=(t // tm,),
        in_specs=[
            pl.BlockSpec((tm, d), lambda i: (i, 0)),
            pl.BlockSpec((tm, d), lambda i: (i, 0)),
            pl.BlockSpec((d, d), lambda i: (0, 0)),
            pl.BlockSpec((1, d), lambda i: (0, 0)),
        ],
        out_specs=[pl.BlockSpec((tm, d), lambda i: (i, 0)), pl.BlockSpec((tm, d), lambda i: (i, 0))],
        out_shape=[jax.ShapeDtypeStruct((t, d), F32), jax.ShapeDtypeStruct((t, d), BF16)],
        compiler_params=pltpu.CompilerParams(
            dimension_semantics=("parallel",),
            vmem_limit_bytes=_vmem_limit(pipelined, 2 * _nbytes((tm, d), F32))),
        name="outproj",
    )(x, merged, w_out, norm_w)


def _mlp_kernel(h_ref, x1_ref, wu_ref, wd_ref, o_ref):
    @pl.when(pl.program_id(1) == 0)
    def _():
        o_ref[...] = x1_ref[...]

    a = jnp.maximum(_dot(h_ref[...], wu_ref[...]), 0.0)
    o_ref[...] += _dot((a * a).astype(BF16), wd_ref[...])


def _mlp(h2, x1, w_up, w_down, tm, tf):
    t, d = x1.shape
    f = w_up.shape[1]
    pipelined = (_nbytes((tm, d), BF16) + 2 * _nbytes((tm, d), F32)
                 + _nbytes((d, tf), BF16) + _nbytes((tf, d), BF16))
    resident = 2 * _nbytes((tm, tf), F32)
    return pl.pallas_call(
        _mlp_kernel,
        grid=(t // tm, f // tf),
        in_specs=[
            pl.BlockSpec((tm, d), lambda i, k: (i, 0)),
            pl.BlockSpec((tm, d), lambda i, k: (i, 0)),
            pl.BlockSpec((d, tf), lambda i, k: (0, k)),
            pl.BlockSpec((tf, d), lambda i, k: (k, 0)),
        ],
        out_specs=pl.BlockSpec((tm, d), lambda i, k: (i, 0)),
        out_shape=jax.ShapeDtypeStruct((t, d), F32),
        compiler_params=pltpu.CompilerParams(
            dimension_semantics=("parallel", "arbitrary"),
            vmem_limit_bytes=_vmem_limit(pipelined, resident)),
        name="mlp",
    )(h2, x1, w_up, w_down)


def _layer(x, mem, attn_norm_w, w_in, dn_conv_w, dn_a_log, dn_dt_bias, dn_out_norm_w, swa_q_norm_w,
           swa_k_norm_w, swa_sinks, rel_bias, mem_norm_w, w_mem_kv, xq_norm_w, xk_norm_w,
           p_dn, p_swa, p_mem, w_out, mlp_norm_w, w_mlp_up, w_mlp_down):
    b, s, d = x.shape
    m = mem.shape[1]
    t = b * s

    dn_heads = dn_a_log.shape[0]
    dn_dim = dn_out_norm_w.shape[0]
    dn_w = dn_heads * dn_dim
    swa_heads = swa_sinks.shape[0]
    swa_dim = swa_q_norm_w.shape[0]
    swa_w = swa_heads * swa_dim
    mem_dim = xq_norm_w.shape[0]
    mem_w = p_mem.shape[0]
    mem_heads = mem_w // mem_dim
    swa_kv_w = (w_in.shape[1] - 4 * dn_w - 2 * dn_heads - swa_w - mem_w - 3 * d) // 2
    swa_kv = swa_kv_w // swa_dim

    n_ba = 2 * dn_heads
    src_ba = 4 * dn_w
    src_swa = src_ba + n_ba
    w_t = w_in.T.astype(BF16)
    w_b = w_t[src_swa:]
    w_c = jnp.concatenate([w_t[src_ba:src_swa], jnp.zeros((LANES - n_ba, d), BF16)], axis=0)
    col = {"qkv": 0, "z": 3 * dn_w, "sq": src_ba}
    col["sk"] = col["sq"] + swa_w
    col["sv"] = col["sk"] + swa_kv_w
    col["mq"] = col["sv"] + swa_kv_w
    col["gates"] = col["mq"] + mem_w

    proj, ba = _inproj(x.reshape(t, d), attn_norm_w.reshape(1, d), w_t, w_b, w_c, src_ba, tm=1024, tn=1024)
    proj3 = proj.reshape(b, s, -1)
    mkv = _norm_matmul(mem.reshape(b * m, d), mem_norm_w.reshape(1, d), w_mem_kv.astype(BF16), tm=512, tn=512)
    mkv3 = mkv.reshape(b, m, -1)

    gate_params = jnp.zeros((SUBLANES, LANES), F32)
    gate_params = gate_params.at[0, dn_heads:2 * dn_heads].set(dn_a_log)
    gate_params = gate_params.at[1, dn_heads:2 * dn_heads].set(dn_dt_bias)
    o_dn = _deltanet(proj3, ba.reshape(b, s, LANES), dn_conv_w, gate_params, dn_out_norm_w.reshape(1, dn_dim),
                     dn_heads, dn_dim, col["z"])
    o_swa = _swa(proj3, rel_bias.reshape(-1), swa_sinks, swa_q_norm_w.reshape(1, swa_dim),
                 swa_k_norm_w.reshape(1, swa_dim), swa_heads, swa_kv, swa_dim, col["sq"], col["sk"], col["sv"])
    o_mem = _memattn(proj3, mkv3, xq_norm_w.reshape(1, mem_dim), xk_norm_w.reshape(1, mem_dim),
                     mem_heads, mem_dim, col["mq"], tq=512)

    merged = _merge(o_dn.reshape(t, dn_w), o_swa.reshape(t, swa_w), o_mem.reshape(t, mem_w),
                    p_dn.astype(BF16), p_swa.astype(BF16), p_mem.astype(BF16), proj, col["gates"],
                    tm=512, tn=1024)
    x1, h2 = _outproj(x.reshape(t, d), merged, w_out.astype(BF16), mlp_norm_w.reshape(1, d), tm=256)
    out = _mlp(h2, x1, w_mlp_up.astype(BF16), w_mlp_down.astype(BF16), tm=512, tf=1024)
    return out.reshape(b, s, d)


def kernel(x, mem, attn_norm_w, w_in, dn_conv_w, dn_A_log, dn_dt_bias, dn_out_norm_w, swa_q_norm_w,
           swa_k_norm_w, swa_sinks, rel_bias, mem_norm_w, w_mem_kv, xq_norm_w, xk_norm_w, p_dn, p_swa,
           p_mem, w_out, mlp_norm_w, w_mlp_up, w_mlp_down):
    depth = w_in.shape[0]
    for l in range(depth):
        x = _layer(x, mem, attn_norm_w[l], w_in[l], dn_conv_w[l], dn_A_log[l], dn_dt_bias[l],
                   dn_out_norm_w[l], swa_q_norm_w[l], swa_k_norm_w[l], swa_sinks[l], rel_bias,
                   mem_norm_w[l], w_mem_kv[l], xq_norm_w[l], xk_norm_w[l], p_dn[l], p_swa[l], p_mem[l],
                   w_out[l], mlp_norm_w[l], w_mlp_up[l], w_mlp_down[l])
    return x
```

```python
import functools
import math

import jax
import jax.numpy as jnp
from jax import lax
from jax.experimental import pallas as pl
from jax.experimental.pallas import tpu as pltpu

F32 = jnp.float32
BF16 = jnp.bfloat16

EPS = 1e-6
LANES = 128
SUBLANES = 8
V7X_VMEM_BYTES = 64 * 1024 * 1024

DN_CHUNK = 64
DN_GROUP = 256
SWA_WINDOW = 128
SWA_BLOCK = 128
N_BUCKETS = 32
MAX_DISTANCE = 128

_NT = (((1,), (1,)), ((), ()))
_TN = (((0,), (0,)), ((), ()))


def _vmem_limit(pipelined_bytes, resident_bytes):
    want = 2 * pipelined_bytes + resident_bytes + (4 << 20)
    return int(min(want, V7X_VMEM_BYTES - (8 << 20)))


def _nbytes(shape, dtype):
    return math.prod(shape) * jnp.dtype(dtype).itemsize


def _sigmoid(v):
    return 0.5 * jnp.tanh(0.5 * v) + 0.5


def _silu(v):
    return v * _sigmoid(v)


def _rms(v, w):
    return (v * lax.rsqrt(jnp.mean(v * v, axis=-1, keepdims=True) + EPS)) * w


def _dot(a, b):
    return jnp.dot(a, b, preferred_element_type=F32)


def _dot_nt(a, b):
    return lax.dot_general(a, b, _NT, preferred_element_type=F32)


def _inproj_kernel(x_ref, nw_ref, wa_ref, wb_ref, wc_ref, o_ref, ba_ref, h_ref, *, n_a):
    j = pl.program_id(1)

    @pl.when(j == 0)
    def _():
        h_ref[...] = _rms(x_ref[...], nw_ref[...]).astype(BF16)
        ba_ref[...] = _dot_nt(h_ref[...], wc_ref[...])

    @pl.when(j < n_a)
    def _():
        o_ref[...] = _dot_nt(h_ref[...], wa_ref[...])

    @pl.when(j >= n_a)
    def _():
        o_ref[...] = _dot_nt(h_ref[...], wb_ref[...])


def _inproj(x, nw, w_a, w_b, w_c, tm, tn):
    m, k = x.shape
    rows_a = w_a.shape[0]
    n = rows_a + w_b.shape[0]
    n_a = rows_a // tn
    assert rows_a % tn == 0
    pipelined = (_nbytes((tm, k), F32) + 2 * _nbytes((tn, k), BF16) + _nbytes((tm, tn), F32)
                 + _nbytes((tm, LANES), F32))
    resident = _nbytes((tm, k), BF16) + _nbytes((tm, k), F32) + _nbytes((LANES, k), BF16)
    return pl.pallas_call(
        functools.partial(_inproj_kernel, n_a=n_a),
        grid=(m // tm, pl.cdiv(n, tn)),
        in_specs=[
            pl.BlockSpec((tm, k), lambda i, j: (i, 0)),
            pl.BlockSpec((1, k), lambda i, j: (0, 0)),
            pl.BlockSpec((tn, k), lambda i, j: (jnp.minimum(j, n_a - 1), 0)),
            pl.BlockSpec((tn, k), lambda i, j: (jnp.maximum(j - n_a, 0), 0)),
            pl.BlockSpec((LANES, k), lambda i, j: (0, 0)),
        ],
        out_specs=[pl.BlockSpec((tm, tn), lambda i, j: (i, j)),
                   pl.BlockSpec((tm, LANES), lambda i, j: (i, 0))],
        out_shape=[jax.ShapeDtypeStruct((m, n), F32), jax.ShapeDtypeStruct((m, LANES), F32)],
        scratch_shapes=[pltpu.VMEM((tm, k), BF16)],
        compiler_params=pltpu.CompilerParams(
            dimension_semantics=("parallel", "arbitrary"),
            vmem_limit_bytes=_vmem_limit(pipelined, resident)),
        name="inproj",
    )(x, nw, w_a, w_b, w_c)


def _norm_matmul_kernel(x_ref, nw_ref, w_ref, o_ref, h_ref):
    @pl.when(pl.program_id(1) == 0)
    def _():
        h_ref[...] = _rms(x_ref[...], nw_ref[...]).astype(BF16)

    o_ref[...] = _dot(h_ref[...], w_ref[...]).astype(o_ref.dtype)


def _norm_matmul(x, nw, w, tm, tn):
    m, k = x.shape
    n = w.shape[1]
    pipelined = _nbytes((tm, k), F32) + _nbytes((k, tn), BF16) + _nbytes((tm, tn), F32)
    resident = _nbytes((tm, k), BF16) + _nbytes((tm, k), F32)
    return pl.pallas_call(
        _norm_matmul_kernel,
        grid=(m // tm, n // tn),
        in_specs=[
            pl.BlockSpec((tm, k), lambda i, j: (i, 0)),
            pl.BlockSpec((1, k), lambda i, j: (0, 0)),
            pl.BlockSpec((k, tn), lambda i, j: (0, j)),
        ],
        out_specs=pl.BlockSpec((tm, tn), lambda i, j: (i, j)),
        out_shape=jax.ShapeDtypeStruct((m, n), F32),
        scratch_shapes=[pltpu.VMEM((tm, k), BF16)],
        compiler_params=pltpu.CompilerParams(
            dimension_semantics=("parallel", "arbitrary"),
            vmem_limit_bytes=_vmem_limit(pipelined, resident)),
        name="norm_matmul",
    )(x, nw, w)


def _dn_kernel(qkv_ref, z_ref, ba_ref, cw_ref, gp_ref, onw_ref, o_ref, ext_ref, state_ref, *, heads, dim):
    gt = DN_GROUP
    c = DN_CHUNK
    hd = heads * dim
    hs = range(heads)
    chunks = range(gt // c)

    @pl.when(pl.program_id(1) == 0)
    def _():
        ext_ref[0:SUBLANES, :] = jnp.zeros((SUBLANES, 3 * hd), F32)
        state_ref[...] = jnp.zeros_like(state_ref)

    ext_ref[SUBLANES:SUBLANES + gt, :] = qkv_ref[...]
    xe = ext_ref[...]
    conv = cw_ref[3:4, :] * xe[SUBLANES:]
    for s in (1, 2, 3):
        conv = conv + cw_ref[3 - s:4 - s, :] * pltpu.roll(xe, s, 0)[SUBLANES:]
    ext_ref[0:SUBLANES, :] = xe[gt:gt + SUBLANES]
    act = _silu(conv)

    ba = ba_ref[...]
    beta_all = _sigmoid(ba)
    xa = ba + gp_ref[1:2, :]
    softplus = jnp.maximum(xa, 0.0) + jnp.log1p(jnp.exp(-jnp.abs(xa)))
    g_all = -jnp.exp(gp_ref[0:1, :]) * softplus
    row_in_chunk = lax.broadcasted_iota(jnp.int32, (gt, LANES), 0) & (c - 1)
    gcum = g_all
    s = 1
    while s < c:
        gcum = gcum + jnp.where(row_in_chunk >= s, pltpu.roll(gcum, s, 0), 0.0)
        s *= 2
    exp_g = jnp.exp(gcum)
    kdec_parts, gc_rows = [], []
    for j in chunks:
        g_last = gcum[c * j + c - 1:c * j + c, :]
        kdec_parts.append(jnp.exp(g_last - gcum[c * j:c * (j + 1), :]))
        gc_rows.append(jnp.exp(g_last))
    kdec = jnp.concatenate(kdec_parts, axis=0)
    gcum_t = gcum.T

    ri = lax.broadcasted_iota(jnp.int32, (gt, gt), 0)
    ci = lax.broadcasted_iota(jnp.int32, (gt, gt), 1)
    same_chunk = (ri // c) == (ci // c)
    strict = same_chunk & (ri > ci)
    incl = same_chunk & (ri >= ci)

    qn, kn, kb, beta, eg = [], [], [], [], []
    for h in hs:
        qh = act[:, h * dim:(h + 1) * dim]
        kh = act[:, hd + h * dim:hd + (h + 1) * dim]
        qn.append(qh * lax.rsqrt(jnp.sum(qh * qh, axis=-1, keepdims=True) + EPS) * (dim ** -0.5))
        kn.append(kh * lax.rsqrt(jnp.sum(kh * kh, axis=-1, keepdims=True) + EPS))
        kb.append(kn[h].astype(BF16))
        beta.append(beta_all[:, h:h + 1])
        eg.append(exp_g[:, heads + h:heads + h + 1])

    decay = [jnp.exp(jnp.where(incl, gcum[:, heads + h:heads + h + 1] - gcum_t[heads + h:heads + h + 1, :],
                               -jnp.inf)) for h in hs]
    kk = [_dot_nt(kb[h], kb[h]) for h in hs]
    pw = [jnp.where(strict, (beta[h] * kk[h]) * decay[h], 0.0).astype(BF16) for h in hs]

    sol = [jnp.concatenate([act[:, 2 * hd + h * dim:2 * hd + (h + 1) * dim] * beta[h],
                            kn[h] * (beta[h] * eg[h])], axis=1) for h in hs]
    sol = [sol[h] - _dot(pw[h], sol[h].astype(BF16)) for h in hs]
    order = 2
    while order < c:
        pw = [_dot(pw[h], pw[h]).astype(BF16) for h in hs]
        sol = [sol[h] + _dot(pw[h], sol[h].astype(BF16)) for h in hs]
        order *= 2

    pmat = [(_dot_nt(qn[h].astype(BF16), kb[h]) * decay[h]).astype(BF16) for h in hs]
    wq = [[jnp.concatenate([sol[h][c * j:c * (j + 1), dim:], (qn[h] * eg[h])[c * j:c * (j + 1)]],
                           axis=0).astype(BF16) for j in chunks] for h in hs]
    kd = [(kn[h] * kdec[:, heads + h:heads + h + 1]).astype(BF16) for h in hs]

    state = [state_ref[h] for h in hs]
    o_parts = [[] for _ in hs]
    for j in chunks:
        r = slice(c * j, c * (j + 1))
        ws = [_dot(wq[h][j], state[h].astype(BF16)) for h in hs]
        db = [(sol[h][r, :dim] - ws[h][:c]).astype(BF16) for h in hs]
        for h in hs:
            o_parts[h].append(ws[h][c:] + _dot(pmat[h][r, r], db[h]))
        state = [gc_rows[j][:, heads + h:heads + h + 1] * state[h]
                 + lax.dot_general(kd[h][r], db[h], _TN, preferred_element_type=F32) for h in hs]
    for h in hs:
        state_ref[h] = state[h]

    for h in hs:
        o = jnp.concatenate(o_parts[h], axis=0)
        zh = z_ref[:, h * dim:(h + 1) * dim]
        o_ref[:, h * dim:(h + 1) * dim] = (_rms(o, onw_ref[...]) * _silu(zh)).astype(o_ref.dtype)


def _deltanet(proj3, ba3, conv_w, gate_params, out_norm_w, heads, dim, col_z):
    b, s, _ = proj3.shape
    hd = heads * dim
    gt = DN_GROUP
    pipelined = (_nbytes((gt, 3 * hd), F32) + _nbytes((gt, hd), F32) + _nbytes((gt, LANES), F32)
                 + _nbytes((gt, hd), BF16))
    resident = (_nbytes((gt + SUBLANES, 3 * hd), F32) * 4 + _nbytes((heads, dim, dim), F32)
                + heads * 6 * _nbytes((gt, gt), F32))
    return pl.pallas_call(
        functools.partial(_dn_kernel, heads=heads, dim=dim),
        grid=(b, s // gt),
        in_specs=[
            pl.BlockSpec((None, gt, 3 * hd), lambda i, g: (i, g, 0)),
            pl.BlockSpec((None, gt, hd), lambda i, g: (i, g, col_z // hd)),
            pl.BlockSpec((None, gt, LANES), lambda i, g: (i, g, 0)),
            pl.BlockSpec((4, 3 * hd), lambda i, g: (0, 0)),
            pl.BlockSpec((SUBLANES, LANES), lambda i, g: (0, 0)),
            pl.BlockSpec((1, dim), lambda i, g: (0, 0)),
        ],
        out_specs=pl.BlockSpec((None, gt, hd), lambda i, g: (i, g, 0)),
        out_shape=jax.ShapeDtypeStruct((b, s, hd), BF16),
        scratch_shapes=[pltpu.VMEM((gt + SUBLANES, 3 * hd), F32), pltpu.VMEM((heads, dim, dim), F32)],
        compiler_params=pltpu.CompilerParams(
            dimension_semantics=("parallel", "arbitrary"),
            vmem_limit_bytes=_vmem_limit(pipelined, resident)),
        name="deltanet",
    )(proj3, proj3, ba3, conv_w, gate_params, out_norm_w)


def _swa_kernel(rb_ref, sink_ref, q_ref, kc_ref, kp_ref, vc_ref, vp_ref, qw_ref, kw_ref, o_ref, bias_ref,
                *, q_heads, kv_heads, dim):
    blk = SWA_BLOCK
    n = pl.program_id(1)
    qi = lax.broadcasted_iota(jnp.int32, (blk, 2 * blk), 0)
    kj = lax.broadcasted_iota(jnp.int32, (blk, 2 * blk), 1)
    dist = qi - kj + blk

    @pl.when((pl.program_id(0) == 0) & (n == 0))
    def _():
        max_exact = N_BUCKETS // 2
        nn = jnp.maximum(dist, 0)
        nf = jnp.maximum(nn, 1).astype(F32)
        large = max_exact + (jnp.log(nf / max_exact) / math.log(MAX_DISTANCE / max_exact)
                             * (N_BUCKETS - max_exact)).astype(jnp.int32)
        bucket = jnp.where(nn < max_exact, nn, jnp.minimum(large, N_BUCKETS - 1))
        for h in range(q_heads):
            acc = jnp.zeros((blk, 2 * blk), F32)
            for bk in range(N_BUCKETS):
                acc = jnp.where(bucket == bk, rb_ref[bk * q_heads + h], acc)
            acc = jnp.where((dist >= 0) & (dist < SWA_WINDOW), acc, -jnp.inf)
            bias_ref[h] = acc
            bias_ref[q_heads + h] = jnp.where(kj >= blk, acc, -jnp.inf)

    first = jnp.where(n == 0, q_heads, 0)
    group = q_heads // kv_heads
    for j in range(kv_heads):
        cols = slice(j * dim, (j + 1) * dim)
        kwin = jnp.concatenate([kp_ref[:, cols], kc_ref[:, cols]], axis=0)
        kwin = _rms(kwin, kw_ref[...]).astype(BF16)
        vwin = jnp.concatenate([vp_ref[:, cols], vc_ref[:, cols]], axis=0).astype(BF16)
        qcat = jnp.concatenate(
            [_rms(q_ref[:, (j * group + i) * dim:(j * group + i + 1) * dim], qw_ref[...]) for i in range(group)],
            axis=0).astype(BF16)
        logits = _dot_nt(qcat, kwin) * (dim ** -0.5)
        for i in range(group):
            h = j * group + i
            lg = logits[i * blk:(i + 1) * blk] + bias_ref[first + h]
            sink = sink_ref[h]
            mx = jnp.maximum(jnp.max(lg, axis=-1, keepdims=True), sink)
            e = jnp.exp(lg - mx)
            den = jnp.sum(e, axis=-1, keepdims=True) + jnp.exp(sink - mx)
            pv = _dot(e.astype(BF16), vwin)
            o_ref[:, h * dim:(h + 1) * dim] = (pv / den).astype(o_ref.dtype)


def _swa(proj3, rel_bias_flat, sinks, q_norm_w, k_norm_w, q_heads, kv_heads, dim, col_q, col_k, col_v):
    b, s, _ = proj3.shape
    blk = SWA_BLOCK
    qw, kvw = q_heads * dim, kv_heads * dim
    smem = pl.BlockSpec(memory_space=pltpu.SMEM)
    pipelined = _nbytes((blk, qw), F32) + 4 * _nbytes((blk, kvw), F32) + _nbytes((blk, qw), BF16)
    resident = _nbytes((2 * q_heads, blk, 2 * blk), F32) + (8 << 20)
    return pl.pallas_call(
        functools.partial(_swa_kernel, q_heads=q_heads, kv_heads=kv_heads, dim=dim),
        grid=(b, s // blk),
        in_specs=[
            smem, smem,
            pl.BlockSpec((None, blk, qw), lambda i, n: (i, n, col_q // qw)),
            pl.BlockSpec((None, blk, kvw), lambda i, n: (i, n, col_k // kvw)),
            pl.BlockSpec((None, blk, kvw), lambda i, n: (i, jnp.maximum(n - 1, 0), col_k // kvw)),
            pl.BlockSpec((None, blk, kvw), lambda i, n: (i, n, col_v // kvw)),
            pl.BlockSpec((None, blk, kvw), lambda i, n: (i, jnp.maximum(n - 1, 0), col_v // kvw)),
            pl.BlockSpec((1, dim), lambda i, n: (0, 0)),
            pl.BlockSpec((1, dim), lambda i, n: (0, 0)),
        ],
        out_specs=pl.BlockSpec((None, blk, qw), lambda i, n: (i, n, 0)),
        out_shape=jax.ShapeDtypeStruct((b, s, qw), BF16),
        scratch_shapes=[pltpu.VMEM((2 * q_heads, blk, 2 * blk), F32)],
        compiler_params=pltpu.CompilerParams(
            dimension_semantics=("arbitrary", "arbitrary"),
            vmem_limit_bytes=_vmem_limit(pipelined, resident)),
        name="swa",
    )(rel_bias_flat, sinks, proj3, proj3, proj3, proj3, proj3, q_norm_w, k_norm_w)


def _memattn_kernel(q_ref, k_ref, v_ref, qw_ref, kw_ref, o_ref, *, heads, dim):
    for h in range(heads):
        cols = slice(h * dim, (h + 1) * dim)
        qn = _rms(q_ref[:, cols], qw_ref[...]).astype(BF16)
        kn = _rms(k_ref[:, cols], kw_ref[...]).astype(BF16)
        lg = _dot_nt(qn, kn) * (dim ** -0.5)
        e = jnp.exp(lg - jnp.max(lg, axis=-1, keepdims=True))
        den = jnp.sum(e, axis=-1, keepdims=True)
        pv = _dot(e.astype(BF16), v_ref[:, cols].astype(BF16))
        o_ref[:, cols] = (pv / den).astype(o_ref.dtype)


def _memattn(proj3, mkv3, q_norm_w, k_norm_w, heads, dim, col_q, tq):
    b, s, _ = proj3.shape
    m = mkv3.shape[1]
    w = heads * dim
    pipelined = _nbytes((tq, w), F32) + 2 * _nbytes((m, w), F32) + _nbytes((tq, w), BF16)
    return pl.pallas_call(
        functools.partial(_memattn_kernel, heads=heads, dim=dim),
        grid=(b, s // tq),
        in_specs=[
            pl.BlockSpec((pl.Element(tq), pl.Element(w)), lambda i, t: (pl.multiple_of(i * s + t * tq, tq), col_q)),
            pl.BlockSpec((None, m, w), lambda i, t: (i, 0, 0)),
            pl.BlockSpec((None, m, w), lambda i, t: (i, 0, 1)),
            pl.BlockSpec((1, dim), lambda i, t: (0, 0)),
            pl.BlockSpec((1, dim), lambda i, t: (0, 0)),
        ],
        out_specs=pl.BlockSpec((None, tq, w), lambda i, t: (i, t, 0)),
        out_shape=jax.ShapeDtypeStruct((b, s, w), BF16),
        compiler_params=pltpu.CompilerParams(
            dimension_semantics=("parallel", "parallel"),
            vmem_limit_bytes=_vmem_limit(pipelined, 8 << 20)),
        name="memattn",
    )(proj3.reshape(b * s, -1), mkv3, mkv3, q_norm_w, k_norm_w)


def _merge_kernel(od_ref, os_ref, om_ref, pd_ref, ps_ref, pm_ref, gd_ref, gs_ref, gm_ref, o_ref):
    merged = (_sigmoid(gd_ref[...]) * _dot(od_ref[...], pd_ref[...])
              + _sigmoid(gs_ref[...]) * _dot(os_ref[...], ps_ref[...])
              + _sigmoid(gm_ref[...]) * _dot(om_ref[...], pm_ref[...]))
    o_ref[...] = merged.astype(o_ref.dtype)


def _merge(o_dn, o_swa, o_mem, p_dn, p_swa, p_mem, proj, col_gates, tm, tn):
    t = o_dn.shape[0]
    d = p_dn.shape[1]
    wd, ws, wm = o_dn.shape[1], o_swa.shape[1], o_mem.shape[1]
    assert col_gates % LANES == 0

    def gate_index(branch, j, i):
        return (pl.multiple_of(i * tm, tm), pl.multiple_of(col_gates + branch * d + j * tn, LANES))

    pipelined = ((_nbytes((tm, wd), BF16) + _nbytes((tm, ws), BF16) + _nbytes((tm, wm), BF16))
                 + (_nbytes((wd, tn), BF16) + _nbytes((ws, tn), BF16) + _nbytes((wm, tn), BF16))
                 + 3 * _nbytes((tm, tn), F32) + _nbytes((tm, tn), BF16))
    return pl.pallas_call(
        _merge_kernel,
        grid=(d // tn, t // tm),
        in_specs=[
            pl.BlockSpec((tm, wd), lambda j, i: (i, 0)),
            pl.BlockSpec((tm, ws), lambda j, i: (i, 0)),
            pl.BlockSpec((tm, wm), lambda j, i: (i, 0)),
            pl.BlockSpec((wd, tn), lambda j, i: (0, j)),
            pl.BlockSpec((ws, tn), lambda j, i: (0, j)),
            pl.BlockSpec((wm, tn), lambda j, i: (0, j)),
            pl.BlockSpec((pl.Element(tm), pl.Element(tn)), functools.partial(gate_index, 0)),
            pl.BlockSpec((pl.Element(tm), pl.Element(tn)), functools.partial(gate_index, 1)),
            pl.BlockSpec((pl.Element(tm), pl.Element(tn)), functools.partial(gate_index, 2)),
        ],
        out_specs=pl.BlockSpec((tm, tn), lambda j, i: (i, j)),
        out_shape=jax.ShapeDtypeStruct((t, d), BF16),
        compiler_params=pltpu.CompilerParams(
            dimension_semantics=("parallel", "parallel"),
            vmem_limit_bytes=_vmem_limit(pipelined, 3 * _nbytes((tm, tn), F32))),
        name="merge",
    )(o_dn, o_swa, o_mem, p_dn, p_swa, p_mem, proj, proj, proj)


def _outproj_kernel(x_ref, m_ref, w_ref, nw_ref, x1_ref, h_ref):
    x1 = x_ref[...] + _dot(m_ref[...], w_ref[...])
    x1_ref[...] = x1
    h_ref[...] = _rms(x1, nw_ref[...]).astype(h_ref.dtype)


def _outproj(x, merged, w_out, norm_w, tm):
    t, d = x.shape
    pipelined = 2 * _nbytes((tm, d), F32) + 2 * _nbytes((tm, d), BF16) + _nbytes((d, d), BF16)
    return pl.pallas_call(
        _outproj_kernel,
        grid=(t // tm,),
        in_specs=[
            pl.BlockSpec((tm, d), lambda i: (i, 0)),
            pl.BlockSpec((tm, d), lambda i: (i, 0)),
            pl.BlockSpec((d, d), lambda i: (0, 0)),
            pl.BlockSpec((1, d), lambda i: (0, 0)),
        ],
        out_specs=[pl.BlockSpec((tm, d), lambda i: (i, 0)), pl.BlockSpec((tm, d), lambda i: (i, 0))],
        out_shape=[jax.ShapeDtypeStruct((t, d), F32), jax.ShapeDtypeStruct((t, d), BF16)],
        compiler_params=pltpu.CompilerParams(
            dimension_semantics=("parallel",),
            vmem_limit_bytes=_vmem_limit(pipelined, 2 * _nbytes((tm, d), F32))),
        name="outproj",
    )(x, merged, w_out, norm_w)


def _mlp_kernel(h_ref, x1_ref, wu_ref, wd_ref, o_ref):
    @pl.when(pl.program_id(1) == 0)
    def _():
        o_ref[...] = x1_ref[...]

    a = jnp.maximum(_dot(h_ref[...], wu_ref[...]), 0.0)
    o_ref[...] += _dot((a * a).astype(BF16), wd_ref[...])


def _mlp(h2, x1, w_up, w_down, tm, tf):
    t, d = x1.shape
    f = w_up.shape[1]
    pipelined = (_nbytes((tm, d), BF16) + 2 * _nbytes((tm, d), F32)
                 + _nbytes((d, tf), BF16) + _nbytes((tf, d), BF16))
    resident = 2 * _nbytes((tm, tf), F32)
    return pl.pallas_call(
        _mlp_kernel,
        grid=(t // tm, f // tf),
        in_specs=[
            pl.BlockSpec((tm, d), lambda i, k: (i, 0)),
            pl.BlockSpec((tm, d), lambda i, k: (i, 0)),
            pl.BlockSpec((d, tf), lambda i, k: (0, k)),
            pl.BlockSpec((tf, d), lambda i, k: (k, 0)),
        ],
        out_specs=pl.BlockSpec((tm, d), lambda i, k: (i, 0)),
        out_shape=jax.ShapeDtypeStruct((t, d), F32),
        compiler_params=pltpu.CompilerParams(
            dimension_semantics=("parallel", "arbitrary"),
            vmem_limit_bytes=_vmem_limit(pipelined, resident)),
        name="mlp",
    )(h2, x1, w_up, w_down)


def _layer(x, mem, attn_norm_w, w_in, dn_conv_w, dn_a_log, dn_dt_bias, dn_out_norm_w, swa_q_norm_w,
           swa_k_norm_w, swa_sinks, rel_bias, mem_norm_w, w_mem_kv, xq_norm_w, xk_norm_w,
           p_dn, p_swa, p_mem, w_out, mlp_norm_w, w_mlp_up, w_mlp_down):
    b, s, d = x.shape
    m = mem.shape[1]
    t = b * s

    dn_heads = dn_a_log.shape[0]
    dn_dim = dn_out_norm_w.shape[0]
    dn_w = dn_heads * dn_dim
    swa_heads = swa_sinks.shape[0]
    swa_dim = swa_q_norm_w.shape[0]
    swa_w = swa_heads * swa_dim
    mem_dim = xq_norm_w.shape[0]
    mem_w = p_mem.shape[0]
    mem_heads = mem_w // mem_dim
    swa_kv_w = (w_in.shape[1] - 4 * dn_w - 2 * dn_heads - swa_w - mem_w - 3 * d) // 2
    swa_kv = swa_kv_w // swa_dim

    n_ba = 2 * dn_heads
    src_ba = 4 * dn_w
    src_swa = src_ba + n_ba
    w_a = w_in[:, :src_ba].T.astype(BF16)
    w_b = w_in[:, src_swa:].T.astype(BF16)
    w_c = jnp.concatenate([w_in[:, src_ba:src_swa].T.astype(BF16), jnp.zeros((LANES - n_ba, d), BF16)], axis=0)
    col = {"qkv": 0, "z": 3 * dn_w, "sq": src_ba}
    col["sk"] = col["sq"] + swa_w
    col["sv"] = col["sk"] + swa_kv_w
    col["mq"] = col["sv"] + swa_kv_w
    col["gates"] = col["mq"] + mem_w

    proj, ba = _inproj(x.reshape(t, d), attn_norm_w.reshape(1, d), w_a, w_b, w_c, tm=1024, tn=1024)
    proj3 = proj.reshape(b, s, -1)
    mkv = _norm_matmul(mem.reshape(b * m, d), mem_norm_w.reshape(1, d), w_mem_kv.astype(BF16), tm=512, tn=512)
    mkv3 = mkv.reshape(b, m, -1)

    gate_params = jnp.zeros((SUBLANES, LANES), F32)
    gate_params = gate_params.at[0, dn_heads:2 * dn_heads].set(dn_a_log)
    gate_params = gate_params.at[1, dn_heads:2 * dn_heads].set(dn_dt_bias)
    o_dn = _deltanet(proj3, ba.reshape(b, s, LANES), dn_conv_w, gate_params, dn_out_norm_w.reshape(1, dn_dim),
                     dn_heads, dn_dim, col["z"])
    o_swa = _swa(proj3, rel_bias.reshape(-1), swa_sinks, swa_q_norm_w.reshape(1, swa_dim),
                 swa_k_norm_w.reshape(1, swa_dim), swa_heads, swa_kv, swa_dim, col["sq"], col["sk"], col["sv"])
    o_mem = _memattn(proj3, mkv3, xq_norm_w.reshape(1, mem_dim), xk_norm_w.reshape(1, mem_dim),
                     mem_heads, mem_dim, col["mq"], tq=512)

    merged = _merge(o_dn.reshape(t, dn_w), o_swa.reshape(t, swa_w), o_mem.reshape(t, mem_w),
                    p_dn.astype(BF16), p_swa.astype(BF16), p_mem.astype(BF16), proj, col["gates"],
                    tm=512, tn=1024)
    x1, h2 = _outproj(x.reshape(t, d), merged, w_out.astype(BF16), mlp_norm_w.reshape(1, d), tm=256)
    out = _mlp(h2, x1, w_mlp_up.astype(BF16), w_mlp_down.astype(BF16), tm=512, tf=1024)
    return out.reshape(b, s, d)


def kernel(x, mem, attn_norm_w, w_in, dn_conv_w, dn_A_log, dn_dt_bias, dn_out_norm_w, swa_q_norm_w,
           swa_k_norm_w, swa_sinks, rel_bias, mem_norm_w, w_mem_kv, xq_norm_w, xk_norm_w, p_dn, p_swa,
           p_mem, w_out, mlp_norm_w, w_mlp_up, w_mlp_down):
    depth = w_in.shape[0]
    for l in range(depth):
        x = _layer(x, mem, attn_norm_w[l], w_in[l], dn_conv_w[l], dn_A_log[l], dn_dt_bias[l],
                   dn_out_norm_w[l], swa_q_norm_w[l], swa_k_norm_w[l], swa_sinks[l], rel_bias,
                   mem_norm_w[l], w_mem_kv[l], xq_norm_w[l], xk_norm_w[l], p_dn[l], p_swa[l], p_mem[l],
                   w_out[l], mlp_norm_w[l], w_mlp_up[l], w_mlp_down[l])
    return x
```

```python
import functools
import math

import jax
import jax.numpy as jnp
from jax import lax
from jax.experimental import pallas as pl
from jax.experimental.pallas import tpu as pltpu

F32 = jnp.float32
BF16 = jnp.bfloat16

EPS = 1e-6
LANES = 128
SUBLANES = 8
V7X_VMEM_BYTES = 64 * 1024 * 1024

DN_CHUNK = 64
DN_GROUP = 256
SWA_WINDOW = 128
SWA_BLOCK = 128
N_BUCKETS = 32
MAX_DISTANCE = 128

_NT = (((1,), (1,)), ((), ()))
_TN = (((0,), (0,)), ((), ()))


def _vmem_limit(pipelined_bytes, resident_bytes):
    want = 2 * pipelined_bytes + resident_bytes + (4 << 20)
    return int(min(want, V7X_VMEM_BYTES - (8 << 20)))


def _nbytes(shape, dtype):
    return math.prod(shape) * jnp.dtype(dtype).itemsize


def _sigmoid(v):
    return 0.5 * jnp.tanh(0.5 * v) + 0.5


def _silu(v):
    return v * _sigmoid(v)


def _rms(v, w):
    return (v * lax.rsqrt(jnp.mean(v * v, axis=-1, keepdims=True) + EPS)) * w


def _dot(a, b):
    return jnp.dot(a, b, preferred_element_type=F32)


def _dot_nt(a, b):
    return lax.dot_general(a, b, _NT, preferred_element_type=F32)


def _inproj_kernel(x_ref, nw_ref, wa_ref, wb_ref, wc_ref, o_ref, g_ref, ba_ref, h_ref, *, n_a, n_p):
    j = pl.program_id(1)

    @pl.when(j == 0)
    def _():
        h_ref[...] = _rms(x_ref[...], nw_ref[...]).astype(BF16)
        ba_ref[...] = _dot_nt(h_ref[...], wc_ref[...])

    @pl.when(j < n_a)
    def _():
        o_ref[...] = _dot_nt(h_ref[...], wa_ref[...])

    @pl.when((j >= n_a) & (j < n_p))
    def _():
        o_ref[...] = _dot_nt(h_ref[...], wb_ref[...])

    @pl.when(j >= n_p)
    def _():
        g_ref[...] = _sigmoid(_dot_nt(h_ref[...], wb_ref[...])).astype(g_ref.dtype)


def _inproj(x, nw, w_a, rows_a, w_b, w_c, n_gates, tm, tn):
    m, k = x.shape
    rows_b1 = w_b.shape[0] - n_gates
    n = rows_a + rows_b1
    n_a = rows_a // tn
    n_p = pl.cdiv(n, tn)
    assert rows_a % tn == 0 and n_gates % tn == 0 and rows_b1 % (2 * SUBLANES) == 0

    def wb_index(i, j):
        row = jnp.where(j < n_p, jnp.maximum(j - n_a, 0) * tn, rows_b1 + (j - n_p) * tn)
        return (pl.multiple_of(row, 2 * SUBLANES), 0)

    pipelined = (_nbytes((tm, k), F32) + 2 * _nbytes((tn, k), BF16) + _nbytes((tm, tn), F32)
                 + _nbytes((tm, tn), BF16) + _nbytes((tm, LANES), F32))
    resident = _nbytes((tm, k), BF16) + _nbytes((tm, k), F32) + _nbytes((LANES, k), BF16)
    return pl.pallas_call(
        functools.partial(_inproj_kernel, n_a=n_a, n_p=n_p),
        grid=(m // tm, n_p + n_gates // tn),
        in_specs=[
            pl.BlockSpec((tm, k), lambda i, j: (i, 0)),
            pl.BlockSpec((1, k), lambda i, j: (0, 0)),
            pl.BlockSpec((tn, k), lambda i, j: (jnp.minimum(j, n_a - 1), 0)),
            pl.BlockSpec((pl.Element(tn), pl.Element(k)), wb_index),
            pl.BlockSpec((LANES, k), lambda i, j: (0, 0)),
        ],
        out_specs=[pl.BlockSpec((tm, tn), lambda i, j: (i, jnp.minimum(j, n_p - 1))),
                   pl.BlockSpec((tm, tn), lambda i, j: (i, jnp.maximum(j - n_p, 0))),
                   pl.BlockSpec((tm, LANES), lambda i, j: (i, 0))],
        out_shape=[jax.ShapeDtypeStruct((m, n), F32), jax.ShapeDtypeStruct((m, n_gates), BF16),
                   jax.ShapeDtypeStruct((m, LANES), F32)],
        scratch_shapes=[pltpu.VMEM((tm, k), BF16)],
        compiler_params=pltpu.CompilerParams(
            dimension_semantics=("parallel", "arbitrary"),
            vmem_limit_bytes=_vmem_limit(pipelined, resident)),
        name="inproj",
    )(x, nw, w_a, w_b, w_c)


def _norm_matmul_kernel(x_ref, nw_ref, w_ref, o_ref, h_ref):
    @pl.when(pl.program_id(1) == 0)
    def _():
        h_ref[...] = _rms(x_ref[...], nw_ref[...]).astype(BF16)

    o_ref[...] = _dot(h_ref[...], w_ref[...]).astype(o_ref.dtype)


def _norm_matmul(x, nw, w, tm, tn):
    m, k = x.shape
    n = w.shape[1]
    pipelined = _nbytes((tm, k), F32) + _nbytes((k, tn), BF16) + _nbytes((tm, tn), F32)
    resident = _nbytes((tm, k), BF16) + _nbytes((tm, k), F32)
    return pl.pallas_call(
        _norm_matmul_kernel,
        grid=(m // tm, n // tn),
        in_specs=[
            pl.BlockSpec((tm, k), lambda i, j: (i, 0)),
            pl.BlockSpec((1, k), lambda i, j: (0, 0)),
            pl.BlockSpec((k, tn), lambda i, j: (0, j)),
        ],
        out_specs=pl.BlockSpec((tm, tn), lambda i, j: (i, j)),
        out_shape=jax.ShapeDtypeStruct((m, n), F32),
        scratch_shapes=[pltpu.VMEM((tm, k), BF16)],
        compiler_params=pltpu.CompilerParams(
            dimension_semantics=("parallel", "arbitrary"),
            vmem_limit_bytes=_vmem_limit(pipelined, resident)),
        name="norm_matmul",
    )(x, nw, w)


def _dn_kernel(qkv_ref, z_ref, ba_ref, cw_ref, gp_ref, onw_ref, o_ref, ext_ref, state_ref, *, heads, dim):
    gt = DN_GROUP
    c = DN_CHUNK
    hd = heads * dim
    hs = range(heads)
    chunks = range(gt // c)

    @pl.when(pl.program_id(1) == 0)
    def _():
        ext_ref[0:SUBLANES, :] = jnp.zeros((SUBLANES, 3 * hd), F32)
        state_ref[...] = jnp.zeros_like(state_ref)

    ext_ref[SUBLANES:SUBLANES + gt, :] = qkv_ref[...]
    xe = ext_ref[...]
    conv = cw_ref[3:4, :] * xe[SUBLANES:]
    for s in (1, 2, 3):
        conv = conv + cw_ref[3 - s:4 - s, :] * pltpu.roll(xe, s, 0)[SUBLANES:]
    ext_ref[0:SUBLANES, :] = xe[gt:gt + SUBLANES]
    act = _silu(conv)

    ba = ba_ref[...]
    beta_all = _sigmoid(ba)
    xa = ba + gp_ref[1:2, :]
    softplus = jnp.maximum(xa, 0.0) + jnp.log1p(jnp.exp(-jnp.abs(xa)))
    g_all = -jnp.exp(gp_ref[0:1, :]) * softplus
    row_in_chunk = lax.broadcasted_iota(jnp.int32, (gt, LANES), 0) & (c - 1)
    gcum = g_all
    s = 1
    while s < c:
        gcum = gcum + jnp.where(row_in_chunk >= s, pltpu.roll(gcum, s, 0), 0.0)
        s *= 2
    exp_g = jnp.exp(gcum)
    kdec_parts, gc_rows = [], []
    for j in chunks:
        g_last = gcum[c * j + c - 1:c * j + c, :]
        kdec_parts.append(jnp.exp(g_last - gcum[c * j:c * (j + 1), :]))
        gc_rows.append(jnp.exp(g_last))
    kdec = jnp.concatenate(kdec_parts, axis=0)
    gcum_t = gcum.T

    ri = lax.broadcasted_iota(jnp.int32, (gt, gt), 0)
    ci = lax.broadcasted_iota(jnp.int32, (gt, gt), 1)
    same_chunk = (ri // c) == (ci // c)
    strict = same_chunk & (ri > ci)
    incl = same_chunk & (ri >= ci)

    qn, kn, kb, beta, eg = [], [], [], [], []
    for h in hs:
        qh = act[:, h * dim:(h + 1) * dim]
        kh = act[:, hd + h * dim:hd + (h + 1) * dim]
        qn.append(qh * lax.rsqrt(jnp.sum(qh * qh, axis=-1, keepdims=True) + EPS) * (dim ** -0.5))
        kn.append(kh * lax.rsqrt(jnp.sum(kh * kh, axis=-1, keepdims=True) + EPS))
        kb.append(kn[h].astype(BF16))
        beta.append(beta_all[:, h:h + 1])
        eg.append(exp_g[:, heads + h:heads + h + 1])

    decay = [jnp.exp(jnp.where(incl, gcum[:, heads + h:heads + h + 1] - gcum_t[heads + h:heads + h + 1, :],
                               -jnp.inf)) for h in hs]
    kk = [_dot_nt(kb[h], kb[h]) for h in hs]
    pw = [jnp.where(strict, (beta[h] * kk[h]) * decay[h], 0.0).astype(BF16) for h in hs]

    sol = [jnp.concatenate([act[:, 2 * hd + h * dim:2 * hd + (h + 1) * dim] * beta[h],
                            kn[h] * (beta[h] * eg[h])], axis=1) for h in hs]
    sol = [sol[h] - _dot(pw[h], sol[h].astype(BF16)) for h in hs]
    order = 2
    while order < c:
        pw = [_dot(pw[h], pw[h]).astype(BF16) for h in hs]
        sol = [sol[h] + _dot(pw[h], sol[h].astype(BF16)) for h in hs]
        order *= 2

    pmat = [(_dot_nt(qn[h].astype(BF16), kb[h]) * decay[h]).astype(BF16) for h in hs]
    wq = [[jnp.concatenate([sol[h][c * j:c * (j + 1), dim:], (qn[h] * eg[h])[c * j:c * (j + 1)]],
                           axis=0).astype(BF16) for j in chunks] for h in hs]
    kd = [(kn[h] * kdec[:, heads + h:heads + h + 1]).astype(BF16) for h in hs]

    state = [state_ref[h] for h in hs]
    o_parts = [[] for _ in hs]
    for j in chunks:
        r = slice(c * j, c * (j + 1))
        ws = [_dot(wq[h][j], state[h].astype(BF16)) for h in hs]
        db = [(sol[h][r, :dim] - ws[h][:c]).astype(BF16) for h in hs]
        for h in hs:
            o_parts[h].append(ws[h][c:] + _dot(pmat[h][r, r], db[h]))
        state = [gc_rows[j][:, heads + h:heads + h + 1] * state[h]
                 + lax.dot_general(kd[h][r], db[h], _TN, preferred_element_type=F32) for h in hs]
    for h in hs:
        state_ref[h] = state[h]

    for h in hs:
        o = jnp.concatenate(o_parts[h], axis=0)
        zh = z_ref[:, h * dim:(h + 1) * dim]
        o_ref[:, h * dim:(h + 1) * dim] = (_rms(o, onw_ref[...]) * _silu(zh)).astype(o_ref.dtype)


def _deltanet(proj3, ba3, conv_w, gate_params, out_norm_w, heads, dim, col_z):
    b, s, _ = proj3.shape
    hd = heads * dim
    gt = DN_GROUP
    pipelined = (_nbytes((gt, 3 * hd), F32) + _nbytes((gt, hd), F32) + _nbytes((gt, LANES), F32)
                 + _nbytes((gt, hd), BF16))
    resident = (_nbytes((gt + SUBLANES, 3 * hd), F32) * 4 + _nbytes((heads, dim, dim), F32)
                + heads * 6 * _nbytes((gt, gt), F32))
    return pl.pallas_call(
        functools.partial(_dn_kernel, heads=heads, dim=dim),
        grid=(b, s // gt),
        in_specs=[
            pl.BlockSpec((None, gt, 3 * hd), lambda i, g: (i, g, 0)),
            pl.BlockSpec((None, gt, hd), lambda i, g: (i, g, col_z // hd)),
            pl.BlockSpec((None, gt, LANES), lambda i, g: (i, g, 0)),
            pl.BlockSpec((4, 3 * hd), lambda i, g: (0, 0)),
            pl.BlockSpec((SUBLANES, LANES), lambda i, g: (0, 0)),
            pl.BlockSpec((1, dim), lambda i, g: (0, 0)),
        ],
        out_specs=pl.BlockSpec((None, gt, hd), lambda i, g: (i, g, 0)),
        out_shape=jax.ShapeDtypeStruct((b, s, hd), BF16),
        scratch_shapes=[pltpu.VMEM((gt + SUBLANES, 3 * hd), F32), pltpu.VMEM((heads, dim, dim), F32)],
        compiler_params=pltpu.CompilerParams(
            dimension_semantics=("parallel", "arbitrary"),
            vmem_limit_bytes=_vmem_limit(pipelined, resident)),
        name="deltanet",
    )(proj3, proj3, ba3, conv_w, gate_params, out_norm_w)


def _swa_kernel(rb_ref, sink_ref, q_ref, kc_ref, kp_ref, vc_ref, vp_ref, qw_ref, kw_ref, o_ref, bias_ref,
                *, q_heads, kv_heads, dim):
    blk = SWA_BLOCK
    n = pl.program_id(1)
    qi = lax.broadcasted_iota(jnp.int32, (blk, 2 * blk), 0)
    kj = lax.broadcasted_iota(jnp.int32, (blk, 2 * blk), 1)
    dist = qi - kj + blk

    @pl.when((pl.program_id(0) == 0) & (n == 0))
    def _():
        max_exact = N_BUCKETS // 2
        nn = jnp.maximum(dist, 0)
        nf = jnp.maximum(nn, 1).astype(F32)
        large = max_exact + (jnp.log(nf / max_exact) / math.log(MAX_DISTANCE / max_exact)
                             * (N_BUCKETS - max_exact)).astype(jnp.int32)
        bucket = jnp.where(nn < max_exact, nn, jnp.minimum(large, N_BUCKETS - 1))
        for h in range(q_heads):
            acc = jnp.zeros((blk, 2 * blk), F32)
            for bk in range(N_BUCKETS):
                acc = jnp.where(bucket == bk, rb_ref[bk * q_heads + h], acc)
            acc = jnp.where((dist >= 0) & (dist < SWA_WINDOW), acc, -jnp.inf)
            bias_ref[h] = acc
            bias_ref[q_heads + h] = jnp.where(kj >= blk, acc, -jnp.inf)

    first = jnp.where(n == 0, q_heads, 0)
    group = q_heads // kv_heads
    for j in range(kv_heads):
        cols = slice(j * dim, (j + 1) * dim)
        kwin = jnp.concatenate([kp_ref[:, cols], kc_ref[:, cols]], axis=0)
        kwin = _rms(kwin, kw_ref[...]).astype(BF16)
        vwin = jnp.concatenate([vp_ref[:, cols], vc_ref[:, cols]], axis=0).astype(BF16)
        qcat = jnp.concatenate(
            [_rms(q_ref[:, (j * group + i) * dim:(j * group + i + 1) * dim], qw_ref[...]) for i in range(group)],
            axis=0).astype(BF16)
        logits = _dot_nt(qcat, kwin) * (dim ** -0.5)
        for i in range(group):
            h = j * group + i
            lg = logits[i * blk:(i + 1) * blk] + bias_ref[first + h]
            sink = sink_ref[h]
            mx = jnp.maximum(jnp.max(lg, axis=-1, keepdims=True), sink)
            e = jnp.exp(lg - mx)
            den = jnp.sum(e, axis=-1, keepdims=True) + jnp.exp(sink - mx)
            pv = _dot(e.astype(BF16), vwin)
            o_ref[:, h * dim:(h + 1) * dim] = (pv / den).astype(o_ref.dtype)


def _swa(proj3, rel_bias_flat, sinks, q_norm_w, k_norm_w, q_heads, kv_heads, dim, col_q, col_k, col_v):
    b, s, _ = proj3.shape
    blk = SWA_BLOCK
    qw, kvw = q_heads * dim, kv_heads * dim
    smem = pl.BlockSpec(memory_space=pltpu.SMEM)
    pipelined = _nbytes((blk, qw), F32) + 4 * _nbytes((blk, kvw), F32) + _nbytes((blk, qw), BF16)
    resident = _nbytes((2 * q_heads, blk, 2 * blk), F32) + (8 << 20)
    return pl.pallas_call(
        functools.partial(_swa_kernel, q_heads=q_heads, kv_heads=kv_heads, dim=dim),
        grid=(b, s // blk),
        in_specs=[
            smem, smem,
            pl.BlockSpec((None, blk, qw), lambda i, n: (i, n, col_q // qw)),
            pl.BlockSpec((None, blk, kvw), lambda i, n: (i, n, col_k // kvw)),
            pl.BlockSpec((None, blk, kvw), lambda i, n: (i, jnp.maximum(n - 1, 0), col_k // kvw)),
            pl.BlockSpec((None, blk, kvw), lambda i, n: (i, n, col_v // kvw)),
            pl.BlockSpec((None, blk, kvw), lambda i, n: (i, jnp.maximum(n - 1, 0), col_v // kvw)),
            pl.BlockSpec((1, dim), lambda i, n: (0, 0)),
            pl.BlockSpec((1, dim), lambda i, n: (0, 0)),
        ],
        out_specs=pl.BlockSpec((None, blk, qw), lambda i, n: (i, n, 0)),
        out_shape=jax.ShapeDtypeStruct((b, s, qw), BF16),
        scratch_shapes=[pltpu.VMEM((2 * q_heads, blk, 2 * blk), F32)],
        compiler_params=pltpu.CompilerParams(
            dimension_semantics=("arbitrary", "arbitrary"),
            vmem_limit_bytes=_vmem_limit(pipelined, resident)),
        name="swa",
    )(rel_bias_flat, sinks, proj3, proj3, proj3, proj3, proj3, q_norm_w, k_norm_w)


def _memattn_kernel(q_ref, k_ref, v_ref, qw_ref, kw_ref, o_ref, *, heads, dim):
    for h in range(heads):
        cols = slice(h * dim, (h + 1) * dim)
        qn = _rms(q_ref[:, cols], qw_ref[...]).astype(BF16)
        kn = _rms(k_ref[:, cols], kw_ref[...]).astype(BF16)
        lg = _dot_nt(qn, kn) * (dim ** -0.5)
        e = jnp.exp(lg - jnp.max(lg, axis=-1, keepdims=True))
        den = jnp.sum(e, axis=-1, keepdims=True)
        pv = _dot(e.astype(BF16), v_ref[:, cols].astype(BF16))
        o_ref[:, cols] = (pv / den).astype(o_ref.dtype)


def _memattn(proj3, mkv3, q_norm_w, k_norm_w, heads, dim, col_q, tq):
    b, s, _ = proj3.shape
    m = mkv3.shape[1]
    w = heads * dim
    pipelined = _nbytes((tq, w), F32) + 2 * _nbytes((m, w), F32) + _nbytes((tq, w), BF16)
    return pl.pallas_call(
        functools.partial(_memattn_kernel, heads=heads, dim=dim),
        grid=(b, s // tq),
        in_specs=[
            pl.BlockSpec((pl.Element(tq), pl.Element(w)), lambda i, t: (pl.multiple_of(i * s + t * tq, tq), col_q)),
            pl.BlockSpec((None, m, w), lambda i, t: (i, 0, 0)),
            pl.BlockSpec((None, m, w), lambda i, t: (i, 0, 1)),
            pl.BlockSpec((1, dim), lambda i, t: (0, 0)),
            pl.BlockSpec((1, dim), lambda i, t: (0, 0)),
        ],
        out_specs=pl.BlockSpec((None, tq, w), lambda i, t: (i, t, 0)),
        out_shape=jax.ShapeDtypeStruct((b, s, w), BF16),
        compiler_params=pltpu.CompilerParams(
            dimension_semantics=("parallel", "parallel"),
            vmem_limit_bytes=_vmem_limit(pipelined, 8 << 20)),
        name="memattn",
    )(proj3.reshape(b * s, -1), mkv3, mkv3, q_norm_w, k_norm_w)


def _merge_kernel(od_ref, os_ref, om_ref, pd_ref, ps_ref, pm_ref, gd_ref, gs_ref, gm_ref, o_ref):
    merged = (gd_ref[...].astype(F32) * _dot(od_ref[...], pd_ref[...])
              + gs_ref[...].astype(F32) * _dot(os_ref[...], ps_ref[...])
              + gm_ref[...].astype(F32) * _dot(om_ref[...], pm_ref[...]))
    o_ref[...] = merged.astype(o_ref.dtype)


def _merge(o_dn, o_swa, o_mem, p_dn, p_swa, p_mem, gates, tm, tn):
    t = o_dn.shape[0]
    d = p_dn.shape[1]
    wd, ws, wm = o_dn.shape[1], o_swa.shape[1], o_mem.shape[1]
    per_branch = d // tn
    pipelined = ((_nbytes((tm, wd), BF16) + _nbytes((tm, ws), BF16) + _nbytes((tm, wm), BF16))
                 + (_nbytes((wd, tn), BF16) + _nbytes((ws, tn), BF16) + _nbytes((wm, tn), BF16))
                 + 4 * _nbytes((tm, tn), BF16))
    return pl.pallas_call(
        _merge_kernel,
        grid=(d // tn, t // tm),
        in_specs=[
            pl.BlockSpec((tm, wd), lambda j, i: (i, 0)),
            pl.BlockSpec((tm, ws), lambda j, i: (i, 0)),
            pl.BlockSpec((tm, wm), lambda j, i: (i, 0)),
            pl.BlockSpec((wd, tn), lambda j, i: (0, j)),
            pl.BlockSpec((ws, tn), lambda j, i: (0, j)),
            pl.BlockSpec((wm, tn), lambda j, i: (0, j)),
            pl.BlockSpec((tm, tn), lambda j, i: (i, j)),
            pl.BlockSpec((tm, tn), lambda j, i: (i, per_branch + j)),
            pl.BlockSpec((tm, tn), lambda j, i: (i, 2 * per_branch + j)),
        ],
        out_specs=pl.BlockSpec((tm, tn), lambda j, i: (i, j)),
        out_shape=jax.ShapeDtypeStruct((t, d), BF16),
        compiler_params=pltpu.CompilerParams(
            dimension_semantics=("parallel", "parallel"),
            vmem_limit_bytes=_vmem_limit(pipelined, 3 * _nbytes((tm, tn), F32))),
        name="merge",
    )(o_dn, o_swa, o_mem, p_dn, p_swa, p_mem, gates, gates, gates)


def _outproj_kernel(x_ref, m_ref, w_ref, nw_ref, x1_ref, h_ref):
    x1 = x_ref[...] + _dot(m_ref[...], w_ref[...])
    x1_ref[...] = x1
    h_ref[...] = _rms(x1, nw_ref[...]).astype(h_ref.dtype)


def _outproj(x, merged, w_out, norm_w, tm):
    t, d = x.shape
    pipelined = 2 * _nbytes((tm, d), F32) + 2 * _nbytes((tm, d), BF16) + _nbytes((d, d), BF16)
    return pl.pallas_call(
        _outproj_kernel,
        grid=(t // tm,),
        in_specs=[
            pl.BlockSpec((tm, d), lambda i: (i, 0)),
            pl.BlockSpec((tm, d), lambda i: (i, 0)),
            pl.BlockSpec((d, d), lambda i: (0, 0)),
            pl.BlockSpec((1, d), lambda i: (0, 0)),
        ],
        out_specs=[pl.BlockSpec((tm, d), lambda i: (i, 0)), pl.BlockSpec((tm, d), lambda i: (i, 0))],
        out_shape=[jax.ShapeDtypeStruct((t, d), F32), jax.ShapeDtypeStruct((t, d), BF16)],
        compiler_params=pltpu.CompilerParams(
            dimension_semantics=("parallel",),
            vmem_limit_bytes=_vmem_limit(pipelined, 2 * _nbytes((tm, d), F32))),
        name="outproj",
    )(x, merged, w_out, norm_w)


def _mlp_kernel(h_ref, x1_ref, wu_ref, wd_ref, o_ref):
    @pl.when(pl.program_id(1) == 0)
    def _():
        o_ref[...] = x1_ref[...]

    a = jnp.maximum(_dot(h_ref[...], wu_ref[...]), 0.0)
    o_ref[...] += _dot((a * a).astype(BF16), wd_ref[...])


def _mlp(h2, x1, w_up, w_down, tm, tf):
    t, d = x1.shape
    f = w_up.shape[1]
    pipelined = (_nbytes((tm, d), BF16) + 2 * _nbytes((tm, d), F32)
                 + _nbytes((d, tf), BF16) + _nbytes((tf, d), BF16))
    resident = 2 * _nbytes((tm, tf), F32)
    return pl.pallas_call(
        _mlp_kernel,
        grid=(t // tm, f // tf),
        in_specs=[
            pl.BlockSpec((tm, d), lambda i, k: (i, 0)),
            pl.BlockSpec((tm, d), lambda i, k: (i, 0)),
            pl.BlockSpec((d, tf), lambda i, k: (0, k)),
            pl.BlockSpec((tf, d), lambda i, k: (k, 0)),
        ],
        out_specs=pl.BlockSpec((tm, d), lambda i, k: (i, 0)),
        out_shape=jax.ShapeDtypeStruct((t, d), F32),
        compiler_params=pltpu.CompilerParams(
            dimension_semantics=("parallel", "arbitrary"),
            vmem_limit_bytes=_vmem_limit(pipelined, resident)),
        name="mlp",
    )(h2, x1, w_up, w_down)


def _layer(x, mem, attn_norm_w, w_in, dn_conv_w, dn_a_log, dn_dt_bias, dn_out_norm_w, swa_q_norm_w,
           swa_k_norm_w, swa_sinks, rel_bias, mem_norm_w, w_mem_kv, xq_norm_w, xk_norm_w,
           p_dn, p_swa, p_mem, w_out, mlp_norm_w, w_mlp_up, w_mlp_down):
    b, s, d = x.shape
    m = mem.shape[1]
    t = b * s

    dn_heads = dn_a_log.shape[0]
    dn_dim = dn_out_norm_w.shape[0]
    dn_w = dn_heads * dn_dim
    swa_heads = swa_sinks.shape[0]
    swa_dim = swa_q_norm_w.shape[0]
    swa_w = swa_heads * swa_dim
    mem_dim = xq_norm_w.shape[0]
    mem_w = p_mem.shape[0]
    mem_heads = mem_w // mem_dim
    swa_kv_w = (w_in.shape[1] - 4 * dn_w - 2 * dn_heads - swa_w - mem_w - 3 * d) // 2
    swa_kv = swa_kv_w // swa_dim

    n_ba = 2 * dn_heads
    src_ba = 4 * dn_w
    src_swa = src_ba + n_ba
    w_t = w_in.T.astype(BF16)
    w_b = w_t[src_swa:]
    w_c = jnp.concatenate([w_t[src_ba:src_swa], jnp.zeros((LANES - n_ba, d), BF16)], axis=0)
    col = {"qkv": 0, "z": 3 * dn_w, "sq": src_ba}
    col["sk"] = col["sq"] + swa_w
    col["sv"] = col["sk"] + swa_kv_w
    col["mq"] = col["sv"] + swa_kv_w

    proj, gates, ba = _inproj(x.reshape(t, d), attn_norm_w.reshape(1, d), w_t, src_ba, w_b, w_c, 3 * d,
                              tm=1024, tn=1024)
    proj3 = proj.reshape(b, s, -1)
    mkv = _norm_matmul(mem.reshape(b * m, d), mem_norm_w.reshape(1, d), w_mem_kv.astype(BF16), tm=512, tn=512)
    mkv3 = mkv.reshape(b, m, -1)

    gate_params = jnp.zeros((SUBLANES, LANES), F32)
    gate_params = gate_params.at[0, dn_heads:2 * dn_heads].set(dn_a_log)
    gate_params = gate_params.at[1, dn_heads:2 * dn_heads].set(dn_dt_bias)
    o_dn = _deltanet(proj3, ba.reshape(b, s, LANES), dn_conv_w, gate_params, dn_out_norm_w.reshape(1, dn_dim),
                     dn_heads, dn_dim, col["z"])
    o_swa = _swa(proj3, rel_bias.reshape(-1), swa_sinks, swa_q_norm_w.reshape(1, swa_dim),
                 swa_k_norm_w.reshape(1, swa_dim), swa_heads, swa_kv, swa_dim, col["sq"], col["sk"], col["sv"])
    o_mem = _memattn(proj3, mkv3, xq_norm_w.reshape(1, mem_dim), xk_norm_w.reshape(1, mem_dim),
                     mem_heads, mem_dim, col["mq"], tq=512)

    merged = _merge(o_dn.reshape(t, dn_w), o_swa.reshape(t, swa_w), o_mem.reshape(t, mem_w),
                    p_dn.astype(BF16), p_swa.astype(BF16), p_mem.astype(BF16), gates, tm=512, tn=1024)
    x1, h2 = _outproj(x.reshape(t, d), merged, w_out.astype(BF16), mlp_norm_w.reshape(1, d), tm=256)
    out = _mlp(h2, x1, w_mlp_up.astype(BF16), w_mlp_down.astype(BF16), tm=512, tf=1024)
    return out.reshape(b, s, d)


def kernel(x, mem, attn_norm_w, w_in, dn_conv_w, dn_A_log, dn_dt_bias, dn_out_norm_w, swa_q_norm_w,
           swa_k_norm_w, swa_sinks, rel_bias, mem_norm_w, w_mem_kv, xq_norm_w, xk_norm_w, p_dn, p_swa,
           p_mem, w_out, mlp_norm_w, w_mlp_up, w_mlp_down):
    depth = w_in.shape[0]
    for l in range(depth):
        x = _layer(x, mem, attn_norm_w[l], w_in[l], dn_conv_w[l], dn_A_log[l], dn_dt_bias[l],
                   dn_out_norm_w[l], swa_q_norm_w[l], swa_k_norm_w[l], swa_sinks[l], rel_bias,
                   mem_norm_w[l], w_mem_kv[l], xq_norm_w[l], xk_norm_w[l], p_dn[l], p_swa[l], p_mem[l],
                   w_out[l], mlp_norm_w[l], w_mlp_up[l], w_mlp_down[l])
    return x
```

```python
import functools
import math

import jax
import jax.numpy as jnp
from jax import lax
from jax.experimental import pallas as pl
from jax.experimental.pallas import tpu as pltpu

F32 = jnp.float32
BF16 = jnp.bfloat16

EPS = 1e-6
LANES = 128
SUBLANES = 8
V7X_VMEM_BYTES = 64 * 1024 * 1024

DN_CHUNK = 64
DN_GROUP = 256
SWA_WINDOW = 128
SWA_BLOCK = 128
N_BUCKETS = 32
MAX_DISTANCE = 128

_NT = (((1,), (1,)), ((), ()))
_TN = (((0,), (0,)), ((), ()))


def _vmem_limit(pipelined_bytes, resident_bytes):
    want = 2 * pipelined_bytes + resident_bytes + (4 << 20)
    return int(min(want, V7X_VMEM_BYTES - (8 << 20)))


def _nbytes(shape, dtype):
    return math.prod(shape) * jnp.dtype(dtype).itemsize


def _sigmoid(v):
    return 0.5 * jnp.tanh(0.5 * v) + 0.5


def _silu(v):
    return v * _sigmoid(v)


def _rms(v, w):
    return (v * lax.rsqrt(jnp.mean(v * v, axis=-1, keepdims=True) + EPS)) * w


def _dot(a, b):
    return jnp.dot(a, b, preferred_element_type=F32)


def _dot_nt(a, b):
    return lax.dot_general(a, b, _NT, preferred_element_type=F32)


def _inproj_kernel(x_ref, nw_ref, wa_ref, wb_ref, wc_ref, o_ref, g_ref, ba_ref, h_ref, *, n_a, n_p):
    j = pl.program_id(1)

    @pl.when(j == 0)
    def _():
        h_ref[...] = _rms(x_ref[...], nw_ref[...]).astype(BF16)
        ba_ref[...] = _dot_nt(h_ref[...], wc_ref[...])

    @pl.when(j < n_a)
    def _():
        o_ref[...] = _dot_nt(h_ref[...], wa_ref[...])

    @pl.when((j >= n_a) & (j < n_p))
    def _():
        o_ref[...] = _dot_nt(h_ref[...], wb_ref[...])

    @pl.when(j >= n_p)
    def _():
        g_ref[...] = _sigmoid(_dot_nt(h_ref[...], wb_ref[...])).astype(g_ref.dtype)


def _inproj(x, nw, w_a, rows_a, w_b, w_c, n_gates, tm, tn):
    m, k = x.shape
    rows_b1 = w_b.shape[0] - n_gates
    n = rows_a + rows_b1
    n_a = rows_a // tn
    n_p = pl.cdiv(n, tn)
    assert rows_a % tn == 0 and n_gates % tn == 0 and rows_b1 % (2 * SUBLANES) == 0

    def wb_index(i, j):
        row = jnp.where(j < n_p, jnp.maximum(j - n_a, 0) * tn, rows_b1 + (j - n_p) * tn)
        return (pl.multiple_of(row, 2 * SUBLANES), 0)

    pipelined = (_nbytes((tm, k), F32) + 2 * _nbytes((tn, k), BF16) + _nbytes((tm, tn), F32)
                 + _nbytes((tm, tn), BF16) + _nbytes((tm, LANES), F32))
    resident = _nbytes((tm, k), BF16) + _nbytes((tm, k), F32) + _nbytes((LANES, k), BF16)
    return pl.pallas_call(
        functools.partial(_inproj_kernel, n_a=n_a, n_p=n_p),
        grid=(m // tm, n_p + n_gates // tn),
        in_specs=[
            pl.BlockSpec((tm, k), lambda i, j: (i, 0)),
            pl.BlockSpec((1, k), lambda i, j: (0, 0)),
            pl.BlockSpec((tn, k), lambda i, j: (jnp.minimum(j, n_a - 1), 0)),
            pl.BlockSpec((pl.Element(tn), pl.Element(k)), wb_index),
            pl.BlockSpec((LANES, k), lambda i, j: (0, 0)),
        ],
        out_specs=[pl.BlockSpec((tm, tn), lambda i, j: (i, jnp.minimum(j, n_p - 1))),
                   pl.BlockSpec((tm, tn), lambda i, j: (i, jnp.maximum(j - n_p, 0))),
                   pl.BlockSpec((tm, LANES), lambda i, j: (i, 0))],
        out_shape=[jax.ShapeDtypeStruct((m, n), F32), jax.ShapeDtypeStruct((m, n_gates), BF16),
                   jax.ShapeDtypeStruct((m, LANES), F32)],
        scratch_shapes=[pltpu.VMEM((tm, k), BF16)],
        compiler_params=pltpu.CompilerParams(
            dimension_semantics=("parallel", "arbitrary"),
            vmem_limit_bytes=_vmem_limit(pipelined, resident)),
        name="inproj",
    )(x, nw, w_a, w_b, w_c)


def _norm_matmul_kernel(x_ref, nw_ref, w_ref, o_ref, h_ref):
    @pl.when(pl.program_id(1) == 0)
    def _():
        h_ref[...] = _rms(x_ref[...], nw_ref[...]).astype(BF16)

    o_ref[...] = _dot(h_ref[...], w_ref[...]).astype(o_ref.dtype)


def _norm_matmul(x, nw, w, tm, tn):
    m, k = x.shape
    n = w.shape[1]
    pipelined = _nbytes((tm, k), F32) + _nbytes((k, tn), BF16) + _nbytes((tm, tn), F32)
    resident = _nbytes((tm, k), BF16) + _nbytes((tm, k), F32)
    return pl.pallas_call(
        _norm_matmul_kernel,
        grid=(m // tm, n // tn),
        in_specs=[
            pl.BlockSpec((tm, k), lambda i, j: (i, 0)),
            pl.BlockSpec((1, k), lambda i, j: (0, 0)),
            pl.BlockSpec((k, tn), lambda i, j: (0, j)),
        ],
        out_specs=pl.BlockSpec((tm, tn), lambda i, j: (i, j)),
        out_shape=jax.ShapeDtypeStruct((m, n), F32),
        scratch_shapes=[pltpu.VMEM((tm, k), BF16)],
        compiler_params=pltpu.CompilerParams(
            dimension_semantics=("parallel", "arbitrary"),
            vmem_limit_bytes=_vmem_limit(pipelined, resident)),
        name="norm_matmul",
    )(x, nw, w)


def _dn_kernel(qkv_ref, z_ref, ba_ref, cw_ref, gp_ref, onw_ref, wu_ref, wd_ref, o_ref, wu_bf_ref, wd_bf_ref,
               ext_ref, state_ref, *, heads, dim):
    wu_bf_ref[...] = wu_ref[...].astype(wu_bf_ref.dtype)
    wd_bf_ref[...] = wd_ref[...].astype(wd_bf_ref.dtype)

    gt = DN_GROUP
    c = DN_CHUNK
    hd = heads * dim
    hs = range(heads)
    chunks = range(gt // c)

    @pl.when(pl.program_id(1) == 0)
    def _():
        ext_ref[0:SUBLANES, :] = jnp.zeros((SUBLANES, 3 * hd), F32)
        state_ref[...] = jnp.zeros_like(state_ref)

    ext_ref[SUBLANES:SUBLANES + gt, :] = qkv_ref[...]
    xe = ext_ref[...]
    conv = cw_ref[3:4, :] * xe[SUBLANES:]
    for s in (1, 2, 3):
        conv = conv + cw_ref[3 - s:4 - s, :] * pltpu.roll(xe, s, 0)[SUBLANES:]
    ext_ref[0:SUBLANES, :] = xe[gt:gt + SUBLANES]
    act = _silu(conv)

    ba = ba_ref[...]
    beta_all = _sigmoid(ba)
    xa = ba + gp_ref[1:2, :]
    softplus = jnp.maximum(xa, 0.0) + jnp.log1p(jnp.exp(-jnp.abs(xa)))
    g_all = -jnp.exp(gp_ref[0:1, :]) * softplus
    row_in_chunk = lax.broadcasted_iota(jnp.int32, (gt, LANES), 0) & (c - 1)
    gcum = g_all
    s = 1
    while s < c:
        gcum = gcum + jnp.where(row_in_chunk >= s, pltpu.roll(gcum, s, 0), 0.0)
        s *= 2
    exp_g = jnp.exp(gcum)
    kdec_parts, gc_rows = [], []
    for j in chunks:
        g_last = gcum[c * j + c - 1:c * j + c, :]
        kdec_parts.append(jnp.exp(g_last - gcum[c * j:c * (j + 1), :]))
        gc_rows.append(jnp.exp(g_last))
    kdec = jnp.concatenate(kdec_parts, axis=0)
    gcum_t = gcum.T

    ri = lax.broadcasted_iota(jnp.int32, (gt, gt), 0)
    ci = lax.broadcasted_iota(jnp.int32, (gt, gt), 1)
    same_chunk = (ri // c) == (ci // c)
    strict = same_chunk & (ri > ci)
    incl = same_chunk & (ri >= ci)

    qn, kn, kb, beta, eg = [], [], [], [], []
    for h in hs:
        qh = act[:, h * dim:(h + 1) * dim]
        kh = act[:, hd + h * dim:hd + (h + 1) * dim]
        qn.append(qh * lax.rsqrt(jnp.sum(qh * qh, axis=-1, keepdims=True) + EPS) * (dim ** -0.5))
        kn.append(kh * lax.rsqrt(jnp.sum(kh * kh, axis=-1, keepdims=True) + EPS))
        kb.append(kn[h].astype(BF16))
        beta.append(beta_all[:, h:h + 1])
        eg.append(exp_g[:, heads + h:heads + h + 1])

    decay = [jnp.exp(jnp.where(incl, gcum[:, heads + h:heads + h + 1] - gcum_t[heads + h:heads + h + 1, :],
                               -jnp.inf)) for h in hs]
    kk = [_dot_nt(kb[h], kb[h]) for h in hs]
    pw = [jnp.where(strict, (beta[h] * kk[h]) * decay[h], 0.0).astype(BF16) for h in hs]

    sol = [jnp.concatenate([act[:, 2 * hd + h * dim:2 * hd + (h + 1) * dim] * beta[h],
                            kn[h] * (beta[h] * eg[h])], axis=1) for h in hs]
    sol = [sol[h] - _dot(pw[h], sol[h].astype(BF16)) for h in hs]
    order = 2
    while order < c:
        pw = [_dot(pw[h], pw[h]).astype(BF16) for h in hs]
        sol = [sol[h] + _dot(pw[h], sol[h].astype(BF16)) for h in hs]
        order *= 2

    pmat = [(_dot_nt(qn[h].astype(BF16), kb[h]) * decay[h]).astype(BF16) for h in hs]
    wq = [[jnp.concatenate([sol[h][c * j:c * (j + 1), dim:], (qn[h] * eg[h])[c * j:c * (j + 1)]],
                           axis=0).astype(BF16) for j in chunks] for h in hs]
    kd = [(kn[h] * kdec[:, heads + h:heads + h + 1]).astype(BF16) for h in hs]

    state = [state_ref[h] for h in hs]
    o_parts = [[] for _ in hs]
    for j in chunks:
        r = slice(c * j, c * (j + 1))
        ws = [_dot(wq[h][j], state[h].astype(BF16)) for h in hs]
        db = [(sol[h][r, :dim] - ws[h][:c]).astype(BF16) for h in hs]
        for h in hs:
            o_parts[h].append(ws[h][c:] + _dot(pmat[h][r, r], db[h]))
        state = [gc_rows[j][:, heads + h:heads + h + 1] * state[h]
                 + lax.dot_general(kd[h][r], db[h], _TN, preferred_element_type=F32) for h in hs]
    for h in hs:
        state_ref[h] = state[h]

    for h in hs:
        o = jnp.concatenate(o_parts[h], axis=0)
        zh = z_ref[:, h * dim:(h + 1) * dim]
        o_ref[:, h * dim:(h + 1) * dim] = (_rms(o, onw_ref[...]) * _silu(zh)).astype(o_ref.dtype)


def _deltanet(proj3, ba3, conv_w, gate_params, out_norm_w, w_up, w_down, heads, dim, col_z):
    b, s, _ = proj3.shape
    hd = heads * dim
    gt = DN_GROUP
    ng = s // gt
    n_steps = b * ng
    up_rows, down_rows = w_up.shape[0] // n_steps, w_down.shape[0] // n_steps
    assert up_rows * n_steps == w_up.shape[0] and down_rows * n_steps == w_down.shape[0]
    assert up_rows % (2 * SUBLANES) == 0 and down_rows % (2 * SUBLANES) == 0
    up_blk, down_blk = (up_rows, w_up.shape[1]), (down_rows, w_down.shape[1])
    pipelined = (_nbytes((gt, 3 * hd), F32) + _nbytes((gt, hd), F32) + _nbytes((gt, LANES), F32)
                 + _nbytes((gt, hd), BF16) + _nbytes(up_blk, F32) + _nbytes(up_blk, BF16)
                 + _nbytes(down_blk, F32) + _nbytes(down_blk, BF16))
    resident = (_nbytes((gt + SUBLANES, 3 * hd), F32) * 4 + _nbytes((heads, dim, dim), F32)
                + heads * 6 * _nbytes((gt, gt), F32))
    return pl.pallas_call(
        functools.partial(_dn_kernel, heads=heads, dim=dim),
        grid=(b, ng),
        in_specs=[
            pl.BlockSpec((None, gt, 3 * hd), lambda i, g: (i, g, 0)),
            pl.BlockSpec((None, gt, hd), lambda i, g: (i, g, col_z // hd)),
            pl.BlockSpec((None, gt, LANES), lambda i, g: (i, g, 0)),
            pl.BlockSpec((4, 3 * hd), lambda i, g: (0, 0)),
            pl.BlockSpec((SUBLANES, LANES), lambda i, g: (0, 0)),
            pl.BlockSpec((1, dim), lambda i, g: (0, 0)),
            pl.BlockSpec(up_blk, lambda i, g: (i * ng + g, 0)),
            pl.BlockSpec(down_blk, lambda i, g: (i * ng + g, 0)),
        ],
        out_specs=[pl.BlockSpec((None, gt, hd), lambda i, g: (i, g, 0)),
                   pl.BlockSpec(up_blk, lambda i, g: (i * ng + g, 0)),
                   pl.BlockSpec(down_blk, lambda i, g: (i * ng + g, 0))],
        out_shape=[jax.ShapeDtypeStruct((b, s, hd), BF16), jax.ShapeDtypeStruct(w_up.shape, BF16),
                   jax.ShapeDtypeStruct(w_down.shape, BF16)],
        scratch_shapes=[pltpu.VMEM((gt + SUBLANES, 3 * hd), F32), pltpu.VMEM((heads, dim, dim), F32)],
        compiler_params=pltpu.CompilerParams(
            dimension_semantics=("parallel", "arbitrary"),
            vmem_limit_bytes=_vmem_limit(pipelined, resident)),
        name="deltanet",
    )(proj3, proj3, ba3, conv_w, gate_params, out_norm_w, w_up, w_down)


def _swa_kernel(rb_ref, sink_ref, q_ref, kc_ref, kp_ref, vc_ref, vp_ref, qw_ref, kw_ref, o_ref, bias_ref,
                *, q_heads, kv_heads, dim):
    blk = SWA_BLOCK
    n = pl.program_id(1)
    qi = lax.broadcasted_iota(jnp.int32, (blk, 2 * blk), 0)
    kj = lax.broadcasted_iota(jnp.int32, (blk, 2 * blk), 1)
    dist = qi - kj + blk

    @pl.when((pl.program_id(0) == 0) & (n == 0))
    def _():
        max_exact = N_BUCKETS // 2
        nn = jnp.maximum(dist, 0)
        nf = jnp.maximum(nn, 1).astype(F32)
        large = max_exact + (jnp.log(nf / max_exact) / math.log(MAX_DISTANCE / max_exact)
                             * (N_BUCKETS - max_exact)).astype(jnp.int32)
        bucket = jnp.where(nn < max_exact, nn, jnp.minimum(large, N_BUCKETS - 1))
        for h in range(q_heads):
            acc = jnp.zeros((blk, 2 * blk), F32)
            for bk in range(N_BUCKETS):
                acc = jnp.where(bucket == bk, rb_ref[bk * q_heads + h], acc)
            acc = jnp.where((dist >= 0) & (dist < SWA_WINDOW), acc, -jnp.inf)
            bias_ref[h] = acc
            bias_ref[q_heads + h] = jnp.where(kj >= blk, acc, -jnp.inf)

    first = jnp.where(n == 0, q_heads, 0)
    group = q_heads // kv_heads
    for j in range(kv_heads):
        cols = slice(j * dim, (j + 1) * dim)
        kwin = jnp.concatenate([kp_ref[:, cols], kc_ref[:, cols]], axis=0)
        kwin = _rms(kwin, kw_ref[...]).astype(BF16)
        vwin = jnp.concatenate([vp_ref[:, cols], vc_ref[:, cols]], axis=0).astype(BF16)
        qcat = jnp.concatenate(
            [_rms(q_ref[:, (j * group + i) * dim:(j * group + i + 1) * dim], qw_ref[...]) for i in range(group)],
            axis=0).astype(BF16)
        logits = _dot_nt(qcat, kwin) * (dim ** -0.5)
        for i in range(group):
            h = j * group + i
            lg = logits[i * blk:(i + 1) * blk] + bias_ref[first + h]
            sink = sink_ref[h]
            mx = jnp.maximum(jnp.max(lg, axis=-1, keepdims=True), sink)
            e = jnp.exp(lg - mx)
            den = jnp.sum(e, axis=-1, keepdims=True) + jnp.exp(sink - mx)
            pv = _dot(e.astype(BF16), vwin)
            o_ref[:, h * dim:(h + 1) * dim] = (pv / den).astype(o_ref.dtype)


def _swa(proj3, rel_bias_flat, sinks, q_norm_w, k_norm_w, q_heads, kv_heads, dim, col_q, col_k, col_v):
    b, s, _ = proj3.shape
    blk = SWA_BLOCK
    qw, kvw = q_heads * dim, kv_heads * dim
    smem = pl.BlockSpec(memory_space=pltpu.SMEM)
    pipelined = _nbytes((blk, qw), F32) + 4 * _nbytes((blk, kvw), F32) + _nbytes((blk, qw), BF16)
    resident = _nbytes((2 * q_heads, blk, 2 * blk), F32) + (8 << 20)
    return pl.pallas_call(
        functools.partial(_swa_kernel, q_heads=q_heads, kv_heads=kv_heads, dim=dim),
        grid=(b, s // blk),
        in_specs=[
            smem, smem,
            pl.BlockSpec((None, blk, qw), lambda i, n: (i, n, col_q // qw)),
            pl.BlockSpec((None, blk, kvw), lambda i, n: (i, n, col_k // kvw)),
            pl.BlockSpec((None, blk, kvw), lambda i, n: (i, jnp.maximum(n - 1, 0), col_k // kvw)),
            pl.BlockSpec((None, blk, kvw), lambda i, n: (i, n, col_v // kvw)),
            pl.BlockSpec((None, blk, kvw), lambda i, n: (i, jnp.maximum(n - 1, 0), col_v // kvw)),
            pl.BlockSpec((1, dim), lambda i, n: (0, 0)),
            pl.BlockSpec((1, dim), lambda i, n: (0, 0)),
        ],
        out_specs=pl.BlockSpec((None, blk, qw), lambda i, n: (i, n, 0)),
        out_shape=jax.ShapeDtypeStruct((b, s, qw), BF16),
        scratch_shapes=[pltpu.VMEM((2 * q_heads, blk, 2 * blk), F32)],
        compiler_params=pltpu.CompilerParams(
            dimension_semantics=("arbitrary", "arbitrary"),
            vmem_limit_bytes=_vmem_limit(pipelined, resident)),
        name="swa",
    )(rel_bias_flat, sinks, proj3, proj3, proj3, proj3, proj3, q_norm_w, k_norm_w)


def _memattn_kernel(q_ref, k_ref, v_ref, qw_ref, kw_ref, o_ref, *, heads, dim):
    for h in range(heads):
        cols = slice(h * dim, (h + 1) * dim)
        qn = _rms(q_ref[:, cols], qw_ref[...]).astype(BF16)
        kn = _rms(k_ref[:, cols], kw_ref[...]).astype(BF16)
        lg = _dot_nt(qn, kn) * (dim ** -0.5)
        e = jnp.exp(lg - jnp.max(lg, axis=-1, keepdims=True))
        den = jnp.sum(e, axis=-1, keepdims=True)
        pv = _dot(e.astype(BF16), v_ref[:, cols].astype(BF16))
        o_ref[:, cols] = (pv / den).astype(o_ref.dtype)


def _memattn(proj3, mkv3, q_norm_w, k_norm_w, heads, dim, col_q, tq):
    b, s, _ = proj3.shape
    m = mkv3.shape[1]
    w = heads * dim
    pipelined = _nbytes((tq, w), F32) + 2 * _nbytes((m, w), F32) + _nbytes((tq, w), BF16)
    return pl.pallas_call(
        functools.partial(_memattn_kernel, heads=heads, dim=dim),
        grid=(b, s // tq),
        in_specs=[
            pl.BlockSpec((pl.Element(tq), pl.Element(w)), lambda i, t: (pl.multiple_of(i * s + t * tq, tq), col_q)),
            pl.BlockSpec((None, m, w), lambda i, t: (i, 0, 0)),
            pl.BlockSpec((None, m, w), lambda i, t: (i, 0, 1)),
            pl.BlockSpec((1, dim), lambda i, t: (0, 0)),
            pl.BlockSpec((1, dim), lambda i, t: (0, 0)),
        ],
        out_specs=pl.BlockSpec((None, tq, w), lambda i, t: (i, t, 0)),
        out_shape=jax.ShapeDtypeStruct((b, s, w), BF16),
        compiler_params=pltpu.CompilerParams(
            dimension_semantics=("parallel", "parallel"),
            vmem_limit_bytes=_vmem_limit(pipelined, 8 << 20)),
        name="memattn",
    )(proj3.reshape(b * s, -1), mkv3, mkv3, q_norm_w, k_norm_w)


def _merge_kernel(od_ref, os_ref, om_ref, pd_ref, ps_ref, pm_ref, gd_ref, gs_ref, gm_ref, o_ref):
    merged = (gd_ref[...].astype(F32) * _dot(od_ref[...], pd_ref[...])
              + gs_ref[...].astype(F32) * _dot(os_ref[...], ps_ref[...])
              + gm_ref[...].astype(F32) * _dot(om_ref[...], pm_ref[...]))
    o_ref[...] = merged.astype(o_ref.dtype)


def _merge(o_dn, o_swa, o_mem, p_dn, p_swa, p_mem, gates, tm, tn):
    t = o_dn.shape[0]
    d = p_dn.shape[1]
    wd, ws, wm = o_dn.shape[1], o_swa.shape[1], o_mem.shape[1]
    per_branch = d // tn
    pipelined = ((_nbytes((tm, wd), BF16) + _nbytes((tm, ws), BF16) + _nbytes((tm, wm), BF16))
                 + (_nbytes((wd, tn), BF16) + _nbytes((ws, tn), BF16) + _nbytes((wm, tn), BF16))
                 + 4 * _nbytes((tm, tn), BF16))
    return pl.pallas_call(
        _merge_kernel,
        grid=(d // tn, t // tm),
        in_specs=[
            pl.BlockSpec((tm, wd), lambda j, i: (i, 0)),
            pl.BlockSpec((tm, ws), lambda j, i: (i, 0)),
            pl.BlockSpec((tm, wm), lambda j, i: (i, 0)),
            pl.BlockSpec((wd, tn), lambda j, i: (0, j)),
            pl.BlockSpec((ws, tn), lambda j, i: (0, j)),
            pl.BlockSpec((wm, tn), lambda j, i: (0, j)),
            pl.BlockSpec((tm, tn), lambda j, i: (i, j)),
            pl.BlockSpec((tm, tn), lambda j, i: (i, per_branch + j)),
            pl.BlockSpec((tm, tn), lambda j, i: (i, 2 * per_branch + j)),
        ],
        out_specs=pl.BlockSpec((tm, tn), lambda j, i: (i, j)),
        out_shape=jax.ShapeDtypeStruct((t, d), BF16),
        compiler_params=pltpu.CompilerParams(
            dimension_semantics=("parallel", "parallel"),
            vmem_limit_bytes=_vmem_limit(pipelined, 3 * _nbytes((tm, tn), F32))),
        name="merge",
    )(o_dn, o_swa, o_mem, p_dn, p_swa, p_mem, gates, gates, gates)


def _outproj_kernel(x_ref, m_ref, w_ref, nw_ref, x1_ref, h_ref):
    x1 = x_ref[...] + _dot(m_ref[...], w_ref[...])
    x1_ref[...] = x1
    h_ref[...] = _rms(x1, nw_ref[...]).astype(h_ref.dtype)


def _outproj(x, merged, w_out, norm_w, tm):
    t, d = x.shape
    pipelined = 2 * _nbytes((tm, d), F32) + 2 * _nbytes((tm, d), BF16) + _nbytes((d, d), BF16)
    return pl.pallas_call(
        _outproj_kernel,
        grid=(t // tm,),
        in_specs=[
            pl.BlockSpec((tm, d), lambda i: (i, 0)),
            pl.BlockSpec((tm, d), lambda i: (i, 0)),
            pl.BlockSpec((d, d), lambda i: (0, 0)),
            pl.BlockSpec((1, d), lambda i: (0, 0)),
        ],
        out_specs=[pl.BlockSpec((tm, d), lambda i: (i, 0)), pl.BlockSpec((tm, d), lambda i: (i, 0))],
        out_shape=[jax.ShapeDtypeStruct((t, d), F32), jax.ShapeDtypeStruct((t, d), BF16)],
        compiler_params=pltpu.CompilerParams(
            dimension_semantics=("parallel",),
            vmem_limit_bytes=_vmem_limit(pipelined, 2 * _nbytes((tm, d), F32))),
        name="outproj",
    )(x, merged, w_out, norm_w)


def _mlp_kernel(h_ref, x1_ref, wu_ref, wd_ref, o_ref):
    @pl.when(pl.program_id(1) == 0)
    def _():
        o_ref[...] = x1_ref[...]

    a = jnp.maximum(_dot(h_ref[...], wu_ref[...]), 0.0)
    o_ref[...] += _dot((a * a).astype(BF16), wd_ref[...])


def _mlp(h2, x1, w_up, w_down, tm, tf):
    t, d = x1.shape
    f = w_up.shape[1]
    pipelined = (_nbytes((tm, d), BF16) + 2 * _nbytes((tm, d), F32)
                 + _nbytes((d, tf), BF16) + _nbytes((tf, d), BF16))
    resident = 2 * _nbytes((tm, tf), F32)
    return pl.pallas_call(
        _mlp_kernel,
        grid=(t // tm, f // tf),
        in_specs=[
            pl.BlockSpec((tm, d), lambda i, k: (i, 0)),
            pl.BlockSpec((tm, d), lambda i, k: (i, 0)),
            pl.BlockSpec((d, tf), lambda i, k: (0, k)),
            pl.BlockSpec((tf, d), lambda i, k: (k, 0)),
        ],
        out_specs=pl.BlockSpec((tm, d), lambda i, k: (i, 0)),
        out_shape=jax.ShapeDtypeStruct((t, d), F32),
        compiler_params=pltpu.CompilerParams(
            dimension_semantics=("parallel", "arbitrary"),
            vmem_limit_bytes=_vmem_limit(pipelined, resident)),
        name="mlp",
    )(h2, x1, w_up, w_down)


def _layer(x, mem, attn_norm_w, w_in, dn_conv_w, dn_a_log, dn_dt_bias, dn_out_norm_w, swa_q_norm_w,
           swa_k_norm_w, swa_sinks, rel_bias, mem_norm_w, w_mem_kv, xq_norm_w, xk_norm_w,
           p_dn, p_swa, p_mem, w_out, mlp_norm_w, w_mlp_up, w_mlp_down):
    b, s, d = x.shape
    m = mem.shape[1]
    t = b * s

    dn_heads = dn_a_log.shape[0]
    dn_dim = dn_out_norm_w.shape[0]
    dn_w = dn_heads * dn_dim
    swa_heads = swa_sinks.shape[0]
    swa_dim = swa_q_norm_w.shape[0]
    swa_w = swa_heads * swa_dim
    mem_dim = xq_norm_w.shape[0]
    mem_w = p_mem.shape[0]
    mem_heads = mem_w // mem_dim
    swa_kv_w = (w_in.shape[1] - 4 * dn_w - 2 * dn_heads - swa_w - mem_w - 3 * d) // 2
    swa_kv = swa_kv_w // swa_dim

    n_ba = 2 * dn_heads
    src_ba = 4 * dn_w
    src_swa = src_ba + n_ba
    w_t = w_in.T.astype(BF16)
    w_b = w_t[src_swa:]
    w_c = jnp.concatenate([w_t[src_ba:src_swa], jnp.zeros((LANES - n_ba, d), BF16)], axis=0)
    col = {"qkv": 0, "z": 3 * dn_w, "sq": src_ba}
    col["sk"] = col["sq"] + swa_w
    col["sv"] = col["sk"] + swa_kv_w
    col["mq"] = col["sv"] + swa_kv_w

    proj, gates, ba = _inproj(x.reshape(t, d), attn_norm_w.reshape(1, d), w_t, src_ba, w_b, w_c, 3 * d,
                              tm=1024, tn=1024)
    proj3 = proj.reshape(b, s, -1)
    mkv = _norm_matmul(mem.reshape(b * m, d), mem_norm_w.reshape(1, d), w_mem_kv.astype(BF16), tm=512, tn=512)
    mkv3 = mkv.reshape(b, m, -1)

    gate_params = jnp.zeros((SUBLANES, LANES), F32)
    gate_params = gate_params.at[0, dn_heads:2 * dn_heads].set(dn_a_log)
    gate_params = gate_params.at[1, dn_heads:2 * dn_heads].set(dn_dt_bias)
    o_dn, w_up_bf, w_down_bf = _deltanet(proj3, ba.reshape(b, s, LANES), dn_conv_w, gate_params,
                                         dn_out_norm_w.reshape(1, dn_dim), w_mlp_up, w_mlp_down,
                                         dn_heads, dn_dim, col["z"])
    o_swa = _swa(proj3, rel_bias.reshape(-1), swa_sinks, swa_q_norm_w.reshape(1, swa_dim),
                 swa_k_norm_w.reshape(1, swa_dim), swa_heads, swa_kv, swa_dim, col["sq"], col["sk"], col["sv"])
    o_mem = _memattn(proj3, mkv3, xq_norm_w.reshape(1, mem_dim), xk_norm_w.reshape(1, mem_dim),
                     mem_heads, mem_dim, col["mq"], tq=512)

    merged = _merge(o_dn.reshape(t, dn_w), o_swa.reshape(t, swa_w), o_mem.reshape(t, mem_w),
                    p_dn.astype(BF16), p_swa.astype(BF16), p_mem.astype(BF16), gates, tm=512, tn=1024)
    x1, h2 = _outproj(x.reshape(t, d), merged, w_out.astype(BF16), mlp_norm_w.reshape(1, d), tm=512)
    out = _mlp(h2, x1, w_up_bf, w_down_bf, tm=512, tf=1024)
    return out.reshape(b, s, d)


def kernel(x, mem, attn_norm_w, w_in, dn_conv_w, dn_A_log, dn_dt_bias, dn_out_norm_w, swa_q_norm_w,
           swa_k_norm_w, swa_sinks, rel_bias, mem_norm_w, w_mem_kv, xq_norm_w, xk_norm_w, p_dn, p_swa,
           p_mem, w_out, mlp_norm_w, w_mlp_up, w_mlp_down):
    depth = w_in.shape[0]
    for l in range(depth):
        x = _layer(x, mem, attn_norm_w[l], w_in[l], dn_conv_w[l], dn_A_log[l], dn_dt_bias[l],
                   dn_out_norm_w[l], swa_q_norm_w[l], swa_k_norm_w[l], swa_sinks[l], rel_bias,
                   mem_norm_w[l], w_mem_kv[l], xq_norm_w[l], xk_norm_w[l], p_dn[l], p_swa[l], p_mem[l],
                   w_out[l], mlp_norm_w[l], w_mlp_up[l], w_mlp_down[l])
    return x
```

```python
import functools
import math

import jax
import jax.numpy as jnp
from jax import lax
from jax.experimental import pallas as pl
from jax.experimental.pallas import tpu as pltpu

F32 = jnp.float32
BF16 = jnp.bfloat16

EPS = 1e-6
LANES = 128
SUBLANES = 8
V7X_VMEM_BYTES = 64 * 1024 * 1024

DN_CHUNK = 64
DN_GROUP = 256
SWA_WINDOW = 128
SWA_BLOCK = 128
SWA_STEP_BLOCKS = 2
N_BUCKETS = 32
MAX_DISTANCE = 128

_NT = (((1,), (1,)), ((), ()))
_TN = (((0,), (0,)), ((), ()))


def _vmem_limit(pipelined_bytes, resident_bytes):
    want = 2 * pipelined_bytes + resident_bytes + (4 << 20)
    return int(min(want, V7X_VMEM_BYTES - (8 << 20)))


def _nbytes(shape, dtype):
    return math.prod(shape) * jnp.dtype(dtype).itemsize


def _sigmoid(v):
    return 0.5 * jnp.tanh(0.5 * v) + 0.5


def _silu(v):
    return v * _sigmoid(v)


def _rms(v, w):
    return (v * lax.rsqrt(jnp.mean(v * v, axis=-1, keepdims=True) + EPS)) * w


def _dot(a, b):
    return jnp.dot(a, b, preferred_element_type=F32)


def _dot_nt(a, b):
    return lax.dot_general(a, b, _NT, preferred_element_type=F32)


def _cast_rider_specs(weights, n_steps, step_of):
    in_specs, out_specs, out_shapes = [], [], []
    for w in weights:
        rows = w.shape[0] // n_steps
        assert rows * n_steps == w.shape[0] and rows % (2 * SUBLANES) == 0
        for specs in (in_specs, out_specs):
            specs.append(pl.BlockSpec((rows, w.shape[1]), lambda *ids: (step_of(*ids), 0)))
        out_shapes.append(jax.ShapeDtypeStruct(w.shape, BF16))
    return in_specs, out_specs, out_shapes


def _cast_riders(in_refs, out_refs):
    for i_ref, o_ref in zip(in_refs, out_refs):
        o_ref[...] = i_ref[...].astype(o_ref.dtype)


def _inproj_kernel(x_ref, nw_ref, wa_ref, wb_ref, wc_ref, o_ref, g_ref, ba_ref, h_ref, *, n_a, n_p):
    j = pl.program_id(1)

    @pl.when(j == 0)
    def _():
        h_ref[...] = _rms(x_ref[...], nw_ref[...]).astype(BF16)
        ba_ref[...] = _dot_nt(h_ref[...], wc_ref[...])

    @pl.when(j < n_a)
    def _():
        o_ref[...] = _dot_nt(h_ref[...], wa_ref[...])

    @pl.when((j >= n_a) & (j < n_p))
    def _():
        o_ref[...] = _dot_nt(h_ref[...], wb_ref[...])

    @pl.when(j >= n_p)
    def _():
        g_ref[...] = _sigmoid(_dot_nt(h_ref[...], wb_ref[...])).astype(g_ref.dtype)


def _inproj(x, nw, w_a, rows_a, w_b, w_c, n_gates, tm, tn):
    m, k = x.shape
    rows_b1 = w_b.shape[0] - n_gates
    n = rows_a + rows_b1
    n_a = rows_a // tn
    n_p = pl.cdiv(n, tn)
    assert rows_a % tn == 0 and n_gates % tn == 0 and rows_b1 % (2 * SUBLANES) == 0

    def wb_index(i, j):
        row = jnp.where(j < n_p, jnp.maximum(j - n_a, 0) * tn, rows_b1 + (j - n_p) * tn)
        return (pl.multiple_of(row, 2 * SUBLANES), 0)

    pipelined = (_nbytes((tm, k), F32) + 2 * _nbytes((tn, k), BF16) + _nbytes((tm, tn), F32)
                 + _nbytes((tm, tn), BF16) + _nbytes((tm, LANES), F32))
    resident = _nbytes((tm, k), BF16) + _nbytes((tm, k), F32) + _nbytes((LANES, k), BF16)
    return pl.pallas_call(
        functools.partial(_inproj_kernel, n_a=n_a, n_p=n_p),
        grid=(m // tm, n_p + n_gates // tn),
        in_specs=[
            pl.BlockSpec((tm, k), lambda i, j: (i, 0)),
            pl.BlockSpec((1, k), lambda i, j: (0, 0)),
            pl.BlockSpec((tn, k), lambda i, j: (jnp.minimum(j, n_a - 1), 0)),
            pl.BlockSpec((pl.Element(tn), pl.Element(k)), wb_index),
            pl.BlockSpec((LANES, k), lambda i, j: (0, 0)),
        ],
        out_specs=[pl.BlockSpec((tm, tn), lambda i, j: (i, jnp.minimum(j, n_p - 1))),
                   pl.BlockSpec((tm, tn), lambda i, j: (i, jnp.maximum(j - n_p, 0))),
                   pl.BlockSpec((tm, LANES), lambda i, j: (i, 0))],
        out_shape=[jax.ShapeDtypeStruct((m, n), F32), jax.ShapeDtypeStruct((m, n_gates), BF16),
                   jax.ShapeDtypeStruct((m, LANES), F32)],
        scratch_shapes=[pltpu.VMEM((tm, k), BF16)],
        compiler_params=pltpu.CompilerParams(
            dimension_semantics=("parallel", "arbitrary"),
            vmem_limit_bytes=_vmem_limit(pipelined, resident)),
        name="inproj",
    )(x, nw, w_a, w_b, w_c)


def _norm_matmul_kernel(x_ref, nw_ref, w_ref, o_ref, h_ref):
    @pl.when(pl.program_id(1) == 0)
    def _():
        h_ref[...] = _rms(x_ref[...], nw_ref[...]).astype(BF16)

    o_ref[...] = _dot(h_ref[...], w_ref[...]).astype(o_ref.dtype)


def _norm_matmul(x, nw, w, tm, tn):
    m, k = x.shape
    n = w.shape[1]
    pipelined = _nbytes((tm, k), F32) + _nbytes((k, tn), BF16) + _nbytes((tm, tn), F32)
    resident = _nbytes((tm, k), BF16) + _nbytes((tm, k), F32)
    return pl.pallas_call(
        _norm_matmul_kernel,
        grid=(m // tm, n // tn),
        in_specs=[
            pl.BlockSpec((tm, k), lambda i, j: (i, 0)),
            pl.BlockSpec((1, k), lambda i, j: (0, 0)),
            pl.BlockSpec((k, tn), lambda i, j: (0, j)),
        ],
        out_specs=pl.BlockSpec((tm, tn), lambda i, j: (i, j)),
        out_shape=jax.ShapeDtypeStruct((m, n), F32),
        scratch_shapes=[pltpu.VMEM((tm, k), BF16)],
        compiler_params=pltpu.CompilerParams(
            dimension_semantics=("parallel", "arbitrary"),
            vmem_limit_bytes=_vmem_limit(pipelined, resident)),
        name="norm_matmul",
    )(x, nw, w)


def _dn_kernel(*refs, heads, dim, n_riders):
    qkv_ref, z_ref, ba_ref, cw_ref, gp_ref, onw_ref = refs[:6]
    o_ref = refs[6 + n_riders]
    ext_ref, state_ref = refs[-2:]
    _cast_riders(refs[6:6 + n_riders], refs[7 + n_riders:7 + 2 * n_riders])

    gt = DN_GROUP
    c = DN_CHUNK
    hd = heads * dim
    hs = range(heads)
    chunks = range(gt // c)

    @pl.when(pl.program_id(1) == 0)
    def _():
        ext_ref[0:SUBLANES, :] = jnp.zeros((SUBLANES, 3 * hd), F32)
        state_ref[...] = jnp.zeros_like(state_ref)

    ext_ref[SUBLANES:SUBLANES + gt, :] = qkv_ref[...]
    xe = ext_ref[...]
    conv = cw_ref[3:4, :] * xe[SUBLANES:]
    for s in (1, 2, 3):
        conv = conv + cw_ref[3 - s:4 - s, :] * pltpu.roll(xe, s, 0)[SUBLANES:]
    ext_ref[0:SUBLANES, :] = xe[gt:gt + SUBLANES]
    act = _silu(conv)

    ba = ba_ref[...]
    beta_all = _sigmoid(ba)
    xa = ba + gp_ref[1:2, :]
    softplus = jnp.maximum(xa, 0.0) + jnp.log1p(jnp.exp(-jnp.abs(xa)))
    g_all = -jnp.exp(gp_ref[0:1, :]) * softplus
    row_in_chunk = lax.broadcasted_iota(jnp.int32, (gt, LANES), 0) & (c - 1)
    gcum = g_all
    s = 1
    while s < c:
        gcum = gcum + jnp.where(row_in_chunk >= s, pltpu.roll(gcum, s, 0), 0.0)
        s *= 2
    exp_g = jnp.exp(gcum)
    kdec_parts, gc_rows = [], []
    for j in chunks:
        g_last = gcum[c * j + c - 1:c * j + c, :]
        kdec_parts.append(jnp.exp(g_last - gcum[c * j:c * (j + 1), :]))
        gc_rows.append(jnp.exp(g_last))
    kdec = jnp.concatenate(kdec_parts, axis=0)
    gcum_t = gcum.T

    ri = lax.broadcasted_iota(jnp.int32, (gt, gt), 0)
    ci = lax.broadcasted_iota(jnp.int32, (gt, gt), 1)
    same_chunk = (ri // c) == (ci // c)
    strict = same_chunk & (ri > ci)
    incl = same_chunk & (ri >= ci)

    qn, kn, kb, beta, eg = [], [], [], [], []
    for h in hs:
        qh = act[:, h * dim:(h + 1) * dim]
        kh = act[:, hd + h * dim:hd + (h + 1) * dim]
        qn.append(qh * lax.rsqrt(jnp.sum(qh * qh, axis=-1, keepdims=True) + EPS) * (dim ** -0.5))
        kn.append(kh * lax.rsqrt(jnp.sum(kh * kh, axis=-1, keepdims=True) + EPS))
        kb.append(kn[h].astype(BF16))
        beta.append(beta_all[:, h:h + 1])
        eg.append(exp_g[:, heads + h:heads + h + 1])

    decay = [jnp.exp(jnp.where(incl, gcum[:, heads + h:heads + h + 1] - gcum_t[heads + h:heads + h + 1, :],
                               -jnp.inf)) for h in hs]
    kk = [_dot_nt(kb[h], kb[h]) for h in hs]
    pw = [jnp.where(strict, (beta[h] * kk[h]) * decay[h], 0.0).astype(BF16) for h in hs]

    sol = [jnp.concatenate([act[:, 2 * hd + h * dim:2 * hd + (h + 1) * dim] * beta[h],
                            kn[h] * (beta[h] * eg[h])], axis=1) for h in hs]
    sol = [sol[h] - _dot(pw[h], sol[h].astype(BF16)) for h in hs]
    order = 2
    while order < c:
        pw = [_dot(pw[h], pw[h]).astype(BF16) for h in hs]
        sol = [sol[h] + _dot(pw[h], sol[h].astype(BF16)) for h in hs]
        order *= 2

    pmat = [(_dot_nt(qn[h].astype(BF16), kb[h]) * decay[h]).astype(BF16) for h in hs]
    wq = [[jnp.concatenate([sol[h][c * j:c * (j + 1), dim:], (qn[h] * eg[h])[c * j:c * (j + 1)]],
                           axis=0).astype(BF16) for j in chunks] for h in hs]
    kd = [(kn[h] * kdec[:, heads + h:heads + h + 1]).astype(BF16) for h in hs]

    state = [state_ref[h] for h in hs]
    o_parts = [[] for _ in hs]
    for j in chunks:
        r = slice(c * j, c * (j + 1))
        ws = [_dot(wq[h][j], state[h].astype(BF16)) for h in hs]
        db = [(sol[h][r, :dim] - ws[h][:c]).astype(BF16) for h in hs]
        for h in hs:
            o_parts[h].append(ws[h][c:] + _dot(pmat[h][r, r], db[h]))
        state = [gc_rows[j][:, heads + h:heads + h + 1] * state[h]
                 + lax.dot_general(kd[h][r], db[h], _TN, preferred_element_type=F32) for h in hs]
    for h in hs:
        state_ref[h] = state[h]

    for h in hs:
        o = jnp.concatenate(o_parts[h], axis=0)
        zh = z_ref[:, h * dim:(h + 1) * dim]
        o_ref[:, h * dim:(h + 1) * dim] = (_rms(o, onw_ref[...]) * _silu(zh)).astype(o_ref.dtype)


def _deltanet(proj3, ba3, conv_w, gate_params, out_norm_w, riders, heads, dim, col_z):
    b, s, _ = proj3.shape
    hd = heads * dim
    gt = DN_GROUP
    ng = s // gt
    rider_in, rider_out, rider_shapes = _cast_rider_specs(riders, b * ng, lambda i, g: i * ng + g)
    pipelined = (_nbytes((gt, 3 * hd), F32) + _nbytes((gt, hd), F32) + _nbytes((gt, LANES), F32)
                 + _nbytes((gt, hd), BF16) + sum(_nbytes(w.shape, F32) * 3 // 2 for w in riders) // (b * ng))
    resident = (_nbytes((gt + SUBLANES, 3 * hd), F32) * 4 + _nbytes((heads, dim, dim), F32)
                + heads * 6 * _nbytes((gt, gt), F32))
    return pl.pallas_call(
        functools.partial(_dn_kernel, heads=heads, dim=dim, n_riders=len(riders)),
        grid=(b, ng),
        in_specs=[
            pl.BlockSpec((None, gt, 3 * hd), lambda i, g: (i, g, 0)),
            pl.BlockSpec((None, gt, hd), lambda i, g: (i, g, col_z // hd)),
            pl.BlockSpec((None, gt, LANES), lambda i, g: (i, g, 0)),
            pl.BlockSpec((4, 3 * hd), lambda i, g: (0, 0)),
            pl.BlockSpec((SUBLANES, LANES), lambda i, g: (0, 0)),
            pl.BlockSpec((1, dim), lambda i, g: (0, 0)),
            *rider_in,
        ],
        out_specs=[pl.BlockSpec((None, gt, hd), lambda i, g: (i, g, 0)), *rider_out],
        out_shape=[jax.ShapeDtypeStruct((b, s, hd), BF16), *rider_shapes],
        scratch_shapes=[pltpu.VMEM((gt + SUBLANES, 3 * hd), F32), pltpu.VMEM((heads, dim, dim), F32)],
        compiler_params=pltpu.CompilerParams(
            dimension_semantics=("parallel", "arbitrary"),
            vmem_limit_bytes=_vmem_limit(pipelined, resident)),
        name="deltanet",
    )(proj3, proj3, ba3, conv_w, gate_params, out_norm_w, *riders)


def _swa_kernel(rb_ref, sink_ref, q_ref, kc_ref, kp_ref, vc_ref, vp_ref, qw_ref, kw_ref, o_ref, bias_ref,
                *, q_heads, kv_heads, dim):
    blk = SWA_BLOCK
    n = pl.program_id(1)
    qi = lax.broadcasted_iota(jnp.int32, (blk, 2 * blk), 0)
    kj = lax.broadcasted_iota(jnp.int32, (blk, 2 * blk), 1)
    dist = qi - kj + blk

    @pl.when((pl.program_id(0) == 0) & (n == 0))
    def _():
        max_exact = N_BUCKETS // 2
        nn = jnp.maximum(dist, 0)
        nf = jnp.maximum(nn, 1).astype(F32)
        large = max_exact + (jnp.log(nf / max_exact) / math.log(MAX_DISTANCE / max_exact)
                             * (N_BUCKETS - max_exact)).astype(jnp.int32)
        bucket = jnp.where(nn < max_exact, nn, jnp.minimum(large, N_BUCKETS - 1))
        for h in range(q_heads):
            acc = jnp.zeros((blk, 2 * blk), F32)
            for bk in range(N_BUCKETS):
                acc = jnp.where(bucket == bk, rb_ref[bk * q_heads + h], acc)
            acc = jnp.where((dist >= 0) & (dist < SWA_WINDOW), acc, -jnp.inf)
            bias_ref[h] = acc
            bias_ref[q_heads + h] = jnp.where(kj >= blk, acc, -jnp.inf)

    first = jnp.where(n == 0, q_heads, 0)
    group = q_heads // kv_heads
    for j in range(kv_heads):
        cols = slice(j * dim, (j + 1) * dim)
        k_all = _rms(jnp.concatenate([kp_ref[:, cols], kc_ref[:, cols]], axis=0), kw_ref[...]).astype(BF16)
        v_all = jnp.concatenate([vp_ref[:, cols], vc_ref[:, cols]], axis=0).astype(BF16)
        for u in range(SWA_STEP_BLOCKS):
            rows = slice(u * blk, (u + 1) * blk)
            kwin, vwin = k_all[u * blk:(u + 2) * blk], v_all[u * blk:(u + 2) * blk]
            qcat = jnp.concatenate(
                [_rms(q_ref[rows, (j * group + i) * dim:(j * group + i + 1) * dim], qw_ref[...])
                 for i in range(group)], axis=0).astype(BF16)
            logits = _dot_nt(qcat, kwin) * (dim ** -0.5)
            for i in range(group):
                h = j * group + i
                lg = logits[i * blk:(i + 1) * blk] + bias_ref[(first if u == 0 else 0) + h]
                sink = sink_ref[h]
                mx = jnp.maximum(jnp.max(lg, axis=-1, keepdims=True), sink)
                e = jnp.exp(lg - mx)
                den = jnp.sum(e, axis=-1, keepdims=True) + jnp.exp(sink - mx)
                pv = _dot(e.astype(BF16), vwin)
                o_ref[rows, h * dim:(h + 1) * dim] = (pv / den).astype(o_ref.dtype)


def _swa(proj3, rel_bias_flat, sinks, q_norm_w, k_norm_w, q_heads, kv_heads, dim, col_q, col_k, col_v):
    b, s, _ = proj3.shape
    blk = SWA_BLOCK
    step = SWA_STEP_BLOCKS * blk
    qw, kvw = q_heads * dim, kv_heads * dim
    smem = pl.BlockSpec(memory_space=pltpu.SMEM)
    pipelined = (_nbytes((step, qw), F32) + 2 * _nbytes((step + blk, kvw), F32) + _nbytes((step, qw), BF16))
    resident = _nbytes((2 * q_heads, blk, 2 * blk), F32) + (8 << 20)

    def prev(n):
        return jnp.maximum(SWA_STEP_BLOCKS * n - 1, 0)

    return pl.pallas_call(
        functools.partial(_swa_kernel, q_heads=q_heads, kv_heads=kv_heads, dim=dim),
        grid=(b, s // step),
        in_specs=[
            smem, smem,
            pl.BlockSpec((None, step, qw), lambda i, n: (i, n, col_q // qw)),
            pl.BlockSpec((None, step, kvw), lambda i, n: (i, n, col_k // kvw)),
            pl.BlockSpec((None, blk, kvw), lambda i, n: (i, prev(n), col_k // kvw)),
            pl.BlockSpec((None, step, kvw), lambda i, n: (i, n, col_v // kvw)),
            pl.BlockSpec((None, blk, kvw), lambda i, n: (i, prev(n), col_v // kvw)),
            pl.BlockSpec((1, dim), lambda i, n: (0, 0)),
            pl.BlockSpec((1, dim), lambda i, n: (0, 0)),
        ],
        out_specs=pl.BlockSpec((None, step, qw), lambda i, n: (i, n, 0)),
        out_shape=jax.ShapeDtypeStruct((b, s, qw), BF16),
        scratch_shapes=[pltpu.VMEM((2 * q_heads, blk, 2 * blk), F32)],
        compiler_params=pltpu.CompilerParams(
            dimension_semantics=("arbitrary", "arbitrary"),
            vmem_limit_bytes=_vmem_limit(pipelined, resident)),
        name="swa",
    )(rel_bias_flat, sinks, proj3, proj3, proj3, proj3, proj3, q_norm_w, k_norm_w)


def _memattn_kernel(*refs, heads, dim, n_riders):
    q_ref, k_ref, v_ref, qw_ref, kw_ref = refs[:5]
    o_ref = refs[5 + n_riders]
    _cast_riders(refs[5:5 + n_riders], refs[6 + n_riders:])
    for h in range(heads):
        cols = slice(h * dim, (h + 1) * dim)
        qn = _rms(q_ref[:, cols], qw_ref[...]).astype(BF16)
        kn = _rms(k_ref[:, cols], kw_ref[...]).astype(BF16)
        lg = _dot_nt(qn, kn) * (dim ** -0.5)
        e = jnp.exp(lg - jnp.max(lg, axis=-1, keepdims=True))
        den = jnp.sum(e, axis=-1, keepdims=True)
        pv = _dot(e.astype(BF16), v_ref[:, cols].astype(BF16))
        o_ref[:, cols] = (pv / den).astype(o_ref.dtype)


def _memattn(proj3, mkv3, q_norm_w, k_norm_w, riders, heads, dim, col_q, tq):
    b, s, _ = proj3.shape
    m = mkv3.shape[1]
    w = heads * dim
    nt = s // tq
    rider_in, rider_out, rider_shapes = _cast_rider_specs(riders, b * nt, lambda i, t: i * nt + t)
    pipelined = (_nbytes((tq, w), F32) + 2 * _nbytes((m, w), F32) + _nbytes((tq, w), BF16)
                 + sum(_nbytes(r.shape, F32) * 3 // 2 for r in riders) // (b * nt))
    return pl.pallas_call(
        functools.partial(_memattn_kernel, heads=heads, dim=dim, n_riders=len(riders)),
        grid=(b, nt),
        in_specs=[
            pl.BlockSpec((pl.Element(tq), pl.Element(w)), lambda i, t: (pl.multiple_of(i * s + t * tq, tq), col_q)),
            pl.BlockSpec((None, m, w), lambda i, t: (i, 0, 0)),
            pl.BlockSpec((None, m, w), lambda i, t: (i, 0, 1)),
            pl.BlockSpec((1, dim), lambda i, t: (0, 0)),
            pl.BlockSpec((1, dim), lambda i, t: (0, 0)),
            *rider_in,
        ],
        out_specs=[pl.BlockSpec((None, tq, w), lambda i, t: (i, t, 0)), *rider_out],
        out_shape=[jax.ShapeDtypeStruct((b, s, w), BF16), *rider_shapes],
        compiler_params=pltpu.CompilerParams(
            dimension_semantics=("parallel", "parallel"),
            vmem_limit_bytes=_vmem_limit(pipelined, 8 << 20)),
        name="memattn",
    )(proj3.reshape(b * s, -1), mkv3, mkv3, q_norm_w, k_norm_w, *riders)


def _merge_kernel(od_ref, os_ref, om_ref, pd_ref, ps_ref, pm_ref, gd_ref, gs_ref, gm_ref, o_ref):
    merged = (gd_ref[...].astype(F32) * _dot(od_ref[...], pd_ref[...])
              + gs_ref[...].astype(F32) * _dot(os_ref[...], ps_ref[...])
              + gm_ref[...].astype(F32) * _dot(om_ref[...], pm_ref[...]))
    o_ref[...] = merged.astype(o_ref.dtype)


def _merge(o_dn, o_swa, o_mem, p_dn, p_swa, p_mem, gates, tm, tn):
    t = o_dn.shape[0]
    d = p_dn.shape[1]
    wd, ws, wm = o_dn.shape[1], o_swa.shape[1], o_mem.shape[1]
    per_branch = d // tn
    pipelined = ((_nbytes((tm, wd), BF16) + _nbytes((tm, ws), BF16) + _nbytes((tm, wm), BF16))
                 + (_nbytes((wd, tn), BF16) + _nbytes((ws, tn), BF16) + _nbytes((wm, tn), BF16))
                 + 4 * _nbytes((tm, tn), BF16))
    return pl.pallas_call(
        _merge_kernel,
        grid=(d // tn, t // tm),
        in_specs=[
            pl.BlockSpec((tm, wd), lambda j, i: (i, 0)),
            pl.BlockSpec((tm, ws), lambda j, i: (i, 0)),
            pl.BlockSpec((tm, wm), lambda j, i: (i, 0)),
            pl.BlockSpec((wd, tn), lambda j, i: (0, j)),
            pl.BlockSpec((ws, tn), lambda j, i: (0, j)),
            pl.BlockSpec((wm, tn), lambda j, i: (0, j)),
            pl.BlockSpec((tm, tn), lambda j, i: (i, j)),
            pl.BlockSpec((tm, tn), lambda j, i: (i, per_branch + j)),
            pl.BlockSpec((tm, tn), lambda j, i: (i, 2 * per_branch + j)),
        ],
        out_specs=pl.BlockSpec((tm, tn), lambda j, i: (i, j)),
        out_shape=jax.ShapeDtypeStruct((t, d), BF16),
        compiler_params=pltpu.CompilerParams(
            dimension_semantics=("parallel", "parallel"),
            vmem_limit_bytes=_vmem_limit(pipelined, 3 * _nbytes((tm, tn), F32))),
        name="merge",
    )(o_dn, o_swa, o_mem, p_dn, p_swa, p_mem, gates, gates, gates)


def _outproj_kernel(x_ref, m_ref, w_ref, nw_ref, x1_ref, h_ref):
    x1 = x_ref[...] + _dot(m_ref[...], w_ref[...])
    x1_ref[...] = x1
    h_ref[...] = _rms(x1, nw_ref[...]).astype(h_ref.dtype)


def _outproj(x, merged, w_out, norm_w, tm):
    t, d = x.shape
    pipelined = 2 * _nbytes((tm, d), F32) + 2 * _nbytes((tm, d), BF16) + _nbytes((d, d), BF16)
    return pl.pallas_call(
        _outproj_kernel,
        grid=(t // tm,),
        in_specs=[
            pl.BlockSpec((tm, d), lambda i: (i, 0)),
            pl.BlockSpec((tm, d), lambda i: (i, 0)),
            pl.BlockSpec((d, d), lambda i: (0, 0)),
            pl.BlockSpec((1, d), lambda i: (0, 0)),
        ],
        out_specs=[pl.BlockSpec((tm, d), lambda i: (i, 0)), pl.BlockSpec((tm, d), lambda i: (i, 0))],
        out_shape=[jax.ShapeDtypeStruct((t, d), F32), jax.ShapeDtypeStruct((t, d), BF16)],
        compiler_params=pltpu.CompilerParams(
            dimension_semantics=("parallel",),
            vmem_limit_bytes=_vmem_limit(pipelined, 2 * _nbytes((tm, d), F32))),
        name="outproj",
    )(x, merged, w_out, norm_w)


def _mlp_kernel(h_ref, x1_ref, wu_ref, wd_ref, o_ref):
    @pl.when(pl.program_id(1) == 0)
    def _():
        o_ref[...] = x1_ref[...]

    a = jnp.maximum(_dot(h_ref[...], wu_ref[...]), 0.0)
    o_ref[...] += _dot((a * a).astype(BF16), wd_ref[...])


def _mlp(h2, x1, w_up, w_down, tm, tf):
    t, d = x1.shape
    f = w_up.shape[1]
    pipelined = (_nbytes((tm, d), BF16) + 2 * _nbytes((tm, d), F32)
                 + _nbytes((d, tf), BF16) + _nbytes((tf, d), BF16))
    resident = 2 * _nbytes((tm, tf), F32)
    return pl.pallas_call(
        _mlp_kernel,
        grid=(t // tm, f // tf),
        in_specs=[
            pl.BlockSpec((tm, d), lambda i, k: (i, 0)),
            pl.BlockSpec((tm, d), lambda i, k: (i, 0)),
            pl.BlockSpec((d, tf), lambda i, k: (0, k)),
            pl.BlockSpec((tf, d), lambda i, k: (k, 0)),
        ],
        out_specs=pl.BlockSpec((tm, d), lambda i, k: (i, 0)),
        out_shape=jax.ShapeDtypeStruct((t, d), F32),
        compiler_params=pltpu.CompilerParams(
            dimension_semantics=("parallel", "arbitrary"),
            vmem_limit_bytes=_vmem_limit(pipelined, resident)),
        name="mlp",
    )(h2, x1, w_up, w_down)


def _layer(x, mem, attn_norm_w, w_in, dn_conv_w, dn_a_log, dn_dt_bias, dn_out_norm_w, swa_q_norm_w,
           swa_k_norm_w, swa_sinks, rel_bias, mem_norm_w, w_mem_kv, xq_norm_w, xk_norm_w,
           p_dn, p_swa, p_mem, w_out, mlp_norm_w, w_mlp_up, w_mlp_down):
    b, s, d = x.shape
    m = mem.shape[1]
    t = b * s

    dn_heads = dn_a_log.shape[0]
    dn_dim = dn_out_norm_w.shape[0]
    dn_w = dn_heads * dn_dim
    swa_heads = swa_sinks.shape[0]
    swa_dim = swa_q_norm_w.shape[0]
    swa_w = swa_heads * swa_dim
    mem_dim = xq_norm_w.shape[0]
    mem_w = p_mem.shape[0]
    mem_heads = mem_w // mem_dim
    swa_kv_w = (w_in.shape[1] - 4 * dn_w - 2 * dn_heads - swa_w - mem_w - 3 * d) // 2
    swa_kv = swa_kv_w // swa_dim

    n_ba = 2 * dn_heads
    src_ba = 4 * dn_w
    src_swa = src_ba + n_ba
    w_t = w_in.T.astype(BF16)
    w_b = w_t[src_swa:]
    w_c = jnp.concatenate([w_t[src_ba:src_swa], jnp.zeros((LANES - n_ba, d), BF16)], axis=0)
    col = {"qkv": 0, "z": 3 * dn_w, "sq": src_ba}
    col["sk"] = col["sq"] + swa_w
    col["sv"] = col["sk"] + swa_kv_w
    col["mq"] = col["sv"] + swa_kv_w

    proj, gates, ba = _inproj(x.reshape(t, d), attn_norm_w.reshape(1, d), w_t, src_ba, w_b, w_c, 3 * d,
                              tm=1024, tn=1024)
    proj3 = proj.reshape(b, s, -1)
    mkv = _norm_matmul(mem.reshape(b * m, d), mem_norm_w.reshape(1, d), w_mem_kv.astype(BF16), tm=512, tn=512)
    mkv3 = mkv.reshape(b, m, -1)

    gate_params = jnp.zeros((SUBLANES, LANES), F32)
    gate_params = gate_params.at[0, dn_heads:2 * dn_heads].set(dn_a_log)
    gate_params = gate_params.at[1, dn_heads:2 * dn_heads].set(dn_dt_bias)
    o_dn, w_up_bf, w_down_bf = _deltanet(proj3, ba.reshape(b, s, LANES), dn_conv_w, gate_params,
                                         dn_out_norm_w.reshape(1, dn_dim), [w_mlp_up, w_mlp_down],
                                         dn_heads, dn_dim, col["z"])
    o_swa = _swa(proj3, rel_bias.reshape(-1), swa_sinks, swa_q_norm_w.reshape(1, swa_dim),
                 swa_k_norm_w.reshape(1, swa_dim), swa_heads, swa_kv, swa_dim, col["sq"], col["sk"], col["sv"])
    o_mem, w_out_bf, p_dn_bf, p_swa_bf, p_mem_bf = _memattn(
        proj3, mkv3, xq_norm_w.reshape(1, mem_dim), xk_norm_w.reshape(1, mem_dim),
        [w_out, p_dn, p_swa, p_mem], mem_heads, mem_dim, col["mq"], tq=512)

    merged = _merge(o_dn.reshape(t, dn_w), o_swa.reshape(t, swa_w), o_mem.reshape(t, mem_w),
                    p_dn_bf, p_swa_bf, p_mem_bf, gates, tm=512, tn=1024)
    x1, h2 = _outproj(x.reshape(t, d), merged, w_out_bf, mlp_norm_w.reshape(1, d), tm=512)
    out = _mlp(h2, x1, w_up_bf, w_down_bf, tm=512, tf=1024)
    return out.reshape(b, s, d)


def kernel(x, mem, attn_norm_w, w_in, dn_conv_w, dn_A_log, dn_dt_bias, dn_out_norm_w, swa_q_norm_w,
           swa_k_norm_w, swa_sinks, rel_bias, mem_norm_w, w_mem_kv, xq_norm_w, xk_norm_w, p_dn, p_swa,
           p_mem, w_out, mlp_norm_w, w_mlp_up, w_mlp_down):
    depth = w_in.shape[0]
    for l in range(depth):
        x = _layer(x, mem, attn_norm_w[l], w_in[l], dn_conv_w[l], dn_A_log[l], dn_dt_bias[l],
                   dn_out_norm_w[l], swa_q_norm_w[l], swa_k_norm_w[l], swa_sinks[l], rel_bias,
                   mem_norm_w[l], w_mem_kv[l], xq_norm_w[l], xk_norm_w[l], p_dn[l], p_swa[l], p_mem[l],
                   w_out[l], mlp_norm_w[l], w_mlp_up[l], w_mlp_down[l])
    return x
```

```python
import functools
import math

import jax
import jax.numpy as jnp
from jax import lax
from jax.experimental import pallas as pl
from jax.experimental.pallas import tpu as pltpu

F32 = jnp.float32
BF16 = jnp.bfloat16

EPS = 1e-6
LANES = 128
SUBLANES = 8
V7X_VMEM_BYTES = 64 * 1024 * 1024

DN_CHUNK = 64
DN_GROUP = 256
SWA_WINDOW = 128
SWA_BLOCK = 128
SWA_STEP_BLOCKS = 4
N_BUCKETS = 32
MAX_DISTANCE = 128

_NT = (((1,), (1,)), ((), ()))
_TN = (((0,), (0,)), ((), ()))


def _vmem_limit(pipelined_bytes, resident_bytes):
    want = 2 * pipelined_bytes + resident_bytes + (4 << 20)
    return int(min(want, V7X_VMEM_BYTES - (8 << 20)))


def _nbytes(shape, dtype):
    return math.prod(shape) * jnp.dtype(dtype).itemsize


def _sigmoid(v):
    return 0.5 * jnp.tanh(0.5 * v) + 0.5


def _silu(v):
    return v * _sigmoid(v)


def _rms(v, w):
    return (v * lax.rsqrt(jnp.mean(v * v, axis=-1, keepdims=True) + EPS)) * w


def _dot(a, b):
    return jnp.dot(a, b, preferred_element_type=F32)


def _dot_nt(a, b):
    return lax.dot_general(a, b, _NT, preferred_element_type=F32)


def _cast_rider_specs(weights, n_steps, step_of):
    in_specs, out_specs, out_shapes = [], [], []
    for w in weights:
        rows = w.shape[0] // n_steps
        assert rows * n_steps == w.shape[0] and rows % (2 * SUBLANES) == 0
        for specs in (in_specs, out_specs):
            specs.append(pl.BlockSpec((rows, w.shape[1]), lambda *ids: (step_of(*ids), 0)))
        out_shapes.append(jax.ShapeDtypeStruct(w.shape, BF16))
    return in_specs, out_specs, out_shapes


def _cast_riders(in_refs, out_refs):
    for i_ref, o_ref in zip(in_refs, out_refs):
        o_ref[...] = i_ref[...].astype(o_ref.dtype)


def _inproj_kernel(x_ref, nw_ref, wa_ref, wb_ref, wc_ref, o_ref, g_ref, ba_ref, h_ref, *, n_a, n_p):
    j = pl.program_id(1)

    @pl.when(j == 0)
    def _():
        h_ref[...] = _rms(x_ref[...], nw_ref[...]).astype(BF16)
        ba_ref[...] = _dot_nt(h_ref[...], wc_ref[...])

    @pl.when(j < n_a)
    def _():
        o_ref[...] = _dot_nt(h_ref[...], wa_ref[...])

    @pl.when((j >= n_a) & (j < n_p))
    def _():
        o_ref[...] = _dot_nt(h_ref[...], wb_ref[...])

    @pl.when(j >= n_p)
    def _():
        g_ref[...] = _sigmoid(_dot_nt(h_ref[...], wb_ref[...])).astype(g_ref.dtype)


def _inproj(x, nw, w_a, rows_a, w_b, w_c, n_gates, tm, tn):
    m, k = x.shape
    rows_b1 = w_b.shape[0] - n_gates
    n = rows_a + rows_b1
    n_a = rows_a // tn
    n_p = pl.cdiv(n, tn)
    assert rows_a % tn == 0 and n_gates % tn == 0 and rows_b1 % (2 * SUBLANES) == 0

    def wb_index(i, j):
        row = jnp.where(j < n_p, jnp.maximum(j - n_a, 0) * tn, rows_b1 + (j - n_p) * tn)
        return (pl.multiple_of(row, 2 * SUBLANES), 0)

    pipelined = (_nbytes((tm, k), F32) + 2 * _nbytes((tn, k), BF16) + _nbytes((tm, tn), F32)
                 + _nbytes((tm, tn), BF16) + _nbytes((tm, LANES), F32))
    resident = _nbytes((tm, k), BF16) + _nbytes((tm, k), F32) + _nbytes((LANES, k), BF16)
    return pl.pallas_call(
        functools.partial(_inproj_kernel, n_a=n_a, n_p=n_p),
        grid=(m // tm, n_p + n_gates // tn),
        in_specs=[
            pl.BlockSpec((tm, k), lambda i, j: (i, 0)),
            pl.BlockSpec((1, k), lambda i, j: (0, 0)),
            pl.BlockSpec((tn, k), lambda i, j: (jnp.minimum(j, n_a - 1), 0)),
            pl.BlockSpec((pl.Element(tn), pl.Element(k)), wb_index),
            pl.BlockSpec((LANES, k), lambda i, j: (0, 0)),
        ],
        out_specs=[pl.BlockSpec((tm, tn), lambda i, j: (i, jnp.minimum(j, n_p - 1))),
                   pl.BlockSpec((tm, tn), lambda i, j: (i, jnp.maximum(j - n_p, 0))),
                   pl.BlockSpec((tm, LANES), lambda i, j: (i, 0))],
        out_shape=[jax.ShapeDtypeStruct((m, n), F32), jax.ShapeDtypeStruct((m, n_gates), BF16),
                   jax.ShapeDtypeStruct((m, LANES), F32)],
        scratch_shapes=[pltpu.VMEM((tm, k), BF16)],
        compiler_params=pltpu.CompilerParams(
            dimension_semantics=("parallel", "arbitrary"),
            vmem_limit_bytes=_vmem_limit(pipelined, resident)),
        name="inproj",
    )(x, nw, w_a, w_b, w_c)


def _norm_matmul_kernel(x_ref, nw_ref, w_ref, o_ref, h_ref):
    @pl.when(pl.program_id(1) == 0)
    def _():
        h_ref[...] = _rms(x_ref[...], nw_ref[...]).astype(BF16)

    o_ref[...] = _dot(h_ref[...], w_ref[...]).astype(o_ref.dtype)


def _norm_matmul(x, nw, w, tm, tn):
    m, k = x.shape
    n = w.shape[1]
    pipelined = _nbytes((tm, k), F32) + _nbytes((k, tn), BF16) + _nbytes((tm, tn), F32)
    resident = _nbytes((tm, k), BF16) + _nbytes((tm, k), F32)
    return pl.pallas_call(
        _norm_matmul_kernel,
        grid=(m // tm, n // tn),
        in_specs=[
            pl.BlockSpec((tm, k), lambda i, j: (i, 0)),
            pl.BlockSpec((1, k), lambda i, j: (0, 0)),
            pl.BlockSpec((k, tn), lambda i, j: (0, j)),
        ],
        out_specs=pl.BlockSpec((tm, tn), lambda i, j: (i, j)),
        out_shape=jax.ShapeDtypeStruct((m, n), F32),
        scratch_shapes=[pltpu.VMEM((tm, k), BF16)],
        compiler_params=pltpu.CompilerParams(
            dimension_semantics=("parallel", "arbitrary"),
            vmem_limit_bytes=_vmem_limit(pipelined, resident)),
        name="norm_matmul",
    )(x, nw, w)


def _dn_kernel(*refs, heads, dim, n_riders):
    qkv_ref, z_ref, ba_ref, cw_ref, gp_ref, onw_ref = refs[:6]
    o_ref = refs[6 + n_riders]
    ext_ref, state_ref = refs[-2:]
    _cast_riders(refs[6:6 + n_riders], refs[7 + n_riders:7 + 2 * n_riders])

    gt = DN_GROUP
    c = DN_CHUNK
    hd = heads * dim
    hs = range(heads)
    chunks = range(gt // c)

    @pl.when(pl.program_id(1) == 0)
    def _():
        ext_ref[0:SUBLANES, :] = jnp.zeros((SUBLANES, 3 * hd), F32)
        state_ref[...] = jnp.zeros_like(state_ref)

    ext_ref[SUBLANES:SUBLANES + gt, :] = qkv_ref[...]
    xe = ext_ref[...]
    conv = cw_ref[3:4, :] * xe[SUBLANES:]
    for s in (1, 2, 3):
        conv = conv + cw_ref[3 - s:4 - s, :] * pltpu.roll(xe, s, 0)[SUBLANES:]
    ext_ref[0:SUBLANES, :] = xe[gt:gt + SUBLANES]
    act = _silu(conv)

    ba = ba_ref[...]
    beta_all = _sigmoid(ba)
    xa = ba + gp_ref[1:2, :]
    softplus = jnp.maximum(xa, 0.0) + jnp.log1p(jnp.exp(-jnp.abs(xa)))
    g_all = -jnp.exp(gp_ref[0:1, :]) * softplus
    row_in_chunk = lax.broadcasted_iota(jnp.int32, (gt, LANES), 0) & (c - 1)
    gcum = g_all
    s = 1
    while s < c:
        gcum = gcum + jnp.where(row_in_chunk >= s, pltpu.roll(gcum, s, 0), 0.0)
        s *= 2
    exp_g = jnp.exp(gcum)
    kdec_parts, gc_rows = [], []
    for j in chunks:
        g_last = gcum[c * j + c - 1:c * j + c, :]
        kdec_parts.append(jnp.exp(g_last - gcum[c * j:c * (j + 1), :]))
        gc_rows.append(jnp.exp(g_last))
    kdec = jnp.concatenate(kdec_parts, axis=0)
    gcum_t = gcum.T

    ri = lax.broadcasted_iota(jnp.int32, (gt, gt), 0)
    ci = lax.broadcasted_iota(jnp.int32, (gt, gt), 1)
    same_chunk = (ri // c) == (ci // c)
    strict = same_chunk & (ri > ci)
    incl = same_chunk & (ri >= ci)

    qn, kn, kb, beta, eg = [], [], [], [], []
    for h in hs:
        qh = act[:, h * dim:(h + 1) * dim]
        kh = act[:, hd + h * dim:hd + (h + 1) * dim]
        qn.append(qh * lax.rsqrt(jnp.sum(qh * qh, axis=-1, keepdims=True) + EPS) * (dim ** -0.5))
        kn.append(kh * lax.rsqrt(jnp.sum(kh * kh, axis=-1, keepdims=True) + EPS))
        kb.append(kn[h].astype(BF16))
        beta.append(beta_all[:, h:h + 1])
        eg.append(exp_g[:, heads + h:heads + h + 1])

    decay = [jnp.exp(jnp.where(incl, gcum[:, heads + h:heads + h + 1] - gcum_t[heads + h:heads + h + 1, :],
                               -jnp.inf)) for h in hs]
    kk = [_dot_nt(kb[h], kb[h]) for h in hs]
    pw = [jnp.where(strict, (beta[h] * kk[h]) * decay[h], 0.0).astype(BF16) for h in hs]

    sol = [jnp.concatenate([act[:, 2 * hd + h * dim:2 * hd + (h + 1) * dim] * beta[h],
                            kn[h] * (beta[h] * eg[h])], axis=1) for h in hs]
    sol = [sol[h] - _dot(pw[h], sol[h].astype(BF16)) for h in hs]
    order = 2
    while order < c:
        pw = [_dot(pw[h], pw[h]).astype(BF16) for h in hs]
        sol = [sol[h] + _dot(pw[h], sol[h].astype(BF16)) for h in hs]
        order *= 2

    pmat = [(_dot_nt(qn[h].astype(BF16), kb[h]) * decay[h]).astype(BF16) for h in hs]
    wq = [[jnp.concatenate([sol[h][c * j:c * (j + 1), dim:], (qn[h] * eg[h])[c * j:c * (j + 1)]],
                           axis=0).astype(BF16) for j in chunks] for h in hs]
    kd = [(kn[h] * kdec[:, heads + h:heads + h + 1]).astype(BF16) for h in hs]

    state = [state_ref[h] for h in hs]
    o_parts = [[] for _ in hs]
    for j in chunks:
        r = slice(c * j, c * (j + 1))
        ws = [_dot(wq[h][j], state[h].astype(BF16)) for h in hs]
        db = [(sol[h][r, :dim] - ws[h][:c]).astype(BF16) for h in hs]
        for h in hs:
            o_parts[h].append(ws[h][c:] + _dot(pmat[h][r, r], db[h]))
        state = [gc_rows[j][:, heads + h:heads + h + 1] * state[h]
                 + lax.dot_general(kd[h][r], db[h], _TN, preferred_element_type=F32) for h in hs]
    for h in hs:
        state_ref[h] = state[h]

    for h in hs:
        o = jnp.concatenate(o_parts[h], axis=0)
        zh = z_ref[:, h * dim:(h + 1) * dim]
        o_ref[:, h * dim:(h + 1) * dim] = (_rms(o, onw_ref[...]) * _silu(zh)).astype(o_ref.dtype)


def _deltanet(proj3, ba3, conv_w, gate_params, out_norm_w, riders, heads, dim, col_z):
    b, s, _ = proj3.shape
    hd = heads * dim
    gt = DN_GROUP
    ng = s // gt
    rider_in, rider_out, rider_shapes = _cast_rider_specs(riders, b * ng, lambda i, g: i * ng + g)
    pipelined = (_nbytes((gt, 3 * hd), F32) + _nbytes((gt, hd), F32) + _nbytes((gt, LANES), F32)
                 + _nbytes((gt, hd), BF16) + sum(_nbytes(w.shape, F32) * 3 // 2 for w in riders) // (b * ng))
    resident = (_nbytes((gt + SUBLANES, 3 * hd), F32) * 4 + _nbytes((heads, dim, dim), F32)
                + heads * 6 * _nbytes((gt, gt), F32))
    return pl.pallas_call(
        functools.partial(_dn_kernel, heads=heads, dim=dim, n_riders=len(riders)),
        grid=(b, ng),
        in_specs=[
            pl.BlockSpec((None, gt, 3 * hd), lambda i, g: (i, g, 0)),
            pl.BlockSpec((None, gt, hd), lambda i, g: (i, g, col_z // hd)),
            pl.BlockSpec((None, gt, LANES), lambda i, g: (i, g, 0)),
            pl.BlockSpec((4, 3 * hd), lambda i, g: (0, 0)),
            pl.BlockSpec((SUBLANES, LANES), lambda i, g: (0, 0)),
            pl.BlockSpec((1, dim), lambda i, g: (0, 0)),
            *rider_in,
        ],
        out_specs=[pl.BlockSpec((None, gt, hd), lambda i, g: (i, g, 0)), *rider_out],
        out_shape=[jax.ShapeDtypeStruct((b, s, hd), BF16), *rider_shapes],
        scratch_shapes=[pltpu.VMEM((gt + SUBLANES, 3 * hd), F32), pltpu.VMEM((heads, dim, dim), F32)],
        compiler_params=pltpu.CompilerParams(
            dimension_semantics=("parallel", "arbitrary"),
            vmem_limit_bytes=_vmem_limit(pipelined, resident)),
        name="deltanet",
    )(proj3, proj3, ba3, conv_w, gate_params, out_norm_w, *riders)


def _swa_kernel(rb_ref, sink_ref, q_ref, kc_ref, kp_ref, vc_ref, vp_ref, qw_ref, kw_ref, o_ref, bias_ref,
                *, q_heads, kv_heads, dim):
    blk = SWA_BLOCK
    n = pl.program_id(1)
    qi = lax.broadcasted_iota(jnp.int32, (blk, 2 * blk), 0)
    kj = lax.broadcasted_iota(jnp.int32, (blk, 2 * blk), 1)
    dist = qi - kj + blk

    @pl.when((pl.program_id(0) == 0) & (n == 0))
    def _():
        max_exact = N_BUCKETS // 2
        nn = jnp.maximum(dist, 0)
        nf = jnp.maximum(nn, 1).astype(F32)
        large = max_exact + (jnp.log(nf / max_exact) / math.log(MAX_DISTANCE / max_exact)
                             * (N_BUCKETS - max_exact)).astype(jnp.int32)
        bucket = jnp.where(nn < max_exact, nn, jnp.minimum(large, N_BUCKETS - 1))
        for h in range(q_heads):
            acc = jnp.zeros((blk, 2 * blk), F32)
            for bk in range(N_BUCKETS):
                acc = jnp.where(bucket == bk, rb_ref[bk * q_heads + h], acc)
            acc = jnp.where((dist >= 0) & (dist < SWA_WINDOW), acc, -jnp.inf)
            bias_ref[h] = acc
            bias_ref[q_heads + h] = jnp.where(kj >= blk, acc, -jnp.inf)

    first = jnp.where(n == 0, q_heads, 0)
    group = q_heads // kv_heads
    for j in range(kv_heads):
        cols = slice(j * dim, (j + 1) * dim)
        k_all = _rms(jnp.concatenate([kp_ref[:, cols], kc_ref[:, cols]], axis=0), kw_ref[...]).astype(BF16)
        v_all = jnp.concatenate([vp_ref[:, cols], vc_ref[:, cols]], axis=0).astype(BF16)
        for u in range(SWA_STEP_BLOCKS):
            rows = slice(u * blk, (u + 1) * blk)
            kwin, vwin = k_all[u * blk:(u + 2) * blk], v_all[u * blk:(u + 2) * blk]
            qcat = jnp.concatenate(
                [_rms(q_ref[rows, (j * group + i) * dim:(j * group + i + 1) * dim], qw_ref[...])
                 for i in range(group)], axis=0).astype(BF16)
            logits = _dot_nt(qcat, kwin) * (dim ** -0.5)
            for i in range(group):
                h = j * group + i
                lg = logits[i * blk:(i + 1) * blk] + bias_ref[(first if u == 0 else 0) + h]
                sink = sink_ref[h]
                mx = jnp.maximum(jnp.max(lg, axis=-1, keepdims=True), sink)
                e = jnp.exp(lg - mx)
                den = jnp.sum(e, axis=-1, keepdims=True) + jnp.exp(sink - mx)
                pv = _dot(e.astype(BF16), vwin)
                o_ref[rows, h * dim:(h + 1) * dim] = (pv / den).astype(o_ref.dtype)


def _swa(proj3, rel_bias_flat, sinks, q_norm_w, k_norm_w, q_heads, kv_heads, dim, col_q, col_k, col_v):
    b, s, _ = proj3.shape
    blk = SWA_BLOCK
    step = SWA_STEP_BLOCKS * blk
    qw, kvw = q_heads * dim, kv_heads * dim
    smem = pl.BlockSpec(memory_space=pltpu.SMEM)
    pipelined = (_nbytes((step, qw), F32) + 2 * _nbytes((step + blk, kvw), F32) + _nbytes((step, qw), BF16))
    resident = _nbytes((2 * q_heads, blk, 2 * blk), F32) + (8 << 20)

    def prev(n):
        return jnp.maximum(SWA_STEP_BLOCKS * n - 1, 0)

    return pl.pallas_call(
        functools.partial(_swa_kernel, q_heads=q_heads, kv_heads=kv_heads, dim=dim),
        grid=(b, s // step),
        in_specs=[
            smem, smem,
            pl.BlockSpec((None, step, qw), lambda i, n: (i, n, col_q // qw)),
            pl.BlockSpec((None, step, kvw), lambda i, n: (i, n, col_k // kvw)),
            pl.BlockSpec((None, blk, kvw), lambda i, n: (i, prev(n), col_k // kvw)),
            pl.BlockSpec((None, step, kvw), lambda i, n: (i, n, col_v // kvw)),
            pl.BlockSpec((None, blk, kvw), lambda i, n: (i, prev(n), col_v // kvw)),
            pl.BlockSpec((1, dim), lambda i, n: (0, 0)),
            pl.BlockSpec((1, dim), lambda i, n: (0, 0)),
        ],
        out_specs=pl.BlockSpec((None, step, qw), lambda i, n: (i, n, 0)),
        out_shape=jax.ShapeDtypeStruct((b, s, qw), BF16),
        scratch_shapes=[pltpu.VMEM((2 * q_heads, blk, 2 * blk), F32)],
        compiler_params=pltpu.CompilerParams(
            dimension_semantics=("arbitrary", "arbitrary"),
            vmem_limit_bytes=_vmem_limit(pipelined, resident)),
        name="swa",
    )(rel_bias_flat, sinks, proj3, proj3, proj3, proj3, proj3, q_norm_w, k_norm_w)


def _memattn_kernel(*refs, heads, dim, n_riders):
    q_ref, k_ref, v_ref, qw_ref, kw_ref = refs[:5]
    o_ref = refs[5 + n_riders]
    _cast_riders(refs[5:5 + n_riders], refs[6 + n_riders:])
    for h in range(heads):
        cols = slice(h * dim, (h + 1) * dim)
        qn = _rms(q_ref[:, cols], qw_ref[...]).astype(BF16)
        kn = _rms(k_ref[:, cols], kw_ref[...]).astype(BF16)
        lg = _dot_nt(qn, kn) * (dim ** -0.5)
        e = jnp.exp(lg - jnp.max(lg, axis=-1, keepdims=True))
        den = jnp.sum(e, axis=-1, keepdims=True)
        pv = _dot(e.astype(BF16), v_ref[:, cols].astype(BF16))
        o_ref[:, cols] = (pv / den).astype(o_ref.dtype)


def _memattn(proj3, mkv3, q_norm_w, k_norm_w, riders, heads, dim, col_q, tq):
    b, s, _ = proj3.shape
    m = mkv3.shape[1]
    w = heads * dim
    nt = s // tq
    rider_in, rider_out, rider_shapes = _cast_rider_specs(riders, b * nt, lambda i, t: i * nt + t)
    pipelined = (_nbytes((tq, w), F32) + 2 * _nbytes((m, w), F32) + _nbytes((tq, w), BF16)
                 + sum(_nbytes(r.shape, F32) * 3 // 2 for r in riders) // (b * nt))
    return pl.pallas_call(
        functools.partial(_memattn_kernel, heads=heads, dim=dim, n_riders=len(riders)),
        grid=(b, nt),
        in_specs=[
            pl.BlockSpec((pl.Element(tq), pl.Element(w)), lambda i, t: (pl.multiple_of(i * s + t * tq, tq), col_q)),
            pl.BlockSpec((None, m, w), lambda i, t: (i, 0, 0)),
            pl.BlockSpec((None, m, w), lambda i, t: (i, 0, 1)),
            pl.BlockSpec((1, dim), lambda i, t: (0, 0)),
            pl.BlockSpec((1, dim), lambda i, t: (0, 0)),
            *rider_in,
        ],
        out_specs=[pl.BlockSpec((None, tq, w), lambda i, t: (i, t, 0)), *rider_out],
        out_shape=[jax.ShapeDtypeStruct((b, s, w), BF16), *rider_shapes],
        compiler_params=pltpu.CompilerParams(
            dimension_semantics=("parallel", "parallel"),
            vmem_limit_bytes=_vmem_limit(pipelined, 8 << 20)),
        name="memattn",
    )(proj3.reshape(b * s, -1), mkv3, mkv3, q_norm_w, k_norm_w, *riders)


def _merge_kernel(od_ref, os_ref, om_ref, pd_ref, ps_ref, pm_ref, gd_ref, gs_ref, gm_ref, o_ref):
    merged = (gd_ref[...].astype(F32) * _dot(od_ref[...], pd_ref[...])
              + gs_ref[...].astype(F32) * _dot(os_ref[...], ps_ref[...])
              + gm_ref[...].astype(F32) * _dot(om_ref[...], pm_ref[...]))
    o_ref[...] = merged.astype(o_ref.dtype)


def _merge(o_dn, o_swa, o_mem, p_dn, p_swa, p_mem, gates, tm, tn):
    t = o_dn.shape[0]
    d = p_dn.shape[1]
    wd, ws, wm = o_dn.shape[1], o_swa.shape[1], o_mem.shape[1]
    per_branch = d // tn
    pipelined = ((_nbytes((tm, wd), BF16) + _nbytes((tm, ws), BF16) + _nbytes((tm, wm), BF16))
                 + (_nbytes((wd, tn), BF16) + _nbytes((ws, tn), BF16) + _nbytes((wm, tn), BF16))
                 + 4 * _nbytes((tm, tn), BF16))
    return pl.pallas_call(
        _merge_kernel,
        grid=(d // tn, t // tm),
        in_specs=[
            pl.BlockSpec((tm, wd), lambda j, i: (i, 0)),
            pl.BlockSpec((tm, ws), lambda j, i: (i, 0)),
            pl.BlockSpec((tm, wm), lambda j, i: (i, 0)),
            pl.BlockSpec((wd, tn), lambda j, i: (0, j)),
            pl.BlockSpec((ws, tn), lambda j, i: (0, j)),
            pl.BlockSpec((wm, tn), lambda j, i: (0, j)),
            pl.BlockSpec((tm, tn), lambda j, i: (i, j)),
            pl.BlockSpec((tm, tn), lambda j, i: (i, per_branch + j)),
            pl.BlockSpec((tm, tn), lambda j, i: (i, 2 * per_branch + j)),
        ],
        out_specs=pl.BlockSpec((tm, tn), lambda j, i: (i, j)),
        out_shape=jax.ShapeDtypeStruct((t, d), BF16),
        compiler_params=pltpu.CompilerParams(
            dimension_semantics=("parallel", "parallel"),
            vmem_limit_bytes=_vmem_limit(pipelined, 3 * _nbytes((tm, tn), F32))),
        name="merge",
    )(o_dn, o_swa, o_mem, p_dn, p_swa, p_mem, gates, gates, gates)


def _outproj_kernel(x_ref, m_ref, w_ref, nw_ref, x1_ref, h_ref):
    x1 = x_ref[...] + _dot(m_ref[...], w_ref[...])
    x1_ref[...] = x1
    h_ref[...] = _rms(x1, nw_ref[...]).astype(h_ref.dtype)


def _outproj(x, merged, w_out, norm_w, tm):
    t, d = x.shape
    pipelined = 2 * _nbytes((tm, d), F32) + 2 * _nbytes((tm, d), BF16) + _nbytes((d, d), BF16)
    return pl.pallas_call(
        _outproj_kernel,
        grid=(t // tm,),
        in_specs=[
            pl.BlockSpec((tm, d), lambda i: (i, 0)),
            pl.BlockSpec((tm, d), lambda i: (i, 0)),
            pl.BlockSpec((d, d), lambda i: (0, 0)),
            pl.BlockSpec((1, d), lambda i: (0, 0)),
        ],
        out_specs=[pl.BlockSpec((tm, d), lambda i: (i, 0)), pl.BlockSpec((tm, d), lambda i: (i, 0))],
        out_shape=[jax.ShapeDtypeStruct((t, d), F32), jax.ShapeDtypeStruct((t, d), BF16)],
        compiler_params=pltpu.CompilerParams(
            dimension_semantics=("parallel",),
            vmem_limit_bytes=_vmem_limit(pipelined, 2 * _nbytes((tm, d), F32))),
        name="outproj",
    )(x, merged, w_out, norm_w)


def _mlp_kernel(h_ref, x1_ref, wu_ref, wd_ref, o_ref):
    @pl.when(pl.program_id(1) == 0)
    def _():
        o_ref[...] = x1_ref[...]

    a = jnp.maximum(_dot(h_ref[...], wu_ref[...]), 0.0)
    o_ref[...] += _dot((a * a).astype(BF16), wd_ref[...])


def _mlp(h2, x1, w_up, w_down, tm, tf):
    t, d = x1.shape
    f = w_up.shape[1]
    pipelined = (_nbytes((tm, d), BF16) + 2 * _nbytes((tm, d), F32)
                 + _nbytes((d, tf), BF16) + _nbytes((tf, d), BF16))
    resident = 2 * _nbytes((tm, tf), F32)
    return pl.pallas_call(
        _mlp_kernel,
        grid=(t // tm, f // tf),
        in_specs=[
            pl.BlockSpec((tm, d), lambda i, k: (i, 0)),
            pl.BlockSpec((tm, d), lambda i, k: (i, 0)),
            pl.BlockSpec((d, tf), lambda i, k: (0, k)),
            pl.BlockSpec((tf, d), lambda i, k: (k, 0)),
        ],
        out_specs=pl.BlockSpec((tm, d), lambda i, k: (i, 0)),
        out_shape=jax.ShapeDtypeStruct((t, d), F32),
        compiler_params=pltpu.CompilerParams(
            dimension_semantics=("parallel", "arbitrary"),
            vmem_limit_bytes=_vmem_limit(pipelined, resident)),
        name="mlp",
    )(h2, x1, w_up, w_down)


def _layer(x, mem, attn_norm_w, w_in, dn_conv_w, dn_a_log, dn_dt_bias, dn_out_norm_w, swa_q_norm_w,
           swa_k_norm_w, swa_sinks, rel_bias, mem_norm_w, w_mem_kv, xq_norm_w, xk_norm_w,
           p_dn, p_swa, p_mem, w_out, mlp_norm_w, w_mlp_up, w_mlp_down):
    b, s, d = x.shape
    m = mem.shape[1]
    t = b * s

    dn_heads = dn_a_log.shape[0]
    dn_dim = dn_out_norm_w.shape[0]
    dn_w = dn_heads * dn_dim
    swa_heads = swa_sinks.shape[0]
    swa_dim = swa_q_norm_w.shape[0]
    swa_w = swa_heads * swa_dim
    mem_dim = xq_norm_w.shape[0]
    mem_w = p_mem.shape[0]
    mem_heads = mem_w // mem_dim
    swa_kv_w = (w_in.shape[1] - 4 * dn_w - 2 * dn_heads - swa_w - mem_w - 3 * d) // 2
    swa_kv = swa_kv_w // swa_dim

    n_ba = 2 * dn_heads
    src_ba = 4 * dn_w
    src_swa = src_ba + n_ba
    w_t = w_in.T.astype(BF16)
    w_b = w_t[src_swa:]
    w_c = jnp.concatenate([w_t[src_ba:src_swa], jnp.zeros((LANES - n_ba, d), BF16)], axis=0)
    col = {"qkv": 0, "z": 3 * dn_w, "sq": src_ba}
    col["sk"] = col["sq"] + swa_w
    col["sv"] = col["sk"] + swa_kv_w
    col["mq"] = col["sv"] + swa_kv_w

    proj, gates, ba = _inproj(x.reshape(t, d), attn_norm_w.reshape(1, d), w_t, src_ba, w_b, w_c, 3 * d,
                              tm=1024, tn=1024)
    proj3 = proj.reshape(b, s, -1)
    mkv = _norm_matmul(mem.reshape(b * m, d), mem_norm_w.reshape(1, d), w_mem_kv.astype(BF16), tm=512, tn=512)
    mkv3 = mkv.reshape(b, m, -1)

    gate_params = jnp.zeros((SUBLANES, LANES), F32)
    gate_params = gate_params.at[0, dn_heads:2 * dn_heads].set(dn_a_log)
    gate_params = gate_params.at[1, dn_heads:2 * dn_heads].set(dn_dt_bias)
    o_dn, w_up_bf, w_down_bf = _deltanet(proj3, ba.reshape(b, s, LANES), dn_conv_w, gate_params,
                                         dn_out_norm_w.reshape(1, dn_dim), [w_mlp_up, w_mlp_down],
                                         dn_heads, dn_dim, col["z"])
    o_swa = _swa(proj3, rel_bias.reshape(-1), swa_sinks, swa_q_norm_w.reshape(1, swa_dim),
                 swa_k_norm_w.reshape(1, swa_dim), swa_heads, swa_kv, swa_dim, col["sq"], col["sk"], col["sv"])
    o_mem, w_out_bf, p_dn_bf, p_swa_bf, p_mem_bf = _memattn(
        proj3, mkv3, xq_norm_w.reshape(1, mem_dim), xk_norm_w.reshape(1, mem_dim),
        [w_out, p_dn, p_swa, p_mem], mem_heads, mem_dim, col["mq"], tq=512)

    merged = _merge(o_dn.reshape(t, dn_w), o_swa.reshape(t, swa_w), o_mem.reshape(t, mem_w),
                    p_dn_bf, p_swa_bf, p_mem_bf, gates, tm=1024, tn=1024)
    x1, h2 = _outproj(x.reshape(t, d), merged, w_out_bf, mlp_norm_w.reshape(1, d), tm=512)
    out = _mlp(h2, x1, w_up_bf, w_down_bf, tm=1024, tf=512)
    return out.reshape(b, s, d)


def kernel(x, mem, attn_norm_w, w_in, dn_conv_w, dn_A_log, dn_dt_bias, dn_out_norm_w, swa_q_norm_w,
           swa_k_norm_w, swa_sinks, rel_bias, mem_norm_w, w_mem_kv, xq_norm_w, xk_norm_w, p_dn, p_swa,
           p_mem, w_out, mlp_norm_w, w_mlp_up, w_mlp_down):
    depth = w_in.shape[0]
    for l in range(depth):
        x = _layer(x, mem, attn_norm_w[l], w_in[l], dn_conv_w[l], dn_A_log[l], dn_dt_bias[l],
                   dn_out_norm_w[l], swa_q_norm_w[l], swa_k_norm_w[l], swa_sinks[l], rel_bias,
                   mem_norm_w[l], w_mem_kv[l], xq_norm_w[l], xk_norm_w[l], p_dn[l], p_swa[l], p_mem[l],
                   w_out[l], mlp_norm_w[l], w_mlp_up[l], w_mlp_down[l])
    return x
```

```python
import functools
import math

import jax
import jax.numpy as jnp
from jax import lax
from jax.experimental import pallas as pl
from jax.experimental.pallas import tpu as pltpu

F32 = jnp.float32
BF16 = jnp.bfloat16

EPS = 1e-6
LANES = 128
SUBLANES = 8
V7X_VMEM_BYTES = 64 * 1024 * 1024

DN_CHUNK = 64
DN_GROUP = 256
SWA_WINDOW = 128
SWA_BLOCK = 128
SWA_STEP_BLOCKS = 4
N_BUCKETS = 32
MAX_DISTANCE = 128

_NT = (((1,), (1,)), ((), ()))
_TN = (((0,), (0,)), ((), ()))


def _vmem_limit(pipelined_bytes, resident_bytes):
    want = 2 * pipelined_bytes + resident_bytes + (4 << 20)
    return int(min(want, V7X_VMEM_BYTES - (8 << 20)))


def _nbytes(shape, dtype):
    return math.prod(shape) * jnp.dtype(dtype).itemsize


def _sigmoid(v):
    return 0.5 * jnp.tanh(0.5 * v) + 0.5


def _silu(v):
    return v * _sigmoid(v)


def _rms(v, w):
    return (v * lax.rsqrt(jnp.mean(v * v, axis=-1, keepdims=True) + EPS)) * w


def _dot(a, b):
    return jnp.dot(a, b, preferred_element_type=F32)


def _dot_nt(a, b):
    return lax.dot_general(a, b, _NT, preferred_element_type=F32)


def _cast_rider_specs(weights, n_steps, step_of):
    in_specs, out_specs, out_shapes = [], [], []
    for w in weights:
        rows = w.shape[0] // n_steps
        assert rows * n_steps == w.shape[0] and rows % (2 * SUBLANES) == 0
        for specs in (in_specs, out_specs):
            specs.append(pl.BlockSpec((rows, w.shape[1]), lambda *ids: (step_of(*ids), 0)))
        out_shapes.append(jax.ShapeDtypeStruct(w.shape, BF16))
    return in_specs, out_specs, out_shapes


def _cast_riders(in_refs, out_refs):
    for i_ref, o_ref in zip(in_refs, out_refs):
        o_ref[...] = i_ref[...].astype(o_ref.dtype)


def _inproj_kernel(x_ref, nw_ref, wa_ref, wb_ref, wc_ref, o_ref, g_ref, ba_ref, h_ref, *, n_a, n_p):
    j = pl.program_id(1)

    @pl.when(j == 0)
    def _():
        h_ref[...] = _rms(x_ref[...], nw_ref[...]).astype(BF16)
        ba_ref[...] = _dot_nt(h_ref[...], wc_ref[...])

    @pl.when(j < n_a)
    def _():
        o_ref[...] = _dot_nt(h_ref[...], wa_ref[...])

    @pl.when((j >= n_a) & (j < n_p))
    def _():
        o_ref[...] = _dot_nt(h_ref[...], wb_ref[...])

    @pl.when(j >= n_p)
    def _():
        g_ref[...] = _sigmoid(_dot_nt(h_ref[...], wb_ref[...])).astype(g_ref.dtype)


def _inproj(x, nw, w_a, rows_a, w_b, w_c, n_gates, tm, tn):
    m, k = x.shape
    rows_b1 = w_b.shape[0] - n_gates
    n = rows_a + rows_b1
    n_a = rows_a // tn
    n_p = pl.cdiv(n, tn)
    assert rows_a % tn == 0 and n_gates % tn == 0 and rows_b1 % (2 * SUBLANES) == 0

    def wb_index(i, j):
        row = jnp.where(j < n_p, jnp.maximum(j - n_a, 0) * tn, rows_b1 + (j - n_p) * tn)
        return (pl.multiple_of(row, 2 * SUBLANES), 0)

    pipelined = (_nbytes((tm, k), F32) + 2 * _nbytes((tn, k), BF16) + _nbytes((tm, tn), F32)
                 + _nbytes((tm, tn), BF16) + _nbytes((tm, LANES), F32))
    resident = _nbytes((tm, k), BF16) + _nbytes((tm, k), F32) + _nbytes((LANES, k), BF16)
    return pl.pallas_call(
        functools.partial(_inproj_kernel, n_a=n_a, n_p=n_p),
        grid=(m // tm, n_p + n_gates // tn),
        in_specs=[
            pl.BlockSpec((tm, k), lambda i, j: (i, 0)),
            pl.BlockSpec((1, k), lambda i, j: (0, 0)),
            pl.BlockSpec((tn, k), lambda i, j: (jnp.minimum(j, n_a - 1), 0)),
            pl.BlockSpec((pl.Element(tn), pl.Element(k)), wb_index),
            pl.BlockSpec((LANES, k), lambda i, j: (0, 0)),
        ],
        out_specs=[pl.BlockSpec((tm, tn), lambda i, j: (i, jnp.minimum(j, n_p - 1))),
                   pl.BlockSpec((tm, tn), lambda i, j: (i, jnp.maximum(j - n_p, 0))),
                   pl.BlockSpec((tm, LANES), lambda i, j: (i, 0))],
        out_shape=[jax.ShapeDtypeStruct((m, n), F32), jax.ShapeDtypeStruct((m, n_gates), BF16),
                   jax.ShapeDtypeStruct((m, LANES), F32)],
        scratch_shapes=[pltpu.VMEM((tm, k), BF16)],
        compiler_params=pltpu.CompilerParams(
            dimension_semantics=("parallel", "arbitrary"),
            vmem_limit_bytes=_vmem_limit(pipelined, resident)),
        name="inproj",
    )(x, nw, w_a, w_b, w_c)


def _norm_matmul_kernel(x_ref, nw_ref, w_ref, o_ref, h_ref):
    @pl.when(pl.program_id(1) == 0)
    def _():
        h_ref[...] = _rms(x_ref[...], nw_ref[...]).astype(BF16)

    o_ref[...] = _dot(h_ref[...], w_ref[...]).astype(o_ref.dtype)


def _norm_matmul(x, nw, w, tm, tn):
    m, k = x.shape
    n = w.shape[1]
    pipelined = _nbytes((tm, k), F32) + _nbytes((k, tn), BF16) + _nbytes((tm, tn), F32)
    resident = _nbytes((tm, k), BF16) + _nbytes((tm, k), F32)
    return pl.pallas_call(
        _norm_matmul_kernel,
        grid=(m // tm, n // tn),
        in_specs=[
            pl.BlockSpec((tm, k), lambda i, j: (i, 0)),
            pl.BlockSpec((1, k), lambda i, j: (0, 0)),
            pl.BlockSpec((k, tn), lambda i, j: (0, j)),
        ],
        out_specs=pl.BlockSpec((tm, tn), lambda i, j: (i, j)),
        out_shape=jax.ShapeDtypeStruct((m, n), F32),
        scratch_shapes=[pltpu.VMEM((tm, k), BF16)],
        compiler_params=pltpu.CompilerParams(
            dimension_semantics=("parallel", "arbitrary"),
            vmem_limit_bytes=_vmem_limit(pipelined, resident)),
        name="norm_matmul",
    )(x, nw, w)


def _dn_kernel(*refs, heads, dim, n_riders):
    qkv_ref, z_ref, ba_ref, cw_ref, gp_ref, onw_ref = refs[:6]
    o_ref = refs[6 + n_riders]
    ext_ref, state_ref = refs[-2:]
    _cast_riders(refs[6:6 + n_riders], refs[7 + n_riders:7 + 2 * n_riders])

    gt = DN_GROUP
    c = DN_CHUNK
    hd = heads * dim
    hs = range(heads)
    chunks = range(gt // c)

    @pl.when(pl.program_id(1) == 0)
    def _():
        ext_ref[0:SUBLANES, :] = jnp.zeros((SUBLANES, 3 * hd), F32)
        state_ref[...] = jnp.zeros_like(state_ref)

    ext_ref[SUBLANES:SUBLANES + gt, :] = qkv_ref[...]
    xe = ext_ref[...]
    conv = cw_ref[3:4, :] * xe[SUBLANES:]
    for s in (1, 2, 3):
        conv = conv + cw_ref[3 - s:4 - s, :] * pltpu.roll(xe, s, 0)[SUBLANES:]
    ext_ref[0:SUBLANES, :] = xe[gt:gt + SUBLANES]
    act = _silu(conv)

    ba = ba_ref[...]
    beta_all = _sigmoid(ba)
    xa = ba + gp_ref[1:2, :]
    softplus = jnp.maximum(xa, 0.0) + jnp.log1p(jnp.exp(-jnp.abs(xa)))
    g_all = -jnp.exp(gp_ref[0:1, :]) * softplus
    row_in_chunk = lax.broadcasted_iota(jnp.int32, (gt, LANES), 0) & (c - 1)
    gcum = g_all
    s = 1
    while s < c:
        gcum = gcum + jnp.where(row_in_chunk >= s, pltpu.roll(gcum, s, 0), 0.0)
        s *= 2
    exp_g = jnp.exp(gcum)
    kdec_parts, gc_rows = [], []
    for j in chunks:
        g_last = gcum[c * j + c - 1:c * j + c, :]
        kdec_parts.append(jnp.exp(g_last - gcum[c * j:c * (j + 1), :]))
        gc_rows.append(jnp.exp(g_last))
    kdec = jnp.concatenate(kdec_parts, axis=0)
    gcum_t = gcum.T

    ri = lax.broadcasted_iota(jnp.int32, (gt, gt), 0)
    ci = lax.broadcasted_iota(jnp.int32, (gt, gt), 1)
    same_chunk = (ri // c) == (ci // c)
    strict = same_chunk & (ri > ci)
    incl = same_chunk & (ri >= ci)

    qn, kn, kb, beta, eg = [], [], [], [], []
    for h in hs:
        qh = act[:, h * dim:(h + 1) * dim]
        kh = act[:, hd + h * dim:hd + (h + 1) * dim]
        qn.append(qh * lax.rsqrt(jnp.sum(qh * qh, axis=-1, keepdims=True) + EPS) * (dim ** -0.5))
        kn.append(kh * lax.rsqrt(jnp.sum(kh * kh, axis=-1, keepdims=True) + EPS))
        kb.append(kn[h].astype(BF16))
        beta.append(beta_all[:, h:h + 1])
        eg.append(exp_g[:, heads + h:heads + h + 1])

    decay = [jnp.exp(jnp.where(incl, gcum[:, heads + h:heads + h + 1] - gcum_t[heads + h:heads + h + 1, :],
                               -jnp.inf)) for h in hs]
    kk = [_dot_nt(kb[h], kb[h]) for h in hs]
    pw = [jnp.where(strict, (beta[h] * kk[h]) * decay[h], 0.0).astype(BF16) for h in hs]

    sol = [jnp.concatenate([act[:, 2 * hd + h * dim:2 * hd + (h + 1) * dim] * beta[h],
                            kn[h] * (beta[h] * eg[h])], axis=1) for h in hs]
    sol = [sol[h] - _dot(pw[h], sol[h].astype(BF16)) for h in hs]
    order = 2
    while order < c:
        pw = [_dot(pw[h], pw[h]).astype(BF16) for h in hs]
        sol = [sol[h] + _dot(pw[h], sol[h].astype(BF16)) for h in hs]
        order *= 2

    pmat = [(_dot_nt(qn[h].astype(BF16), kb[h]) * decay[h]).astype(BF16) for h in hs]
    wq = [[jnp.concatenate([sol[h][c * j:c * (j + 1), dim:], (qn[h] * eg[h])[c * j:c * (j + 1)]],
                           axis=0).astype(BF16) for j in chunks] for h in hs]
    kd = [(kn[h] * kdec[:, heads + h:heads + h + 1]).astype(BF16) for h in hs]

    state = [state_ref[h] for h in hs]
    o_parts = [[] for _ in hs]
    for j in chunks:
        r = slice(c * j, c * (j + 1))
        ws = [_dot(wq[h][j], state[h].astype(BF16)) for h in hs]
        db = [(sol[h][r, :dim] - ws[h][:c]).astype(BF16) for h in hs]
        for h in hs:
            o_parts[h].append(ws[h][c:] + _dot(pmat[h][r, r], db[h]))
        state = [gc_rows[j][:, heads + h:heads + h + 1] * state[h]
                 + lax.dot_general(kd[h][r], db[h], _TN, preferred_element_type=F32) for h in hs]
    for h in hs:
        state_ref[h] = state[h]

    for h in hs:
        o = jnp.concatenate(o_parts[h], axis=0)
        zh = z_ref[:, h * dim:(h + 1) * dim]
        o_ref[:, h * dim:(h + 1) * dim] = (_rms(o, onw_ref[...]) * _silu(zh)).astype(o_ref.dtype)


def _deltanet(proj3, ba3, conv_w, gate_params, out_norm_w, riders, heads, dim, col_z):
    b, s, _ = proj3.shape
    hd = heads * dim
    gt = DN_GROUP
    ng = s // gt
    rider_in, rider_out, rider_shapes = _cast_rider_specs(riders, b * ng, lambda i, g: i * ng + g)
    pipelined = (_nbytes((gt, 3 * hd), F32) + _nbytes((gt, hd), F32) + _nbytes((gt, LANES), F32)
                 + _nbytes((gt, hd), BF16) + sum(_nbytes(w.shape, F32) * 3 // 2 for w in riders) // (b * ng))
    resident = (_nbytes((gt + SUBLANES, 3 * hd), F32) * 4 + _nbytes((heads, dim, dim), F32)
                + heads * 6 * _nbytes((gt, gt), F32))
    return pl.pallas_call(
        functools.partial(_dn_kernel, heads=heads, dim=dim, n_riders=len(riders)),
        grid=(b, ng),
        in_specs=[
            pl.BlockSpec((None, gt, 3 * hd), lambda i, g: (i, g, 0)),
            pl.BlockSpec((None, gt, hd), lambda i, g: (i, g, col_z // hd)),
            pl.BlockSpec((None, gt, LANES), lambda i, g: (i, g, 0)),
            pl.BlockSpec((4, 3 * hd), lambda i, g: (0, 0)),
            pl.BlockSpec((SUBLANES, LANES), lambda i, g: (0, 0)),
            pl.BlockSpec((1, dim), lambda i, g: (0, 0)),
            *rider_in,
        ],
        out_specs=[pl.BlockSpec((None, gt, hd), lambda i, g: (i, g, 0)), *rider_out],
        out_shape=[jax.ShapeDtypeStruct((b, s, hd), BF16), *rider_shapes],
        scratch_shapes=[pltpu.VMEM((gt + SUBLANES, 3 * hd), F32), pltpu.VMEM((heads, dim, dim), F32)],
        compiler_params=pltpu.CompilerParams(
            dimension_semantics=("parallel", "arbitrary"),
            vmem_limit_bytes=_vmem_limit(pipelined, resident)),
        name="deltanet",
    )(proj3, proj3, ba3, conv_w, gate_params, out_norm_w, *riders)


def _swa_kernel(rb_ref, sink_ref, q_ref, kc_ref, kp_ref, vc_ref, vp_ref, qw_ref, kw_ref, o_ref, bias_ref,
                *, q_heads, kv_heads, dim):
    blk = SWA_BLOCK
    n = pl.program_id(1)
    qi = lax.broadcasted_iota(jnp.int32, (blk, 2 * blk), 0)
    kj = lax.broadcasted_iota(jnp.int32, (blk, 2 * blk), 1)
    dist = qi - kj + blk

    @pl.when((pl.program_id(0) == 0) & (n == 0))
    def _():
        max_exact = N_BUCKETS // 2
        nn = jnp.maximum(dist, 0)
        nf = jnp.maximum(nn, 1).astype(F32)
        large = max_exact + (jnp.log(nf / max_exact) / math.log(MAX_DISTANCE / max_exact)
                             * (N_BUCKETS - max_exact)).astype(jnp.int32)
        bucket = jnp.where(nn < max_exact, nn, jnp.minimum(large, N_BUCKETS - 1))
        for h in range(q_heads):
            acc = jnp.zeros((blk, 2 * blk), F32)
            for bk in range(N_BUCKETS):
                acc = jnp.where(bucket == bk, rb_ref[bk * q_heads + h], acc)
            acc = jnp.where((dist >= 0) & (dist < SWA_WINDOW), acc, -jnp.inf)
            bias_ref[h] = acc
            bias_ref[q_heads + h] = jnp.where(kj >= blk, acc, -jnp.inf)

    first = jnp.where(n == 0, q_heads, 0)
    group = q_heads // kv_heads
    for j in range(kv_heads):
        cols = slice(j * dim, (j + 1) * dim)
        k_all = _rms(jnp.concatenate([kp_ref[:, cols], kc_ref[:, cols]], axis=0), kw_ref[...]).astype(BF16)
        v_all = jnp.concatenate([vp_ref[:, cols], vc_ref[:, cols]], axis=0).astype(BF16)
        for u in range(SWA_STEP_BLOCKS):
            rows = slice(u * blk, (u + 1) * blk)
            kwin, vwin = k_all[u * blk:(u + 2) * blk], v_all[u * blk:(u + 2) * blk]
            qcat = jnp.concatenate(
                [_rms(q_ref[rows, (j * group + i) * dim:(j * group + i + 1) * dim], qw_ref[...])
                 for i in range(group)], axis=0).astype(BF16)
            logits = _dot_nt(qcat, kwin) * (dim ** -0.5)
            for i in range(group):
                h = j * group + i
                lg = logits[i * blk:(i + 1) * blk] + bias_ref[(first if u == 0 else 0) + h]
                sink = sink_ref[h]
                mx = jnp.maximum(jnp.max(lg, axis=-1, keepdims=True), sink)
                e = jnp.exp(lg - mx)
                den = jnp.sum(e, axis=-1, keepdims=True) + jnp.exp(sink - mx)
                pv = _dot(e.astype(BF16), vwin)
                o_ref[rows, h * dim:(h + 1) * dim] = (pv / den).astype(o_ref.dtype)


def _swa(proj3, rel_bias_flat, sinks, q_norm_w, k_norm_w, q_heads, kv_heads, dim, col_q, col_k, col_v):
    b, s, _ = proj3.shape
    blk = SWA_BLOCK
    step = SWA_STEP_BLOCKS * blk
    qw, kvw = q_heads * dim, kv_heads * dim
    smem = pl.BlockSpec(memory_space=pltpu.SMEM)
    pipelined = (_nbytes((step, qw), F32) + 2 * _nbytes((step + blk, kvw), F32) + _nbytes((step, qw), BF16))
    resident = _nbytes((2 * q_heads, blk, 2 * blk), F32) + (8 << 20)

    def prev(n):
        return jnp.maximum(SWA_STEP_BLOCKS * n - 1, 0)

    return pl.pallas_call(
        functools.partial(_swa_kernel, q_heads=q_heads, kv_heads=kv_heads, dim=dim),
        grid=(b, s // step),
        in_specs=[
            smem, smem,
            pl.BlockSpec((None, step, qw), lambda i, n: (i, n, col_q // qw)),
            pl.BlockSpec((None, step, kvw), lambda i, n: (i, n, col_k // kvw)),
            pl.BlockSpec((None, blk, kvw), lambda i, n: (i, prev(n), col_k // kvw)),
            pl.BlockSpec((None, step, kvw), lambda i, n: (i, n, col_v // kvw)),
            pl.BlockSpec((None, blk, kvw), lambda i, n: (i, prev(n), col_v // kvw)),
            pl.BlockSpec((1, dim), lambda i, n: (0, 0)),
            pl.BlockSpec((1, dim), lambda i, n: (0, 0)),
        ],
        out_specs=pl.BlockSpec((None, step, qw), lambda i, n: (i, n, 0)),
        out_shape=jax.ShapeDtypeStruct((b, s, qw), BF16),
        scratch_shapes=[pltpu.VMEM((2 * q_heads, blk, 2 * blk), F32)],
        compiler_params=pltpu.CompilerParams(
            dimension_semantics=("arbitrary", "arbitrary"),
            vmem_limit_bytes=_vmem_limit(pipelined, resident)),
        name="swa",
    )(rel_bias_flat, sinks, proj3, proj3, proj3, proj3, proj3, q_norm_w, k_norm_w)


def _memattn_kernel(*refs, heads, dim, n_riders):
    q_ref, k_ref, v_ref, qw_ref, kw_ref = refs[:5]
    o_ref = refs[5 + n_riders]
    _cast_riders(refs[5:5 + n_riders], refs[6 + n_riders:])
    for h in range(heads):
        cols = slice(h * dim, (h + 1) * dim)
        qn = _rms(q_ref[:, cols], qw_ref[...]).astype(BF16)
        kn = _rms(k_ref[:, cols], kw_ref[...]).astype(BF16)
        lg = _dot_nt(qn, kn) * (dim ** -0.5)
        e = jnp.exp(lg - jnp.max(lg, axis=-1, keepdims=True))
        den = jnp.sum(e, axis=-1, keepdims=True)
        pv = _dot(e.astype(BF16), v_ref[:, cols].astype(BF16))
        o_ref[:, cols] = (pv / den).astype(o_ref.dtype)


def _memattn(proj3, mkv3, q_norm_w, k_norm_w, riders, heads, dim, col_q, tq):
    b, s, _ = proj3.shape
    m = mkv3.shape[1]
    w = heads * dim
    nt = s // tq
    rider_in, rider_out, rider_shapes = _cast_rider_specs(riders, b * nt, lambda i, t: i * nt + t)
    pipelined = (_nbytes((tq, w), F32) + 2 * _nbytes((m, w), F32) + _nbytes((tq, w), BF16)
                 + sum(_nbytes(r.shape, F32) * 3 // 2 for r in riders) // (b * nt))
    return pl.pallas_call(
        functools.partial(_memattn_kernel, heads=heads, dim=dim, n_riders=len(riders)),
        grid=(b, nt),
        in_specs=[
            pl.BlockSpec((pl.Element(tq), pl.Element(w)), lambda i, t: (pl.multiple_of(i * s + t * tq, tq), col_q)),
            pl.BlockSpec((None, m, w), lambda i, t: (i, 0, 0)),
            pl.BlockSpec((None, m, w), lambda i, t: (i, 0, 1)),
            pl.BlockSpec((1, dim), lambda i, t: (0, 0)),
            pl.BlockSpec((1, dim), lambda i, t: (0, 0)),
            *rider_in,
        ],
        out_specs=[pl.BlockSpec((None, tq, w), lambda i, t: (i, t, 0)), *rider_out],
        out_shape=[jax.ShapeDtypeStruct((b, s, w), BF16), *rider_shapes],
        compiler_params=pltpu.CompilerParams(
            dimension_semantics=("parallel", "parallel"),
            vmem_limit_bytes=_vmem_limit(pipelined, 8 << 20)),
        name="memattn",
    )(proj3.reshape(b * s, -1), mkv3, mkv3, q_norm_w, k_norm_w, *riders)


def _merge_kernel(od_ref, os_ref, om_ref, pd_ref, ps_ref, pm_ref, gd_ref, gs_ref, gm_ref, o_ref):
    merged = (gd_ref[...].astype(F32) * _dot(od_ref[...], pd_ref[...])
              + gs_ref[...].astype(F32) * _dot(os_ref[...], ps_ref[...])
              + gm_ref[...].astype(F32) * _dot(om_ref[...], pm_ref[...]))
    o_ref[...] = merged.astype(o_ref.dtype)


def _merge(o_dn, o_swa, o_mem, p_dn, p_swa, p_mem, gates, tm, tn):
    t = o_dn.shape[0]
    d = p_dn.shape[1]
    wd, ws, wm = o_dn.shape[1], o_swa.shape[1], o_mem.shape[1]
    per_branch = d // tn
    pipelined = ((_nbytes((tm, wd), BF16) + _nbytes((tm, ws), BF16) + _nbytes((tm, wm), BF16))
                 + (_nbytes((wd, tn), BF16) + _nbytes((ws, tn), BF16) + _nbytes((wm, tn), BF16))
                 + 4 * _nbytes((tm, tn), BF16))
    return pl.pallas_call(
        _merge_kernel,
        grid=(d // tn, t // tm),
        in_specs=[
            pl.BlockSpec((tm, wd), lambda j, i: (i, 0)),
            pl.BlockSpec((tm, ws), lambda j, i: (i, 0)),
            pl.BlockSpec((tm, wm), lambda j, i: (i, 0)),
            pl.BlockSpec((wd, tn), lambda j, i: (0, j)),
            pl.BlockSpec((ws, tn), lambda j, i: (0, j)),
            pl.BlockSpec((wm, tn), lambda j, i: (0, j)),
            pl.BlockSpec((tm, tn), lambda j, i: (i, j)),
            pl.BlockSpec((tm, tn), lambda j, i: (i, per_branch + j)),
            pl.BlockSpec((tm, tn), lambda j, i: (i, 2 * per_branch + j)),
        ],
        out_specs=pl.BlockSpec((tm, tn), lambda j, i: (i, j)),
        out_shape=jax.ShapeDtypeStruct((t, d), BF16),
        compiler_params=pltpu.CompilerParams(
            dimension_semantics=("parallel", "parallel"),
            vmem_limit_bytes=_vmem_limit(pipelined, 3 * _nbytes((tm, tn), F32))),
        name="merge",
    )(o_dn, o_swa, o_mem, p_dn, p_swa, p_mem, gates, gates, gates)


def _outproj_kernel(x_ref, m_ref, w_ref, nw_ref, x1_ref, h_ref):
    x1 = x_ref[...] + _dot(m_ref[...], w_ref[...])
    x1_ref[...] = x1
    h_ref[...] = _rms(x1, nw_ref[...]).astype(h_ref.dtype)


def _outproj(x, merged, w_out, norm_w, tm):
    t, d = x.shape
    pipelined = 2 * _nbytes((tm, d), F32) + 2 * _nbytes((tm, d), BF16) + _nbytes((d, d), BF16)
    return pl.pallas_call(
        _outproj_kernel,
        grid=(t // tm,),
        in_specs=[
            pl.BlockSpec((tm, d), lambda i: (i, 0)),
            pl.BlockSpec((tm, d), lambda i: (i, 0)),
            pl.BlockSpec((d, d), lambda i: (0, 0)),
            pl.BlockSpec((1, d), lambda i: (0, 0)),
        ],
        out_specs=[pl.BlockSpec((tm, d), lambda i: (i, 0)), pl.BlockSpec((tm, d), lambda i: (i, 0))],
        out_shape=[jax.ShapeDtypeStruct((t, d), F32), jax.ShapeDtypeStruct((t, d), BF16)],
        compiler_params=pltpu.CompilerParams(
            dimension_semantics=("parallel",),
            vmem_limit_bytes=_vmem_limit(pipelined, 2 * _nbytes((tm, d), F32))),
        name="outproj",
    )(x, merged, w_out, norm_w)


def _mlp_kernel(h_ref, x1_ref, wu_ref, wd_ref, o_ref):
    @pl.when(pl.program_id(1) == 0)
    def _():
        o_ref[...] = x1_ref[...]

    a = jnp.maximum(_dot(h_ref[...], wu_ref[...]), 0.0)
    o_ref[...] += _dot((a * a).astype(BF16), wd_ref[...])


def _mlp(h2, x1, w_up, w_down, tm, tf):
    t, d = x1.shape
    f = w_up.shape[1]
    pipelined = (_nbytes((tm, d), BF16) + 2 * _nbytes((tm, d), F32)
                 + _nbytes((d, tf), BF16) + _nbytes((tf, d), BF16))
    resident = 2 * _nbytes((tm, tf), F32)
    return pl.pallas_call(
        _mlp_kernel,
        grid=(t // tm, f // tf),
        in_specs=[
            pl.BlockSpec((tm, d), lambda i, k: (i, 0)),
            pl.BlockSpec((tm, d), lambda i, k: (i, 0)),
            pl.BlockSpec((d, tf), lambda i, k: (0, k)),
            pl.BlockSpec((tf, d), lambda i, k: (k, 0)),
        ],
        out_specs=pl.BlockSpec((tm, d), lambda i, k: (i, 0)),
        out_shape=jax.ShapeDtypeStruct((t, d), F32),
        compiler_params=pltpu.CompilerParams(
            dimension_semantics=("parallel", "arbitrary"),
            vmem_limit_bytes=_vmem_limit(pipelined, resident)),
        name="mlp",
    )(h2, x1, w_up, w_down)


def _layer(x, mem, attn_norm_w, w_in, dn_conv_w, dn_a_log, dn_dt_bias, dn_out_norm_w, swa_q_norm_w,
           swa_k_norm_w, swa_sinks, rel_bias, mem_norm_w, w_mem_kv, xq_norm_w, xk_norm_w,
           p_dn, p_swa, p_mem, w_out, mlp_norm_w, w_mlp_up, w_mlp_down):
    b, s, d = x.shape
    m = mem.shape[1]
    t = b * s

    dn_heads = dn_a_log.shape[0]
    dn_dim = dn_out_norm_w.shape[0]
    dn_w = dn_heads * dn_dim
    swa_heads = swa_sinks.shape[0]
    swa_dim = swa_q_norm_w.shape[0]
    swa_w = swa_heads * swa_dim
    mem_dim = xq_norm_w.shape[0]
    mem_w = p_mem.shape[0]
    mem_heads = mem_w // mem_dim
    swa_kv_w = (w_in.shape[1] - 4 * dn_w - 2 * dn_heads - swa_w - mem_w - 3 * d) // 2
    swa_kv = swa_kv_w // swa_dim

    n_ba = 2 * dn_heads
    src_ba = 4 * dn_w
    src_swa = src_ba + n_ba
    w_t = w_in.T.astype(BF16)
    w_b = w_t[src_swa:]
    w_c = jnp.concatenate([w_t[src_ba:src_swa], jnp.zeros((LANES - n_ba, d), BF16)], axis=0)
    col = {"qkv": 0, "z": 3 * dn_w, "sq": src_ba}
    col["sk"] = col["sq"] + swa_w
    col["sv"] = col["sk"] + swa_kv_w
    col["mq"] = col["sv"] + swa_kv_w

    proj, gates, ba = _inproj(x.reshape(t, d), attn_norm_w.reshape(1, d), w_t, src_ba, w_b, w_c, 3 * d,
                              tm=1024, tn=1024)
    proj3 = proj.reshape(b, s, -1)
    mkv = _norm_matmul(mem.reshape(b * m, d), mem_norm_w.reshape(1, d), w_mem_kv.astype(BF16), tm=512, tn=512)
    mkv3 = mkv.reshape(b, m, -1)

    gate_params = jnp.zeros((SUBLANES, LANES), F32)
    gate_params = gate_params.at[0, dn_heads:2 * dn_heads].set(dn_a_log)
    gate_params = gate_params.at[1, dn_heads:2 * dn_heads].set(dn_dt_bias)
    o_dn, w_up_bf, w_down_bf = _deltanet(proj3, ba.reshape(b, s, LANES), dn_conv_w, gate_params,
                                         dn_out_norm_w.reshape(1, dn_dim), [w_mlp_up, w_mlp_down],
                                         dn_heads, dn_dim, col["z"])
    o_swa = _swa(proj3, rel_bias.reshape(-1), swa_sinks, swa_q_norm_w.reshape(1, swa_dim),
                 swa_k_norm_w.reshape(1, swa_dim), swa_heads, swa_kv, swa_dim, col["sq"], col["sk"], col["sv"])
    o_mem, w_out_bf, p_dn_bf, p_swa_bf, p_mem_bf = _memattn(
        proj3, mkv3, xq_norm_w.reshape(1, mem_dim), xk_norm_w.reshape(1, mem_dim),
        [w_out, p_dn, p_swa, p_mem], mem_heads, mem_dim, col["mq"], tq=1024)

    merged = _merge(o_dn.reshape(t, dn_w), o_swa.reshape(t, swa_w), o_mem.reshape(t, mem_w),
                    p_dn_bf, p_swa_bf, p_mem_bf, gates, tm=1024, tn=1024)
    x1, h2 = _outproj(x.reshape(t, d), merged, w_out_bf, mlp_norm_w.reshape(1, d), tm=512)
    out = _mlp(h2, x1, w_up_bf, w_down_bf, tm=512, tf=1024)
    return out.reshape(b, s, d)


def kernel(x, mem, attn_norm_w, w_in, dn_conv_w, dn_A_log, dn_dt_bias, dn_out_norm_w, swa_q_norm_w,
           swa_k_norm_w, swa_sinks, rel_bias, mem_norm_w, w_mem_kv, xq_norm_w, xk_norm_w, p_dn, p_swa,
           p_mem, w_out, mlp_norm_w, w_mlp_up, w_mlp_down):
    depth = w_in.shape[0]
    for l in range(depth):
        x = _layer(x, mem, attn_norm_w[l], w_in[l], dn_conv_w[l], dn_A_log[l], dn_dt_bias[l],
                   dn_out_norm_w[l], swa_q_norm_w[l], swa_k_norm_w[l], swa_sinks[l], rel_bias,
                   mem_norm_w[l], w_mem_kv[l], xq_norm_w[l], xk_norm_w[l], p_dn[l], p_swa[l], p_mem[l],
                   w_out[l], mlp_norm_w[l], w_mlp_up[l], w_mlp_down[l])
    return x
```

```python
import functools
import math

import jax
import jax.numpy as jnp
from jax import lax
from jax.experimental import pallas as pl
from jax.experimental.pallas import tpu as pltpu

F32 = jnp.float32
BF16 = jnp.bfloat16

EPS = 1e-6
LANES = 128
SUBLANES = 8
V7X_VMEM_BYTES = 64 * 1024 * 1024

DN_CHUNK = 64
DN_GROUP = 256
DN_BASE = 8
SWA_WINDOW = 128
SWA_BLOCK = 128
SWA_STEP_BLOCKS = 4
N_BUCKETS = 32
MAX_DISTANCE = 128

_NT = (((1,), (1,)), ((), ()))
_TN = (((0,), (0,)), ((), ()))


def _vmem_limit(pipelined_bytes, resident_bytes):
    want = 2 * pipelined_bytes + resident_bytes + (4 << 20)
    return int(min(want, V7X_VMEM_BYTES - (8 << 20)))


def _nbytes(shape, dtype):
    return math.prod(shape) * jnp.dtype(dtype).itemsize


def _sigmoid(v):
    return 0.5 * jnp.tanh(0.5 * v) + 0.5


def _silu(v):
    return v * _sigmoid(v)


def _rms(v, w):
    return (v * lax.rsqrt(jnp.mean(v * v, axis=-1, keepdims=True) + EPS)) * w


def _dot(a, b):
    return jnp.dot(a, b, preferred_element_type=F32)


def _dot_nt(a, b):
    return lax.dot_general(a, b, _NT, preferred_element_type=F32)


def _cast_rider_specs(weights, n_steps, step_of):
    in_specs, out_specs, out_shapes = [], [], []
    for w in weights:
        rows = w.shape[0] // n_steps
        assert rows * n_steps == w.shape[0] and rows % (2 * SUBLANES) == 0
        for specs in (in_specs, out_specs):
            specs.append(pl.BlockSpec((rows, w.shape[1]), lambda *ids: (step_of(*ids), 0)))
        out_shapes.append(jax.ShapeDtypeStruct(w.shape, BF16))
    return in_specs, out_specs, out_shapes


def _cast_riders(in_refs, out_refs):
    for i_ref, o_ref in zip(in_refs, out_refs):
        o_ref[...] = i_ref[...].astype(o_ref.dtype)


def _inproj_kernel(x_ref, nw_ref, wa_ref, wb_ref, wc_ref, o_ref, g_ref, ba_ref, h_ref, *, n_a, n_p):
    j = pl.program_id(1)

    @pl.when(j == 0)
    def _():
        h_ref[...] = _rms(x_ref[...], nw_ref[...]).astype(BF16)
        ba_ref[...] = _dot_nt(h_ref[...], wc_ref[...])

    @pl.when(j < n_a)
    def _():
        o_ref[...] = _dot_nt(h_ref[...], wa_ref[...])

    @pl.when((j >= n_a) & (j < n_p))
    def _():
        o_ref[...] = _dot_nt(h_ref[...], wb_ref[...])

    @pl.when(j >= n_p)
    def _():
        g_ref[...] = _sigmoid(_dot_nt(h_ref[...], wb_ref[...])).astype(g_ref.dtype)


def _inproj(x, nw, w_a, rows_a, w_b, w_c, n_gates, tm, tn):
    m, k = x.shape
    rows_b1 = w_b.shape[0] - n_gates
    n = rows_a + rows_b1
    n_a = rows_a // tn
    n_p = pl.cdiv(n, tn)
    assert rows_a % tn == 0 and n_gates % tn == 0 and rows_b1 % (2 * SUBLANES) == 0

    def wb_index(i, j):
        row = jnp.where(j < n_p, jnp.maximum(j - n_a, 0) * tn, rows_b1 + (j - n_p) * tn)
        return (pl.multiple_of(row, 2 * SUBLANES), 0)

    pipelined = (_nbytes((tm, k), F32) + 2 * _nbytes((tn, k), BF16) + _nbytes((tm, tn), F32)
                 + _nbytes((tm, tn), BF16) + _nbytes((tm, LANES), F32))
    resident = _nbytes((tm, k), BF16) + _nbytes((tm, k), F32) + _nbytes((LANES, k), BF16)
    return pl.pallas_call(
        functools.partial(_inproj_kernel, n_a=n_a, n_p=n_p),
        grid=(m // tm, n_p + n_gates // tn),
        in_specs=[
            pl.BlockSpec((tm, k), lambda i, j: (i, 0)),
            pl.BlockSpec((1, k), lambda i, j: (0, 0)),
            pl.BlockSpec((tn, k), lambda i, j: (jnp.minimum(j, n_a - 1), 0)),
            pl.BlockSpec((pl.Element(tn), pl.Element(k)), wb_index),
            pl.BlockSpec((LANES, k), lambda i, j: (0, 0)),
        ],
        out_specs=[pl.BlockSpec((tm, tn), lambda i, j: (i, jnp.minimum(j, n_p - 1))),
                   pl.BlockSpec((tm, tn), lambda i, j: (i, jnp.maximum(j - n_p, 0))),
                   pl.BlockSpec((tm, LANES), lambda i, j: (i, 0))],
        out_shape=[jax.ShapeDtypeStruct((m, n), F32), jax.ShapeDtypeStruct((m, n_gates), BF16),
                   jax.ShapeDtypeStruct((m, LANES), F32)],
        scratch_shapes=[pltpu.VMEM((tm, k), BF16)],
        compiler_params=pltpu.CompilerParams(
            dimension_semantics=("parallel", "arbitrary"),
            vmem_limit_bytes=_vmem_limit(pipelined, resident)),
        name="inproj",
    )(x, nw, w_a, w_b, w_c)


def _norm_matmul_kernel(x_ref, nw_ref, w_ref, o_ref, h_ref):
    @pl.when(pl.program_id(1) == 0)
    def _():
        h_ref[...] = _rms(x_ref[...], nw_ref[...]).astype(BF16)

    o_ref[...] = _dot(h_ref[...], w_ref[...]).astype(o_ref.dtype)


def _norm_matmul(x, nw, w, tm, tn):
    m, k = x.shape
    n = w.shape[1]
    pipelined = _nbytes((tm, k), F32) + _nbytes((k, tn), BF16) + _nbytes((tm, tn), F32)
    resident = _nbytes((tm, k), BF16) + _nbytes((tm, k), F32)
    return pl.pallas_call(
        _norm_matmul_kernel,
        grid=(m // tm, n // tn),
        in_specs=[
            pl.BlockSpec((tm, k), lambda i, j: (i, 0)),
            pl.BlockSpec((1, k), lambda i, j: (0, 0)),
            pl.BlockSpec((k, tn), lambda i, j: (0, j)),
        ],
        out_specs=pl.BlockSpec((tm, tn), lambda i, j: (i, j)),
        out_shape=jax.ShapeDtypeStruct((m, n), F32),
        scratch_shapes=[pltpu.VMEM((tm, k), BF16)],
        compiler_params=pltpu.CompilerParams(
            dimension_semantics=("parallel", "arbitrary"),
            vmem_limit_bytes=_vmem_limit(pipelined, resident)),
        name="norm_matmul",
    )(x, nw, w)


def _dn_kernel(*refs, heads, dim, n_riders):
    qkv_ref, z_ref, ba_ref, cw_ref, gp_ref, onw_ref = refs[:6]
    o_ref = refs[6 + n_riders]
    ext_ref, state_ref = refs[-2:]
    _cast_riders(refs[6:6 + n_riders], refs[7 + n_riders:7 + 2 * n_riders])

    gt = DN_GROUP
    c = DN_CHUNK
    hd = heads * dim
    hs = range(heads)
    chunks = range(gt // c)

    @pl.when(pl.program_id(1) == 0)
    def _():
        ext_ref[0:SUBLANES, :] = jnp.zeros((SUBLANES, 3 * hd), F32)
        state_ref[...] = jnp.zeros_like(state_ref)

    ext_ref[SUBLANES:SUBLANES + gt, :] = qkv_ref[...]
    xe = ext_ref[...]
    conv = cw_ref[3:4, :] * xe[SUBLANES:]
    for s in (1, 2, 3):
        conv = conv + cw_ref[3 - s:4 - s, :] * pltpu.roll(xe, s, 0)[SUBLANES:]
    ext_ref[0:SUBLANES, :] = xe[gt:gt + SUBLANES]
    act = _silu(conv)

    ba = ba_ref[...]
    beta_all = _sigmoid(ba)
    xa = ba + gp_ref[1:2, :]
    softplus = jnp.maximum(xa, 0.0) + jnp.log1p(jnp.exp(-jnp.abs(xa)))
    g_all = -jnp.exp(gp_ref[0:1, :]) * softplus
    row_in_chunk = lax.broadcasted_iota(jnp.int32, (gt, LANES), 0) & (c - 1)
    gcum = g_all
    s = 1
    while s < c:
        gcum = gcum + jnp.where(row_in_chunk >= s, pltpu.roll(gcum, s, 0), 0.0)
        s *= 2
    exp_g = jnp.exp(gcum)
    kdec_parts, gc_rows = [], []
    for j in chunks:
        g_last = gcum[c * j + c - 1:c * j + c, :]
        kdec_parts.append(jnp.exp(g_last - gcum[c * j:c * (j + 1), :]))
        gc_rows.append(jnp.exp(g_last))
    kdec = jnp.concatenate(kdec_parts, axis=0)
    gcum_t = gcum.T

    ri = lax.broadcasted_iota(jnp.int32, (gt, gt), 0)
    ci = lax.broadcasted_iota(jnp.int32, (gt, gt), 1)
    same_chunk = (ri // c) == (ci // c)
    strict = same_chunk & (ri > ci)
    incl = same_chunk & (ri >= ci)

    qn, kn, kb, beta, eg = [], [], [], [], []
    for h in hs:
        qh = act[:, h * dim:(h + 1) * dim]
        kh = act[:, hd + h * dim:hd + (h + 1) * dim]
        qn.append(qh * lax.rsqrt(jnp.sum(qh * qh, axis=-1, keepdims=True) + EPS) * (dim ** -0.5))
        kn.append(kh * lax.rsqrt(jnp.sum(kh * kh, axis=-1, keepdims=True) + EPS))
        kb.append(kn[h].astype(BF16))
        beta.append(beta_all[:, h:h + 1])
        eg.append(exp_g[:, heads + h:heads + h + 1])

    decay = [jnp.exp(jnp.where(incl, gcum[:, heads + h:heads + h + 1] - gcum_t[heads + h:heads + h + 1, :],
                               -jnp.inf)) for h in hs]
    kk = [_dot_nt(kb[h], kb[h]) for h in hs]
    nmat = [jnp.where(strict, (beta[h] * kk[h]) * decay[h], 0.0) for h in hs]

    def same_block(size):
        return (ri // size) == (ci // size)

    eye = jnp.where(ri == ci, 1.0, 0.0)
    base = same_block(DN_BASE)
    pw = [jnp.where(base, nmat[h], 0.0) for h in hs]
    tinv = [eye - pw[h] for h in hs]
    pw = [pw[h].astype(BF16) for h in hs]
    order = 2
    while order < DN_BASE:
        pw = [_dot(pw[h], pw[h]).astype(BF16) for h in hs]
        tinv = [tinv[h] + _dot(tinv[h].astype(BF16), pw[h]) for h in hs]
        order *= 2
    size = DN_BASE
    while size < c:
        level = same_block(2 * size) & jnp.logical_not(same_block(size))
        off = [jnp.where(level, nmat[h], 0.0).astype(BF16) for h in hs]
        ct = [_dot(off[h], tinv[h].astype(BF16)).astype(BF16) for h in hs]
        tinv = [tinv[h] - _dot(tinv[h].astype(BF16), ct[h]) for h in hs]
        size *= 2

    sol = [jnp.concatenate([act[:, 2 * hd + h * dim:2 * hd + (h + 1) * dim] * beta[h],
                            kn[h] * (beta[h] * eg[h])], axis=1) for h in hs]
    sol = [sol[h] + _dot((tinv[h] - eye).astype(BF16), sol[h].astype(BF16)) for h in hs]

    pmat = [(_dot_nt(qn[h].astype(BF16), kb[h]) * decay[h]).astype(BF16) for h in hs]
    wq = [[jnp.concatenate([sol[h][c * j:c * (j + 1), dim:], (qn[h] * eg[h])[c * j:c * (j + 1)]],
                           axis=0).astype(BF16) for j in chunks] for h in hs]
    kd = [(kn[h] * kdec[:, heads + h:heads + h + 1]).astype(BF16) for h in hs]

    state = [state_ref[h] for h in hs]
    o_parts = [[] for _ in hs]
    for j in chunks:
        r = slice(c * j, c * (j + 1))
        ws = [_dot(wq[h][j], state[h].astype(BF16)) for h in hs]
        db = [(sol[h][r, :dim] - ws[h][:c]).astype(BF16) for h in hs]
        for h in hs:
            o_parts[h].append(ws[h][c:] + _dot(pmat[h][r, r], db[h]))
        state = [gc_rows[j][:, heads + h:heads + h + 1] * state[h]
                 + lax.dot_general(kd[h][r], db[h], _TN, preferred_element_type=F32) for h in hs]
    for h in hs:
        state_ref[h] = state[h]

    for h in hs:
        o = jnp.concatenate(o_parts[h], axis=0)
        zh = z_ref[:, h * dim:(h + 1) * dim]
        o_ref[:, h * dim:(h + 1) * dim] = (_rms(o, onw_ref[...]) * _silu(zh)).astype(o_ref.dtype)


def _deltanet(proj3, ba3, conv_w, gate_params, out_norm_w, riders, heads, dim, col_z):
    b, s, _ = proj3.shape
    hd = heads * dim
    gt = DN_GROUP
    ng = s // gt
    rider_in, rider_out, rider_shapes = _cast_rider_specs(riders, b * ng, lambda i, g: i * ng + g)
    pipelined = (_nbytes((gt, 3 * hd), F32) + _nbytes((gt, hd), F32) + _nbytes((gt, LANES), F32)
                 + _nbytes((gt, hd), BF16) + sum(_nbytes(w.shape, F32) * 3 // 2 for w in riders) // (b * ng))
    resident = (_nbytes((gt + SUBLANES, 3 * hd), F32) * 4 + _nbytes((heads, dim, dim), F32)
                + heads * 6 * _nbytes((gt, gt), F32))
    return pl.pallas_call(
        functools.partial(_dn_kernel, heads=heads, dim=dim, n_riders=len(riders)),
        grid=(b, ng),
        in_specs=[
            pl.BlockSpec((None, gt, 3 * hd), lambda i, g: (i, g, 0)),
            pl.BlockSpec((None, gt, hd), lambda i, g: (i, g, col_z // hd)),
            pl.BlockSpec((None, gt, LANES), lambda i, g: (i, g, 0)),
            pl.BlockSpec((4, 3 * hd), lambda i, g: (0, 0)),
            pl.BlockSpec((SUBLANES, LANES), lambda i, g: (0, 0)),
            pl.BlockSpec((1, dim), lambda i, g: (0, 0)),
            *rider_in,
        ],
        out_specs=[pl.BlockSpec((None, gt, hd), lambda i, g: (i, g, 0)), *rider_out],
        out_shape=[jax.ShapeDtypeStruct((b, s, hd), BF16), *rider_shapes],
        scratch_shapes=[pltpu.VMEM((gt + SUBLANES, 3 * hd), F32), pltpu.VMEM((heads, dim, dim), F32)],
        compiler_params=pltpu.CompilerParams(
            dimension_semantics=("parallel", "arbitrary"),
            vmem_limit_bytes=_vmem_limit(pipelined, resident)),
        name="deltanet",
    )(proj3, proj3, ba3, conv_w, gate_params, out_norm_w, *riders)


def _swa_kernel(rb_ref, sink_ref, q_ref, kc_ref, kp_ref, vc_ref, vp_ref, qw_ref, kw_ref, o_ref, bias_ref,
                *, q_heads, kv_heads, dim):
    blk = SWA_BLOCK
    n = pl.program_id(1)
    qi = lax.broadcasted_iota(jnp.int32, (blk, 2 * blk), 0)
    kj = lax.broadcasted_iota(jnp.int32, (blk, 2 * blk), 1)
    dist = qi - kj + blk

    @pl.when((pl.program_id(0) == 0) & (n == 0))
    def _():
        max_exact = N_BUCKETS // 2
        nn = jnp.maximum(dist, 0)
        nf = jnp.maximum(nn, 1).astype(F32)
        large = max_exact + (jnp.log(nf / max_exact) / math.log(MAX_DISTANCE / max_exact)
                             * (N_BUCKETS - max_exact)).astype(jnp.int32)
        bucket = jnp.where(nn < max_exact, nn, jnp.minimum(large, N_BUCKETS - 1))
        for h in range(q_heads):
            acc = jnp.zeros((blk, 2 * blk), F32)
            for bk in range(N_BUCKETS):
                acc = jnp.where(bucket == bk, rb_ref[bk * q_heads + h], acc)
            acc = jnp.where((dist >= 0) & (dist < SWA_WINDOW), acc, -jnp.inf)
            bias_ref[h] = acc
            bias_ref[q_heads + h] = jnp.where(kj >= blk, acc, -jnp.inf)

    first = jnp.where(n == 0, q_heads, 0)
    group = q_heads // kv_heads
    for j in range(kv_heads):
        cols = slice(j * dim, (j + 1) * dim)
        k_all = _rms(jnp.concatenate([kp_ref[:, cols], kc_ref[:, cols]], axis=0), kw_ref[...]).astype(BF16)
        v_all = jnp.concatenate([vp_ref[:, cols], vc_ref[:, cols]], axis=0).astype(BF16)
        for u in range(SWA_STEP_BLOCKS):
            rows = slice(u * blk, (u + 1) * blk)
            kwin, vwin = k_all[u * blk:(u + 2) * blk], v_all[u * blk:(u + 2) * blk]
            qcat = jnp.concatenate(
                [_rms(q_ref[rows, (j * group + i) * dim:(j * group + i + 1) * dim], qw_ref[...])
                 for i in range(group)], axis=0).astype(BF16)
            logits = _dot_nt(qcat, kwin) * (dim ** -0.5)
            for i in range(group):
                h = j * group + i
                lg = logits[i * blk:(i + 1) * blk] + bias_ref[(first if u == 0 else 0) + h]
                sink = sink_ref[h]
                mx = jnp.maximum(jnp.max(lg, axis=-1, keepdims=True), sink)
                e = jnp.exp(lg - mx)
                den = jnp.sum(e, axis=-1, keepdims=True) + jnp.exp(sink - mx)
                pv = _dot(e.astype(BF16), vwin)
                o_ref[rows, h * dim:(h + 1) * dim] = (pv / den).astype(o_ref.dtype)


def _swa(proj3, rel_bias_flat, sinks, q_norm_w, k_norm_w, q_heads, kv_heads, dim, col_q, col_k, col_v):
    b, s, _ = proj3.shape
    blk = SWA_BLOCK
    step = SWA_STEP_BLOCKS * blk
    qw, kvw = q_heads * dim, kv_heads * dim
    smem = pl.BlockSpec(memory_space=pltpu.SMEM)
    pipelined = (_nbytes((step, qw), F32) + 2 * _nbytes((step + blk, kvw), F32) + _nbytes((step, qw), BF16))
    resident = _nbytes((2 * q_heads, blk, 2 * blk), F32) + (8 << 20)

    def prev(n):
        return jnp.maximum(SWA_STEP_BLOCKS * n - 1, 0)

    return pl.pallas_call(
        functools.partial(_swa_kernel, q_heads=q_heads, kv_heads=kv_heads, dim=dim),
        grid=(b, s // step),
        in_specs=[
            smem, smem,
            pl.BlockSpec((None, step, qw), lambda i, n: (i, n, col_q // qw)),
            pl.BlockSpec((None, step, kvw), lambda i, n: (i, n, col_k // kvw)),
            pl.BlockSpec((None, blk, kvw), lambda i, n: (i, prev(n), col_k // kvw)),
            pl.BlockSpec((None, step, kvw), lambda i, n: (i, n, col_v // kvw)),
            pl.BlockSpec((None, blk, kvw), lambda i, n: (i, prev(n), col_v // kvw)),
            pl.BlockSpec((1, dim), lambda i, n: (0, 0)),
            pl.BlockSpec((1, dim), lambda i, n: (0, 0)),
        ],
        out_specs=pl.BlockSpec((None, step, qw), lambda i, n: (i, n, 0)),
        out_shape=jax.ShapeDtypeStruct((b, s, qw), BF16),
        scratch_shapes=[pltpu.VMEM((2 * q_heads, blk, 2 * blk), F32)],
        compiler_params=pltpu.CompilerParams(
            dimension_semantics=("arbitrary", "arbitrary"),
            vmem_limit_bytes=_vmem_limit(pipelined, resident)),
        name="swa",
    )(rel_bias_flat, sinks, proj3, proj3, proj3, proj3, proj3, q_norm_w, k_norm_w)


def _memattn_kernel(*refs, heads, dim, n_riders):
    q_ref, k_ref, v_ref, qw_ref, kw_ref = refs[:5]
    o_ref = refs[5 + n_riders]
    _cast_riders(refs[5:5 + n_riders], refs[6 + n_riders:])
    for h in range(heads):
        cols = slice(h * dim, (h + 1) * dim)
        qn = _rms(q_ref[:, cols], qw_ref[...]).astype(BF16)
        kn = _rms(k_ref[:, cols], kw_ref[...]).astype(BF16)
        lg = _dot_nt(qn, kn) * (dim ** -0.5)
        e = jnp.exp(lg - jnp.max(lg, axis=-1, keepdims=True))
        den = jnp.sum(e, axis=-1, keepdims=True)
        pv = _dot(e.astype(BF16), v_ref[:, cols].astype(BF16))
        o_ref[:, cols] = (pv / den).astype(o_ref.dtype)


def _memattn(proj3, mkv3, q_norm_w, k_norm_w, riders, heads, dim, col_q, tq):
    b, s, _ = proj3.shape
    m = mkv3.shape[1]
    w = heads * dim
    nt = s // tq
    rider_in, rider_out, rider_shapes = _cast_rider_specs(riders, b * nt, lambda i, t: i * nt + t)
    pipelined = (_nbytes((tq, w), F32) + 2 * _nbytes((m, w), F32) + _nbytes((tq, w), BF16)
                 + sum(_nbytes(r.shape, F32) * 3 // 2 for r in riders) // (b * nt))
    return pl.pallas_call(
        functools.partial(_memattn_kernel, heads=heads, dim=dim, n_riders=len(riders)),
        grid=(b, nt),
        in_specs=[
            pl.BlockSpec((pl.Element(tq), pl.Element(w)), lambda i, t: (pl.multiple_of(i * s + t * tq, tq), col_q)),
            pl.BlockSpec((None, m, w), lambda i, t: (i, 0, 0)),
            pl.BlockSpec((None, m, w), lambda i, t: (i, 0, 1)),
            pl.BlockSpec((1, dim), lambda i, t: (0, 0)),
            pl.BlockSpec((1, dim), lambda i, t: (0, 0)),
            *rider_in,
        ],
        out_specs=[pl.BlockSpec((None, tq, w), lambda i, t: (i, t, 0)), *rider_out],
        out_shape=[jax.ShapeDtypeStruct((b, s, w), BF16), *rider_shapes],
        compiler_params=pltpu.CompilerParams(
            dimension_semantics=("parallel", "parallel"),
            vmem_limit_bytes=_vmem_limit(pipelined, 8 << 20)),
        name="memattn",
    )(proj3.reshape(b * s, -1), mkv3, mkv3, q_norm_w, k_norm_w, *riders)


def _merge_kernel(od_ref, os_ref, om_ref, pd_ref, ps_ref, pm_ref, gd_ref, gs_ref, gm_ref, o_ref):
    merged = (gd_ref[...].astype(F32) * _dot(od_ref[...], pd_ref[...])
              + gs_ref[...].astype(F32) * _dot(os_ref[...], ps_ref[...])
              + gm_ref[...].astype(F32) * _dot(om_ref[...], pm_ref[...]))
    o_ref[...] = merged.astype(o_ref.dtype)


def _merge(o_dn, o_swa, o_mem, p_dn, p_swa, p_mem, gates, tm, tn):
    t = o_dn.shape[0]
    d = p_dn.shape[1]
    wd, ws, wm = o_dn.shape[1], o_swa.shape[1], o_mem.shape[1]
    per_branch = d // tn
    pipelined = ((_nbytes((tm, wd), BF16) + _nbytes((tm, ws), BF16) + _nbytes((tm, wm), BF16))
                 + (_nbytes((wd, tn), BF16) + _nbytes((ws, tn), BF16) + _nbytes((wm, tn), BF16))
                 + 4 * _nbytes((tm, tn), BF16))
    return pl.pallas_call(
        _merge_kernel,
        grid=(d // tn, t // tm),
        in_specs=[
            pl.BlockSpec((tm, wd), lambda j, i: (i, 0)),
            pl.BlockSpec((tm, ws), lambda j, i: (i, 0)),
            pl.BlockSpec((tm, wm), lambda j, i: (i, 0)),
            pl.BlockSpec((wd, tn), lambda j, i: (0, j)),
            pl.BlockSpec((ws, tn), lambda j, i: (0, j)),
            pl.BlockSpec((wm, tn), lambda j, i: (0, j)),
            pl.BlockSpec((tm, tn), lambda j, i: (i, j)),
            pl.BlockSpec((tm, tn), lambda j, i: (i, per_branch + j)),
            pl.BlockSpec((tm, tn), lambda j, i: (i, 2 * per_branch + j)),
        ],
        out_specs=pl.BlockSpec((tm, tn), lambda j, i: (i, j)),
        out_shape=jax.ShapeDtypeStruct((t, d), BF16),
        compiler_params=pltpu.CompilerParams(
            dimension_semantics=("parallel", "parallel"),
            vmem_limit_bytes=_vmem_limit(pipelined, 3 * _nbytes((tm, tn), F32))),
        name="merge",
    )(o_dn, o_swa, o_mem, p_dn, p_swa, p_mem, gates, gates, gates)


def _outproj_kernel(x_ref, m_ref, w_ref, nw_ref, x1_ref, h_ref):
    x1 = x_ref[...] + _dot(m_ref[...], w_ref[...])
    x1_ref[...] = x1
    h_ref[...] = _rms(x1, nw_ref[...]).astype(h_ref.dtype)


def _outproj(x, merged, w_out, norm_w, tm):
    t, d = x.shape
    pipelined = 2 * _nbytes((tm, d), F32) + 2 * _nbytes((tm, d), BF16) + _nbytes((d, d), BF16)
    return pl.pallas_call(
        _outproj_kernel,
        grid=(t // tm,),
        in_specs=[
            pl.BlockSpec((tm, d), lambda i: (i, 0)),
            pl.BlockSpec((tm, d), lambda i: (i, 0)),
            pl.BlockSpec((d, d), lambda i: (0, 0)),
            pl.BlockSpec((1, d), lambda i: (0, 0)),
        ],
        out_specs=[pl.BlockSpec((tm, d), lambda i: (i, 0)), pl.BlockSpec((tm, d), lambda i: (i, 0))],
        out_shape=[jax.ShapeDtypeStruct((t, d), F32), jax.ShapeDtypeStruct((t, d), BF16)],
        compiler_params=pltpu.CompilerParams(
            dimension_semantics=("parallel",),
            vmem_limit_bytes=_vmem_limit(pipelined, 2 * _nbytes((tm, d), F32))),
        name="outproj",
    )(x, merged, w_out, norm_w)


def _mlp_kernel(h_ref, x1_ref, wu_ref, wd_ref, o_ref):
    @pl.when(pl.program_id(1) == 0)
    def _():
        o_ref[...] = x1_ref[...]

    a = jnp.maximum(_dot(h_ref[...], wu_ref[...]), 0.0)
    o_ref[...] += _dot((a * a).astype(BF16), wd_ref[...])


def _mlp(h2, x1, w_up, w_down, tm, tf):
    t, d = x1.shape
    f = w_up.shape[1]
    pipelined = (_nbytes((tm, d), BF16) + 2 * _nbytes((tm, d), F32)
                 + _nbytes((d, tf), BF16) + _nbytes((tf, d), BF16))
    resident = 2 * _nbytes((tm, tf), F32)
    return pl.pallas_call(
        _mlp_kernel,
        grid=(t // tm, f // tf),
        in_specs=[
            pl.BlockSpec((tm, d), lambda i, k: (i, 0)),
            pl.BlockSpec((tm, d), lambda i, k: (i, 0)),
            pl.BlockSpec((d, tf), lambda i, k: (0, k)),
            pl.BlockSpec((tf, d), lambda i, k: (k, 0)),
        ],
        out_specs=pl.BlockSpec((tm, d), lambda i, k: (i, 0)),
        out_shape=jax.ShapeDtypeStruct((t, d), F32),
        compiler_params=pltpu.CompilerParams(
            dimension_semantics=("parallel", "arbitrary"),
            vmem_limit_bytes=_vmem_limit(pipelined, resident)),
        name="mlp",
    )(h2, x1, w_up, w_down)


def _layer(x, mem, attn_norm_w, w_in, dn_conv_w, dn_a_log, dn_dt_bias, dn_out_norm_w, swa_q_norm_w,
           swa_k_norm_w, swa_sinks, rel_bias, mem_norm_w, w_mem_kv, xq_norm_w, xk_norm_w,
           p_dn, p_swa, p_mem, w_out, mlp_norm_w, w_mlp_up, w_mlp_down):
    b, s, d = x.shape
    m = mem.shape[1]
    t = b * s

    dn_heads = dn_a_log.shape[0]
    dn_dim = dn_out_norm_w.shape[0]
    dn_w = dn_heads * dn_dim
    swa_heads = swa_sinks.shape[0]
    swa_dim = swa_q_norm_w.shape[0]
    swa_w = swa_heads * swa_dim
    mem_dim = xq_norm_w.shape[0]
    mem_w = p_mem.shape[0]
    mem_heads = mem_w // mem_dim
    swa_kv_w = (w_in.shape[1] - 4 * dn_w - 2 * dn_heads - swa_w - mem_w - 3 * d) // 2
    swa_kv = swa_kv_w // swa_dim

    n_ba = 2 * dn_heads
    src_ba = 4 * dn_w
    src_swa = src_ba + n_ba
    w_t = w_in.T.astype(BF16)
    w_b = w_t[src_swa:]
    w_c = jnp.concatenate([w_t[src_ba:src_swa], jnp.zeros((LANES - n_ba, d), BF16)], axis=0)
    col = {"qkv": 0, "z": 3 * dn_w, "sq": src_ba}
    col["sk"] = col["sq"] + swa_w
    col["sv"] = col["sk"] + swa_kv_w
    col["mq"] = col["sv"] + swa_kv_w

    proj, gates, ba = _inproj(x.reshape(t, d), attn_norm_w.reshape(1, d), w_t, src_ba, w_b, w_c, 3 * d,
                              tm=1024, tn=1024)
    proj3 = proj.reshape(b, s, -1)
    mkv = _norm_matmul(mem.reshape(b * m, d), mem_norm_w.reshape(1, d), w_mem_kv.astype(BF16), tm=512, tn=512)
    mkv3 = mkv.reshape(b, m, -1)

    gate_params = jnp.zeros((SUBLANES, LANES), F32)
    gate_params = gate_params.at[0, dn_heads:2 * dn_heads].set(dn_a_log)
    gate_params = gate_params.at[1, dn_heads:2 * dn_heads].set(dn_dt_bias)
    o_dn, w_up_bf, w_down_bf = _deltanet(proj3, ba.reshape(b, s, LANES), dn_conv_w, gate_params,
                                         dn_out_norm_w.reshape(1, dn_dim), [w_mlp_up, w_mlp_down],
                                         dn_heads, dn_dim, col["z"])
    o_swa = _swa(proj3, rel_bias.reshape(-1), swa_sinks, swa_q_norm_w.reshape(1, swa_dim),
                 swa_k_norm_w.reshape(1, swa_dim), swa_heads, swa_kv, swa_dim, col["sq"], col["sk"], col["sv"])
    o_mem, w_out_bf, p_dn_bf, p_swa_bf, p_mem_bf = _memattn(
        proj3, mkv3, xq_norm_w.reshape(1, mem_dim), xk_norm_w.reshape(1, mem_dim),
        [w_out, p_dn, p_swa, p_mem], mem_heads, mem_dim, col["mq"], tq=1024)

    merged = _merge(o_dn.reshape(t, dn_w), o_swa.reshape(t, swa_w), o_mem.reshape(t, mem_w),
                    p_dn_bf, p_swa_bf, p_mem_bf, gates, tm=1024, tn=1024)
    x1, h2 = _outproj(x.reshape(t, d), merged, w_out_bf, mlp_norm_w.reshape(1, d), tm=512)
    out = _mlp(h2, x1, w_up_bf, w_down_bf, tm=512, tf=1024)
    return out.reshape(b, s, d)


def kernel(x, mem, attn_norm_w, w_in, dn_conv_w, dn_A_log, dn_dt_bias, dn_out_norm_w, swa_q_norm_w,
           swa_k_norm_w, swa_sinks, rel_bias, mem_norm_w, w_mem_kv, xq_norm_w, xk_norm_w, p_dn, p_swa,
           p_mem, w_out, mlp_norm_w, w_mlp_up, w_mlp_down):
    depth = w_in.shape[0]
    for l in range(depth):
        x = _layer(x, mem, attn_norm_w[l], w_in[l], dn_conv_w[l], dn_A_log[l], dn_dt_bias[l],
                   dn_out_norm_w[l], swa_q_norm_w[l], swa_k_norm_w[l], swa_sinks[l], rel_bias,
                   mem_norm_w[l], w_mem_kv[l], xq_norm_w[l], xk_norm_w[l], p_dn[l], p_swa[l], p_mem[l],
                   w_out[l], mlp_norm_w[l], w_mlp_up[l], w_mlp_down[l])
    return x
```

```python
import functools
import math

import jax
import jax.numpy as jnp
from jax import lax
from jax.experimental import pallas as pl
from jax.experimental.pallas import tpu as pltpu

F32 = jnp.float32
BF16 = jnp.bfloat16

EPS = 1e-6
LANES = 128
SUBLANES = 8
V7X_VMEM_BYTES = 64 * 1024 * 1024

DN_CHUNK = 64
DN_GROUP = 256
DN_BASE = 8
SWA_WINDOW = 128
SWA_BLOCK = 128
SWA_STEP_BLOCKS = 4
N_BUCKETS = 32
MAX_DISTANCE = 128

_NT = (((1,), (1,)), ((), ()))
_TN = (((0,), (0,)), ((), ()))


COMPILER_SCRATCH_BYTES = 4 * 1024 * 1024
VMEM_RESERVE_BYTES = 8 * 1024 * 1024


def _vmem_limit(pipelined_bytes, resident_bytes):
    want = 2 * pipelined_bytes + resident_bytes + COMPILER_SCRATCH_BYTES
    assert want <= V7X_VMEM_BYTES, (want, V7X_VMEM_BYTES)
    return int(V7X_VMEM_BYTES - VMEM_RESERVE_BYTES)


def _nbytes(shape, dtype):
    return math.prod(shape) * jnp.dtype(dtype).itemsize


def _sigmoid(v):
    return 0.5 * jnp.tanh(0.5 * v) + 0.5


def _silu(v):
    return v * _sigmoid(v)


def _rms(v, w):
    return (v * lax.rsqrt(jnp.mean(v * v, axis=-1, keepdims=True) + EPS)) * w


def _dot(a, b):
    return jnp.dot(a, b, preferred_element_type=F32)


def _dot_nt(a, b):
    return lax.dot_general(a, b, _NT, preferred_element_type=F32)


def _cast_rider_specs(weights, n_steps, step_of):
    in_specs, out_specs, out_shapes = [], [], []
    for w in weights:
        rows = w.shape[0] // n_steps
        assert rows * n_steps == w.shape[0] and rows % (2 * SUBLANES) == 0
        for specs in (in_specs, out_specs):
            specs.append(pl.BlockSpec((rows, w.shape[1]), lambda *ids: (step_of(*ids), 0)))
        out_shapes.append(jax.ShapeDtypeStruct(w.shape, BF16))
    return in_specs, out_specs, out_shapes


def _cast_riders(in_refs, out_refs):
    for i_ref, o_ref in zip(in_refs, out_refs):
        o_ref[...] = i_ref[...].astype(o_ref.dtype)


def _inproj_kernel(x_ref, nw_ref, wa_ref, wb_ref, wc_ref, o_ref, g_ref, ba_ref, h_ref, *, n_a, n_p, tail):
    j = pl.program_id(1)

    @pl.when(j == 0)
    def _():
        h_ref[...] = _rms(x_ref[...], nw_ref[...]).astype(BF16)
        ba_ref[...] = _dot_nt(h_ref[...], wc_ref[...])

    @pl.when(j < n_a)
    def _():
        o_ref[...] = _dot_nt(h_ref[...], wa_ref[...])

    @pl.when((j >= n_a) & (j < n_p - 1))
    def _():
        o_ref[...] = _dot_nt(h_ref[...], wb_ref[...])

    @pl.when(j == n_p - 1)
    def _():
        o_ref[:, :tail] = _dot_nt(h_ref[...], wb_ref[:tail, :])

    @pl.when(j >= n_p)
    def _():
        g_ref[...] = _sigmoid(_dot_nt(h_ref[...], wb_ref[...])).astype(g_ref.dtype)


def _inproj(x, nw, w_a, rows_a, w_b, w_c, n_gates, tm, tn):
    m, k = x.shape
    rows_b1 = w_b.shape[0] - n_gates
    n = rows_a + rows_b1
    n_a = rows_a // tn
    n_p = pl.cdiv(n, tn)
    assert rows_a % tn == 0 and n_gates % tn == 0 and rows_b1 % (2 * SUBLANES) == 0

    def wb_index(i, j):
        row = jnp.where(j < n_p, jnp.maximum(j - n_a, 0) * tn, rows_b1 + (j - n_p) * tn)
        return (pl.multiple_of(row, 2 * SUBLANES), 0)

    pipelined = (_nbytes((tm, k), F32) + 2 * _nbytes((tn, k), BF16) + _nbytes((tm, tn), F32)
                 + _nbytes((tm, tn), BF16) + _nbytes((tm, LANES), F32))
    resident = _nbytes((tm, k), BF16) + _nbytes((tm, k), F32) + _nbytes((LANES, k), BF16)
    return pl.pallas_call(
        functools.partial(_inproj_kernel, n_a=n_a, n_p=n_p, tail=n - (n_p - 1) * tn),
        grid=(m // tm, n_p + n_gates // tn),
        in_specs=[
            pl.BlockSpec((tm, k), lambda i, j: (i, 0)),
            pl.BlockSpec((1, k), lambda i, j: (0, 0)),
            pl.BlockSpec((tn, k), lambda i, j: (jnp.minimum(j, n_a - 1), 0)),
            pl.BlockSpec((pl.Element(tn), pl.Element(k)), wb_index),
            pl.BlockSpec((LANES, k), lambda i, j: (0, 0)),
        ],
        out_specs=[pl.BlockSpec((tm, tn), lambda i, j: (i, jnp.minimum(j, n_p - 1))),
                   pl.BlockSpec((tm, tn), lambda i, j: (i, jnp.maximum(j - n_p, 0))),
                   pl.BlockSpec((tm, LANES), lambda i, j: (i, 0))],
        out_shape=[jax.ShapeDtypeStruct((m, n), F32), jax.ShapeDtypeStruct((m, n_gates), BF16),
                   jax.ShapeDtypeStruct((m, LANES), F32)],
        scratch_shapes=[pltpu.VMEM((tm, k), BF16)],
        compiler_params=pltpu.CompilerParams(
            dimension_semantics=("parallel", "arbitrary"),
            vmem_limit_bytes=_vmem_limit(pipelined, resident)),
        name="inproj",
    )(x, nw, w_a, w_b, w_c)


def _norm_matmul_kernel(x_ref, nw_ref, w_ref, o_ref, h_ref):
    @pl.when(pl.program_id(1) == 0)
    def _():
        h_ref[...] = _rms(x_ref[...], nw_ref[...]).astype(BF16)

    o_ref[...] = _dot(h_ref[...], w_ref[...]).astype(o_ref.dtype)


def _norm_matmul(x, nw, w, tm, tn):
    m, k = x.shape
    n = w.shape[1]
    pipelined = _nbytes((tm, k), F32) + _nbytes((k, tn), BF16) + _nbytes((tm, tn), F32)
    resident = _nbytes((tm, k), BF16) + _nbytes((tm, k), F32)
    return pl.pallas_call(
        _norm_matmul_kernel,
        grid=(m // tm, n // tn),
        in_specs=[
            pl.BlockSpec((tm, k), lambda i, j: (i, 0)),
            pl.BlockSpec((1, k), lambda i, j: (0, 0)),
            pl.BlockSpec((k, tn), lambda i, j: (0, j)),
        ],
        out_specs=pl.BlockSpec((tm, tn), lambda i, j: (i, j)),
        out_shape=jax.ShapeDtypeStruct((m, n), F32),
        scratch_shapes=[pltpu.VMEM((tm, k), BF16)],
        compiler_params=pltpu.CompilerParams(
            dimension_semantics=("parallel", "arbitrary"),
            vmem_limit_bytes=_vmem_limit(pipelined, resident)),
        name="norm_matmul",
    )(x, nw, w)


def _dn_kernel(*refs, heads, dim, n_riders):
    qkv_ref, z_ref, ba_ref, cw_ref, gp_ref, onw_ref = refs[:6]
    o_ref = refs[6 + n_riders]
    ext_ref, state_ref = refs[-2:]
    _cast_riders(refs[6:6 + n_riders], refs[7 + n_riders:7 + 2 * n_riders])

    gt = DN_GROUP
    c = DN_CHUNK
    hd = heads * dim
    hs = range(heads)
    chunks = range(gt // c)

    @pl.when(pl.program_id(1) == 0)
    def _():
        ext_ref[0:SUBLANES, :] = jnp.zeros((SUBLANES, 3 * hd), F32)
        state_ref[...] = jnp.zeros_like(state_ref)

    ext_ref[SUBLANES:SUBLANES + gt, :] = qkv_ref[...]
    xe = ext_ref[...]
    conv = cw_ref[3:4, :] * xe[SUBLANES:]
    for s in (1, 2, 3):
        conv = conv + cw_ref[3 - s:4 - s, :] * pltpu.roll(xe, s, 0)[SUBLANES:]
    ext_ref[0:SUBLANES, :] = xe[gt:gt + SUBLANES]
    act = _silu(conv)

    ba = ba_ref[...]
    beta_all = _sigmoid(ba)
    xa = ba + gp_ref[1:2, :]
    softplus = jnp.maximum(xa, 0.0) + jnp.log1p(jnp.exp(-jnp.abs(xa)))
    g_all = -jnp.exp(gp_ref[0:1, :]) * softplus
    row_in_chunk = lax.broadcasted_iota(jnp.int32, (gt, LANES), 0) & (c - 1)
    gcum = g_all
    s = 1
    while s < c:
        gcum = gcum + jnp.where(row_in_chunk >= s, pltpu.roll(gcum, s, 0), 0.0)
        s *= 2
    exp_g = jnp.exp(gcum)
    kdec_parts, gc_rows = [], []
    for j in chunks:
        g_last = gcum[c * j + c - 1:c * j + c, :]
        kdec_parts.append(jnp.exp(g_last - gcum[c * j:c * (j + 1), :]))
        gc_rows.append(jnp.exp(g_last))
    kdec = jnp.concatenate(kdec_parts, axis=0)
    gcum_t = gcum.T

    ri = lax.broadcasted_iota(jnp.int32, (gt, gt), 0)
    ci = lax.broadcasted_iota(jnp.int32, (gt, gt), 1)
    same_chunk = (ri // c) == (ci // c)
    strict = same_chunk & (ri > ci)
    incl = same_chunk & (ri >= ci)

    qn, kn, kb, beta, eg = [], [], [], [], []
    for h in hs:
        qh = act[:, h * dim:(h + 1) * dim]
        kh = act[:, hd + h * dim:hd + (h + 1) * dim]
        qn.append(qh * lax.rsqrt(jnp.sum(qh * qh, axis=-1, keepdims=True) + EPS) * (dim ** -0.5))
        kn.append(kh * lax.rsqrt(jnp.sum(kh * kh, axis=-1, keepdims=True) + EPS))
        kb.append(kn[h].astype(BF16))
        beta.append(beta_all[:, h:h + 1])
        eg.append(exp_g[:, heads + h:heads + h + 1])

    decay = [jnp.exp(jnp.where(incl, gcum[:, heads + h:heads + h + 1] - gcum_t[heads + h:heads + h + 1, :],
                               -jnp.inf)) for h in hs]
    kk = [_dot_nt(kb[h], kb[h]) for h in hs]
    nmat = [jnp.where(strict, (beta[h] * kk[h]) * decay[h], 0.0) for h in hs]

    def same_block(size):
        return (ri // size) == (ci // size)

    eye = jnp.where(ri == ci, 1.0, 0.0)
    base = same_block(DN_BASE)
    pw = [jnp.where(base, nmat[h], 0.0) for h in hs]
    tinv = [eye - pw[h] for h in hs]
    pw = [pw[h].astype(BF16) for h in hs]
    order = 2
    while order < DN_BASE:
        pw = [_dot(pw[h], pw[h]).astype(BF16) for h in hs]
        tinv = [tinv[h] + _dot(tinv[h].astype(BF16), pw[h]) for h in hs]
        order *= 2
    size = DN_BASE
    while size < c:
        level = same_block(2 * size) & jnp.logical_not(same_block(size))
        off = [jnp.where(level, nmat[h], 0.0).astype(BF16) for h in hs]
        ct = [_dot(off[h], tinv[h].astype(BF16)).astype(BF16) for h in hs]
        tinv = [tinv[h] - _dot(tinv[h].astype(BF16), ct[h]) for h in hs]
        size *= 2

    sol = [jnp.concatenate([act[:, 2 * hd + h * dim:2 * hd + (h + 1) * dim] * beta[h],
                            kn[h] * (beta[h] * eg[h])], axis=1) for h in hs]
    sol = [sol[h] + _dot((tinv[h] - eye).astype(BF16), sol[h].astype(BF16)) for h in hs]

    pmat = [(_dot_nt(qn[h].astype(BF16), kb[h]) * decay[h]).astype(BF16) for h in hs]
    wq = [[jnp.concatenate([sol[h][c * j:c * (j + 1), dim:], (qn[h] * eg[h])[c * j:c * (j + 1)]],
                           axis=0).astype(BF16) for j in chunks] for h in hs]
    kd = [(kn[h] * kdec[:, heads + h:heads + h + 1]).astype(BF16) for h in hs]

    state = [state_ref[h] for h in hs]
    o_parts = [[] for _ in hs]
    for j in chunks:
        r = slice(c * j, c * (j + 1))
        ws = [_dot(wq[h][j], state[h].astype(BF16)) for h in hs]
        db = [(sol[h][r, :dim] - ws[h][:c]).astype(BF16) for h in hs]
        for h in hs:
            o_parts[h].append(ws[h][c:] + _dot(pmat[h][r, r], db[h]))
        state = [gc_rows[j][:, heads + h:heads + h + 1] * state[h]
                 + lax.dot_general(kd[h][r], db[h], _TN, preferred_element_type=F32) for h in hs]
    for h in hs:
        state_ref[h] = state[h]

    for h in hs:
        o = jnp.concatenate(o_parts[h], axis=0)
        zh = z_ref[:, h * dim:(h + 1) * dim]
        o_ref[:, h * dim:(h + 1) * dim] = (_rms(o, onw_ref[...]) * _silu(zh)).astype(o_ref.dtype)


def _deltanet(proj3, ba3, conv_w, gate_params, out_norm_w, riders, heads, dim, col_z):
    b, s, _ = proj3.shape
    hd = heads * dim
    gt = DN_GROUP
    ng = s // gt
    rider_in, rider_out, rider_shapes = _cast_rider_specs(riders, b * ng, lambda i, g: i * ng + g)
    pipelined = (_nbytes((gt, 3 * hd), F32) + _nbytes((gt, hd), F32) + _nbytes((gt, LANES), F32)
                 + _nbytes((gt, hd), BF16) + sum(_nbytes(w.shape, F32) * 3 // 2 for w in riders) // (b * ng))
    resident = (_nbytes((gt + SUBLANES, 3 * hd), F32) * 4 + _nbytes((heads, dim, dim), F32)
                + heads * 6 * _nbytes((gt, gt), F32))
    return pl.pallas_call(
        functools.partial(_dn_kernel, heads=heads, dim=dim, n_riders=len(riders)),
        grid=(b, ng),
        in_specs=[
            pl.BlockSpec((None, gt, 3 * hd), lambda i, g: (i, g, 0)),
            pl.BlockSpec((None, gt, hd), lambda i, g: (i, g, col_z // hd)),
            pl.BlockSpec((None, gt, LANES), lambda i, g: (i, g, 0)),
            pl.BlockSpec((4, 3 * hd), lambda i, g: (0, 0)),
            pl.BlockSpec((SUBLANES, LANES), lambda i, g: (0, 0)),
            pl.BlockSpec((1, dim), lambda i, g: (0, 0)),
            *rider_in,
        ],
        out_specs=[pl.BlockSpec((None, gt, hd), lambda i, g: (i, g, 0)), *rider_out],
        out_shape=[jax.ShapeDtypeStruct((b, s, hd), BF16), *rider_shapes],
        scratch_shapes=[pltpu.VMEM((gt + SUBLANES, 3 * hd), F32), pltpu.VMEM((heads, dim, dim), F32)],
        compiler_params=pltpu.CompilerParams(
            dimension_semantics=("parallel", "arbitrary"),
            vmem_limit_bytes=_vmem_limit(pipelined, resident)),
        name="deltanet",
    )(proj3, proj3, ba3, conv_w, gate_params, out_norm_w, *riders)


def _swa_kernel(rb_ref, sink_ref, q_ref, kc_ref, kp_ref, vc_ref, vp_ref, qw_ref, kw_ref, o_ref, bias_ref,
                *, q_heads, kv_heads, dim):
    blk = SWA_BLOCK
    n = pl.program_id(1)
    qi = lax.broadcasted_iota(jnp.int32, (blk, 2 * blk), 0)
    kj = lax.broadcasted_iota(jnp.int32, (blk, 2 * blk), 1)
    dist = qi - kj + blk

    @pl.when((pl.program_id(0) == 0) & (n == 0))
    def _():
        max_exact = N_BUCKETS // 2
        nn = jnp.maximum(dist, 0)
        nf = jnp.maximum(nn, 1).astype(F32)
        large = max_exact + (jnp.log(nf / max_exact) / math.log(MAX_DISTANCE / max_exact)
                             * (N_BUCKETS - max_exact)).astype(jnp.int32)
        bucket = jnp.where(nn < max_exact, nn, jnp.minimum(large, N_BUCKETS - 1))
        for h in range(q_heads):
            acc = jnp.zeros((blk, 2 * blk), F32)
            for bk in range(N_BUCKETS):
                acc = jnp.where(bucket == bk, rb_ref[bk * q_heads + h], acc)
            acc = jnp.where((dist >= 0) & (dist < SWA_WINDOW), acc, -jnp.inf)
            bias_ref[h] = acc
            bias_ref[q_heads + h] = jnp.where(kj >= blk, acc, -jnp.inf)

    first = jnp.where(n == 0, q_heads, 0)
    group = q_heads // kv_heads
    for j in range(kv_heads):
        cols = slice(j * dim, (j + 1) * dim)
        k_all = _rms(jnp.concatenate([kp_ref[:, cols], kc_ref[:, cols]], axis=0), kw_ref[...]).astype(BF16)
        v_all = jnp.concatenate([vp_ref[:, cols], vc_ref[:, cols]], axis=0).astype(BF16)
        for u in range(SWA_STEP_BLOCKS):
            rows = slice(u * blk, (u + 1) * blk)
            kwin, vwin = k_all[u * blk:(u + 2) * blk], v_all[u * blk:(u + 2) * blk]
            qcat = jnp.concatenate(
                [_rms(q_ref[rows, (j * group + i) * dim:(j * group + i + 1) * dim], qw_ref[...])
                 for i in range(group)], axis=0).astype(BF16)
            logits = _dot_nt(qcat, kwin) * (dim ** -0.5)
            for i in range(group):
                h = j * group + i
                lg = logits[i * blk:(i + 1) * blk] + bias_ref[(first if u == 0 else 0) + h]
                sink = sink_ref[h]
                mx = jnp.maximum(jnp.max(lg, axis=-1, keepdims=True), sink)
                e = jnp.exp(lg - mx)
                den = jnp.sum(e, axis=-1, keepdims=True) + jnp.exp(sink - mx)
                pv = _dot(e.astype(BF16), vwin)
                o_ref[rows, h * dim:(h + 1) * dim] = (pv / den).astype(o_ref.dtype)


def _swa(proj3, rel_bias_flat, sinks, q_norm_w, k_norm_w, q_heads, kv_heads, dim, col_q, col_k, col_v):
    b, s, _ = proj3.shape
    blk = SWA_BLOCK
    step = SWA_STEP_BLOCKS * blk
    qw, kvw = q_heads * dim, kv_heads * dim
    smem = pl.BlockSpec(memory_space=pltpu.SMEM)
    pipelined = (_nbytes((step, qw), F32) + 2 * _nbytes((step + blk, kvw), F32) + _nbytes((step, qw), BF16))
    resident = (_nbytes((2 * q_heads, blk, 2 * blk), F32) + 2 * pipelined
                + 2 * SWA_STEP_BLOCKS * _nbytes((q_heads // kv_heads * blk, 2 * blk), F32))

    def prev(n):
        return jnp.maximum(SWA_STEP_BLOCKS * n - 1, 0)

    return pl.pallas_call(
        functools.partial(_swa_kernel, q_heads=q_heads, kv_heads=kv_heads, dim=dim),
        grid=(b, s // step),
        in_specs=[
            smem, smem,
            pl.BlockSpec((None, step, qw), lambda i, n: (i, n, col_q // qw)),
            pl.BlockSpec((None, step, kvw), lambda i, n: (i, n, col_k // kvw)),
            pl.BlockSpec((None, blk, kvw), lambda i, n: (i, prev(n), col_k // kvw)),
            pl.BlockSpec((None, step, kvw), lambda i, n: (i, n, col_v // kvw)),
            pl.BlockSpec((None, blk, kvw), lambda i, n: (i, prev(n), col_v // kvw)),
            pl.BlockSpec((1, dim), lambda i, n: (0, 0)),
            pl.BlockSpec((1, dim), lambda i, n: (0, 0)),
        ],
        out_specs=pl.BlockSpec((None, step, qw), lambda i, n: (i, n, 0)),
        out_shape=jax.ShapeDtypeStruct((b, s, qw), BF16),
        scratch_shapes=[pltpu.VMEM((2 * q_heads, blk, 2 * blk), F32)],
        compiler_params=pltpu.CompilerParams(
            dimension_semantics=("arbitrary", "arbitrary"),
            vmem_limit_bytes=_vmem_limit(pipelined, resident)),
        name="swa",
    )(rel_bias_flat, sinks, proj3, proj3, proj3, proj3, proj3, q_norm_w, k_norm_w)


def _memattn_kernel(*refs, heads, dim, n_riders):
    q_ref, k_ref, v_ref, qw_ref, kw_ref = refs[:5]
    o_ref = refs[5 + n_riders]
    _cast_riders(refs[5:5 + n_riders], refs[6 + n_riders:])
    for h in range(heads):
        cols = slice(h * dim, (h + 1) * dim)
        qn = _rms(q_ref[:, cols], qw_ref[...]).astype(BF16)
        kn = _rms(k_ref[:, cols], kw_ref[...]).astype(BF16)
        lg = _dot_nt(qn, kn) * (dim ** -0.5)
        e = jnp.exp(lg - jnp.max(lg, axis=-1, keepdims=True))
        den = jnp.sum(e, axis=-1, keepdims=True)
        pv = _dot(e.astype(BF16), v_ref[:, cols].astype(BF16))
        o_ref[:, cols] = (pv / den).astype(o_ref.dtype)


def _memattn(proj3, mkv3, q_norm_w, k_norm_w, riders, heads, dim, col_q, tq):
    b, s, _ = proj3.shape
    m = mkv3.shape[1]
    w = heads * dim
    nt = s // tq
    rider_in, rider_out, rider_shapes = _cast_rider_specs(riders, b * nt, lambda i, t: i * nt + t)
    pipelined = (_nbytes((tq, w), F32) + 2 * _nbytes((m, w), F32) + _nbytes((tq, w), BF16)
                 + sum(_nbytes(r.shape, F32) * 3 // 2 for r in riders) // (b * nt))
    return pl.pallas_call(
        functools.partial(_memattn_kernel, heads=heads, dim=dim, n_riders=len(riders)),
        grid=(b, nt),
        in_specs=[
            pl.BlockSpec((pl.Element(tq), pl.Element(w)), lambda i, t: (pl.multiple_of(i * s + t * tq, tq), col_q)),
            pl.BlockSpec((None, m, w), lambda i, t: (i, 0, 0)),
            pl.BlockSpec((None, m, w), lambda i, t: (i, 0, 1)),
            pl.BlockSpec((1, dim), lambda i, t: (0, 0)),
            pl.BlockSpec((1, dim), lambda i, t: (0, 0)),
            *rider_in,
        ],
        out_specs=[pl.BlockSpec((None, tq, w), lambda i, t: (i, t, 0)), *rider_out],
        out_shape=[jax.ShapeDtypeStruct((b, s, w), BF16), *rider_shapes],
        compiler_params=pltpu.CompilerParams(
            dimension_semantics=("parallel", "parallel"),
            vmem_limit_bytes=_vmem_limit(pipelined, 2 * heads * _nbytes((tq, m), F32))),
        name="memattn",
    )(proj3.reshape(b * s, -1), mkv3, mkv3, q_norm_w, k_norm_w, *riders)


def _merge_kernel(od_ref, os_ref, om_ref, pd_ref, ps_ref, pm_ref, gd_ref, gs_ref, gm_ref, o_ref):
    merged = (gd_ref[...].astype(F32) * _dot(od_ref[...], pd_ref[...])
              + gs_ref[...].astype(F32) * _dot(os_ref[...], ps_ref[...])
              + gm_ref[...].astype(F32) * _dot(om_ref[...], pm_ref[...]))
    o_ref[...] = merged.astype(o_ref.dtype)


def _merge(o_dn, o_swa, o_mem, p_dn, p_swa, p_mem, gates, tm, tn):
    t = o_dn.shape[0]
    d = p_dn.shape[1]
    wd, ws, wm = o_dn.shape[1], o_swa.shape[1], o_mem.shape[1]
    per_branch = d // tn
    pipelined = ((_nbytes((tm, wd), BF16) + _nbytes((tm, ws), BF16) + _nbytes((tm, wm), BF16))
                 + (_nbytes((wd, tn), BF16) + _nbytes((ws, tn), BF16) + _nbytes((wm, tn), BF16))
                 + 4 * _nbytes((tm, tn), BF16))
    return pl.pallas_call(
        _merge_kernel,
        grid=(d // tn, t // tm),
        in_specs=[
            pl.BlockSpec((tm, wd), lambda j, i: (i, 0)),
            pl.BlockSpec((tm, ws), lambda j, i: (i, 0)),
            pl.BlockSpec((tm, wm), lambda j, i: (i, 0)),
            pl.BlockSpec((wd, tn), lambda j, i: (0, j)),
            pl.BlockSpec((ws, tn), lambda j, i: (0, j)),
            pl.BlockSpec((wm, tn), lambda j, i: (0, j)),
            pl.BlockSpec((tm, tn), lambda j, i: (i, j)),
            pl.BlockSpec((tm, tn), lambda j, i: (i, per_branch + j)),
            pl.BlockSpec((tm, tn), lambda j, i: (i, 2 * per_branch + j)),
        ],
        out_specs=pl.BlockSpec((tm, tn), lambda j, i: (i, j)),
        out_shape=jax.ShapeDtypeStruct((t, d), BF16),
        compiler_params=pltpu.CompilerParams(
            dimension_semantics=("parallel", "parallel"),
            vmem_limit_bytes=_vmem_limit(pipelined, 3 * _nbytes((tm, tn), F32))),
        name="merge",
    )(o_dn, o_swa, o_mem, p_dn, p_swa, p_mem, gates, gates, gates)


def _outproj_kernel(x_ref, m_ref, w_ref, nw_ref, x1_ref, h_ref):
    x1 = x_ref[...] + _dot(m_ref[...], w_ref[...])
    x1_ref[...] = x1
    h_ref[...] = _rms(x1, nw_ref[...]).astype(h_ref.dtype)


def _outproj(x, merged, w_out, norm_w, tm):
    t, d = x.shape
    pipelined = 2 * _nbytes((tm, d), F32) + 2 * _nbytes((tm, d), BF16) + _nbytes((d, d), BF16)
    return pl.pallas_call(
        _outproj_kernel,
        grid=(t // tm,),
        in_specs=[
            pl.BlockSpec((tm, d), lambda i: (i, 0)),
            pl.BlockSpec((tm, d), lambda i: (i, 0)),
            pl.BlockSpec((d, d), lambda i: (0, 0)),
            pl.BlockSpec((1, d), lambda i: (0, 0)),
        ],
        out_specs=[pl.BlockSpec((tm, d), lambda i: (i, 0)), pl.BlockSpec((tm, d), lambda i: (i, 0))],
        out_shape=[jax.ShapeDtypeStruct((t, d), F32), jax.ShapeDtypeStruct((t, d), BF16)],
        compiler_params=pltpu.CompilerParams(
            dimension_semantics=("parallel",),
            vmem_limit_bytes=_vmem_limit(pipelined, 2 * _nbytes((tm, d), F32))),
        name="outproj",
    )(x, merged, w_out, norm_w)


def _mlp_kernel(h_ref, x1_ref, wu_ref, wd_ref, o_ref):
    @pl.when(pl.program_id(1) == 0)
    def _():
        o_ref[...] = x1_ref[...]

    a = jnp.maximum(_dot(h_ref[...], wu_ref[...]), 0.0)
    o_ref[...] += _dot((a * a).astype(BF16), wd_ref[...])


def _mlp(h2, x1, w_up, w_down, tm, tf):
    t, d = x1.shape
    f = w_up.shape[1]
    pipelined = (_nbytes((tm, d), BF16) + 2 * _nbytes((tm, d), F32)
                 + _nbytes((d, tf), BF16) + _nbytes((tf, d), BF16))
    resident = 2 * _nbytes((tm, tf), F32)
    return pl.pallas_call(
        _mlp_kernel,
        grid=(t // tm, f // tf),
        in_specs=[
            pl.BlockSpec((tm, d), lambda i, k: (i, 0)),
            pl.BlockSpec((tm, d), lambda i, k: (i, 0)),
            pl.BlockSpec((d, tf), lambda i, k: (0, k)),
            pl.BlockSpec((tf, d), lambda i, k: (k, 0)),
        ],
        out_specs=pl.BlockSpec((tm, d), lambda i, k: (i, 0)),
        out_shape=jax.ShapeDtypeStruct((t, d), F32),
        compiler_params=pltpu.CompilerParams(
            dimension_semantics=("parallel", "arbitrary"),
            vmem_limit_bytes=_vmem_limit(pipelined, resident)),
        name="mlp",
    )(h2, x1, w_up, w_down)


def _layer(x, mem, attn_norm_w, w_in, dn_conv_w, dn_a_log, dn_dt_bias, dn_out_norm_w, swa_q_norm_w,
           swa_k_norm_w, swa_sinks, rel_bias, mem_norm_w, w_mem_kv, xq_norm_w, xk_norm_w,
           p_dn, p_swa, p_mem, w_out, mlp_norm_w, w_mlp_up, w_mlp_down):
    b, s, d = x.shape
    m = mem.shape[1]
    t = b * s

    dn_heads = dn_a_log.shape[0]
    dn_dim = dn_out_norm_w.shape[0]
    dn_w = dn_heads * dn_dim
    swa_heads = swa_sinks.shape[0]
    swa_dim = swa_q_norm_w.shape[0]
    swa_w = swa_heads * swa_dim
    mem_dim = xq_norm_w.shape[0]
    mem_w = p_mem.shape[0]
    mem_heads = mem_w // mem_dim
    swa_kv_w = (w_in.shape[1] - 4 * dn_w - 2 * dn_heads - swa_w - mem_w - 3 * d) // 2
    swa_kv = swa_kv_w // swa_dim

    n_ba = 2 * dn_heads
    src_ba = 4 * dn_w
    src_swa = src_ba + n_ba
    w_t = w_in.T.astype(BF16)
    w_b = w_t[src_swa:]
    w_c = jnp.concatenate([w_t[src_ba:src_swa], jnp.zeros((LANES - n_ba, d), BF16)], axis=0)
    col = {"qkv": 0, "z": 3 * dn_w, "sq": src_ba}
    col["sk"] = col["sq"] + swa_w
    col["sv"] = col["sk"] + swa_kv_w
    col["mq"] = col["sv"] + swa_kv_w

    proj, gates, ba = _inproj(x.reshape(t, d), attn_norm_w.reshape(1, d), w_t, src_ba, w_b, w_c, 3 * d,
                              tm=1024, tn=1024)
    proj3 = proj.reshape(b, s, -1)
    mkv = _norm_matmul(mem.reshape(b * m, d), mem_norm_w.reshape(1, d), w_mem_kv.astype(BF16), tm=512, tn=512)
    mkv3 = mkv.reshape(b, m, -1)

    gate_params = jnp.zeros((SUBLANES, LANES), F32)
    gate_params = gate_params.at[0, dn_heads:2 * dn_heads].set(dn_a_log)
    gate_params = gate_params.at[1, dn_heads:2 * dn_heads].set(dn_dt_bias)
    o_dn, w_up_bf, w_down_bf = _deltanet(proj3, ba.reshape(b, s, LANES), dn_conv_w, gate_params,
                                         dn_out_norm_w.reshape(1, dn_dim), [w_mlp_up, w_mlp_down],
                                         dn_heads, dn_dim, col["z"])
    o_swa = _swa(proj3, rel_bias.reshape(-1), swa_sinks, swa_q_norm_w.reshape(1, swa_dim),
                 swa_k_norm_w.reshape(1, swa_dim), swa_heads, swa_kv, swa_dim, col["sq"], col["sk"], col["sv"])
    o_mem, w_out_bf, p_dn_bf, p_swa_bf, p_mem_bf = _memattn(
        proj3, mkv3, xq_norm_w.reshape(1, mem_dim), xk_norm_w.reshape(1, mem_dim),
        [w_out, p_dn, p_swa, p_mem], mem_heads, mem_dim, col["mq"], tq=1024)

    merged = _merge(o_dn.reshape(t, dn_w), o_swa.reshape(t, swa_w), o_mem.reshape(t, mem_w),
                    p_dn_bf, p_swa_bf, p_mem_bf, gates, tm=1024, tn=1024)
    x1, h2 = _outproj(x.reshape(t, d), merged, w_out_bf, mlp_norm_w.reshape(1, d), tm=512)
    out = _mlp(h2, x1, w_up_bf, w_down_bf, tm=512, tf=1024)
    return out.reshape(b, s, d)


def kernel(x, mem, attn_norm_w, w_in, dn_conv_w, dn_A_log, dn_dt_bias, dn_out_norm_w, swa_q_norm_w,
           swa_k_norm_w, swa_sinks, rel_bias, mem_norm_w, w_mem_kv, xq_norm_w, xk_norm_w, p_dn, p_swa,
           p_mem, w_out, mlp_norm_w, w_mlp_up, w_mlp_down):
    depth = w_in.shape[0]
    for l in range(depth):
        x = _layer(x, mem, attn_norm_w[l], w_in[l], dn_conv_w[l], dn_A_log[l], dn_dt_bias[l],
                   dn_out_norm_w[l], swa_q_norm_w[l], swa_k_norm_w[l], swa_sinks[l], rel_bias,
                   mem_norm_w[l], w_mem_kv[l], xq_norm_w[l], xk_norm_w[l], p_dn[l], p_swa[l], p_mem[l],
                   w_out[l], mlp_norm_w[l], w_mlp_up[l], w_mlp_down[l])
    return x
```

```python
import functools
import math

import jax
import jax.numpy as jnp
from jax import lax
from jax.experimental import pallas as pl
from jax.experimental.pallas import tpu as pltpu

F32 = jnp.float32
BF16 = jnp.bfloat16

EPS = 1e-6
LANES = 128
SUBLANES = 8
V7X_VMEM_BYTES = 64 * 1024 * 1024

DN_CHUNK = 64
DN_GROUP = 256
DN_BASE = 8
SWA_WINDOW = 128
SWA_BLOCK = 128
SWA_STEP_BLOCKS = 4
N_BUCKETS = 32
MAX_DISTANCE = 128

_NT = (((1,), (1,)), ((), ()))
_TN = (((0,), (0,)), ((), ()))


COMPILER_SCRATCH_BYTES = 4 * 1024 * 1024
VMEM_RESERVE_BYTES = 8 * 1024 * 1024


def _vmem_limit(pipelined_bytes, resident_bytes, claim_all=False):
    cap = V7X_VMEM_BYTES - VMEM_RESERVE_BYTES
    want = 2 * pipelined_bytes + resident_bytes + COMPILER_SCRATCH_BYTES
    return int(cap if claim_all else min(want, cap))


def _nbytes(shape, dtype):
    return math.prod(shape) * jnp.dtype(dtype).itemsize


def _sigmoid(v):
    return 0.5 * jnp.tanh(0.5 * v) + 0.5


def _silu(v):
    return v * _sigmoid(v)


def _rms(v, w):
    return (v * lax.rsqrt(jnp.mean(v * v, axis=-1, keepdims=True) + EPS)) * w


def _dot(a, b):
    return jnp.dot(a, b, preferred_element_type=F32)


def _dot_nt(a, b):
    return lax.dot_general(a, b, _NT, preferred_element_type=F32)


def _cast_rider_specs(weights, n_steps, step_of):
    in_specs, out_specs, out_shapes = [], [], []
    for w in weights:
        rows = w.shape[0] // n_steps
        assert rows * n_steps == w.shape[0] and rows % (2 * SUBLANES) == 0
        for specs in (in_specs, out_specs):
            specs.append(pl.BlockSpec((rows, w.shape[1]), lambda *ids: (step_of(*ids), 0)))
        out_shapes.append(jax.ShapeDtypeStruct(w.shape, BF16))
    return in_specs, out_specs, out_shapes


def _cast_riders(in_refs, out_refs):
    for i_ref, o_ref in zip(in_refs, out_refs):
        o_ref[...] = i_ref[...].astype(o_ref.dtype)


def _inproj_kernel(x_ref, nw_ref, wa_ref, wb_ref, wc_ref, o_ref, g_ref, ba_ref, h_ref, *, n_a, n_p, tail):
    j = pl.program_id(1)

    @pl.when(j == 0)
    def _():
        h_ref[...] = _rms(x_ref[...], nw_ref[...]).astype(BF16)
        ba_ref[...] = _dot_nt(h_ref[...], wc_ref[...])

    @pl.when(j < n_a)
    def _():
        o_ref[...] = _dot_nt(h_ref[...], wa_ref[...])

    @pl.when((j >= n_a) & (j < n_p - 1))
    def _():
        o_ref[...] = _dot_nt(h_ref[...], wb_ref[...])

    @pl.when(j == n_p - 1)
    def _():
        o_ref[:, :tail] = _dot_nt(h_ref[...], wb_ref[:tail, :])

    @pl.when(j >= n_p)
    def _():
        g_ref[...] = _sigmoid(_dot_nt(h_ref[...], wb_ref[...])).astype(g_ref.dtype)


def _inproj(x, nw, w_a, rows_a, w_b, w_c, n_gates, tm, tn):
    m, k = x.shape
    rows_b1 = w_b.shape[0] - n_gates
    n = rows_a + rows_b1
    n_a = rows_a // tn
    n_p = pl.cdiv(n, tn)
    assert rows_a % tn == 0 and n_gates % tn == 0 and rows_b1 % (2 * SUBLANES) == 0

    def wb_index(i, j):
        row = jnp.where(j < n_p, jnp.maximum(j - n_a, 0) * tn, rows_b1 + (j - n_p) * tn)
        return (pl.multiple_of(row, 2 * SUBLANES), 0)

    pipelined = (_nbytes((tm, k), F32) + 2 * _nbytes((tn, k), BF16) + _nbytes((tm, tn), F32)
                 + _nbytes((tm, tn), BF16) + _nbytes((tm, LANES), F32))
    resident = _nbytes((tm, k), BF16) + _nbytes((tm, k), F32) + _nbytes((LANES, k), BF16)
    return pl.pallas_call(
        functools.partial(_inproj_kernel, n_a=n_a, n_p=n_p, tail=n - (n_p - 1) * tn),
        grid=(m // tm, n_p + n_gates // tn),
        in_specs=[
            pl.BlockSpec((tm, k), lambda i, j: (i, 0)),
            pl.BlockSpec((1, k), lambda i, j: (0, 0)),
            pl.BlockSpec((tn, k), lambda i, j: (jnp.minimum(j, n_a - 1), 0)),
            pl.BlockSpec((pl.Element(tn), pl.Element(k)), wb_index),
            pl.BlockSpec((LANES, k), lambda i, j: (0, 0)),
        ],
        out_specs=[pl.BlockSpec((tm, tn), lambda i, j: (i, jnp.minimum(j, n_p - 1))),
                   pl.BlockSpec((tm, tn), lambda i, j: (i, jnp.maximum(j - n_p, 0))),
                   pl.BlockSpec((tm, LANES), lambda i, j: (i, 0))],
        out_shape=[jax.ShapeDtypeStruct((m, n), F32), jax.ShapeDtypeStruct((m, n_gates), BF16),
                   jax.ShapeDtypeStruct((m, LANES), F32)],
        scratch_shapes=[pltpu.VMEM((tm, k), BF16)],
        compiler_params=pltpu.CompilerParams(
            dimension_semantics=("parallel", "arbitrary"),
            vmem_limit_bytes=_vmem_limit(pipelined, resident)),
        name="inproj",
    )(x, nw, w_a, w_b, w_c)


def _norm_matmul_kernel(x_ref, nw_ref, w_ref, o_ref, h_ref):
    @pl.when(pl.program_id(1) == 0)
    def _():
        h_ref[...] = _rms(x_ref[...], nw_ref[...]).astype(BF16)

    o_ref[...] = _dot(h_ref[...], w_ref[...]).astype(o_ref.dtype)


def _norm_matmul(x, nw, w, tm, tn):
    m, k = x.shape
    n = w.shape[1]
    pipelined = _nbytes((tm, k), F32) + _nbytes((k, tn), BF16) + _nbytes((tm, tn), F32)
    resident = _nbytes((tm, k), BF16) + _nbytes((tm, k), F32)
    return pl.pallas_call(
        _norm_matmul_kernel,
        grid=(m // tm, n // tn),
        in_specs=[
            pl.BlockSpec((tm, k), lambda i, j: (i, 0)),
            pl.BlockSpec((1, k), lambda i, j: (0, 0)),
            pl.BlockSpec((k, tn), lambda i, j: (0, j)),
        ],
        out_specs=pl.BlockSpec((tm, tn), lambda i, j: (i, j)),
        out_shape=jax.ShapeDtypeStruct((m, n), F32),
        scratch_shapes=[pltpu.VMEM((tm, k), BF16)],
        compiler_params=pltpu.CompilerParams(
            dimension_semantics=("parallel", "arbitrary"),
            vmem_limit_bytes=_vmem_limit(pipelined, resident)),
        name="norm_matmul",
    )(x, nw, w)


def _dn_kernel(*refs, heads, dim, n_riders):
    qkv_ref, z_ref, ba_ref, cw_ref, gp_ref, onw_ref = refs[:6]
    o_ref = refs[6 + n_riders]
    ext_ref, state_ref = refs[-2:]
    _cast_riders(refs[6:6 + n_riders], refs[7 + n_riders:7 + 2 * n_riders])

    gt = DN_GROUP
    c = DN_CHUNK
    hd = heads * dim
    hs = range(heads)
    chunks = range(gt // c)

    @pl.when(pl.program_id(1) == 0)
    def _():
        ext_ref[0:SUBLANES, :] = jnp.zeros((SUBLANES, 3 * hd), F32)
        state_ref[...] = jnp.zeros_like(state_ref)

    ext_ref[SUBLANES:SUBLANES + gt, :] = qkv_ref[...]
    xe = ext_ref[...]
    conv = cw_ref[3:4, :] * xe[SUBLANES:]
    for s in (1, 2, 3):
        conv = conv + cw_ref[3 - s:4 - s, :] * pltpu.roll(xe, s, 0)[SUBLANES:]
    ext_ref[0:SUBLANES, :] = xe[gt:gt + SUBLANES]
    act = _silu(conv)

    ba = ba_ref[...]
    beta_all = _sigmoid(ba)
    xa = ba + gp_ref[1:2, :]
    softplus = jnp.maximum(xa, 0.0) + jnp.log1p(jnp.exp(-jnp.abs(xa)))
    g_all = -jnp.exp(gp_ref[0:1, :]) * softplus
    row_in_chunk = lax.broadcasted_iota(jnp.int32, (gt, LANES), 0) & (c - 1)
    gcum = g_all
    s = 1
    while s < c:
        gcum = gcum + jnp.where(row_in_chunk >= s, pltpu.roll(gcum, s, 0), 0.0)
        s *= 2
    exp_g = jnp.exp(gcum)
    kdec_parts, gc_rows = [], []
    for j in chunks:
        g_last = gcum[c * j + c - 1:c * j + c, :]
        kdec_parts.append(jnp.exp(g_last - gcum[c * j:c * (j + 1), :]))
        gc_rows.append(jnp.exp(g_last))
    kdec = jnp.concatenate(kdec_parts, axis=0)
    gcum_t = gcum.T

    ri = lax.broadcasted_iota(jnp.int32, (gt, gt), 0)
    ci = lax.broadcasted_iota(jnp.int32, (gt, gt), 1)
    same_chunk = (ri // c) == (ci // c)
    strict = same_chunk & (ri > ci)
    incl = same_chunk & (ri >= ci)

    qn, kn, kb, beta, eg = [], [], [], [], []
    for h in hs:
        qh = act[:, h * dim:(h + 1) * dim]
        kh = act[:, hd + h * dim:hd + (h + 1) * dim]
        qn.append(qh * lax.rsqrt(jnp.sum(qh * qh, axis=-1, keepdims=True) + EPS) * (dim ** -0.5))
        kn.append(kh * lax.rsqrt(jnp.sum(kh * kh, axis=-1, keepdims=True) + EPS))
        kb.append(kn[h].astype(BF16))
        beta.append(beta_all[:, h:h + 1])
        eg.append(exp_g[:, heads + h:heads + h + 1])

    decay = [jnp.exp(jnp.where(incl, gcum[:, heads + h:heads + h + 1] - gcum_t[heads + h:heads + h + 1, :],
                               -jnp.inf)) for h in hs]
    kk = [_dot_nt(kb[h], kb[h]) for h in hs]
    nmat = [jnp.where(strict, (beta[h] * kk[h]) * decay[h], 0.0) for h in hs]

    def same_block(size):
        return (ri // size) == (ci // size)

    eye = jnp.where(ri == ci, 1.0, 0.0)
    base = same_block(DN_BASE)
    pw = [jnp.where(base, nmat[h], 0.0) for h in hs]
    tinv = [eye - pw[h] for h in hs]
    pw = [pw[h].astype(BF16) for h in hs]
    order = 2
    while order < DN_BASE:
        pw = [_dot(pw[h], pw[h]).astype(BF16) for h in hs]
        tinv = [tinv[h] + _dot(tinv[h].astype(BF16), pw[h]) for h in hs]
        order *= 2
    size = DN_BASE
    while size < c:
        level = same_block(2 * size) & jnp.logical_not(same_block(size))
        off = [jnp.where(level, nmat[h], 0.0).astype(BF16) for h in hs]
        ct = [_dot(off[h], tinv[h].astype(BF16)).astype(BF16) for h in hs]
        tinv = [tinv[h] - _dot(tinv[h].astype(BF16), ct[h]) for h in hs]
        size *= 2

    sol = [jnp.concatenate([act[:, 2 * hd + h * dim:2 * hd + (h + 1) * dim] * beta[h],
                            kn[h] * (beta[h] * eg[h])], axis=1) for h in hs]
    sol = [sol[h] + _dot((tinv[h] - eye).astype(BF16), sol[h].astype(BF16)) for h in hs]

    pmat = [(_dot_nt(qn[h].astype(BF16), kb[h]) * decay[h]).astype(BF16) for h in hs]
    wq = [[jnp.concatenate([sol[h][c * j:c * (j + 1), dim:], (qn[h] * eg[h])[c * j:c * (j + 1)]],
                           axis=0).astype(BF16) for j in chunks] for h in hs]
    kd = [(kn[h] * kdec[:, heads + h:heads + h + 1]).astype(BF16) for h in hs]

    state = [state_ref[h] for h in hs]
    o_parts = [[] for _ in hs]
    for j in chunks:
        r = slice(c * j, c * (j + 1))
        ws = [_dot(wq[h][j], state[h].astype(BF16)) for h in hs]
        db = [(sol[h][r, :dim] - ws[h][:c]).astype(BF16) for h in hs]
        for h in hs:
            o_parts[h].append(ws[h][c:] + _dot(pmat[h][r, r], db[h]))
        state = [gc_rows[j][:, heads + h:heads + h + 1] * state[h]
                 + lax.dot_general(kd[h][r], db[h], _TN, preferred_element_type=F32) for h in hs]
    for h in hs:
        state_ref[h] = state[h]

    for h in hs:
        o = jnp.concatenate(o_parts[h], axis=0)
        zh = z_ref[:, h * dim:(h + 1) * dim]
        o_ref[:, h * dim:(h + 1) * dim] = (_rms(o, onw_ref[...]) * _silu(zh)).astype(o_ref.dtype)


def _deltanet(proj3, ba3, conv_w, gate_params, out_norm_w, riders, heads, dim, col_z):
    b, s, _ = proj3.shape
    hd = heads * dim
    gt = DN_GROUP
    ng = s // gt
    rider_in, rider_out, rider_shapes = _cast_rider_specs(riders, b * ng, lambda i, g: i * ng + g)
    pipelined = (_nbytes((gt, 3 * hd), F32) + _nbytes((gt, hd), F32) + _nbytes((gt, LANES), F32)
                 + _nbytes((gt, hd), BF16) + sum(_nbytes(w.shape, F32) * 3 // 2 for w in riders) // (b * ng))
    resident = (_nbytes((gt + SUBLANES, 3 * hd), F32) * 4 + _nbytes((heads, dim, dim), F32)
                + heads * 6 * _nbytes((gt, gt), F32))
    return pl.pallas_call(
        functools.partial(_dn_kernel, heads=heads, dim=dim, n_riders=len(riders)),
        grid=(b, ng),
        in_specs=[
            pl.BlockSpec((None, gt, 3 * hd), lambda i, g: (i, g, 0)),
            pl.BlockSpec((None, gt, hd), lambda i, g: (i, g, col_z // hd)),
            pl.BlockSpec((None, gt, LANES), lambda i, g: (i, g, 0)),
            pl.BlockSpec((4, 3 * hd), lambda i, g: (0, 0)),
            pl.BlockSpec((SUBLANES, LANES), lambda i, g: (0, 0)),
            pl.BlockSpec((1, dim), lambda i, g: (0, 0)),
            *rider_in,
        ],
        out_specs=[pl.BlockSpec((None, gt, hd), lambda i, g: (i, g, 0)), *rider_out],
        out_shape=[jax.ShapeDtypeStruct((b, s, hd), BF16), *rider_shapes],
        scratch_shapes=[pltpu.VMEM((gt + SUBLANES, 3 * hd), F32), pltpu.VMEM((heads, dim, dim), F32)],
        compiler_params=pltpu.CompilerParams(
            dimension_semantics=("parallel", "arbitrary"),
            vmem_limit_bytes=_vmem_limit(pipelined, resident, claim_all=True)),
        name="deltanet",
    )(proj3, proj3, ba3, conv_w, gate_params, out_norm_w, *riders)


def _swa_kernel(rb_ref, sink_ref, q_ref, kc_ref, kp_ref, vc_ref, vp_ref, qw_ref, kw_ref, o_ref, bias_ref,
                *, q_heads, kv_heads, dim):
    blk = SWA_BLOCK
    n = pl.program_id(1)
    qi = lax.broadcasted_iota(jnp.int32, (blk, 2 * blk), 0)
    kj = lax.broadcasted_iota(jnp.int32, (blk, 2 * blk), 1)
    dist = qi - kj + blk

    @pl.when((pl.program_id(0) == 0) & (n == 0))
    def _():
        max_exact = N_BUCKETS // 2
        nn = jnp.maximum(dist, 0)
        nf = jnp.maximum(nn, 1).astype(F32)
        large = max_exact + (jnp.log(nf / max_exact) / math.log(MAX_DISTANCE / max_exact)
                             * (N_BUCKETS - max_exact)).astype(jnp.int32)
        bucket = jnp.where(nn < max_exact, nn, jnp.minimum(large, N_BUCKETS - 1))
        for h in range(q_heads):
            acc = jnp.zeros((blk, 2 * blk), F32)
            for bk in range(N_BUCKETS):
                acc = jnp.where(bucket == bk, rb_ref[bk * q_heads + h], acc)
            acc = jnp.where((dist >= 0) & (dist < SWA_WINDOW), acc, -jnp.inf)
            bias_ref[h] = acc
            bias_ref[q_heads + h] = jnp.where(kj >= blk, acc, -jnp.inf)

    first = jnp.where(n == 0, q_heads, 0)
    group = q_heads // kv_heads
    for j in range(kv_heads):
        cols = slice(j * dim, (j + 1) * dim)
        k_all = _rms(jnp.concatenate([kp_ref[:, cols], kc_ref[:, cols]], axis=0), kw_ref[...]).astype(BF16)
        v_all = jnp.concatenate([vp_ref[:, cols], vc_ref[:, cols]], axis=0).astype(BF16)
        for u in range(SWA_STEP_BLOCKS):
            rows = slice(u * blk, (u + 1) * blk)
            kwin, vwin = k_all[u * blk:(u + 2) * blk], v_all[u * blk:(u + 2) * blk]
            qcat = jnp.concatenate(
                [_rms(q_ref[rows, (j * group + i) * dim:(j * group + i + 1) * dim], qw_ref[...])
                 for i in range(group)], axis=0).astype(BF16)
            logits = _dot_nt(qcat, kwin) * (dim ** -0.5)
            for i in range(group):
                h = j * group + i
                lg = logits[i * blk:(i + 1) * blk] + bias_ref[(first if u == 0 else 0) + h]
                sink = sink_ref[h]
                mx = jnp.maximum(jnp.max(lg, axis=-1, keepdims=True), sink)
                e = jnp.exp(lg - mx)
                den = jnp.sum(e, axis=-1, keepdims=True) + jnp.exp(sink - mx)
                pv = _dot(e.astype(BF16), vwin)
                o_ref[rows, h * dim:(h + 1) * dim] = (pv / den).astype(o_ref.dtype)


def _swa(proj3, rel_bias_flat, sinks, q_norm_w, k_norm_w, q_heads, kv_heads, dim, col_q, col_k, col_v):
    b, s, _ = proj3.shape
    blk = SWA_BLOCK
    step = SWA_STEP_BLOCKS * blk
    qw, kvw = q_heads * dim, kv_heads * dim
    smem = pl.BlockSpec(memory_space=pltpu.SMEM)
    pipelined = (_nbytes((step, qw), F32) + 2 * _nbytes((step + blk, kvw), F32) + _nbytes((step, qw), BF16))
    resident = (_nbytes((2 * q_heads, blk, 2 * blk), F32) + 2 * pipelined
                + 2 * SWA_STEP_BLOCKS * _nbytes((q_heads // kv_heads * blk, 2 * blk), F32))

    def prev(n):
        return jnp.maximum(SWA_STEP_BLOCKS * n - 1, 0)

    return pl.pallas_call(
        functools.partial(_swa_kernel, q_heads=q_heads, kv_heads=kv_heads, dim=dim),
        grid=(b, s // step),
        in_specs=[
            smem, smem,
            pl.BlockSpec((None, step, qw), lambda i, n: (i, n, col_q // qw)),
            pl.BlockSpec((None, step, kvw), lambda i, n: (i, n, col_k // kvw)),
            pl.BlockSpec((None, blk, kvw), lambda i, n: (i, prev(n), col_k // kvw)),
            pl.BlockSpec((None, step, kvw), lambda i, n: (i, n, col_v // kvw)),
            pl.BlockSpec((None, blk, kvw), lambda i, n: (i, prev(n), col_v // kvw)),
            pl.BlockSpec((1, dim), lambda i, n: (0, 0)),
            pl.BlockSpec((1, dim), lambda i, n: (0, 0)),
        ],
        out_specs=pl.BlockSpec((None, step, qw), lambda i, n: (i, n, 0)),
        out_shape=jax.ShapeDtypeStruct((b, s, qw), BF16),
        scratch_shapes=[pltpu.VMEM((2 * q_heads, blk, 2 * blk), F32)],
        compiler_params=pltpu.CompilerParams(
            dimension_semantics=("arbitrary", "arbitrary"),
            vmem_limit_bytes=_vmem_limit(pipelined, resident, claim_all=True)),
        name="swa",
    )(rel_bias_flat, sinks, proj3, proj3, proj3, proj3, proj3, q_norm_w, k_norm_w)


def _memattn_kernel(*refs, heads, dim, n_riders):
    q_ref, k_ref, v_ref, qw_ref, kw_ref = refs[:5]
    o_ref = refs[5 + n_riders]
    _cast_riders(refs[5:5 + n_riders], refs[6 + n_riders:])
    for h in range(heads):
        cols = slice(h * dim, (h + 1) * dim)
        qn = _rms(q_ref[:, cols], qw_ref[...]).astype(BF16)
        kn = _rms(k_ref[:, cols], kw_ref[...]).astype(BF16)
        lg = _dot_nt(qn, kn) * (dim ** -0.5)
        e = jnp.exp(lg - jnp.max(lg, axis=-1, keepdims=True))
        den = jnp.sum(e, axis=-1, keepdims=True)
        pv = _dot(e.astype(BF16), v_ref[:, cols].astype(BF16))
        o_ref[:, cols] = (pv / den).astype(o_ref.dtype)


def _memattn(proj3, mkv3, q_norm_w, k_norm_w, riders, heads, dim, col_q, tq):
    b, s, _ = proj3.shape
    m = mkv3.shape[1]
    w = heads * dim
    nt = s // tq
    rider_in, rider_out, rider_shapes = _cast_rider_specs(riders, b * nt, lambda i, t: i * nt + t)
    pipelined = (_nbytes((tq, w), F32) + 2 * _nbytes((m, w), F32) + _nbytes((tq, w), BF16)
                 + sum(_nbytes(r.shape, F32) * 3 // 2 for r in riders) // (b * nt))
    return pl.pallas_call(
        functools.partial(_memattn_kernel, heads=heads, dim=dim, n_riders=len(riders)),
        grid=(b, nt),
        in_specs=[
            pl.BlockSpec((pl.Element(tq), pl.Element(w)), lambda i, t: (pl.multiple_of(i * s + t * tq, tq), col_q)),
            pl.BlockSpec((None, m, w), lambda i, t: (i, 0, 0)),
            pl.BlockSpec((None, m, w), lambda i, t: (i, 0, 1)),
            pl.BlockSpec((1, dim), lambda i, t: (0, 0)),
            pl.BlockSpec((1, dim), lambda i, t: (0, 0)),
            *rider_in,
        ],
        out_specs=[pl.BlockSpec((None, tq, w), lambda i, t: (i, t, 0)), *rider_out],
        out_shape=[jax.ShapeDtypeStruct((b, s, w), BF16), *rider_shapes],
        compiler_params=pltpu.CompilerParams(
            dimension_semantics=("parallel", "parallel"),
            vmem_limit_bytes=_vmem_limit(pipelined, 2 * heads * _nbytes((tq, m), F32), claim_all=True)),
        name="memattn",
    )(proj3.reshape(b * s, -1), mkv3, mkv3, q_norm_w, k_norm_w, *riders)


def _merge_kernel(od_ref, os_ref, om_ref, pd_ref, ps_ref, pm_ref, gd_ref, gs_ref, gm_ref, o_ref):
    merged = (gd_ref[...].astype(F32) * _dot(od_ref[...], pd_ref[...])
              + gs_ref[...].astype(F32) * _dot(os_ref[...], ps_ref[...])
              + gm_ref[...].astype(F32) * _dot(om_ref[...], pm_ref[...]))
    o_ref[...] = merged.astype(o_ref.dtype)


def _merge(o_dn, o_swa, o_mem, p_dn, p_swa, p_mem, gates, tm, tn):
    t = o_dn.shape[0]
    d = p_dn.shape[1]
    wd, ws, wm = o_dn.shape[1], o_swa.shape[1], o_mem.shape[1]
    per_branch = d // tn
    pipelined = ((_nbytes((tm, wd), BF16) + _nbytes((tm, ws), BF16) + _nbytes((tm, wm), BF16))
                 + (_nbytes((wd, tn), BF16) + _nbytes((ws, tn), BF16) + _nbytes((wm, tn), BF16))
                 + 4 * _nbytes((tm, tn), BF16))
    return pl.pallas_call(
        _merge_kernel,
        grid=(d // tn, t // tm),
        in_specs=[
            pl.BlockSpec((tm, wd), lambda j, i: (i, 0)),
            pl.BlockSpec((tm, ws), lambda j, i: (i, 0)),
            pl.BlockSpec((tm, wm), lambda j, i: (i, 0)),
            pl.BlockSpec((wd, tn), lambda j, i: (0, j)),
            pl.BlockSpec((ws, tn), lambda j, i: (0, j)),
            pl.BlockSpec((wm, tn), lambda j, i: (0, j)),
            pl.BlockSpec((tm, tn), lambda j, i: (i, j)),
            pl.BlockSpec((tm, tn), lambda j, i: (i, per_branch + j)),
            pl.BlockSpec((tm, tn), lambda j, i: (i, 2 * per_branch + j)),
        ],
        out_specs=pl.BlockSpec((tm, tn), lambda j, i: (i, j)),
        out_shape=jax.ShapeDtypeStruct((t, d), BF16),
        compiler_params=pltpu.CompilerParams(
            dimension_semantics=("parallel", "parallel"),
            vmem_limit_bytes=_vmem_limit(pipelined, 3 * _nbytes((tm, tn), F32))),
        name="merge",
    )(o_dn, o_swa, o_mem, p_dn, p_swa, p_mem, gates, gates, gates)


def _outproj_kernel(x_ref, m_ref, w_ref, nw_ref, x1_ref, h_ref):
    x1 = x_ref[...] + _dot(m_ref[...], w_ref[...])
    x1_ref[...] = x1
    h_ref[...] = _rms(x1, nw_ref[...]).astype(h_ref.dtype)


def _outproj(x, merged, w_out, norm_w, tm):
    t, d = x.shape
    pipelined = 2 * _nbytes((tm, d), F32) + 2 * _nbytes((tm, d), BF16) + _nbytes((d, d), BF16)
    return pl.pallas_call(
        _outproj_kernel,
        grid=(t // tm,),
        in_specs=[
            pl.BlockSpec((tm, d), lambda i: (i, 0)),
            pl.BlockSpec((tm, d), lambda i: (i, 0)),
            pl.BlockSpec((d, d), lambda i: (0, 0)),
            pl.BlockSpec((1, d), lambda i: (0, 0)),
        ],
        out_specs=[pl.BlockSpec((tm, d), lambda i: (i, 0)), pl.BlockSpec((tm, d), lambda i: (i, 0))],
        out_shape=[jax.ShapeDtypeStruct((t, d), F32), jax.ShapeDtypeStruct((t, d), BF16)],
        compiler_params=pltpu.CompilerParams(
            dimension_semantics=("parallel",),
            vmem_limit_bytes=_vmem_limit(pipelined, 2 * _nbytes((tm, d), F32))),
        name="outproj",
    )(x, merged, w_out, norm_w)


def _mlp_kernel(h_ref, x1_ref, wu_ref, wd_ref, o_ref):
    @pl.when(pl.program_id(1) == 0)
    def _():
        o_ref[...] = x1_ref[...]

    a = jnp.maximum(_dot(h_ref[...], wu_ref[...]), 0.0)
    o_ref[...] += _dot((a * a).astype(BF16), wd_ref[...])


def _mlp(h2, x1, w_up, w_down, tm, tf):
    t, d = x1.shape
    f = w_up.shape[1]
    pipelined = (_nbytes((tm, d), BF16) + 2 * _nbytes((tm, d), F32)
                 + _nbytes((d, tf), BF16) + _nbytes((tf, d), BF16))
    resident = 2 * _nbytes((tm, tf), F32)
    return pl.pallas_call(
        _mlp_kernel,
        grid=(t // tm, f // tf),
        in_specs=[
            pl.BlockSpec((tm, d), lambda i, k: (i, 0)),
            pl.BlockSpec((tm, d), lambda i, k: (i, 0)),
            pl.BlockSpec((d, tf), lambda i, k: (0, k)),
            pl.BlockSpec((tf, d), lambda i, k: (k, 0)),
        ],
        out_specs=pl.BlockSpec((tm, d), lambda i, k: (i, 0)),
        out_shape=jax.ShapeDtypeStruct((t, d), F32),
        compiler_params=pltpu.CompilerParams(
            dimension_semantics=("parallel", "arbitrary"),
            vmem_limit_bytes=_vmem_limit(pipelined, resident)),
        name="mlp",
    )(h2, x1, w_up, w_down)


def _layer(x, mem, attn_norm_w, w_in, dn_conv_w, dn_a_log, dn_dt_bias, dn_out_norm_w, swa_q_norm_w,
           swa_k_norm_w, swa_sinks, rel_bias, mem_norm_w, w_mem_kv, xq_norm_w, xk_norm_w,
           p_dn, p_swa, p_mem, w_out, mlp_norm_w, w_mlp_up, w_mlp_down):
    b, s, d = x.shape
    m = mem.shape[1]
    t = b * s

    dn_heads = dn_a_log.shape[0]
    dn_dim = dn_out_norm_w.shape[0]
    dn_w = dn_heads * dn_dim
    swa_heads = swa_sinks.shape[0]
    swa_dim = swa_q_norm_w.shape[0]
    swa_w = swa_heads * swa_dim
    mem_dim = xq_norm_w.shape[0]
    mem_w = p_mem.shape[0]
    mem_heads = mem_w // mem_dim
    swa_kv_w = (w_in.shape[1] - 4 * dn_w - 2 * dn_heads - swa_w - mem_w - 3 * d) // 2
    swa_kv = swa_kv_w // swa_dim

    n_ba = 2 * dn_heads
    src_ba = 4 * dn_w
    src_swa = src_ba + n_ba
    w_t = w_in.T.astype(BF16)
    w_b = w_t[src_swa:]
    w_c = jnp.concatenate([w_t[src_ba:src_swa], jnp.zeros((LANES - n_ba, d), BF16)], axis=0)
    col = {"qkv": 0, "z": 3 * dn_w, "sq": src_ba}
    col["sk"] = col["sq"] + swa_w
    col["sv"] = col["sk"] + swa_kv_w
    col["mq"] = col["sv"] + swa_kv_w

    proj, gates, ba = _inproj(x.reshape(t, d), attn_norm_w.reshape(1, d), w_t, src_ba, w_b, w_c, 3 * d,
                              tm=1024, tn=1024)
    proj3 = proj.reshape(b, s, -1)
    mkv = _norm_matmul(mem.reshape(b * m, d), mem_norm_w.reshape(1, d), w_mem_kv.astype(BF16), tm=512, tn=512)
    mkv3 = mkv.reshape(b, m, -1)

    gate_params = jnp.zeros((SUBLANES, LANES), F32)
    gate_params = gate_params.at[0, dn_heads:2 * dn_heads].set(dn_a_log)
    gate_params = gate_params.at[1, dn_heads:2 * dn_heads].set(dn_dt_bias)
    o_dn, w_up_bf, w_down_bf = _deltanet(proj3, ba.reshape(b, s, LANES), dn_conv_w, gate_params,
                                         dn_out_norm_w.reshape(1, dn_dim), [w_mlp_up, w_mlp_down],
                                         dn_heads, dn_dim, col["z"])
    o_swa = _swa(proj3, rel_bias.reshape(-1), swa_sinks, swa_q_norm_w.reshape(1, swa_dim),
                 swa_k_norm_w.reshape(1, swa_dim), swa_heads, swa_kv, swa_dim, col["sq"], col["sk"], col["sv"])
    o_mem, w_out_bf, p_dn_bf, p_swa_bf, p_mem_bf = _memattn(
        proj3, mkv3, xq_norm_w.reshape(1, mem_dim), xk_norm_w.reshape(1, mem_dim),
        [w_out, p_dn, p_swa, p_mem], mem_heads, mem_dim, col["mq"], tq=1024)

    merged = _merge(o_dn.reshape(t, dn_w), o_swa.reshape(t, swa_w), o_mem.reshape(t, mem_w),
                    p_dn_bf, p_swa_bf, p_mem_bf, gates, tm=1024, tn=1024)
    x1, h2 = _outproj(x.reshape(t, d), merged, w_out_bf, mlp_norm_w.reshape(1, d), tm=512)
    out = _mlp(h2, x1, w_up_bf, w_down_bf, tm=512, tf=1024)
    return out.reshape(b, s, d)


def kernel(x, mem, attn_norm_w, w_in, dn_conv_w, dn_A_log, dn_dt_bias, dn_out_norm_w, swa_q_norm_w,
           swa_k_norm_w, swa_sinks, rel_bias, mem_norm_w, w_mem_kv, xq_norm_w, xk_norm_w, p_dn, p_swa,
           p_mem, w_out, mlp_norm_w, w_mlp_up, w_mlp_down):
    depth = w_in.shape[0]
    for l in range(depth):
        x = _layer(x, mem, attn_norm_w[l], w_in[l], dn_conv_w[l], dn_A_log[l], dn_dt_bias[l],
                   dn_out_norm_w[l], swa_q_norm_w[l], swa_k_norm_w[l], swa_sinks[l], rel_bias,
                   mem_norm_w[l], w_mem_kv[l], xq_norm_w[l], xk_norm_w[l], p_dn[l], p_swa[l], p_mem[l],
                   w_out[l], mlp_norm_w[l], w_mlp_up[l], w_mlp_down[l])
    return x
```

```python
import functools
import math

import jax
import jax.numpy as jnp
from jax import lax
from jax.experimental import pallas as pl
from jax.experimental.pallas import tpu as pltpu

F32 = jnp.float32
BF16 = jnp.bfloat16

EPS = 1e-6
LANES = 128
SUBLANES = 8
V7X_VMEM_BYTES = 64 * 1024 * 1024

DN_CHUNK = 64
DN_GROUP = 256
DN_BASE = 8
SWA_WINDOW = 128
SWA_BLOCK = 128
SWA_STEP_BLOCKS = 4
N_BUCKETS = 32
MAX_DISTANCE = 128

_NT = (((1,), (1,)), ((), ()))
_TN = (((0,), (0,)), ((), ()))


COMPILER_SCRATCH_BYTES = 4 * 1024 * 1024
VMEM_RESERVE_BYTES = 8 * 1024 * 1024


def _vmem_limit(pipelined_bytes, resident_bytes, claim_all=False):
    cap = V7X_VMEM_BYTES - VMEM_RESERVE_BYTES
    want = 2 * pipelined_bytes + resident_bytes + COMPILER_SCRATCH_BYTES
    return int(cap if claim_all else min(want, cap))


def _nbytes(shape, dtype):
    return math.prod(shape) * jnp.dtype(dtype).itemsize


def _sigmoid(v):
    return 0.5 * jnp.tanh(0.5 * v) + 0.5


def _silu(v):
    return v * _sigmoid(v)


def _rms(v, w):
    return (v * lax.rsqrt(jnp.mean(v * v, axis=-1, keepdims=True) + EPS)) * w


def _dot(a, b):
    return jnp.dot(a, b, preferred_element_type=F32)


def _dot_nt(a, b):
    return lax.dot_general(a, b, _NT, preferred_element_type=F32)


def _cast_rider_specs(weights, n_steps, step_of):
    in_specs, out_specs, out_shapes = [], [], []
    for w in weights:
        rows = w.shape[0] // n_steps
        assert rows * n_steps == w.shape[0] and rows % (2 * SUBLANES) == 0
        for specs in (in_specs, out_specs):
            specs.append(pl.BlockSpec((rows, w.shape[1]), lambda *ids: (step_of(*ids), 0)))
        out_shapes.append(jax.ShapeDtypeStruct(w.shape, BF16))
    return in_specs, out_specs, out_shapes


def _cast_riders(in_refs, out_refs):
    for i_ref, o_ref in zip(in_refs, out_refs):
        o_ref[...] = i_ref[...].astype(o_ref.dtype)


IN_RING = 3


def _inproj_kernel(x_ref, nw_ref, wa_hbm, wb_hbm, wc_ref, o_ref, g_ref, ba_ref, h_ref, wbuf, sem,
                   *, n_a, n_p, tail, rows_b1):
    i, j = pl.program_id(0), pl.program_id(1)
    n_j = pl.num_programs(1)
    tn = wbuf.shape[1]
    step = i * n_j + j
    n_steps = pl.num_programs(0) * n_j

    def tile_copy(col_step, slot, from_b):
        if from_b:
            row = jnp.where(col_step < n_p, (col_step - n_a) * tn, rows_b1 + (col_step - n_p) * tn)
            src = wb_hbm.at[pl.ds(pl.multiple_of(row, 2 * SUBLANES), tn), :]
        else:
            src = wa_hbm.at[pl.ds(pl.multiple_of(col_step * tn, tn), tn), :]
        return pltpu.make_async_copy(src, wbuf.at[slot], sem.at[slot])

    def request(ahead):
        target = step + ahead
        col_step = lax.rem(target, n_j)
        slot = lax.rem(target, IN_RING)

        @pl.when((target < n_steps) & (col_step < n_a))
        def _():
            tile_copy(col_step, slot, from_b=False).start()

        @pl.when((target < n_steps) & (col_step >= n_a))
        def _():
            tile_copy(col_step, slot, from_b=True).start()

    @pl.when(step == 0)
    def _():
        for ahead in range(IN_RING - 1):
            request(ahead)

    request(IN_RING - 1)
    slot = lax.rem(step, IN_RING)
    pltpu.make_async_copy(wb_hbm.at[pl.ds(0, tn), :], wbuf.at[slot], sem.at[slot]).wait()
    w_ref = wbuf.at[slot]

    @pl.when(j == 0)
    def _():
        h_ref[...] = _rms(x_ref[...], nw_ref[...]).astype(BF16)
        ba_ref[...] = _dot_nt(h_ref[...], wc_ref[...])

    @pl.when(j < n_p - 1)
    def _():
        o_ref[...] = _dot_nt(h_ref[...], w_ref[...])

    @pl.when(j == n_p - 1)
    def _():
        o_ref[:, :tail] = _dot_nt(h_ref[...], w_ref[:tail, :])

    @pl.when(j >= n_p)
    def _():
        g_ref[...] = _sigmoid(_dot_nt(h_ref[...], w_ref[...])).astype(g_ref.dtype)


def _inproj(x, nw, w_a, rows_a, w_b, w_c, n_gates, tm, tn):
    m, k = x.shape
    rows_b1 = w_b.shape[0] - n_gates
    n = rows_a + rows_b1
    n_a = rows_a // tn
    n_p = pl.cdiv(n, tn)
    assert rows_a % tn == 0 and n_gates % tn == 0 and rows_b1 % (2 * SUBLANES) == 0

    pipelined = (_nbytes((tm, k), F32) + _nbytes((tm, tn), F32) + _nbytes((tm, tn), BF16)
                 + _nbytes((tm, LANES), F32))
    resident = (_nbytes((tm, k), BF16) + _nbytes((tm, k), F32) + _nbytes((LANES, k), BF16)
                + IN_RING * _nbytes((tn, k), BF16))
    return pl.pallas_call(
        functools.partial(_inproj_kernel, n_a=n_a, n_p=n_p, tail=n - (n_p - 1) * tn, rows_b1=rows_b1),
        grid=(m // tm, n_p + n_gates // tn),
        in_specs=[
            pl.BlockSpec((tm, k), lambda i, j: (i, 0)),
            pl.BlockSpec((1, k), lambda i, j: (0, 0)),
            pl.BlockSpec(memory_space=pl.ANY),
            pl.BlockSpec(memory_space=pl.ANY),
            pl.BlockSpec((LANES, k), lambda i, j: (0, 0)),
        ],
        out_specs=[pl.BlockSpec((tm, tn), lambda i, j: (i, jnp.minimum(j, n_p - 1))),
                   pl.BlockSpec((tm, tn), lambda i, j: (i, jnp.maximum(j - n_p, 0))),
                   pl.BlockSpec((tm, LANES), lambda i, j: (i, 0))],
        out_shape=[jax.ShapeDtypeStruct((m, n), F32), jax.ShapeDtypeStruct((m, n_gates), BF16),
                   jax.ShapeDtypeStruct((m, LANES), F32)],
        scratch_shapes=[pltpu.VMEM((tm, k), BF16), pltpu.VMEM((IN_RING, tn, k), BF16),
                        pltpu.SemaphoreType.DMA((IN_RING,))],
        compiler_params=pltpu.CompilerParams(
            dimension_semantics=("arbitrary", "arbitrary"),
            vmem_limit_bytes=_vmem_limit(pipelined, resident)),
        name="inproj",
    )(x, nw, w_a, w_b, w_c)


def _norm_matmul_kernel(x_ref, nw_ref, w_ref, o_ref, h_ref):
    @pl.when(pl.program_id(1) == 0)
    def _():
        h_ref[...] = _rms(x_ref[...], nw_ref[...]).astype(BF16)

    o_ref[...] = _dot(h_ref[...], w_ref[...]).astype(o_ref.dtype)


def _norm_matmul(x, nw, w, tm, tn):
    m, k = x.shape
    n = w.shape[1]
    pipelined = _nbytes((tm, k), F32) + _nbytes((k, tn), BF16) + _nbytes((tm, tn), F32)
    resident = _nbytes((tm, k), BF16) + _nbytes((tm, k), F32)
    return pl.pallas_call(
        _norm_matmul_kernel,
        grid=(m // tm, n // tn),
        in_specs=[
            pl.BlockSpec((tm, k), lambda i, j: (i, 0)),
            pl.BlockSpec((1, k), lambda i, j: (0, 0)),
            pl.BlockSpec((k, tn), lambda i, j: (0, j)),
        ],
        out_specs=pl.BlockSpec((tm, tn), lambda i, j: (i, j)),
        out_shape=jax.ShapeDtypeStruct((m, n), F32),
        scratch_shapes=[pltpu.VMEM((tm, k), BF16)],
        compiler_params=pltpu.CompilerParams(
            dimension_semantics=("parallel", "arbitrary"),
            vmem_limit_bytes=_vmem_limit(pipelined, resident)),
        name="norm_matmul",
    )(x, nw, w)


def _dn_kernel(*refs, heads, dim, n_riders):
    qkv_ref, z_ref, ba_ref, cw_ref, gp_ref, onw_ref = refs[:6]
    o_ref = refs[6 + n_riders]
    ext_ref, state_ref = refs[-2:]
    _cast_riders(refs[6:6 + n_riders], refs[7 + n_riders:7 + 2 * n_riders])

    gt = DN_GROUP
    c = DN_CHUNK
    hd = heads * dim
    hs = range(heads)
    chunks = range(gt // c)

    @pl.when(pl.program_id(1) == 0)
    def _():
        ext_ref[0:SUBLANES, :] = jnp.zeros((SUBLANES, 3 * hd), F32)
        state_ref[...] = jnp.zeros_like(state_ref)

    ext_ref[SUBLANES:SUBLANES + gt, :] = qkv_ref[...]
    xe = ext_ref[...]
    conv = cw_ref[3:4, :] * xe[SUBLANES:]
    for s in (1, 2, 3):
        conv = conv + cw_ref[3 - s:4 - s, :] * pltpu.roll(xe, s, 0)[SUBLANES:]
    ext_ref[0:SUBLANES, :] = xe[gt:gt + SUBLANES]
    act = _silu(conv)

    ba = ba_ref[...]
    beta_all = _sigmoid(ba)
    xa = ba + gp_ref[1:2, :]
    softplus = jnp.maximum(xa, 0.0) + jnp.log1p(jnp.exp(-jnp.abs(xa)))
    g_all = -jnp.exp(gp_ref[0:1, :]) * softplus
    row_in_chunk = lax.broadcasted_iota(jnp.int32, (gt, LANES), 0) & (c - 1)
    gcum = g_all
    s = 1
    while s < c:
        gcum = gcum + jnp.where(row_in_chunk >= s, pltpu.roll(gcum, s, 0), 0.0)
        s *= 2
    exp_g = jnp.exp(gcum)
    kdec_parts, gc_rows = [], []
    for j in chunks:
        g_last = gcum[c * j + c - 1:c * j + c, :]
        kdec_parts.append(jnp.exp(g_last - gcum[c * j:c * (j + 1), :]))
        gc_rows.append(jnp.exp(g_last))
    kdec = jnp.concatenate(kdec_parts, axis=0)
    gcum_t = gcum.T

    ri = lax.broadcasted_iota(jnp.int32, (gt, gt), 0)
    ci = lax.broadcasted_iota(jnp.int32, (gt, gt), 1)
    same_chunk = (ri // c) == (ci // c)
    strict = same_chunk & (ri > ci)
    incl = same_chunk & (ri >= ci)

    qn, kn, kb, beta, eg = [], [], [], [], []
    for h in hs:
        qh = act[:, h * dim:(h + 1) * dim]
        kh = act[:, hd + h * dim:hd + (h + 1) * dim]
        qn.append(qh * lax.rsqrt(jnp.sum(qh * qh, axis=-1, keepdims=True) + EPS) * (dim ** -0.5))
        kn.append(kh * lax.rsqrt(jnp.sum(kh * kh, axis=-1, keepdims=True) + EPS))
        kb.append(kn[h].astype(BF16))
        beta.append(beta_all[:, h:h + 1])
        eg.append(exp_g[:, heads + h:heads + h + 1])

    decay = [jnp.exp(jnp.where(incl, gcum[:, heads + h:heads + h + 1] - gcum_t[heads + h:heads + h + 1, :],
                               -jnp.inf)) for h in hs]
    kk = [_dot_nt(kb[h], kb[h]) for h in hs]
    nmat = [jnp.where(strict, (beta[h] * kk[h]) * decay[h], 0.0) for h in hs]

    def same_block(size):
        return (ri // size) == (ci // size)

    eye = jnp.where(ri == ci, 1.0, 0.0)
    base = same_block(DN_BASE)
    pw = [jnp.where(base, nmat[h], 0.0) for h in hs]
    tinv = [eye - pw[h] for h in hs]
    pw = [pw[h].astype(BF16) for h in hs]
    order = 2
    while order < DN_BASE:
        pw = [_dot(pw[h], pw[h]).astype(BF16) for h in hs]
        tinv = [tinv[h] + _dot(tinv[h].astype(BF16), pw[h]) for h in hs]
        order *= 2
    size = DN_BASE
    while size < c:
        level = same_block(2 * size) & jnp.logical_not(same_block(size))
        off = [jnp.where(level, nmat[h], 0.0).astype(BF16) for h in hs]
        ct = [_dot(off[h], tinv[h].astype(BF16)).astype(BF16) for h in hs]
        tinv = [tinv[h] - _dot(tinv[h].astype(BF16), ct[h]) for h in hs]
        size *= 2

    sol = [jnp.concatenate([act[:, 2 * hd + h * dim:2 * hd + (h + 1) * dim] * beta[h],
                            kn[h] * (beta[h] * eg[h])], axis=1) for h in hs]
    sol = [sol[h] + _dot((tinv[h] - eye).astype(BF16), sol[h].astype(BF16)) for h in hs]

    pmat = [(_dot_nt(qn[h].astype(BF16), kb[h]) * decay[h]).astype(BF16) for h in hs]
    wq = [[jnp.concatenate([sol[h][c * j:c * (j + 1), dim:], (qn[h] * eg[h])[c * j:c * (j + 1)]],
                           axis=0).astype(BF16) for j in chunks] for h in hs]
    kd = [(kn[h] * kdec[:, heads + h:heads + h + 1]).astype(BF16) for h in hs]

    state = [state_ref[h] for h in hs]
    o_parts = [[] for _ in hs]
    for j in chunks:
        r = slice(c * j, c * (j + 1))
        ws = [_dot(wq[h][j], state[h].astype(BF16)) for h in hs]
        db = [(sol[h][r, :dim] - ws[h][:c]).astype(BF16) for h in hs]
        for h in hs:
            o_parts[h].append(ws[h][c:] + _dot(pmat[h][r, r], db[h]))
        state = [gc_rows[j][:, heads + h:heads + h + 1] * state[h]
                 + lax.dot_general(kd[h][r], db[h], _TN, preferred_element_type=F32) for h in hs]
    for h in hs:
        state_ref[h] = state[h]

    for h in hs:
        o = jnp.concatenate(o_parts[h], axis=0)
        zh = z_ref[:, h * dim:(h + 1) * dim]
        o_ref[:, h * dim:(h + 1) * dim] = (_rms(o, onw_ref[...]) * _silu(zh)).astype(o_ref.dtype)


def _deltanet(proj3, ba3, conv_w, gate_params, out_norm_w, riders, heads, dim, col_z):
    b, s, _ = proj3.shape
    hd = heads * dim
    gt = DN_GROUP
    ng = s // gt
    rider_in, rider_out, rider_shapes = _cast_rider_specs(riders, b * ng, lambda i, g: i * ng + g)
    pipelined = (_nbytes((gt, 3 * hd), F32) + _nbytes((gt, hd), F32) + _nbytes((gt, LANES), F32)
                 + _nbytes((gt, hd), BF16) + sum(_nbytes(w.shape, F32) * 3 // 2 for w in riders) // (b * ng))
    resident = (_nbytes((gt + SUBLANES, 3 * hd), F32) * 4 + _nbytes((heads, dim, dim), F32)
                + heads * 6 * _nbytes((gt, gt), F32))
    return pl.pallas_call(
        functools.partial(_dn_kernel, heads=heads, dim=dim, n_riders=len(riders)),
        grid=(b, ng),
        in_specs=[
            pl.BlockSpec((None, gt, 3 * hd), lambda i, g: (i, g, 0)),
            pl.BlockSpec((None, gt, hd), lambda i, g: (i, g, col_z // hd)),
            pl.BlockSpec((None, gt, LANES), lambda i, g: (i, g, 0)),
            pl.BlockSpec((4, 3 * hd), lambda i, g: (0, 0)),
            pl.BlockSpec((SUBLANES, LANES), lambda i, g: (0, 0)),
            pl.BlockSpec((1, dim), lambda i, g: (0, 0)),
            *rider_in,
        ],
        out_specs=[pl.BlockSpec((None, gt, hd), lambda i, g: (i, g, 0)), *rider_out],
        out_shape=[jax.ShapeDtypeStruct((b, s, hd), BF16), *rider_shapes],
        scratch_shapes=[pltpu.VMEM((gt + SUBLANES, 3 * hd), F32), pltpu.VMEM((heads, dim, dim), F32)],
        compiler_params=pltpu.CompilerParams(
            dimension_semantics=("parallel", "arbitrary"),
            vmem_limit_bytes=_vmem_limit(pipelined, resident, claim_all=True)),
        name="deltanet",
    )(proj3, proj3, ba3, conv_w, gate_params, out_norm_w, *riders)


def _swa_kernel(rb_ref, sink_ref, q_ref, kc_ref, kp_ref, vc_ref, vp_ref, qw_ref, kw_ref, o_ref, bias_ref,
                *, q_heads, kv_heads, dim):
    blk = SWA_BLOCK
    n = pl.program_id(1)
    qi = lax.broadcasted_iota(jnp.int32, (blk, 2 * blk), 0)
    kj = lax.broadcasted_iota(jnp.int32, (blk, 2 * blk), 1)
    dist = qi - kj + blk

    @pl.when((pl.program_id(0) == 0) & (n == 0))
    def _():
        max_exact = N_BUCKETS // 2
        nn = jnp.maximum(dist, 0)
        nf = jnp.maximum(nn, 1).astype(F32)
        large = max_exact + (jnp.log(nf / max_exact) / math.log(MAX_DISTANCE / max_exact)
                             * (N_BUCKETS - max_exact)).astype(jnp.int32)
        bucket = jnp.where(nn < max_exact, nn, jnp.minimum(large, N_BUCKETS - 1))
        for h in range(q_heads):
            acc = jnp.zeros((blk, 2 * blk), F32)
            for bk in range(N_BUCKETS):
                acc = jnp.where(bucket == bk, rb_ref[bk * q_heads + h], acc)
            acc = jnp.where((dist >= 0) & (dist < SWA_WINDOW), acc, -jnp.inf)
            bias_ref[h] = acc
            bias_ref[q_heads + h] = jnp.where(kj >= blk, acc, -jnp.inf)

    first = jnp.where(n == 0, q_heads, 0)
    group = q_heads // kv_heads
    for j in range(kv_heads):
        cols = slice(j * dim, (j + 1) * dim)
        k_all = _rms(jnp.concatenate([kp_ref[:, cols], kc_ref[:, cols]], axis=0), kw_ref[...]).astype(BF16)
        v_all = jnp.concatenate([vp_ref[:, cols], vc_ref[:, cols]], axis=0).astype(BF16)
        for u in range(SWA_STEP_BLOCKS):
            rows = slice(u * blk, (u + 1) * blk)
            kwin, vwin = k_all[u * blk:(u + 2) * blk], v_all[u * blk:(u + 2) * blk]
            qcat = jnp.concatenate(
                [_rms(q_ref[rows, (j * group + i) * dim:(j * group + i + 1) * dim], qw_ref[...])
                 for i in range(group)], axis=0).astype(BF16)
            logits = _dot_nt(qcat, kwin) * (dim ** -0.5)
            for i in range(group):
                h = j * group + i
                lg = logits[i * blk:(i + 1) * blk] + bias_ref[(first if u == 0 else 0) + h]
                sink = sink_ref[h]
                mx = jnp.maximum(jnp.max(lg, axis=-1, keepdims=True), sink)
                e = jnp.exp(lg - mx)
                den = jnp.sum(e, axis=-1, keepdims=True) + jnp.exp(sink - mx)
                pv = _dot(e.astype(BF16), vwin)
                o_ref[rows, h * dim:(h + 1) * dim] = (pv / den).astype(o_ref.dtype)


def _swa(proj3, rel_bias_flat, sinks, q_norm_w, k_norm_w, q_heads, kv_heads, dim, col_q, col_k, col_v):
    b, s, _ = proj3.shape
    blk = SWA_BLOCK
    step = SWA_STEP_BLOCKS * blk
    qw, kvw = q_heads * dim, kv_heads * dim
    smem = pl.BlockSpec(memory_space=pltpu.SMEM)
    pipelined = (_nbytes((step, qw), F32) + 2 * _nbytes((step + blk, kvw), F32) + _nbytes((step, qw), BF16))
    resident = (_nbytes((2 * q_heads, blk, 2 * blk), F32) + 2 * pipelined
                + 2 * SWA_STEP_BLOCKS * _nbytes((q_heads // kv_heads * blk, 2 * blk), F32))

    def prev(n):
        return jnp.maximum(SWA_STEP_BLOCKS * n - 1, 0)

    return pl.pallas_call(
        functools.partial(_swa_kernel, q_heads=q_heads, kv_heads=kv_heads, dim=dim),
        grid=(b, s // step),
        in_specs=[
            smem, smem,
            pl.BlockSpec((None, step, qw), lambda i, n: (i, n, col_q // qw)),
            pl.BlockSpec((None, step, kvw), lambda i, n: (i, n, col_k // kvw)),
            pl.BlockSpec((None, blk, kvw), lambda i, n: (i, prev(n), col_k // kvw)),
            pl.BlockSpec((None, step, kvw), lambda i, n: (i, n, col_v // kvw)),
            pl.BlockSpec((None, blk, kvw), lambda i, n: (i, prev(n), col_v // kvw)),
            pl.BlockSpec((1, dim), lambda i, n: (0, 0)),
            pl.BlockSpec((1, dim), lambda i, n: (0, 0)),
        ],
        out_specs=pl.BlockSpec((None, step, qw), lambda i, n: (i, n, 0)),
        out_shape=jax.ShapeDtypeStruct((b, s, qw), BF16),
        scratch_shapes=[pltpu.VMEM((2 * q_heads, blk, 2 * blk), F32)],
        compiler_params=pltpu.CompilerParams(
            dimension_semantics=("arbitrary", "arbitrary"),
            vmem_limit_bytes=_vmem_limit(pipelined, resident, claim_all=True)),
        name="swa",
    )(rel_bias_flat, sinks, proj3, proj3, proj3, proj3, proj3, q_norm_w, k_norm_w)


def _memattn_kernel(*refs, heads, dim, n_riders):
    q_ref, k_ref, v_ref, qw_ref, kw_ref = refs[:5]
    o_ref = refs[5 + n_riders]
    _cast_riders(refs[5:5 + n_riders], refs[6 + n_riders:])
    for h in range(heads):
        cols = slice(h * dim, (h + 1) * dim)
        qn = _rms(q_ref[:, cols], qw_ref[...]).astype(BF16)
        kn = _rms(k_ref[:, cols], kw_ref[...]).astype(BF16)
        lg = _dot_nt(qn, kn) * (dim ** -0.5)
        e = jnp.exp(lg - jnp.max(lg, axis=-1, keepdims=True))
        den = jnp.sum(e, axis=-1, keepdims=True)
        pv = _dot(e.astype(BF16), v_ref[:, cols].astype(BF16))
        o_ref[:, cols] = (pv / den).astype(o_ref.dtype)


def _memattn(proj3, mkv3, q_norm_w, k_norm_w, riders, heads, dim, col_q, tq):
    b, s, _ = proj3.shape
    m = mkv3.shape[1]
    w = heads * dim
    nt = s // tq
    rider_in, rider_out, rider_shapes = _cast_rider_specs(riders, b * nt, lambda i, t: i * nt + t)
    pipelined = (_nbytes((tq, w), F32) + 2 * _nbytes((m, w), F32) + _nbytes((tq, w), BF16)
                 + sum(_nbytes(r.shape, F32) * 3 // 2 for r in riders) // (b * nt))
    return pl.pallas_call(
        functools.partial(_memattn_kernel, heads=heads, dim=dim, n_riders=len(riders)),
        grid=(b, nt),
        in_specs=[
            pl.BlockSpec((pl.Element(tq), pl.Element(w)), lambda i, t: (pl.multiple_of(i * s + t * tq, tq), col_q)),
            pl.BlockSpec((None, m, w), lambda i, t: (i, 0, 0)),
            pl.BlockSpec((None, m, w), lambda i, t: (i, 0, 1)),
            pl.BlockSpec((1, dim), lambda i, t: (0, 0)),
            pl.BlockSpec((1, dim), lambda i, t: (0, 0)),
            *rider_in,
        ],
        out_specs=[pl.BlockSpec((None, tq, w), lambda i, t: (i, t, 0)), *rider_out],
        out_shape=[jax.ShapeDtypeStruct((b, s, w), BF16), *rider_shapes],
        compiler_params=pltpu.CompilerParams(
            dimension_semantics=("parallel", "parallel"),
            vmem_limit_bytes=_vmem_limit(pipelined, 2 * heads * _nbytes((tq, m), F32), claim_all=True)),
        name="memattn",
    )(proj3.reshape(b * s, -1), mkv3, mkv3, q_norm_w, k_norm_w, *riders)


def _merge_kernel(od_ref, os_ref, om_ref, pd_ref, ps_ref, pm_ref, gd_ref, gs_ref, gm_ref, o_ref):
    merged = (gd_ref[...].astype(F32) * _dot(od_ref[...], pd_ref[...])
              + gs_ref[...].astype(F32) * _dot(os_ref[...], ps_ref[...])
              + gm_ref[...].astype(F32) * _dot(om_ref[...], pm_ref[...]))
    o_ref[...] = merged.astype(o_ref.dtype)


def _merge(o_dn, o_swa, o_mem, p_dn, p_swa, p_mem, gates, tm, tn):
    t = o_dn.shape[0]
    d = p_dn.shape[1]
    wd, ws, wm = o_dn.shape[1], o_swa.shape[1], o_mem.shape[1]
    per_branch = d // tn
    pipelined = ((_nbytes((tm, wd), BF16) + _nbytes((tm, ws), BF16) + _nbytes((tm, wm), BF16))
                 + (_nbytes((wd, tn), BF16) + _nbytes((ws, tn), BF16) + _nbytes((wm, tn), BF16))
                 + 4 * _nbytes((tm, tn), BF16))
    return pl.pallas_call(
        _merge_kernel,
        grid=(d // tn, t // tm),
        in_specs=[
            pl.BlockSpec((tm, wd), lambda j, i: (i, 0)),
            pl.BlockSpec((tm, ws), lambda j, i: (i, 0)),
            pl.BlockSpec((tm, wm), lambda j, i: (i, 0)),
            pl.BlockSpec((wd, tn), lambda j, i: (0, j)),
            pl.BlockSpec((ws, tn), lambda j, i: (0, j)),
            pl.BlockSpec((wm, tn), lambda j, i: (0, j)),
            pl.BlockSpec((tm, tn), lambda j, i: (i, j)),
            pl.BlockSpec((tm, tn), lambda j, i: (i, per_branch + j)),
            pl.BlockSpec((tm, tn), lambda j, i: (i, 2 * per_branch + j)),
        ],
        out_specs=pl.BlockSpec((tm, tn), lambda j, i: (i, j)),
        out_shape=jax.ShapeDtypeStruct((t, d), BF16),
        compiler_params=pltpu.CompilerParams(
            dimension_semantics=("parallel", "parallel"),
            vmem_limit_bytes=_vmem_limit(pipelined, 3 * _nbytes((tm, tn), F32))),
        name="merge",
    )(o_dn, o_swa, o_mem, p_dn, p_swa, p_mem, gates, gates, gates)


def _outproj_kernel(x_ref, m_ref, w_ref, nw_ref, x1_ref, h_ref):
    x1 = x_ref[...] + _dot(m_ref[...], w_ref[...])
    x1_ref[...] = x1
    h_ref[...] = _rms(x1, nw_ref[...]).astype(h_ref.dtype)


def _outproj(x, merged, w_out, norm_w, tm):
    t, d = x.shape
    pipelined = 2 * _nbytes((tm, d), F32) + 2 * _nbytes((tm, d), BF16) + _nbytes((d, d), BF16)
    return pl.pallas_call(
        _outproj_kernel,
        grid=(t // tm,),
        in_specs=[
            pl.BlockSpec((tm, d), lambda i: (i, 0)),
            pl.BlockSpec((tm, d), lambda i: (i, 0)),
            pl.BlockSpec((d, d), lambda i: (0, 0)),
            pl.BlockSpec((1, d), lambda i: (0, 0)),
        ],
        out_specs=[pl.BlockSpec((tm, d), lambda i: (i, 0)), pl.BlockSpec((tm, d), lambda i: (i, 0))],
        out_shape=[jax.ShapeDtypeStruct((t, d), F32), jax.ShapeDtypeStruct((t, d), BF16)],
        compiler_params=pltpu.CompilerParams(
            dimension_semantics=("parallel",),
            vmem_limit_bytes=_vmem_limit(pipelined, 2 * _nbytes((tm, d), F32))),
        name="outproj",
    )(x, merged, w_out, norm_w)


def _mlp_kernel(h_ref, x1_ref, wu_ref, wd_ref, o_ref):
    @pl.when(pl.program_id(1) == 0)
    def _():
        o_ref[...] = x1_ref[...]

    a = jnp.maximum(_dot(h_ref[...], wu_ref[...]), 0.0)
    o_ref[...] += _dot((a * a).astype(BF16), wd_ref[...])


def _mlp(h2, x1, w_up, w_down, tm, tf):
    t, d = x1.shape
    f = w_up.shape[1]
    pipelined = (_nbytes((tm, d), BF16) + 2 * _nbytes((tm, d), F32)
                 + _nbytes((d, tf), BF16) + _nbytes((tf, d), BF16))
    resident = 2 * _nbytes((tm, tf), F32)
    return pl.pallas_call(
        _mlp_kernel,
        grid=(t // tm, f // tf),
        in_specs=[
            pl.BlockSpec((tm, d), lambda i, k: (i, 0)),
            pl.BlockSpec((tm, d), lambda i, k: (i, 0)),
            pl.BlockSpec((d, tf), lambda i, k: (0, k)),
            pl.BlockSpec((tf, d), lambda i, k: (k, 0)),
        ],
        out_specs=pl.BlockSpec((tm, d), lambda i, k: (i, 0)),
        out_shape=jax.ShapeDtypeStruct((t, d), F32),
        compiler_params=pltpu.CompilerParams(
            dimension_semantics=("parallel", "arbitrary"),
            vmem_limit_bytes=_vmem_limit(pipelined, resident)),
        name="mlp",
    )(h2, x1, w_up, w_down)


def _layer(x, mem, attn_norm_w, w_in, dn_conv_w, dn_a_log, dn_dt_bias, dn_out_norm_w, swa_q_norm_w,
           swa_k_norm_w, swa_sinks, rel_bias, mem_norm_w, w_mem_kv, xq_norm_w, xk_norm_w,
           p_dn, p_swa, p_mem, w_out, mlp_norm_w, w_mlp_up, w_mlp_down):
    b, s, d = x.shape
    m = mem.shape[1]
    t = b * s

    dn_heads = dn_a_log.shape[0]
    dn_dim = dn_out_norm_w.shape[0]
    dn_w = dn_heads * dn_dim
    swa_heads = swa_sinks.shape[0]
    swa_dim = swa_q_norm_w.shape[0]
    swa_w = swa_heads * swa_dim
    mem_dim = xq_norm_w.shape[0]
    mem_w = p_mem.shape[0]
    mem_heads = mem_w // mem_dim
    swa_kv_w = (w_in.shape[1] - 4 * dn_w - 2 * dn_heads - swa_w - mem_w - 3 * d) // 2
    swa_kv = swa_kv_w // swa_dim

    n_ba = 2 * dn_heads
    src_ba = 4 * dn_w
    src_swa = src_ba + n_ba
    w_t = w_in.T.astype(BF16)
    w_b = w_t[src_swa:]
    w_c = jnp.concatenate([w_t[src_ba:src_swa], jnp.zeros((LANES - n_ba, d), BF16)], axis=0)
    col = {"qkv": 0, "z": 3 * dn_w, "sq": src_ba}
    col["sk"] = col["sq"] + swa_w
    col["sv"] = col["sk"] + swa_kv_w
    col["mq"] = col["sv"] + swa_kv_w

    proj, gates, ba = _inproj(x.reshape(t, d), attn_norm_w.reshape(1, d), w_t, src_ba, w_b, w_c, 3 * d,
                              tm=1024, tn=1024)
    proj3 = proj.reshape(b, s, -1)
    mkv = _norm_matmul(mem.reshape(b * m, d), mem_norm_w.reshape(1, d), w_mem_kv.astype(BF16), tm=512, tn=512)
    mkv3 = mkv.reshape(b, m, -1)

    gate_params = jnp.zeros((SUBLANES, LANES), F32)
    gate_params = gate_params.at[0, dn_heads:2 * dn_heads].set(dn_a_log)
    gate_params = gate_params.at[1, dn_heads:2 * dn_heads].set(dn_dt_bias)
    o_dn, w_up_bf, w_down_bf = _deltanet(proj3, ba.reshape(b, s, LANES), dn_conv_w, gate_params,
                                         dn_out_norm_w.reshape(1, dn_dim), [w_mlp_up, w_mlp_down],
                                         dn_heads, dn_dim, col["z"])
    o_swa = _swa(proj3, rel_bias.reshape(-1), swa_sinks, swa_q_norm_w.reshape(1, swa_dim),
                 swa_k_norm_w.reshape(1, swa_dim), swa_heads, swa_kv, swa_dim, col["sq"], col["sk"], col["sv"])
    o_mem, w_out_bf, p_dn_bf, p_swa_bf, p_mem_bf = _memattn(
        proj3, mkv3, xq_norm_w.reshape(1, mem_dim), xk_norm_w.reshape(1, mem_dim),
        [w_out, p_dn, p_swa, p_mem], mem_heads, mem_dim, col["mq"], tq=1024)

    merged = _merge(o_dn.reshape(t, dn_w), o_swa.reshape(t, swa_w), o_mem.reshape(t, mem_w),
                    p_dn_bf, p_swa_bf, p_mem_bf, gates, tm=1024, tn=1024)
    x1, h2 = _outproj(x.reshape(t, d), merged, w_out_bf, mlp_norm_w.reshape(1, d), tm=512)
    out = _mlp(h2, x1, w_up_bf, w_down_bf, tm=512, tf=1024)
    return out.reshape(b, s, d)


def kernel(x, mem, attn_norm_w, w_in, dn_conv_w, dn_A_log, dn_dt_bias, dn_out_norm_w, swa_q_norm_w,
           swa_k_norm_w, swa_sinks, rel_bias, mem_norm_w, w_mem_kv, xq_norm_w, xk_norm_w, p_dn, p_swa,
           p_mem, w_out, mlp_norm_w, w_mlp_up, w_mlp_down):
    depth = w_in.shape[0]
    for l in range(depth):
        x = _layer(x, mem, attn_norm_w[l], w_in[l], dn_conv_w[l], dn_A_log[l], dn_dt_bias[l],
                   dn_out_norm_w[l], swa_q_norm_w[l], swa_k_norm_w[l], swa_sinks[l], rel_bias,
                   mem_norm_w[l], w_mem_kv[l], xq_norm_w[l], xk_norm_w[l], p_dn[l], p_swa[l], p_mem[l],
                   w_out[l], mlp_norm_w[l], w_mlp_up[l], w_mlp_down[l])
    return x
```

```python
import functools
import math

import jax
import jax.numpy as jnp
from jax import lax
from jax.experimental import pallas as pl
from jax.experimental.pallas import tpu as pltpu

F32 = jnp.float32
BF16 = jnp.bfloat16

EPS = 1e-6
LANES = 128
SUBLANES = 8
V7X_VMEM_BYTES = 64 * 1024 * 1024

DN_CHUNK = 64
DN_GROUP = 256
DN_BASE = 8
SWA_WINDOW = 128
SWA_BLOCK = 128
SWA_STEP_BLOCKS = 4
N_BUCKETS = 32
MAX_DISTANCE = 128

_NT = (((1,), (1,)), ((), ()))
_TN = (((0,), (0,)), ((), ()))


COMPILER_SCRATCH_BYTES = 4 * 1024 * 1024
VMEM_RESERVE_BYTES = 8 * 1024 * 1024


def _vmem_limit(pipelined_bytes, resident_bytes, claim_all=False):
    cap = V7X_VMEM_BYTES - VMEM_RESERVE_BYTES
    want = 2 * pipelined_bytes + resident_bytes + COMPILER_SCRATCH_BYTES
    return int(cap if claim_all else min(want, cap))


def _nbytes(shape, dtype):
    return math.prod(shape) * jnp.dtype(dtype).itemsize


def _sigmoid(v):
    return 0.5 * jnp.tanh(0.5 * v) + 0.5


def _silu(v):
    return v * _sigmoid(v)


def _rms(v, w):
    return (v * lax.rsqrt(jnp.mean(v * v, axis=-1, keepdims=True) + EPS)) * w


def _dot(a, b):
    return jnp.dot(a, b, preferred_element_type=F32)


def _dot_nt(a, b):
    return lax.dot_general(a, b, _NT, preferred_element_type=F32)


def _cast_rider_specs(weights, n_steps, step_of):
    in_specs, out_specs, out_shapes = [], [], []
    for w in weights:
        rows = w.shape[0] // n_steps
        assert rows * n_steps == w.shape[0] and rows % (2 * SUBLANES) == 0
        for specs in (in_specs, out_specs):
            specs.append(pl.BlockSpec((rows, w.shape[1]), lambda *ids: (step_of(*ids), 0)))
        out_shapes.append(jax.ShapeDtypeStruct(w.shape, BF16))
    return in_specs, out_specs, out_shapes


def _cast_riders(in_refs, out_refs):
    for i_ref, o_ref in zip(in_refs, out_refs):
        o_ref[...] = i_ref[...].astype(o_ref.dtype)


IN_RING = 4


def _inproj_kernel(x_ref, nw_ref, wa_hbm, wb_hbm, wc_ref, o_ref, g_ref, ba_ref, h_ref, wbuf, sem,
                   *, n_a, n_p, tail, rows_b1):
    i, j = pl.program_id(0), pl.program_id(1)
    n_j = pl.num_programs(1)
    tn = wbuf.shape[1]
    step = i * n_j + j
    n_steps = pl.num_programs(0) * n_j

    def tile_copy(col_step, slot, from_b):
        if from_b:
            row = jnp.where(col_step < n_p, (col_step - n_a) * tn, rows_b1 + (col_step - n_p) * tn)
            src = wb_hbm.at[pl.ds(pl.multiple_of(row, 2 * SUBLANES), tn), :]
        else:
            src = wa_hbm.at[pl.ds(pl.multiple_of(col_step * tn, tn), tn), :]
        return pltpu.make_async_copy(src, wbuf.at[slot], sem.at[slot])

    def request(ahead):
        target = step + ahead
        col_step = lax.rem(target, n_j)
        slot = lax.rem(target, IN_RING)

        @pl.when((target < n_steps) & (col_step < n_a))
        def _():
            tile_copy(col_step, slot, from_b=False).start()

        @pl.when((target < n_steps) & (col_step >= n_a))
        def _():
            tile_copy(col_step, slot, from_b=True).start()

    @pl.when(step == 0)
    def _():
        for ahead in range(IN_RING - 1):
            request(ahead)

    request(IN_RING - 1)
    slot = lax.rem(step, IN_RING)
    pltpu.make_async_copy(wb_hbm.at[pl.ds(0, tn), :], wbuf.at[slot], sem.at[slot]).wait()
    w_ref = wbuf.at[slot]

    @pl.when(j == 0)
    def _():
        h_ref[...] = _rms(x_ref[...], nw_ref[...]).astype(BF16)
        ba_ref[...] = _dot_nt(h_ref[...], wc_ref[...])

    @pl.when(j < n_p - 1)
    def _():
        o_ref[...] = _dot_nt(h_ref[...], w_ref[...])

    @pl.when(j == n_p - 1)
    def _():
        o_ref[:, :tail] = _dot_nt(h_ref[...], w_ref[:tail, :])

    @pl.when(j >= n_p)
    def _():
        g_ref[...] = _sigmoid(_dot_nt(h_ref[...], w_ref[...])).astype(g_ref.dtype)


def _inproj(x, nw, w_a, rows_a, w_b, w_c, n_gates, tm, tn):
    m, k = x.shape
    rows_b1 = w_b.shape[0] - n_gates
    n = rows_a + rows_b1
    n_a = rows_a // tn
    n_p = pl.cdiv(n, tn)
    assert rows_a % tn == 0 and n_gates % tn == 0 and rows_b1 % (2 * SUBLANES) == 0

    pipelined = (_nbytes((tm, k), F32) + _nbytes((tm, tn), F32) + _nbytes((tm, tn), BF16)
                 + _nbytes((tm, LANES), F32))
    resident = (_nbytes((tm, k), BF16) + _nbytes((tm, k), F32) + _nbytes((LANES, k), BF16)
                + IN_RING * _nbytes((tn, k), BF16))
    return pl.pallas_call(
        functools.partial(_inproj_kernel, n_a=n_a, n_p=n_p, tail=n - (n_p - 1) * tn, rows_b1=rows_b1),
        grid=(m // tm, n_p + n_gates // tn),
        in_specs=[
            pl.BlockSpec((tm, k), lambda i, j: (i, 0)),
            pl.BlockSpec((1, k), lambda i, j: (0, 0)),
            pl.BlockSpec(memory_space=pl.ANY),
            pl.BlockSpec(memory_space=pl.ANY),
            pl.BlockSpec((LANES, k), lambda i, j: (0, 0)),
        ],
        out_specs=[pl.BlockSpec((tm, tn), lambda i, j: (i, jnp.minimum(j, n_p - 1))),
                   pl.BlockSpec((tm, tn), lambda i, j: (i, jnp.maximum(j - n_p, 0))),
                   pl.BlockSpec((tm, LANES), lambda i, j: (i, 0))],
        out_shape=[jax.ShapeDtypeStruct((m, n), F32), jax.ShapeDtypeStruct((m, n_gates), BF16),
                   jax.ShapeDtypeStruct((m, LANES), F32)],
        scratch_shapes=[pltpu.VMEM((tm, k), BF16), pltpu.VMEM((IN_RING, tn, k), BF16),
                        pltpu.SemaphoreType.DMA((IN_RING,))],
        compiler_params=pltpu.CompilerParams(
            dimension_semantics=("arbitrary", "arbitrary"),
            vmem_limit_bytes=_vmem_limit(pipelined, resident)),
        name="inproj",
    )(x, nw, w_a, w_b, w_c)


def _norm_matmul_kernel(x_ref, nw_ref, w_ref, o_ref, h_ref):
    @pl.when(pl.program_id(1) == 0)
    def _():
        h_ref[...] = _rms(x_ref[...], nw_ref[...]).astype(BF16)

    o_ref[...] = _dot(h_ref[...], w_ref[...]).astype(o_ref.dtype)


def _norm_matmul(x, nw, w, tm, tn):
    m, k = x.shape
    n = w.shape[1]
    pipelined = _nbytes((tm, k), F32) + _nbytes((k, tn), BF16) + _nbytes((tm, tn), F32)
    resident = _nbytes((tm, k), BF16) + _nbytes((tm, k), F32)
    return pl.pallas_call(
        _norm_matmul_kernel,
        grid=(m // tm, n // tn),
        in_specs=[
            pl.BlockSpec((tm, k), lambda i, j: (i, 0)),
            pl.BlockSpec((1, k), lambda i, j: (0, 0)),
            pl.BlockSpec((k, tn), lambda i, j: (0, j)),
        ],
        out_specs=pl.BlockSpec((tm, tn), lambda i, j: (i, j)),
        out_shape=jax.ShapeDtypeStruct((m, n), F32),
        scratch_shapes=[pltpu.VMEM((tm, k), BF16)],
        compiler_params=pltpu.CompilerParams(
            dimension_semantics=("parallel", "arbitrary"),
            vmem_limit_bytes=_vmem_limit(pipelined, resident)),
        name="norm_matmul",
    )(x, nw, w)


def _dn_kernel(*refs, heads, dim, n_riders):
    qkv_ref, z_ref, ba_ref, cw_ref, gp_ref, onw_ref = refs[:6]
    o_ref = refs[6 + n_riders]
    ext_ref, state_ref = refs[-2:]
    _cast_riders(refs[6:6 + n_riders], refs[7 + n_riders:7 + 2 * n_riders])

    gt = DN_GROUP
    c = DN_CHUNK
    hd = heads * dim
    hs = range(heads)
    chunks = range(gt // c)

    @pl.when(pl.program_id(1) == 0)
    def _():
        ext_ref[0:SUBLANES, :] = jnp.zeros((SUBLANES, 3 * hd), F32)
        state_ref[...] = jnp.zeros_like(state_ref)

    ext_ref[SUBLANES:SUBLANES + gt, :] = qkv_ref[...]
    xe = ext_ref[...]
    conv = cw_ref[3:4, :] * xe[SUBLANES:]
    for s in (1, 2, 3):
        conv = conv + cw_ref[3 - s:4 - s, :] * pltpu.roll(xe, s, 0)[SUBLANES:]
    ext_ref[0:SUBLANES, :] = xe[gt:gt + SUBLANES]
    act = _silu(conv)

    ba = ba_ref[...]
    beta_all = _sigmoid(ba)
    xa = ba + gp_ref[1:2, :]
    softplus = jnp.maximum(xa, 0.0) + jnp.log1p(jnp.exp(-jnp.abs(xa)))
    g_all = -jnp.exp(gp_ref[0:1, :]) * softplus
    row_in_chunk = lax.broadcasted_iota(jnp.int32, (gt, LANES), 0) & (c - 1)
    gcum = g_all
    s = 1
    while s < c:
        gcum = gcum + jnp.where(row_in_chunk >= s, pltpu.roll(gcum, s, 0), 0.0)
        s *= 2
    exp_g = jnp.exp(gcum)
    kdec_parts, gc_rows = [], []
    for j in chunks:
        g_last = gcum[c * j + c - 1:c * j + c, :]
        kdec_parts.append(jnp.exp(g_last - gcum[c * j:c * (j + 1), :]))
        gc_rows.append(jnp.exp(g_last))
    kdec = jnp.concatenate(kdec_parts, axis=0)
    gcum_t = gcum.T

    ri = lax.broadcasted_iota(jnp.int32, (gt, gt), 0)
    ci = lax.broadcasted_iota(jnp.int32, (gt, gt), 1)
    same_chunk = (ri // c) == (ci // c)
    strict = same_chunk & (ri > ci)
    incl = same_chunk & (ri >= ci)

    qn, kn, kb, beta, eg = [], [], [], [], []
    for h in hs:
        qh = act[:, h * dim:(h + 1) * dim]
        kh = act[:, hd + h * dim:hd + (h + 1) * dim]
        qn.append(qh * lax.rsqrt(jnp.sum(qh * qh, axis=-1, keepdims=True) + EPS) * (dim ** -0.5))
        kn.append(kh * lax.rsqrt(jnp.sum(kh * kh, axis=-1, keepdims=True) + EPS))
        kb.append(kn[h].astype(BF16))
        beta.append(beta_all[:, h:h + 1])
        eg.append(exp_g[:, heads + h:heads + h + 1])

    decay = [jnp.exp(jnp.where(incl, gcum[:, heads + h:heads + h + 1] - gcum_t[heads + h:heads + h + 1, :],
                               -jnp.inf)) for h in hs]
    kk = [_dot_nt(kb[h], kb[h]) for h in hs]
    nmat = [jnp.where(strict, (beta[h] * kk[h]) * decay[h], 0.0) for h in hs]

    def same_block(size):
        return (ri // size) == (ci // size)

    eye = jnp.where(ri == ci, 1.0, 0.0)
    base = same_block(DN_BASE)
    pw = [jnp.where(base, nmat[h], 0.0) for h in hs]
    tinv = [eye - pw[h] for h in hs]
    pw = [pw[h].astype(BF16) for h in hs]
    order = 2
    while order < DN_BASE:
        pw = [_dot(pw[h], pw[h]).astype(BF16) for h in hs]
        tinv = [tinv[h] + _dot(tinv[h].astype(BF16), pw[h]) for h in hs]
        order *= 2
    size = DN_BASE
    while size < c:
        level = same_block(2 * size) & jnp.logical_not(same_block(size))
        off = [jnp.where(level, nmat[h], 0.0).astype(BF16) for h in hs]
        ct = [_dot(off[h], tinv[h].astype(BF16)).astype(BF16) for h in hs]
        tinv = [tinv[h] - _dot(tinv[h].astype(BF16), ct[h]) for h in hs]
        size *= 2

    sol = [jnp.concatenate([act[:, 2 * hd + h * dim:2 * hd + (h + 1) * dim] * beta[h],
                            kn[h] * (beta[h] * eg[h])], axis=1) for h in hs]
    sol = [sol[h] + _dot((tinv[h] - eye).astype(BF16), sol[h].astype(BF16)) for h in hs]

    pmat = [(_dot_nt(qn[h].astype(BF16), kb[h]) * decay[h]).astype(BF16) for h in hs]
    wq = [[jnp.concatenate([sol[h][c * j:c * (j + 1), dim:], (qn[h] * eg[h])[c * j:c * (j + 1)]],
                           axis=0).astype(BF16) for j in chunks] for h in hs]
    kd = [(kn[h] * kdec[:, heads + h:heads + h + 1]).astype(BF16) for h in hs]

    state = [state_ref[h] for h in hs]
    o_parts = [[] for _ in hs]
    for j in chunks:
        r = slice(c * j, c * (j + 1))
        ws = [_dot(wq[h][j], state[h].astype(BF16)) for h in hs]
        db = [(sol[h][r, :dim] - ws[h][:c]).astype(BF16) for h in hs]
        for h in hs:
            o_parts[h].append(ws[h][c:] + _dot(pmat[h][r, r], db[h]))
        state = [gc_rows[j][:, heads + h:heads + h + 1] * state[h]
                 + lax.dot_general(kd[h][r], db[h], _TN, preferred_element_type=F32) for h in hs]
    for h in hs:
        state_ref[h] = state[h]

    for h in hs:
        o = jnp.concatenate(o_parts[h], axis=0)
        zh = z_ref[:, h * dim:(h + 1) * dim]
        o_ref[:, h * dim:(h + 1) * dim] = (_rms(o, onw_ref[...]) * _silu(zh)).astype(o_ref.dtype)


def _deltanet(proj3, ba3, conv_w, gate_params, out_norm_w, riders, heads, dim, col_z):
    b, s, _ = proj3.shape
    hd = heads * dim
    gt = DN_GROUP
    ng = s // gt
    rider_in, rider_out, rider_shapes = _cast_rider_specs(riders, b * ng, lambda i, g: i * ng + g)
    pipelined = (_nbytes((gt, 3 * hd), F32) + _nbytes((gt, hd), F32) + _nbytes((gt, LANES), F32)
                 + _nbytes((gt, hd), BF16) + sum(_nbytes(w.shape, F32) * 3 // 2 for w in riders) // (b * ng))
    resident = (_nbytes((gt + SUBLANES, 3 * hd), F32) * 4 + _nbytes((heads, dim, dim), F32)
                + heads * 6 * _nbytes((gt, gt), F32))
    return pl.pallas_call(
        functools.partial(_dn_kernel, heads=heads, dim=dim, n_riders=len(riders)),
        grid=(b, ng),
        in_specs=[
            pl.BlockSpec((None, gt, 3 * hd), lambda i, g: (i, g, 0)),
            pl.BlockSpec((None, gt, hd), lambda i, g: (i, g, col_z // hd)),
            pl.BlockSpec((None, gt, LANES), lambda i, g: (i, g, 0)),
            pl.BlockSpec((4, 3 * hd), lambda i, g: (0, 0)),
            pl.BlockSpec((SUBLANES, LANES), lambda i, g: (0, 0)),
            pl.BlockSpec((1, dim), lambda i, g: (0, 0)),
            *rider_in,
        ],
        out_specs=[pl.BlockSpec((None, gt, hd), lambda i, g: (i, g, 0)), *rider_out],
        out_shape=[jax.ShapeDtypeStruct((b, s, hd), BF16), *rider_shapes],
        scratch_shapes=[pltpu.VMEM((gt + SUBLANES, 3 * hd), F32), pltpu.VMEM((heads, dim, dim), F32)],
        compiler_params=pltpu.CompilerParams(
            dimension_semantics=("parallel", "arbitrary"),
            vmem_limit_bytes=_vmem_limit(pipelined, resident, claim_all=True)),
        name="deltanet",
    )(proj3, proj3, ba3, conv_w, gate_params, out_norm_w, *riders)


def _swa_kernel(rb_ref, sink_ref, q_ref, kc_ref, kp_ref, vc_ref, vp_ref, qw_ref, kw_ref, o_ref, bias_ref,
                *, q_heads, kv_heads, dim):
    blk = SWA_BLOCK
    n = pl.program_id(1)
    qi = lax.broadcasted_iota(jnp.int32, (blk, 2 * blk), 0)
    kj = lax.broadcasted_iota(jnp.int32, (blk, 2 * blk), 1)
    dist = qi - kj + blk

    @pl.when((pl.program_id(0) == 0) & (n == 0))
    def _():
        max_exact = N_BUCKETS // 2
        nn = jnp.maximum(dist, 0)
        nf = jnp.maximum(nn, 1).astype(F32)
        large = max_exact + (jnp.log(nf / max_exact) / math.log(MAX_DISTANCE / max_exact)
                             * (N_BUCKETS - max_exact)).astype(jnp.int32)
        bucket = jnp.where(nn < max_exact, nn, jnp.minimum(large, N_BUCKETS - 1))
        for h in range(q_heads):
            acc = jnp.zeros((blk, 2 * blk), F32)
            for bk in range(N_BUCKETS):
                acc = jnp.where(bucket == bk, rb_ref[bk * q_heads + h], acc)
            acc = jnp.where((dist >= 0) & (dist < SWA_WINDOW), acc, -jnp.inf)
            bias_ref[h] = acc
            bias_ref[q_heads + h] = jnp.where(kj >= blk, acc, -jnp.inf)

    first = jnp.where(n == 0, q_heads, 0)
    group = q_heads // kv_heads
    for j in range(kv_heads):
        cols = slice(j * dim, (j + 1) * dim)
        k_all = _rms(jnp.concatenate([kp_ref[:, cols], kc_ref[:, cols]], axis=0), kw_ref[...]).astype(BF16)
        v_all = jnp.concatenate([vp_ref[:, cols], vc_ref[:, cols]], axis=0).astype(BF16)
        for u in range(SWA_STEP_BLOCKS):
            rows = slice(u * blk, (u + 1) * blk)
            kwin, vwin = k_all[u * blk:(u + 2) * blk], v_all[u * blk:(u + 2) * blk]
            qcat = jnp.concatenate(
                [_rms(q_ref[rows, (j * group + i) * dim:(j * group + i + 1) * dim], qw_ref[...])
                 for i in range(group)], axis=0).astype(BF16)
            logits = _dot_nt(qcat, kwin) * (dim ** -0.5)
            for i in range(group):
                h = j * group + i
                lg = logits[i * blk:(i + 1) * blk] + bias_ref[(first if u == 0 else 0) + h]
                sink = sink_ref[h]
                mx = jnp.maximum(jnp.max(lg, axis=-1, keepdims=True), sink)
                e = jnp.exp(lg - mx)
                den = jnp.sum(e, axis=-1, keepdims=True) + jnp.exp(sink - mx)
                pv = _dot(e.astype(BF16), vwin)
                o_ref[rows, h * dim:(h + 1) * dim] = (pv / den).astype(o_ref.dtype)


def _swa(proj3, rel_bias_flat, sinks, q_norm_w, k_norm_w, q_heads, kv_heads, dim, col_q, col_k, col_v):
    b, s, _ = proj3.shape
    blk = SWA_BLOCK
    step = SWA_STEP_BLOCKS * blk
    qw, kvw = q_heads * dim, kv_heads * dim
    smem = pl.BlockSpec(memory_space=pltpu.SMEM)
    pipelined = (_nbytes((step, qw), F32) + 2 * _nbytes((step + blk, kvw), F32) + _nbytes((step, qw), BF16))
    resident = (_nbytes((2 * q_heads, blk, 2 * blk), F32) + 2 * pipelined
                + 2 * SWA_STEP_BLOCKS * _nbytes((q_heads // kv_heads * blk, 2 * blk), F32))

    def prev(n):
        return jnp.maximum(SWA_STEP_BLOCKS * n - 1, 0)

    return pl.pallas_call(
        functools.partial(_swa_kernel, q_heads=q_heads, kv_heads=kv_heads, dim=dim),
        grid=(b, s // step),
        in_specs=[
            smem, smem,
            pl.BlockSpec((None, step, qw), lambda i, n: (i, n, col_q // qw)),
            pl.BlockSpec((None, step, kvw), lambda i, n: (i, n, col_k // kvw)),
            pl.BlockSpec((None, blk, kvw), lambda i, n: (i, prev(n), col_k // kvw)),
            pl.BlockSpec((None, step, kvw), lambda i, n: (i, n, col_v // kvw)),
            pl.BlockSpec((None, blk, kvw), lambda i, n: (i, prev(n), col_v // kvw)),
            pl.BlockSpec((1, dim), lambda i, n: (0, 0)),
            pl.BlockSpec((1, dim), lambda i, n: (0, 0)),
        ],
        out_specs=pl.BlockSpec((None, step, qw), lambda i, n: (i, n, 0)),
        out_shape=jax.ShapeDtypeStruct((b, s, qw), BF16),
        scratch_shapes=[pltpu.VMEM((2 * q_heads, blk, 2 * blk), F32)],
        compiler_params=pltpu.CompilerParams(
            dimension_semantics=("arbitrary", "arbitrary"),
            vmem_limit_bytes=_vmem_limit(pipelined, resident, claim_all=True)),
        name="swa",
    )(rel_bias_flat, sinks, proj3, proj3, proj3, proj3, proj3, q_norm_w, k_norm_w)


def _memattn_kernel(*refs, heads, dim, n_riders):
    q_ref, k_ref, v_ref, qw_ref, kw_ref = refs[:5]
    o_ref = refs[5 + n_riders]
    _cast_riders(refs[5:5 + n_riders], refs[6 + n_riders:])
    for h in range(heads):
        cols = slice(h * dim, (h + 1) * dim)
        qn = _rms(q_ref[:, cols], qw_ref[...]).astype(BF16)
        kn = _rms(k_ref[:, cols], kw_ref[...]).astype(BF16)
        lg = _dot_nt(qn, kn) * (dim ** -0.5)
        e = jnp.exp(lg - jnp.max(lg, axis=-1, keepdims=True))
        den = jnp.sum(e, axis=-1, keepdims=True)
        pv = _dot(e.astype(BF16), v_ref[:, cols].astype(BF16))
        o_ref[:, cols] = (pv / den).astype(o_ref.dtype)


def _memattn(proj3, mkv3, q_norm_w, k_norm_w, riders, heads, dim, col_q, tq):
    b, s, _ = proj3.shape
    m = mkv3.shape[1]
    w = heads * dim
    nt = s // tq
    rider_in, rider_out, rider_shapes = _cast_rider_specs(riders, b * nt, lambda i, t: i * nt + t)
    pipelined = (_nbytes((tq, w), F32) + 2 * _nbytes((m, w), F32) + _nbytes((tq, w), BF16)
                 + sum(_nbytes(r.shape, F32) * 3 // 2 for r in riders) // (b * nt))
    return pl.pallas_call(
        functools.partial(_memattn_kernel, heads=heads, dim=dim, n_riders=len(riders)),
        grid=(b, nt),
        in_specs=[
            pl.BlockSpec((pl.Element(tq), pl.Element(w)), lambda i, t: (pl.multiple_of(i * s + t * tq, tq), col_q)),
            pl.BlockSpec((None, m, w), lambda i, t: (i, 0, 0)),
            pl.BlockSpec((None, m, w), lambda i, t: (i, 0, 1)),
            pl.BlockSpec((1, dim), lambda i, t: (0, 0)),
            pl.BlockSpec((1, dim), lambda i, t: (0, 0)),
            *rider_in,
        ],
        out_specs=[pl.BlockSpec((None, tq, w), lambda i, t: (i, t, 0)), *rider_out],
        out_shape=[jax.ShapeDtypeStruct((b, s, w), BF16), *rider_shapes],
        compiler_params=pltpu.CompilerParams(
            dimension_semantics=("parallel", "parallel"),
            vmem_limit_bytes=_vmem_limit(pipelined, 2 * heads * _nbytes((tq, m), F32), claim_all=True)),
        name="memattn",
    )(proj3.reshape(b * s, -1), mkv3, mkv3, q_norm_w, k_norm_w, *riders)


def _merge_kernel(od_ref, os_ref, om_ref, pd_ref, ps_ref, pm_ref, gd_ref, gs_ref, gm_ref, o_ref):
    merged = (gd_ref[...].astype(F32) * _dot(od_ref[...], pd_ref[...])
              + gs_ref[...].astype(F32) * _dot(os_ref[...], ps_ref[...])
              + gm_ref[...].astype(F32) * _dot(om_ref[...], pm_ref[...]))
    o_ref[...] = merged.astype(o_ref.dtype)


def _merge(o_dn, o_swa, o_mem, p_dn, p_swa, p_mem, gates, tm, tn):
    t = o_dn.shape[0]
    d = p_dn.shape[1]
    wd, ws, wm = o_dn.shape[1], o_swa.shape[1], o_mem.shape[1]
    per_branch = d // tn
    pipelined = ((_nbytes((tm, wd), BF16) + _nbytes((tm, ws), BF16) + _nbytes((tm, wm), BF16))
                 + (_nbytes((wd, tn), BF16) + _nbytes((ws, tn), BF16) + _nbytes((wm, tn), BF16))
                 + 4 * _nbytes((tm, tn), BF16))
    return pl.pallas_call(
        _merge_kernel,
        grid=(d // tn, t // tm),
        in_specs=[
            pl.BlockSpec((tm, wd), lambda j, i: (i, 0)),
            pl.BlockSpec((tm, ws), lambda j, i: (i, 0)),
            pl.BlockSpec((tm, wm), lambda j, i: (i, 0)),
            pl.BlockSpec((wd, tn), lambda j, i: (0, j)),
            pl.BlockSpec((ws, tn), lambda j, i: (0, j)),
            pl.BlockSpec((wm, tn), lambda j, i: (0, j)),
            pl.BlockSpec((tm, tn), lambda j, i: (i, j)),
            pl.BlockSpec((tm, tn), lambda j, i: (i, per_branch + j)),
            pl.BlockSpec((tm, tn), lambda j, i: (i, 2 * per_branch + j)),
        ],
        out_specs=pl.BlockSpec((tm, tn), lambda j, i: (i, j)),
        out_shape=jax.ShapeDtypeStruct((t, d), BF16),
        compiler_params=pltpu.CompilerParams(
            dimension_semantics=("parallel", "parallel"),
            vmem_limit_bytes=_vmem_limit(pipelined, 3 * _nbytes((tm, tn), F32))),
        name="merge",
    )(o_dn, o_swa, o_mem, p_dn, p_swa, p_mem, gates, gates, gates)


def _outproj_kernel(x_ref, m_ref, w_ref, nw_ref, x1_ref, h_ref):
    x1 = x_ref[...] + _dot(m_ref[...], w_ref[...])
    x1_ref[...] = x1
    h_ref[...] = _rms(x1, nw_ref[...]).astype(h_ref.dtype)


def _outproj(x, merged, w_out, norm_w, tm):
    t, d = x.shape
    pipelined = 2 * _nbytes((tm, d), F32) + 2 * _nbytes((tm, d), BF16) + _nbytes((d, d), BF16)
    return pl.pallas_call(
        _outproj_kernel,
        grid=(t // tm,),
        in_specs=[
            pl.BlockSpec((tm, d), lambda i: (i, 0)),
            pl.BlockSpec((tm, d), lambda i: (i, 0)),
            pl.BlockSpec((d, d), lambda i: (0, 0)),
            pl.BlockSpec((1, d), lambda i: (0, 0)),
        ],
        out_specs=[pl.BlockSpec((tm, d), lambda i: (i, 0)), pl.BlockSpec((tm, d), lambda i: (i, 0))],
        out_shape=[jax.ShapeDtypeStruct((t, d), F32), jax.ShapeDtypeStruct((t, d), BF16)],
        compiler_params=pltpu.CompilerParams(
            dimension_semantics=("parallel",),
            vmem_limit_bytes=_vmem_limit(pipelined, 2 * _nbytes((tm, d), F32))),
        name="outproj",
    )(x, merged, w_out, norm_w)


MLP_RING = 3


def _mlp_kernel(h_ref, x1_ref, wu_hbm, wd_hbm, o_ref, ubuf, dbuf, sem):
    i, k = pl.program_id(0), pl.program_id(1)
    n_k = pl.num_programs(1)
    tf = dbuf.shape[1]
    step = i * n_k + k
    n_steps = pl.num_programs(0) * n_k

    def tile_copies(ff_step, slot):
        start = pl.multiple_of(ff_step * tf, tf)
        return (pltpu.make_async_copy(wu_hbm.at[:, pl.ds(start, tf)], ubuf.at[slot], sem.at[0, slot]),
                pltpu.make_async_copy(wd_hbm.at[pl.ds(start, tf), :], dbuf.at[slot], sem.at[1, slot]))

    def request(ahead):
        target = step + ahead

        @pl.when(target < n_steps)
        def _():
            for copy in tile_copies(lax.rem(target, n_k), lax.rem(target, MLP_RING)):
                copy.start()

    @pl.when(step == 0)
    def _():
        for ahead in range(MLP_RING - 1):
            request(ahead)

    request(MLP_RING - 1)
    slot = lax.rem(step, MLP_RING)
    for copy in tile_copies(k, slot):
        copy.wait()

    @pl.when(k == 0)
    def _():
        o_ref[...] = x1_ref[...]

    a = jnp.maximum(_dot(h_ref[...], ubuf[slot]), 0.0)
    o_ref[...] += _dot((a * a).astype(BF16), dbuf[slot])


def _mlp(h2, x1, w_up, w_down, tm, tf):
    t, d = x1.shape
    f = w_up.shape[1]
    pipelined = _nbytes((tm, d), BF16) + 2 * _nbytes((tm, d), F32)
    resident = 2 * _nbytes((tm, tf), F32) + MLP_RING * (_nbytes((d, tf), BF16) + _nbytes((tf, d), BF16))
    return pl.pallas_call(
        _mlp_kernel,
        grid=(t // tm, f // tf),
        in_specs=[
            pl.BlockSpec((tm, d), lambda i, k: (i, 0)),
            pl.BlockSpec((tm, d), lambda i, k: (i, 0)),
            pl.BlockSpec(memory_space=pl.ANY),
            pl.BlockSpec(memory_space=pl.ANY),
        ],
        out_specs=pl.BlockSpec((tm, d), lambda i, k: (i, 0)),
        out_shape=jax.ShapeDtypeStruct((t, d), F32),
        scratch_shapes=[pltpu.VMEM((MLP_RING, d, tf), BF16), pltpu.VMEM((MLP_RING, tf, d), BF16),
                        pltpu.SemaphoreType.DMA((2, MLP_RING))],
        compiler_params=pltpu.CompilerParams(
            dimension_semantics=("arbitrary", "arbitrary"),
            vmem_limit_bytes=_vmem_limit(pipelined, resident)),
        name="mlp",
    )(h2, x1, w_up, w_down)


def _layer(x, mem, attn_norm_w, w_in, dn_conv_w, dn_a_log, dn_dt_bias, dn_out_norm_w, swa_q_norm_w,
           swa_k_norm_w, swa_sinks, rel_bias, mem_norm_w, w_mem_kv, xq_norm_w, xk_norm_w,
           p_dn, p_swa, p_mem, w_out, mlp_norm_w, w_mlp_up, w_mlp_down):
    b, s, d = x.shape
    m = mem.shape[1]
    t = b * s

    dn_heads = dn_a_log.shape[0]
    dn_dim = dn_out_norm_w.shape[0]
    dn_w = dn_heads * dn_dim
    swa_heads = swa_sinks.shape[0]
    swa_dim = swa_q_norm_w.shape[0]
    swa_w = swa_heads * swa_dim
    mem_dim = xq_norm_w.shape[0]
    mem_w = p_mem.shape[0]
    mem_heads = mem_w // mem_dim
    swa_kv_w = (w_in.shape[1] - 4 * dn_w - 2 * dn_heads - swa_w - mem_w - 3 * d) // 2
    swa_kv = swa_kv_w // swa_dim

    n_ba = 2 * dn_heads
    src_ba = 4 * dn_w
    src_swa = src_ba + n_ba
    w_t = w_in.T.astype(BF16)
    w_b = w_t[src_swa:]
    w_c = jnp.concatenate([w_t[src_ba:src_swa], jnp.zeros((LANES - n_ba, d), BF16)], axis=0)
    col = {"qkv": 0, "z": 3 * dn_w, "sq": src_ba}
    col["sk"] = col["sq"] + swa_w
    col["sv"] = col["sk"] + swa_kv_w
    col["mq"] = col["sv"] + swa_kv_w

    proj, gates, ba = _inproj(x.reshape(t, d), attn_norm_w.reshape(1, d), w_t, src_ba, w_b, w_c, 3 * d,
                              tm=1024, tn=1024)
    proj3 = proj.reshape(b, s, -1)
    mkv = _norm_matmul(mem.reshape(b * m, d), mem_norm_w.reshape(1, d), w_mem_kv.astype(BF16), tm=512, tn=512)
    mkv3 = mkv.reshape(b, m, -1)

    gate_params = jnp.zeros((SUBLANES, LANES), F32)
    gate_params = gate_params.at[0, dn_heads:2 * dn_heads].set(dn_a_log)
    gate_params = gate_params.at[1, dn_heads:2 * dn_heads].set(dn_dt_bias)
    o_dn, w_up_bf, w_down_bf = _deltanet(proj3, ba.reshape(b, s, LANES), dn_conv_w, gate_params,
                                         dn_out_norm_w.reshape(1, dn_dim), [w_mlp_up, w_mlp_down],
                                         dn_heads, dn_dim, col["z"])
    o_swa = _swa(proj3, rel_bias.reshape(-1), swa_sinks, swa_q_norm_w.reshape(1, swa_dim),
                 swa_k_norm_w.reshape(1, swa_dim), swa_heads, swa_kv, swa_dim, col["sq"], col["sk"], col["sv"])
    o_mem, w_out_bf, p_dn_bf, p_swa_bf, p_mem_bf = _memattn(
        proj3, mkv3, xq_norm_w.reshape(1, mem_dim), xk_norm_w.reshape(1, mem_dim),
        [w_out, p_dn, p_swa, p_mem], mem_heads, mem_dim, col["mq"], tq=1024)

    merged = _merge(o_dn.reshape(t, dn_w), o_swa.reshape(t, swa_w), o_mem.reshape(t, mem_w),
                    p_dn_bf, p_swa_bf, p_mem_bf, gates, tm=1024, tn=1024)
    x1, h2 = _outproj(x.reshape(t, d), merged, w_out_bf, mlp_norm_w.reshape(1, d), tm=512)
    out = _mlp(h2, x1, w_up_bf, w_down_bf, tm=512, tf=1024)
    return out.reshape(b, s, d)


def kernel(x, mem, attn_norm_w, w_in, dn_conv_w, dn_A_log, dn_dt_bias, dn_out_norm_w, swa_q_norm_w,
           swa_k_norm_w, swa_sinks, rel_bias, mem_norm_w, w_mem_kv, xq_norm_w, xk_norm_w, p_dn, p_swa,
           p_mem, w_out, mlp_norm_w, w_mlp_up, w_mlp_down):
    depth = w_in.shape[0]
    for l in range(depth):
        x = _layer(x, mem, attn_norm_w[l], w_in[l], dn_conv_w[l], dn_A_log[l], dn_dt_bias[l],
                   dn_out_norm_w[l], swa_q_norm_w[l], swa_k_norm_w[l], swa_sinks[l], rel_bias,
                   mem_norm_w[l], w_mem_kv[l], xq_norm_w[l], xk_norm_w[l], p_dn[l], p_swa[l], p_mem[l],
                   w_out[l], mlp_norm_w[l], w_mlp_up[l], w_mlp_down[l])
    return x
```

```python
import functools
import math

import jax
import jax.numpy as jnp
from jax import lax
from jax.experimental import pallas as pl
from jax.experimental.pallas import tpu as pltpu

F32 = jnp.float32
BF16 = jnp.bfloat16

EPS = 1e-6
LANES = 128
SUBLANES = 8
V7X_VMEM_BYTES = 64 * 1024 * 1024

DN_CHUNK = 64
DN_GROUP = 256
DN_BASE = 8
SWA_WINDOW = 128
SWA_BLOCK = 128
SWA_STEP_BLOCKS = 4
N_BUCKETS = 32
MAX_DISTANCE = 128

_NT = (((1,), (1,)), ((), ()))
_TN = (((0,), (0,)), ((), ()))


COMPILER_SCRATCH_BYTES = 4 * 1024 * 1024
VMEM_RESERVE_BYTES = 8 * 1024 * 1024


def _vmem_limit(pipelined_bytes, resident_bytes, claim_all=False):
    cap = V7X_VMEM_BYTES - VMEM_RESERVE_BYTES
    want = 2 * pipelined_bytes + resident_bytes + COMPILER_SCRATCH_BYTES
    return int(cap if claim_all else min(want, cap))


def _nbytes(shape, dtype):
    return math.prod(shape) * jnp.dtype(dtype).itemsize


def _sigmoid(v):
    return 0.5 * jnp.tanh(0.5 * v) + 0.5


def _silu(v):
    return v * _sigmoid(v)


def _rms(v, w):
    return (v * lax.rsqrt(jnp.mean(v * v, axis=-1, keepdims=True) + EPS)) * w


def _dot(a, b):
    return jnp.dot(a, b, preferred_element_type=F32)


def _dot_nt(a, b):
    return lax.dot_general(a, b, _NT, preferred_element_type=F32)


def _cast_rider_specs(weights, n_steps, step_of):
    in_specs, out_specs, out_shapes = [], [], []
    for w in weights:
        rows = w.shape[0] // n_steps
        assert rows * n_steps == w.shape[0] and rows % (2 * SUBLANES) == 0
        for specs in (in_specs, out_specs):
            specs.append(pl.BlockSpec((rows, w.shape[1]), lambda *ids: (step_of(*ids), 0)))
        out_shapes.append(jax.ShapeDtypeStruct(w.shape, BF16))
    return in_specs, out_specs, out_shapes


def _cast_riders(in_refs, out_refs):
    for i_ref, o_ref in zip(in_refs, out_refs):
        o_ref[...] = i_ref[...].astype(o_ref.dtype)


IN_RING = 4


def _inproj_kernel(x_ref, nw_ref, wa_hbm, wb_hbm, wc_ref, o_ref, g_ref, ba_ref, h_ref, wbuf, sem,
                   *, n_a, n_p, tail, rows_b1):
    i, j = pl.program_id(0), pl.program_id(1)
    n_j = pl.num_programs(1)
    tn = wbuf.shape[1]
    step = i * n_j + j
    n_steps = pl.num_programs(0) * n_j

    def tile_copy(col_step, slot, from_b):
        if from_b:
            row = jnp.where(col_step < n_p, (col_step - n_a) * tn, rows_b1 + (col_step - n_p) * tn)
            src = wb_hbm.at[pl.ds(pl.multiple_of(row, 2 * SUBLANES), tn), :]
        else:
            src = wa_hbm.at[pl.ds(pl.multiple_of(col_step * tn, tn), tn), :]
        return pltpu.make_async_copy(src, wbuf.at[slot], sem.at[slot])

    def request(ahead):
        target = step + ahead
        col_step = lax.rem(target, n_j)
        slot = lax.rem(target, IN_RING)

        @pl.when((target < n_steps) & (col_step < n_a))
        def _():
            tile_copy(col_step, slot, from_b=False).start()

        @pl.when((target < n_steps) & (col_step >= n_a))
        def _():
            tile_copy(col_step, slot, from_b=True).start()

    @pl.when(step == 0)
    def _():
        for ahead in range(IN_RING - 1):
            request(ahead)

    request(IN_RING - 1)
    slot = lax.rem(step, IN_RING)
    pltpu.make_async_copy(wb_hbm.at[pl.ds(0, tn), :], wbuf.at[slot], sem.at[slot]).wait()
    w_ref = wbuf.at[slot]

    @pl.when(j == 0)
    def _():
        h_ref[...] = _rms(x_ref[...], nw_ref[...]).astype(BF16)
        ba_ref[...] = _dot_nt(h_ref[...], wc_ref[...])

    @pl.when(j < n_p - 1)
    def _():
        o_ref[...] = _dot_nt(h_ref[...], w_ref[...])

    @pl.when(j == n_p - 1)
    def _():
        o_ref[:, :tail] = _dot_nt(h_ref[...], w_ref[:tail, :])

    @pl.when(j >= n_p)
    def _():
        g_ref[...] = _sigmoid(_dot_nt(h_ref[...], w_ref[...])).astype(g_ref.dtype)


def _inproj(x, nw, w_a, rows_a, w_b, w_c, n_gates, tm, tn):
    m, k = x.shape
    rows_b1 = w_b.shape[0] - n_gates
    n = rows_a + rows_b1
    n_a = rows_a // tn
    n_p = pl.cdiv(n, tn)
    assert rows_a % tn == 0 and n_gates % tn == 0 and rows_b1 % (2 * SUBLANES) == 0

    pipelined = (_nbytes((tm, k), F32) + _nbytes((tm, tn), F32) + _nbytes((tm, tn), BF16)
                 + _nbytes((tm, LANES), F32))
    resident = (_nbytes((tm, k), BF16) + _nbytes((tm, k), F32) + _nbytes((LANES, k), BF16)
                + IN_RING * _nbytes((tn, k), BF16))
    return pl.pallas_call(
        functools.partial(_inproj_kernel, n_a=n_a, n_p=n_p, tail=n - (n_p - 1) * tn, rows_b1=rows_b1),
        grid=(m // tm, n_p + n_gates // tn),
        in_specs=[
            pl.BlockSpec((tm, k), lambda i, j: (i, 0)),
            pl.BlockSpec((1, k), lambda i, j: (0, 0)),
            pl.BlockSpec(memory_space=pl.ANY),
            pl.BlockSpec(memory_space=pl.ANY),
            pl.BlockSpec((LANES, k), lambda i, j: (0, 0)),
        ],
        out_specs=[pl.BlockSpec((tm, tn), lambda i, j: (i, jnp.minimum(j, n_p - 1))),
                   pl.BlockSpec((tm, tn), lambda i, j: (i, jnp.maximum(j - n_p, 0))),
                   pl.BlockSpec((tm, LANES), lambda i, j: (i, 0))],
        out_shape=[jax.ShapeDtypeStruct((m, n), F32), jax.ShapeDtypeStruct((m, n_gates), BF16),
                   jax.ShapeDtypeStruct((m, LANES), F32)],
        scratch_shapes=[pltpu.VMEM((tm, k), BF16), pltpu.VMEM((IN_RING, tn, k), BF16),
                        pltpu.SemaphoreType.DMA((IN_RING,))],
        compiler_params=pltpu.CompilerParams(
            dimension_semantics=("arbitrary", "arbitrary"),
            vmem_limit_bytes=_vmem_limit(pipelined, resident)),
        name="inproj",
    )(x, nw, w_a, w_b, w_c)


def _norm_matmul_kernel(x_ref, nw_ref, w_ref, o_ref, h_ref):
    @pl.when(pl.program_id(1) == 0)
    def _():
        h_ref[...] = _rms(x_ref[...], nw_ref[...]).astype(BF16)

    o_ref[...] = _dot(h_ref[...], w_ref[...]).astype(o_ref.dtype)


def _norm_matmul(x, nw, w, tm, tn):
    m, k = x.shape
    n = w.shape[1]
    pipelined = _nbytes((tm, k), F32) + _nbytes((k, tn), BF16) + _nbytes((tm, tn), F32)
    resident = _nbytes((tm, k), BF16) + _nbytes((tm, k), F32)
    return pl.pallas_call(
        _norm_matmul_kernel,
        grid=(m // tm, n // tn),
        in_specs=[
            pl.BlockSpec((tm, k), lambda i, j: (i, 0)),
            pl.BlockSpec((1, k), lambda i, j: (0, 0)),
            pl.BlockSpec((k, tn), lambda i, j: (0, j)),
        ],
        out_specs=pl.BlockSpec((tm, tn), lambda i, j: (i, j)),
        out_shape=jax.ShapeDtypeStruct((m, n), F32),
        scratch_shapes=[pltpu.VMEM((tm, k), BF16)],
        compiler_params=pltpu.CompilerParams(
            dimension_semantics=("parallel", "arbitrary"),
            vmem_limit_bytes=_vmem_limit(pipelined, resident)),
        name="norm_matmul",
    )(x, nw, w)


def _dn_kernel(*refs, heads, dim, n_riders):
    qkv_ref, z_ref, ba_ref, cw_ref, gp_ref, onw_ref = refs[:6]
    o_ref = refs[6 + n_riders]
    ext_ref, state_ref = refs[-2:]
    _cast_riders(refs[6:6 + n_riders], refs[7 + n_riders:7 + 2 * n_riders])

    gt = DN_GROUP
    c = DN_CHUNK
    hd = heads * dim
    hs = range(heads)
    chunks = range(gt // c)

    @pl.when(pl.program_id(1) == 0)
    def _():
        ext_ref[0:SUBLANES, :] = jnp.zeros((SUBLANES, 3 * hd), F32)
        state_ref[...] = jnp.zeros_like(state_ref)

    ext_ref[SUBLANES:SUBLANES + gt, :] = qkv_ref[...]
    xe = ext_ref[...]
    conv = cw_ref[3:4, :] * xe[SUBLANES:]
    for s in (1, 2, 3):
        conv = conv + cw_ref[3 - s:4 - s, :] * pltpu.roll(xe, s, 0)[SUBLANES:]
    ext_ref[0:SUBLANES, :] = xe[gt:gt + SUBLANES]
    act = _silu(conv)

    ba = ba_ref[...]
    beta_all = _sigmoid(ba)
    xa = ba + gp_ref[1:2, :]
    softplus = jnp.maximum(xa, 0.0) + jnp.log1p(jnp.exp(-jnp.abs(xa)))
    g_all = -jnp.exp(gp_ref[0:1, :]) * softplus
    row_in_chunk = lax.broadcasted_iota(jnp.int32, (gt, LANES), 0) & (c - 1)
    gcum = g_all
    s = 1
    while s < c:
        gcum = gcum + jnp.where(row_in_chunk >= s, pltpu.roll(gcum, s, 0), 0.0)
        s *= 2
    exp_g = jnp.exp(gcum)
    kdec_parts, gc_rows = [], []
    for j in chunks:
        g_last = gcum[c * j + c - 1:c * j + c, :]
        kdec_parts.append(jnp.exp(g_last - gcum[c * j:c * (j + 1), :]))
        gc_rows.append(jnp.exp(g_last))
    kdec = jnp.concatenate(kdec_parts, axis=0)
    gcum_t = gcum.T

    ri = lax.broadcasted_iota(jnp.int32, (gt, gt), 0)
    ci = lax.broadcasted_iota(jnp.int32, (gt, gt), 1)
    same_chunk = (ri // c) == (ci // c)
    strict = same_chunk & (ri > ci)
    incl = same_chunk & (ri >= ci)

    qn, kn, kb, beta, eg = [], [], [], [], []
    for h in hs:
        qh = act[:, h * dim:(h + 1) * dim]
        kh = act[:, hd + h * dim:hd + (h + 1) * dim]
        qn.append(qh * lax.rsqrt(jnp.sum(qh * qh, axis=-1, keepdims=True) + EPS) * (dim ** -0.5))
        kn.append(kh * lax.rsqrt(jnp.sum(kh * kh, axis=-1, keepdims=True) + EPS))
        kb.append(kn[h].astype(BF16))
        beta.append(beta_all[:, h:h + 1])
        eg.append(exp_g[:, heads + h:heads + h + 1])

    decay = [jnp.exp(jnp.where(incl, gcum[:, heads + h:heads + h + 1] - gcum_t[heads + h:heads + h + 1, :],
                               -jnp.inf)) for h in hs]
    kk = [_dot_nt(kb[h], kb[h]) for h in hs]
    nmat = [jnp.where(strict, (beta[h] * kk[h]) * decay[h], 0.0) for h in hs]

    def same_block(size):
        return (ri // size) == (ci // size)

    eye = jnp.where(ri == ci, 1.0, 0.0)
    base = same_block(DN_BASE)
    pw = [jnp.where(base, nmat[h], 0.0) for h in hs]
    tinv = [eye - pw[h] for h in hs]
    pw = [pw[h].astype(BF16) for h in hs]
    order = 2
    while order < DN_BASE:
        pw = [_dot(pw[h], pw[h]).astype(BF16) for h in hs]
        tinv = [tinv[h] + _dot(tinv[h].astype(BF16), pw[h]) for h in hs]
        order *= 2
    size = DN_BASE
    while size < c:
        level = same_block(2 * size) & jnp.logical_not(same_block(size))
        off = [jnp.where(level, nmat[h], 0.0).astype(BF16) for h in hs]
        ct = [_dot(off[h], tinv[h].astype(BF16)).astype(BF16) for h in hs]
        tinv = [tinv[h] - _dot(tinv[h].astype(BF16), ct[h]) for h in hs]
        size *= 2

    sol = [jnp.concatenate([act[:, 2 * hd + h * dim:2 * hd + (h + 1) * dim] * beta[h],
                            kn[h] * (beta[h] * eg[h])], axis=1) for h in hs]
    sol = [sol[h] + _dot((tinv[h] - eye).astype(BF16), sol[h].astype(BF16)) for h in hs]

    pmat = [(_dot_nt(qn[h].astype(BF16), kb[h]) * decay[h]).astype(BF16) for h in hs]
    wq = [[jnp.concatenate([sol[h][c * j:c * (j + 1), dim:], (qn[h] * eg[h])[c * j:c * (j + 1)]],
                           axis=0).astype(BF16) for j in chunks] for h in hs]
    kd = [(kn[h] * kdec[:, heads + h:heads + h + 1]).astype(BF16) for h in hs]

    state = [state_ref[h] for h in hs]
    o_parts = [[] for _ in hs]
    for j in chunks:
        r = slice(c * j, c * (j + 1))
        ws = [_dot(wq[h][j], state[h].astype(BF16)) for h in hs]
        db = [(sol[h][r, :dim] - ws[h][:c]).astype(BF16) for h in hs]
        for h in hs:
            o_parts[h].append(ws[h][c:] + _dot(pmat[h][r, r], db[h]))
        state = [gc_rows[j][:, heads + h:heads + h + 1] * state[h]
                 + lax.dot_general(kd[h][r], db[h], _TN, preferred_element_type=F32) for h in hs]
    for h in hs:
        state_ref[h] = state[h]

    for h in hs:
        o = jnp.concatenate(o_parts[h], axis=0)
        zh = z_ref[:, h * dim:(h + 1) * dim]
        o_ref[:, h * dim:(h + 1) * dim] = (_rms(o, onw_ref[...]) * _silu(zh)).astype(o_ref.dtype)


def _deltanet(proj3, ba3, conv_w, gate_params, out_norm_w, riders, heads, dim, col_z):
    b, s, _ = proj3.shape
    hd = heads * dim
    gt = DN_GROUP
    ng = s // gt
    rider_in, rider_out, rider_shapes = _cast_rider_specs(riders, b * ng, lambda i, g: i * ng + g)
    pipelined = (_nbytes((gt, 3 * hd), F32) + _nbytes((gt, hd), F32) + _nbytes((gt, LANES), F32)
                 + _nbytes((gt, hd), BF16) + sum(_nbytes(w.shape, F32) * 3 // 2 for w in riders) // (b * ng))
    resident = (_nbytes((gt + SUBLANES, 3 * hd), F32) * 4 + _nbytes((heads, dim, dim), F32)
                + heads * 6 * _nbytes((gt, gt), F32))
    return pl.pallas_call(
        functools.partial(_dn_kernel, heads=heads, dim=dim, n_riders=len(riders)),
        grid=(b, ng),
        in_specs=[
            pl.BlockSpec((None, gt, 3 * hd), lambda i, g: (i, g, 0)),
            pl.BlockSpec((None, gt, hd), lambda i, g: (i, g, col_z // hd)),
            pl.BlockSpec((None, gt, LANES), lambda i, g: (i, g, 0)),
            pl.BlockSpec((4, 3 * hd), lambda i, g: (0, 0)),
            pl.BlockSpec((SUBLANES, LANES), lambda i, g: (0, 0)),
            pl.BlockSpec((1, dim), lambda i, g: (0, 0)),
            *rider_in,
        ],
        out_specs=[pl.BlockSpec((None, gt, hd), lambda i, g: (i, g, 0)), *rider_out],
        out_shape=[jax.ShapeDtypeStruct((b, s, hd), BF16), *rider_shapes],
        scratch_shapes=[pltpu.VMEM((gt + SUBLANES, 3 * hd), F32), pltpu.VMEM((heads, dim, dim), F32)],
        compiler_params=pltpu.CompilerParams(
            dimension_semantics=("parallel", "arbitrary"),
            vmem_limit_bytes=_vmem_limit(pipelined, resident, claim_all=True)),
        name="deltanet",
    )(proj3, proj3, ba3, conv_w, gate_params, out_norm_w, *riders)


def _swa_kernel(rb_ref, sink_ref, q_ref, kc_ref, kp_ref, vc_ref, vp_ref, qw_ref, kw_ref, o_ref, bias_ref,
                *, q_heads, kv_heads, dim):
    blk = SWA_BLOCK
    n = pl.program_id(1)
    qi = lax.broadcasted_iota(jnp.int32, (blk, 2 * blk), 0)
    kj = lax.broadcasted_iota(jnp.int32, (blk, 2 * blk), 1)
    dist = qi - kj + blk

    @pl.when((pl.program_id(0) == 0) & (n == 0))
    def _():
        max_exact = N_BUCKETS // 2
        nn = jnp.maximum(dist, 0)
        nf = jnp.maximum(nn, 1).astype(F32)
        large = max_exact + (jnp.log(nf / max_exact) / math.log(MAX_DISTANCE / max_exact)
                             * (N_BUCKETS - max_exact)).astype(jnp.int32)
        bucket = jnp.where(nn < max_exact, nn, jnp.minimum(large, N_BUCKETS - 1))
        for h in range(q_heads):
            acc = jnp.zeros((blk, 2 * blk), F32)
            for bk in range(N_BUCKETS):
                acc = jnp.where(bucket == bk, rb_ref[bk * q_heads + h], acc)
            acc = jnp.where((dist >= 0) & (dist < SWA_WINDOW), acc, -jnp.inf)
            bias_ref[h] = acc
            bias_ref[q_heads + h] = jnp.where(kj >= blk, acc, -jnp.inf)

    first = jnp.where(n == 0, q_heads, 0)
    group = q_heads // kv_heads
    for j in range(kv_heads):
        cols = slice(j * dim, (j + 1) * dim)
        k_all = _rms(jnp.concatenate([kp_ref[:, cols], kc_ref[:, cols]], axis=0), kw_ref[...]).astype(BF16)
        v_all = jnp.concatenate([vp_ref[:, cols], vc_ref[:, cols]], axis=0).astype(BF16)
        for u in range(SWA_STEP_BLOCKS):
            rows = slice(u * blk, (u + 1) * blk)
            kwin, vwin = k_all[u * blk:(u + 2) * blk], v_all[u * blk:(u + 2) * blk]
            qcat = jnp.concatenate(
                [_rms(q_ref[rows, (j * group + i) * dim:(j * group + i + 1) * dim], qw_ref[...])
                 for i in range(group)], axis=0).astype(BF16)
            logits = _dot_nt(qcat, kwin) * (dim ** -0.5)
            for i in range(group):
                h = j * group + i
                lg = logits[i * blk:(i + 1) * blk] + bias_ref[(first if u == 0 else 0) + h]
                sink = sink_ref[h]
                mx = jnp.maximum(jnp.max(lg, axis=-1, keepdims=True), sink)
                e = jnp.exp(lg - mx)
                den = jnp.sum(e, axis=-1, keepdims=True) + jnp.exp(sink - mx)
                pv = _dot(e.astype(BF16), vwin)
                o_ref[rows, h * dim:(h + 1) * dim] = (pv / den).astype(o_ref.dtype)


def _swa(proj3, rel_bias_flat, sinks, q_norm_w, k_norm_w, q_heads, kv_heads, dim, col_q, col_k, col_v):
    b, s, _ = proj3.shape
    blk = SWA_BLOCK
    step = SWA_STEP_BLOCKS * blk
    qw, kvw = q_heads * dim, kv_heads * dim
    smem = pl.BlockSpec(memory_space=pltpu.SMEM)
    pipelined = (_nbytes((step, qw), F32) + 2 * _nbytes((step + blk, kvw), F32) + _nbytes((step, qw), BF16))
    resident = (_nbytes((2 * q_heads, blk, 2 * blk), F32) + 2 * pipelined
                + 2 * SWA_STEP_BLOCKS * _nbytes((q_heads // kv_heads * blk, 2 * blk), F32))

    def prev(n):
        return jnp.maximum(SWA_STEP_BLOCKS * n - 1, 0)

    return pl.pallas_call(
        functools.partial(_swa_kernel, q_heads=q_heads, kv_heads=kv_heads, dim=dim),
        grid=(b, s // step),
        in_specs=[
            smem, smem,
            pl.BlockSpec((None, step, qw), lambda i, n: (i, n, col_q // qw)),
            pl.BlockSpec((None, step, kvw), lambda i, n: (i, n, col_k // kvw)),
            pl.BlockSpec((None, blk, kvw), lambda i, n: (i, prev(n), col_k // kvw)),
            pl.BlockSpec((None, step, kvw), lambda i, n: (i, n, col_v // kvw)),
            pl.BlockSpec((None, blk, kvw), lambda i, n: (i, prev(n), col_v // kvw)),
            pl.BlockSpec((1, dim), lambda i, n: (0, 0)),
            pl.BlockSpec((1, dim), lambda i, n: (0, 0)),
        ],
        out_specs=pl.BlockSpec((None, step, qw), lambda i, n: (i, n, 0)),
        out_shape=jax.ShapeDtypeStruct((b, s, qw), BF16),
        scratch_shapes=[pltpu.VMEM((2 * q_heads, blk, 2 * blk), F32)],
        compiler_params=pltpu.CompilerParams(
            dimension_semantics=("arbitrary", "arbitrary"),
            vmem_limit_bytes=_vmem_limit(pipelined, resident, claim_all=True)),
        name="swa",
    )(rel_bias_flat, sinks, proj3, proj3, proj3, proj3, proj3, q_norm_w, k_norm_w)


def _memattn_kernel(*refs, heads, dim, n_riders):
    q_ref, k_ref, v_ref, qw_ref, kw_ref = refs[:5]
    o_ref = refs[5 + n_riders]
    _cast_riders(refs[5:5 + n_riders], refs[6 + n_riders:])
    for h in range(heads):
        cols = slice(h * dim, (h + 1) * dim)
        qn = _rms(q_ref[:, cols], qw_ref[...]).astype(BF16)
        kn = _rms(k_ref[:, cols], kw_ref[...]).astype(BF16)
        lg = _dot_nt(qn, kn) * (dim ** -0.5)
        e = jnp.exp(lg - jnp.max(lg, axis=-1, keepdims=True))
        den = jnp.sum(e, axis=-1, keepdims=True)
        pv = _dot(e.astype(BF16), v_ref[:, cols].astype(BF16))
        o_ref[:, cols] = (pv / den).astype(o_ref.dtype)


def _memattn(proj3, mkv3, q_norm_w, k_norm_w, riders, heads, dim, col_q, tq):
    b, s, _ = proj3.shape
    m = mkv3.shape[1]
    w = heads * dim
    nt = s // tq
    rider_in, rider_out, rider_shapes = _cast_rider_specs(riders, b * nt, lambda i, t: i * nt + t)
    pipelined = (_nbytes((tq, w), F32) + 2 * _nbytes((m, w), F32) + _nbytes((tq, w), BF16)
                 + sum(_nbytes(r.shape, F32) * 3 // 2 for r in riders) // (b * nt))
    return pl.pallas_call(
        functools.partial(_memattn_kernel, heads=heads, dim=dim, n_riders=len(riders)),
        grid=(b, nt),
        in_specs=[
            pl.BlockSpec((pl.Element(tq), pl.Element(w)), lambda i, t: (pl.multiple_of(i * s + t * tq, tq), col_q)),
            pl.BlockSpec((None, m, w), lambda i, t: (i, 0, 0)),
            pl.BlockSpec((None, m, w), lambda i, t: (i, 0, 1)),
            pl.BlockSpec((1, dim), lambda i, t: (0, 0)),
            pl.BlockSpec((1, dim), lambda i, t: (0, 0)),
            *rider_in,
        ],
        out_specs=[pl.BlockSpec((None, tq, w), lambda i, t: (i, t, 0)), *rider_out],
        out_shape=[jax.ShapeDtypeStruct((b, s, w), BF16), *rider_shapes],
        compiler_params=pltpu.CompilerParams(
            dimension_semantics=("parallel", "parallel"),
            vmem_limit_bytes=_vmem_limit(pipelined, 2 * heads * _nbytes((tq, m), F32), claim_all=True)),
        name="memattn",
    )(proj3.reshape(b * s, -1), mkv3, mkv3, q_norm_w, k_norm_w, *riders)


def _merge_kernel(od_ref, os_ref, om_ref, pd_ref, ps_ref, pm_ref, gd_ref, gs_ref, gm_ref, o_ref):
    merged = (gd_ref[...].astype(F32) * _dot(od_ref[...], pd_ref[...])
              + gs_ref[...].astype(F32) * _dot(os_ref[...], ps_ref[...])
              + gm_ref[...].astype(F32) * _dot(om_ref[...], pm_ref[...]))
    o_ref[...] = merged.astype(o_ref.dtype)


def _merge(o_dn, o_swa, o_mem, p_dn, p_swa, p_mem, gates, tm, tn):
    t = o_dn.shape[0]
    d = p_dn.shape[1]
    wd, ws, wm = o_dn.shape[1], o_swa.shape[1], o_mem.shape[1]
    per_branch = d // tn
    pipelined = ((_nbytes((tm, wd), BF16) + _nbytes((tm, ws), BF16) + _nbytes((tm, wm), BF16))
                 + (_nbytes((wd, tn), BF16) + _nbytes((ws, tn), BF16) + _nbytes((wm, tn), BF16))
                 + 4 * _nbytes((tm, tn), BF16))
    return pl.pallas_call(
        _merge_kernel,
        grid=(d // tn, t // tm),
        in_specs=[
            pl.BlockSpec((tm, wd), lambda j, i: (i, 0)),
            pl.BlockSpec((tm, ws), lambda j, i: (i, 0)),
            pl.BlockSpec((tm, wm), lambda j, i: (i, 0)),
            pl.BlockSpec((wd, tn), lambda j, i: (0, j)),
            pl.BlockSpec((ws, tn), lambda j, i: (0, j)),
            pl.BlockSpec((wm, tn), lambda j, i: (0, j)),
            pl.BlockSpec((tm, tn), lambda j, i: (i, j)),
            pl.BlockSpec((tm, tn), lambda j, i: (i, per_branch + j)),
            pl.BlockSpec((tm, tn), lambda j, i: (i, 2 * per_branch + j)),
        ],
        out_specs=pl.BlockSpec((tm, tn), lambda j, i: (i, j)),
        out_shape=jax.ShapeDtypeStruct((t, d), BF16),
        compiler_params=pltpu.CompilerParams(
            dimension_semantics=("parallel", "parallel"),
            vmem_limit_bytes=_vmem_limit(pipelined, 3 * _nbytes((tm, tn), F32))),
        name="merge",
    )(o_dn, o_swa, o_mem, p_dn, p_swa, p_mem, gates, gates, gates)


def _outproj_kernel(x_ref, m_ref, w_ref, nw_ref, x1_ref, h_ref):
    x1 = x_ref[...] + _dot(m_ref[...], w_ref[...])
    x1_ref[...] = x1
    h_ref[...] = _rms(x1, nw_ref[...]).astype(h_ref.dtype)


def _outproj(x, merged, w_out, norm_w, tm):
    t, d = x.shape
    pipelined = 2 * _nbytes((tm, d), F32) + 2 * _nbytes((tm, d), BF16) + _nbytes((d, d), BF16)
    return pl.pallas_call(
        _outproj_kernel,
        grid=(t // tm,),
        in_specs=[
            pl.BlockSpec((tm, d), lambda i: (i, 0)),
            pl.BlockSpec((tm, d), lambda i: (i, 0)),
            pl.BlockSpec((d, d), lambda i: (0, 0)),
            pl.BlockSpec((1, d), lambda i: (0, 0)),
        ],
        out_specs=[pl.BlockSpec((tm, d), lambda i: (i, 0)), pl.BlockSpec((tm, d), lambda i: (i, 0))],
        out_shape=[jax.ShapeDtypeStruct((t, d), F32), jax.ShapeDtypeStruct((t, d), BF16)],
        compiler_params=pltpu.CompilerParams(
            dimension_semantics=("parallel",),
            vmem_limit_bytes=_vmem_limit(pipelined, 2 * _nbytes((tm, d), F32))),
        name="outproj",
    )(x, merged, w_out, norm_w)


MLP_RING = 3


def _mlp_kernel(h_ref, x1_ref, wu_hbm, wd_hbm, o_ref, ubuf, dbuf, sem, *, n_k):
    tf = dbuf.shape[1]
    base = pl.program_id(0) * n_k
    n_tiles = pl.num_programs(0) * n_k

    def tile_copies(t):
        slot = lax.rem(t, MLP_RING)
        start = pl.multiple_of(lax.rem(t, n_k) * tf, tf)
        return (pltpu.make_async_copy(wu_hbm.at[:, pl.ds(start, tf)], ubuf.at[slot], sem.at[0, slot]),
                pltpu.make_async_copy(wd_hbm.at[pl.ds(start, tf), :], dbuf.at[slot], sem.at[1, slot]))

    def request(t):
        @pl.when(t < n_tiles)
        def _():
            for copy in tile_copies(t):
                copy.start()

    def wait(t):
        for copy in tile_copies(t):
            copy.wait()

    def up(t):
        a = jnp.maximum(_dot(h_ref[...], ubuf[lax.rem(t, MLP_RING)]), 0.0)
        return (a * a).astype(BF16)

    @pl.when(base == 0)
    def _():
        for t in range(MLP_RING):
            request(base + t)

    o_ref[...] = x1_ref[...]
    wait(base)
    act = up(base)
    for k in range(n_k):
        if k + 1 < n_k:
            wait(base + k + 1)
        o_ref[...] += _dot(act, dbuf[lax.rem(base + k, MLP_RING)])
        if k + 1 < n_k:
            act = up(base + k + 1)
        request(base + k + MLP_RING)


def _mlp(h2, x1, w_up, w_down, tm, tf):
    t, d = x1.shape
    f = w_up.shape[1]
    pipelined = _nbytes((tm, d), BF16) + 2 * _nbytes((tm, d), F32)
    resident = 2 * _nbytes((tm, tf), F32) + MLP_RING * (_nbytes((d, tf), BF16) + _nbytes((tf, d), BF16))
    return pl.pallas_call(
        functools.partial(_mlp_kernel, n_k=f // tf),
        grid=(t // tm,),
        in_specs=[
            pl.BlockSpec((tm, d), lambda i: (i, 0)),
            pl.BlockSpec((tm, d), lambda i: (i, 0)),
            pl.BlockSpec(memory_space=pl.ANY),
            pl.BlockSpec(memory_space=pl.ANY),
        ],
        out_specs=pl.BlockSpec((tm, d), lambda i: (i, 0)),
        out_shape=jax.ShapeDtypeStruct((t, d), F32),
        scratch_shapes=[pltpu.VMEM((MLP_RING, d, tf), BF16), pltpu.VMEM((MLP_RING, tf, d), BF16),
                        pltpu.SemaphoreType.DMA((2, MLP_RING))],
        compiler_params=pltpu.CompilerParams(
            dimension_semantics=("arbitrary",),
            vmem_limit_bytes=_vmem_limit(pipelined, resident)),
        name="mlp",
    )(h2, x1, w_up, w_down)


def _layer(x, mem, attn_norm_w, w_in, dn_conv_w, dn_a_log, dn_dt_bias, dn_out_norm_w, swa_q_norm_w,
           swa_k_norm_w, swa_sinks, rel_bias, mem_norm_w, w_mem_kv, xq_norm_w, xk_norm_w,
           p_dn, p_swa, p_mem, w_out, mlp_norm_w, w_mlp_up, w_mlp_down):
    b, s, d = x.shape
    m = mem.shape[1]
    t = b * s

    dn_heads = dn_a_log.shape[0]
    dn_dim = dn_out_norm_w.shape[0]
    dn_w = dn_heads * dn_dim
    swa_heads = swa_sinks.shape[0]
    swa_dim = swa_q_norm_w.shape[0]
    swa_w = swa_heads * swa_dim
    mem_dim = xq_norm_w.shape[0]
    mem_w = p_mem.shape[0]
    mem_heads = mem_w // mem_dim
    swa_kv_w = (w_in.shape[1] - 4 * dn_w - 2 * dn_heads - swa_w - mem_w - 3 * d) // 2
    swa_kv = swa_kv_w // swa_dim

    n_ba = 2 * dn_heads
    src_ba = 4 * dn_w
    src_swa = src_ba + n_ba
    w_t = w_in.T.astype(BF16)
    w_b = w_t[src_swa:]
    w_c = jnp.concatenate([w_t[src_ba:src_swa], jnp.zeros((LANES - n_ba, d), BF16)], axis=0)
    col = {"qkv": 0, "z": 3 * dn_w, "sq": src_ba}
    col["sk"] = col["sq"] + swa_w
    col["sv"] = col["sk"] + swa_kv_w
    col["mq"] = col["sv"] + swa_kv_w

    proj, gates, ba = _inproj(x.reshape(t, d), attn_norm_w.reshape(1, d), w_t, src_ba, w_b, w_c, 3 * d,
                              tm=1024, tn=1024)
    proj3 = proj.reshape(b, s, -1)
    mkv = _norm_matmul(mem.reshape(b * m, d), mem_norm_w.reshape(1, d), w_mem_kv.astype(BF16), tm=512, tn=512)
    mkv3 = mkv.reshape(b, m, -1)

    gate_params = jnp.zeros((SUBLANES, LANES), F32)
    gate_params = gate_params.at[0, dn_heads:2 * dn_heads].set(dn_a_log)
    gate_params = gate_params.at[1, dn_heads:2 * dn_heads].set(dn_dt_bias)
    o_dn, w_up_bf, w_down_bf = _deltanet(proj3, ba.reshape(b, s, LANES), dn_conv_w, gate_params,
                                         dn_out_norm_w.reshape(1, dn_dim), [w_mlp_up, w_mlp_down],
                                         dn_heads, dn_dim, col["z"])
    o_swa = _swa(proj3, rel_bias.reshape(-1), swa_sinks, swa_q_norm_w.reshape(1, swa_dim),
                 swa_k_norm_w.reshape(1, swa_dim), swa_heads, swa_kv, swa_dim, col["sq"], col["sk"], col["sv"])
    o_mem, w_out_bf, p_dn_bf, p_swa_bf, p_mem_bf = _memattn(
        proj3, mkv3, xq_norm_w.reshape(1, mem_dim), xk_norm_w.reshape(1, mem_dim),
        [w_out, p_dn, p_swa, p_mem], mem_heads, mem_dim, col["mq"], tq=1024)

    merged = _merge(o_dn.reshape(t, dn_w), o_swa.reshape(t, swa_w), o_mem.reshape(t, mem_w),
                    p_dn_bf, p_swa_bf, p_mem_bf, gates, tm=1024, tn=1024)
    x1, h2 = _outproj(x.reshape(t, d), merged, w_out_bf, mlp_norm_w.reshape(1, d), tm=512)
    out = _mlp(h2, x1, w_up_bf, w_down_bf, tm=512, tf=1024)
    return out.reshape(b, s, d)


def kernel(x, mem, attn_norm_w, w_in, dn_conv_w, dn_A_log, dn_dt_bias, dn_out_norm_w, swa_q_norm_w,
           swa_k_norm_w, swa_sinks, rel_bias, mem_norm_w, w_mem_kv, xq_norm_w, xk_norm_w, p_dn, p_swa,
           p_mem, w_out, mlp_norm_w, w_mlp_up, w_mlp_down):
    depth = w_in.shape[0]
    for l in range(depth):
        x = _layer(x, mem, attn_norm_w[l], w_in[l], dn_conv_w[l], dn_A_log[l], dn_dt_bias[l],
                   dn_out_norm_w[l], swa_q_norm_w[l], swa_k_norm_w[l], swa_sinks[l], rel_bias,
                   mem_norm_w[l], w_mem_kv[l], xq_norm_w[l], xk_norm_w[l], p_dn[l], p_swa[l], p_mem[l],
                   w_out[l], mlp_norm_w[l], w_mlp_up[l], w_mlp_down[l])
    return x
```

```python
import functools
import math

import jax
import jax.numpy as jnp
from jax import lax
from jax.experimental import pallas as pl
from jax.experimental.pallas import tpu as pltpu

F32 = jnp.float32
BF16 = jnp.bfloat16

EPS = 1e-6
LANES = 128
SUBLANES = 8
V7X_VMEM_BYTES = 64 * 1024 * 1024

DN_CHUNK = 64
DN_GROUP = 256
DN_BASE = 8
SWA_WINDOW = 128
SWA_BLOCK = 128
SWA_STEP_BLOCKS = 4
N_BUCKETS = 32
MAX_DISTANCE = 128

_NT = (((1,), (1,)), ((), ()))
_TN = (((0,), (0,)), ((), ()))


COMPILER_SCRATCH_BYTES = 4 * 1024 * 1024
VMEM_RESERVE_BYTES = 8 * 1024 * 1024


def _vmem_limit(pipelined_bytes, resident_bytes, claim_all=False):
    cap = V7X_VMEM_BYTES - VMEM_RESERVE_BYTES
    want = 2 * pipelined_bytes + resident_bytes + COMPILER_SCRATCH_BYTES
    return int(cap if claim_all else min(want, cap))


def _nbytes(shape, dtype):
    return math.prod(shape) * jnp.dtype(dtype).itemsize


def _sigmoid(v):
    return 0.5 * jnp.tanh(0.5 * v) + 0.5


def _silu(v):
    return v * _sigmoid(v)


def _rms(v, w):
    return (v * lax.rsqrt(jnp.mean(v * v, axis=-1, keepdims=True) + EPS)) * w


def _dot(a, b):
    return jnp.dot(a, b, preferred_element_type=F32)


def _dot_nt(a, b):
    return lax.dot_general(a, b, _NT, preferred_element_type=F32)


def _cast_rider_specs(weights, n_steps, step_of):
    in_specs, out_specs, out_shapes = [], [], []
    for w in weights:
        rows = w.shape[0] // n_steps
        assert rows * n_steps == w.shape[0] and rows % (2 * SUBLANES) == 0
        for specs in (in_specs, out_specs):
            specs.append(pl.BlockSpec((rows, w.shape[1]), lambda *ids: (step_of(*ids), 0)))
        out_shapes.append(jax.ShapeDtypeStruct(w.shape, BF16))
    return in_specs, out_specs, out_shapes


def _cast_riders(in_refs, out_refs):
    for i_ref, o_ref in zip(in_refs, out_refs):
        o_ref[...] = i_ref[...].astype(o_ref.dtype)


IN_RING = 4


def _inproj_kernel(x_ref, nw_ref, wa_hbm, wb_hbm, wc_ref, o_ref, g_ref, ba_ref, h_ref, wbuf, sem,
                   *, n_a, n_p, tail, rows_b1):
    i, j = pl.program_id(0), pl.program_id(1)
    n_j = pl.num_programs(1)
    tn = wbuf.shape[1]
    step = i * n_j + j
    n_steps = pl.num_programs(0) * n_j

    def tile_copy(col_step, slot, from_b):
        if from_b:
            row = jnp.where(col_step < n_p, (col_step - n_a) * tn, rows_b1 + (col_step - n_p) * tn)
            src = wb_hbm.at[pl.ds(pl.multiple_of(row, 2 * SUBLANES), tn), :]
        else:
            src = wa_hbm.at[pl.ds(pl.multiple_of(col_step * tn, tn), tn), :]
        return pltpu.make_async_copy(src, wbuf.at[slot], sem.at[slot])

    def request(ahead):
        target = step + ahead
        col_step = lax.rem(target, n_j)
        slot = lax.rem(target, IN_RING)

        @pl.when((target < n_steps) & (col_step < n_a))
        def _():
            tile_copy(col_step, slot, from_b=False).start()

        @pl.when((target < n_steps) & (col_step >= n_a))
        def _():
            tile_copy(col_step, slot, from_b=True).start()

    @pl.when(step == 0)
    def _():
        for ahead in range(IN_RING - 1):
            request(ahead)

    request(IN_RING - 1)
    slot = lax.rem(step, IN_RING)
    pltpu.make_async_copy(wb_hbm.at[pl.ds(0, tn), :], wbuf.at[slot], sem.at[slot]).wait()
    w_ref = wbuf.at[slot]

    @pl.when(j == 0)
    def _():
        h_ref[...] = _rms(x_ref[...], nw_ref[...]).astype(BF16)
        ba_ref[...] = _dot_nt(h_ref[...], wc_ref[...])

    @pl.when(j < n_p - 1)
    def _():
        o_ref[...] = _dot_nt(h_ref[...], w_ref[...])

    @pl.when(j == n_p - 1)
    def _():
        o_ref[:, :tail] = _dot_nt(h_ref[...], w_ref[:tail, :])

    @pl.when(j >= n_p)
    def _():
        g_ref[...] = _sigmoid(_dot_nt(h_ref[...], w_ref[...])).astype(g_ref.dtype)


def _inproj(x, nw, w_a, rows_a, w_b, w_c, n_gates, tm, tn):
    m, k = x.shape
    rows_b1 = w_b.shape[0] - n_gates
    n = rows_a + rows_b1
    n_a = rows_a // tn
    n_p = pl.cdiv(n, tn)
    assert rows_a % tn == 0 and n_gates % tn == 0 and rows_b1 % (2 * SUBLANES) == 0

    pipelined = (_nbytes((tm, k), F32) + _nbytes((tm, tn), F32) + _nbytes((tm, tn), BF16)
                 + _nbytes((tm, LANES), F32))
    resident = (_nbytes((tm, k), BF16) + _nbytes((tm, k), F32) + _nbytes((LANES, k), BF16)
                + IN_RING * _nbytes((tn, k), BF16))
    return pl.pallas_call(
        functools.partial(_inproj_kernel, n_a=n_a, n_p=n_p, tail=n - (n_p - 1) * tn, rows_b1=rows_b1),
        grid=(m // tm, n_p + n_gates // tn),
        in_specs=[
            pl.BlockSpec((tm, k), lambda i, j: (i, 0)),
            pl.BlockSpec((1, k), lambda i, j: (0, 0)),
            pl.BlockSpec(memory_space=pl.ANY),
            pl.BlockSpec(memory_space=pl.ANY),
            pl.BlockSpec((LANES, k), lambda i, j: (0, 0)),
        ],
        out_specs=[pl.BlockSpec((tm, tn), lambda i, j: (i, jnp.minimum(j, n_p - 1))),
                   pl.BlockSpec((tm, tn), lambda i, j: (i, jnp.maximum(j - n_p, 0))),
                   pl.BlockSpec((tm, LANES), lambda i, j: (i, 0))],
        out_shape=[jax.ShapeDtypeStruct((m, n), F32), jax.ShapeDtypeStruct((m, n_gates), BF16),
                   jax.ShapeDtypeStruct((m, LANES), F32)],
        scratch_shapes=[pltpu.VMEM((tm, k), BF16), pltpu.VMEM((IN_RING, tn, k), BF16),
                        pltpu.SemaphoreType.DMA((IN_RING,))],
        compiler_params=pltpu.CompilerParams(
            dimension_semantics=("arbitrary", "arbitrary"),
            vmem_limit_bytes=_vmem_limit(pipelined, resident)),
        name="inproj",
    )(x, nw, w_a, w_b, w_c)


def _norm_matmul_kernel(x_ref, nw_ref, w_ref, o_ref, h_ref):
    @pl.when(pl.program_id(1) == 0)
    def _():
        h_ref[...] = _rms(x_ref[...], nw_ref[...]).astype(BF16)

    o_ref[...] = _dot(h_ref[...], w_ref[...]).astype(o_ref.dtype)


def _norm_matmul(x, nw, w, tm, tn):
    m, k = x.shape
    n = w.shape[1]
    pipelined = _nbytes((tm, k), F32) + _nbytes((k, tn), BF16) + _nbytes((tm, tn), F32)
    resident = _nbytes((tm, k), BF16) + _nbytes((tm, k), F32)
    return pl.pallas_call(
        _norm_matmul_kernel,
        grid=(m // tm, n // tn),
        in_specs=[
            pl.BlockSpec((tm, k), lambda i, j: (i, 0)),
            pl.BlockSpec((1, k), lambda i, j: (0, 0)),
            pl.BlockSpec((k, tn), lambda i, j: (0, j)),
        ],
        out_specs=pl.BlockSpec((tm, tn), lambda i, j: (i, j)),
        out_shape=jax.ShapeDtypeStruct((m, n), F32),
        scratch_shapes=[pltpu.VMEM((tm, k), BF16)],
        compiler_params=pltpu.CompilerParams(
            dimension_semantics=("parallel", "arbitrary"),
            vmem_limit_bytes=_vmem_limit(pipelined, resident)),
        name="norm_matmul",
    )(x, nw, w)


def _dn_kernel(*refs, heads, dim, n_riders):
    qkv_ref, z_ref, ba_ref, cw_ref, gp_ref, onw_ref = refs[:6]
    o_ref = refs[6 + n_riders]
    ext_ref, state_ref = refs[-2:]
    _cast_riders(refs[6:6 + n_riders], refs[7 + n_riders:7 + 2 * n_riders])

    gt = DN_GROUP
    c = DN_CHUNK
    hd = heads * dim
    hs = range(heads)
    chunks = range(gt // c)

    @pl.when(pl.program_id(1) == 0)
    def _():
        ext_ref[0:SUBLANES, :] = jnp.zeros((SUBLANES, 3 * hd), F32)
        state_ref[...] = jnp.zeros_like(state_ref)

    ext_ref[SUBLANES:SUBLANES + gt, :] = qkv_ref[...]
    xe = ext_ref[...]
    conv = cw_ref[3:4, :] * xe[SUBLANES:]
    for s in (1, 2, 3):
        conv = conv + cw_ref[3 - s:4 - s, :] * pltpu.roll(xe, s, 0)[SUBLANES:]
    ext_ref[0:SUBLANES, :] = xe[gt:gt + SUBLANES]
    act = _silu(conv)

    ba = ba_ref[...]
    beta_all = _sigmoid(ba)
    xa = ba + gp_ref[1:2, :]
    softplus = jnp.maximum(xa, 0.0) + jnp.log1p(jnp.exp(-jnp.abs(xa)))
    g_all = -jnp.exp(gp_ref[0:1, :]) * softplus
    row_in_chunk = lax.broadcasted_iota(jnp.int32, (gt, LANES), 0) & (c - 1)
    gcum = g_all
    s = 1
    while s < c:
        gcum = gcum + jnp.where(row_in_chunk >= s, pltpu.roll(gcum, s, 0), 0.0)
        s *= 2
    exp_g = jnp.exp(gcum)
    kdec_parts, gc_rows = [], []
    for j in chunks:
        g_last = gcum[c * j + c - 1:c * j + c, :]
        kdec_parts.append(jnp.exp(g_last - gcum[c * j:c * (j + 1), :]))
        gc_rows.append(jnp.exp(g_last))
    kdec = jnp.concatenate(kdec_parts, axis=0)
    gcum_t = gcum.T

    ri = lax.broadcasted_iota(jnp.int32, (gt, gt), 0)
    ci = lax.broadcasted_iota(jnp.int32, (gt, gt), 1)
    same_chunk = (ri // c) == (ci // c)
    strict = same_chunk & (ri > ci)
    incl = same_chunk & (ri >= ci)

    qn, kn, kb, beta, eg = [], [], [], [], []
    for h in hs:
        qh = act[:, h * dim:(h + 1) * dim]
        kh = act[:, hd + h * dim:hd + (h + 1) * dim]
        qn.append(qh * lax.rsqrt(jnp.sum(qh * qh, axis=-1, keepdims=True) + EPS) * (dim ** -0.5))
        kn.append(kh * lax.rsqrt(jnp.sum(kh * kh, axis=-1, keepdims=True) + EPS))
        kb.append(kn[h].astype(BF16))
        beta.append(beta_all[:, h:h + 1])
        eg.append(exp_g[:, heads + h:heads + h + 1])

    decay = [jnp.exp(jnp.where(incl, gcum[:, heads + h:heads + h + 1] - gcum_t[heads + h:heads + h + 1, :],
                               -jnp.inf)) for h in hs]
    kk = [_dot_nt(kb[h], kb[h]) for h in hs]
    nmat = [jnp.where(strict, (beta[h] * kk[h]) * decay[h], 0.0) for h in hs]

    def same_block(size):
        return (ri // size) == (ci // size)

    eye = jnp.where(ri == ci, 1.0, 0.0)
    base = same_block(DN_BASE)
    pw = [jnp.where(base, nmat[h], 0.0) for h in hs]
    tinv = [eye - pw[h] for h in hs]
    pw = [pw[h].astype(BF16) for h in hs]
    order = 2
    while order < DN_BASE:
        pw = [_dot(pw[h], pw[h]).astype(BF16) for h in hs]
        tinv = [tinv[h] + _dot(tinv[h].astype(BF16), pw[h]) for h in hs]
        order *= 2
    size = DN_BASE
    while size < c:
        level = same_block(2 * size) & jnp.logical_not(same_block(size))
        off = [jnp.where(level, nmat[h], 0.0).astype(BF16) for h in hs]
        ct = [_dot(off[h], tinv[h].astype(BF16)).astype(BF16) for h in hs]
        tinv = [tinv[h] - _dot(tinv[h].astype(BF16), ct[h]) for h in hs]
        size *= 2

    sol = [jnp.concatenate([act[:, 2 * hd + h * dim:2 * hd + (h + 1) * dim] * beta[h],
                            kn[h] * (beta[h] * eg[h])], axis=1) for h in hs]
    sol = [sol[h] + _dot((tinv[h] - eye).astype(BF16), sol[h].astype(BF16)) for h in hs]

    pmat = [(_dot_nt(qn[h].astype(BF16), kb[h]) * decay[h]).astype(BF16) for h in hs]
    wq = [[jnp.concatenate([sol[h][c * j:c * (j + 1), dim:], (qn[h] * eg[h])[c * j:c * (j + 1)]],
                           axis=0).astype(BF16) for j in chunks] for h in hs]
    kd = [(kn[h] * kdec[:, heads + h:heads + h + 1]).astype(BF16) for h in hs]

    state = [state_ref[h] for h in hs]
    o_parts = [[] for _ in hs]
    for j in chunks:
        r = slice(c * j, c * (j + 1))
        ws = [_dot(wq[h][j], state[h].astype(BF16)) for h in hs]
        db = [(sol[h][r, :dim] - ws[h][:c]).astype(BF16) for h in hs]
        for h in hs:
            o_parts[h].append(ws[h][c:] + _dot(pmat[h][r, r], db[h]))
        state = [gc_rows[j][:, heads + h:heads + h + 1] * state[h]
                 + lax.dot_general(kd[h][r], db[h], _TN, preferred_element_type=F32) for h in hs]
    for h in hs:
        state_ref[h] = state[h]

    for h in hs:
        o = jnp.concatenate(o_parts[h], axis=0)
        zh = z_ref[:, h * dim:(h + 1) * dim]
        o_ref[:, h * dim:(h + 1) * dim] = (_rms(o, onw_ref[...]) * _silu(zh)).astype(o_ref.dtype)


def _deltanet(proj3, ba3, conv_w, gate_params, out_norm_w, riders, heads, dim, col_z):
    b, s, _ = proj3.shape
    hd = heads * dim
    gt = DN_GROUP
    ng = s // gt
    rider_in, rider_out, rider_shapes = _cast_rider_specs(riders, b * ng, lambda i, g: i * ng + g)
    pipelined = (_nbytes((gt, 3 * hd), F32) + _nbytes((gt, hd), F32) + _nbytes((gt, LANES), F32)
                 + _nbytes((gt, hd), BF16) + sum(_nbytes(w.shape, F32) * 3 // 2 for w in riders) // (b * ng))
    resident = (_nbytes((gt + SUBLANES, 3 * hd), F32) * 4 + _nbytes((heads, dim, dim), F32)
                + heads * 6 * _nbytes((gt, gt), F32))
    return pl.pallas_call(
        functools.partial(_dn_kernel, heads=heads, dim=dim, n_riders=len(riders)),
        grid=(b, ng),
        in_specs=[
            pl.BlockSpec((None, gt, 3 * hd), lambda i, g: (i, g, 0)),
            pl.BlockSpec((None, gt, hd), lambda i, g: (i, g, col_z // hd)),
            pl.BlockSpec((None, gt, LANES), lambda i, g: (i, g, 0)),
            pl.BlockSpec((4, 3 * hd), lambda i, g: (0, 0)),
            pl.BlockSpec((SUBLANES, LANES), lambda i, g: (0, 0)),
            pl.BlockSpec((1, dim), lambda i, g: (0, 0)),
            *rider_in,
        ],
        out_specs=[pl.BlockSpec((None, gt, hd), lambda i, g: (i, g, 0)), *rider_out],
        out_shape=[jax.ShapeDtypeStruct((b, s, hd), BF16), *rider_shapes],
        scratch_shapes=[pltpu.VMEM((gt + SUBLANES, 3 * hd), F32), pltpu.VMEM((heads, dim, dim), F32)],
        compiler_params=pltpu.CompilerParams(
            dimension_semantics=("parallel", "arbitrary"),
            vmem_limit_bytes=_vmem_limit(pipelined, resident, claim_all=True)),
        name="deltanet",
    )(proj3, proj3, ba3, conv_w, gate_params, out_norm_w, *riders)


def _swa_kernel(rb_ref, sink_ref, q_ref, kc_ref, kp_ref, vc_ref, vp_ref, qw_ref, kw_ref, o_ref, bias_ref,
                *, q_heads, kv_heads, dim):
    blk = SWA_BLOCK
    n = pl.program_id(1)
    qi = lax.broadcasted_iota(jnp.int32, (blk, 2 * blk), 0)
    kj = lax.broadcasted_iota(jnp.int32, (blk, 2 * blk), 1)
    dist = qi - kj + blk

    @pl.when((pl.program_id(0) == 0) & (n == 0))
    def _():
        max_exact = N_BUCKETS // 2
        nn = jnp.maximum(dist, 0)
        nf = jnp.maximum(nn, 1).astype(F32)
        large = max_exact + (jnp.log(nf / max_exact) / math.log(MAX_DISTANCE / max_exact)
                             * (N_BUCKETS - max_exact)).astype(jnp.int32)
        bucket = jnp.where(nn < max_exact, nn, jnp.minimum(large, N_BUCKETS - 1))
        for h in range(q_heads):
            acc = jnp.zeros((blk, 2 * blk), F32)
            for bk in range(N_BUCKETS):
                acc = jnp.where(bucket == bk, rb_ref[bk * q_heads + h], acc)
            acc = jnp.where((dist >= 0) & (dist < SWA_WINDOW), acc, -jnp.inf)
            bias_ref[h] = acc
            bias_ref[q_heads + h] = jnp.where(kj >= blk, acc, -jnp.inf)

    first = jnp.where(n == 0, q_heads, 0)
    group = q_heads // kv_heads
    for j in range(kv_heads):
        cols = slice(j * dim, (j + 1) * dim)
        k_all = _rms(jnp.concatenate([kp_ref[:, cols], kc_ref[:, cols]], axis=0), kw_ref[...]).astype(BF16)
        v_all = jnp.concatenate([vp_ref[:, cols], vc_ref[:, cols]], axis=0).astype(BF16)
        for u in range(SWA_STEP_BLOCKS):
            rows = slice(u * blk, (u + 1) * blk)
            kwin, vwin = k_all[u * blk:(u + 2) * blk], v_all[u * blk:(u + 2) * blk]
            qcat = jnp.concatenate(
                [_rms(q_ref[rows, (j * group + i) * dim:(j * group + i + 1) * dim], qw_ref[...])
                 for i in range(group)], axis=0).astype(BF16)
            logits = _dot_nt(qcat, kwin) * (dim ** -0.5)
            for i in range(group):
                h = j * group + i
                lg = logits[i * blk:(i + 1) * blk] + bias_ref[(first if u == 0 else 0) + h]
                sink = sink_ref[h]
                mx = jnp.maximum(jnp.max(lg, axis=-1, keepdims=True), sink)
                e = jnp.exp(lg - mx)
                den = jnp.sum(e, axis=-1, keepdims=True) + jnp.exp(sink - mx)
                pv = _dot(e.astype(BF16), vwin)
                o_ref[rows, h * dim:(h + 1) * dim] = (pv / den).astype(o_ref.dtype)


def _swa(proj3, rel_bias_flat, sinks, q_norm_w, k_norm_w, q_heads, kv_heads, dim, col_q, col_k, col_v):
    b, s, _ = proj3.shape
    blk = SWA_BLOCK
    step = SWA_STEP_BLOCKS * blk
    qw, kvw = q_heads * dim, kv_heads * dim
    smem = pl.BlockSpec(memory_space=pltpu.SMEM)
    pipelined = (_nbytes((step, qw), F32) + 2 * _nbytes((step + blk, kvw), F32) + _nbytes((step, qw), BF16))
    resident = (_nbytes((2 * q_heads, blk, 2 * blk), F32) + 2 * pipelined
                + 2 * SWA_STEP_BLOCKS * _nbytes((q_heads // kv_heads * blk, 2 * blk), F32))

    def prev(n):
        return jnp.maximum(SWA_STEP_BLOCKS * n - 1, 0)

    return pl.pallas_call(
        functools.partial(_swa_kernel, q_heads=q_heads, kv_heads=kv_heads, dim=dim),
        grid=(b, s // step),
        in_specs=[
            smem, smem,
            pl.BlockSpec((None, step, qw), lambda i, n: (i, n, col_q // qw)),
            pl.BlockSpec((None, step, kvw), lambda i, n: (i, n, col_k // kvw)),
            pl.BlockSpec((None, blk, kvw), lambda i, n: (i, prev(n), col_k // kvw)),
            pl.BlockSpec((None, step, kvw), lambda i, n: (i, n, col_v // kvw)),
            pl.BlockSpec((None, blk, kvw), lambda i, n: (i, prev(n), col_v // kvw)),
            pl.BlockSpec((1, dim), lambda i, n: (0, 0)),
            pl.BlockSpec((1, dim), lambda i, n: (0, 0)),
        ],
        out_specs=pl.BlockSpec((None, step, qw), lambda i, n: (i, n, 0)),
        out_shape=jax.ShapeDtypeStruct((b, s, qw), BF16),
        scratch_shapes=[pltpu.VMEM((2 * q_heads, blk, 2 * blk), F32)],
        compiler_params=pltpu.CompilerParams(
            dimension_semantics=("arbitrary", "arbitrary"),
            vmem_limit_bytes=_vmem_limit(pipelined, resident, claim_all=True)),
        name="swa",
    )(rel_bias_flat, sinks, proj3, proj3, proj3, proj3, proj3, q_norm_w, k_norm_w)


RIDER_RING = 3


def _cast_riders_ringed(hbm_refs, out_refs, bufs, sem, step, n_steps):
    def copies(target):
        slot = lax.rem(target, RIDER_RING)
        out = []
        for r, (hbm, buf) in enumerate(zip(hbm_refs, bufs)):
            rows = buf.shape[1]
            src = hbm.at[pl.ds(pl.multiple_of(target * rows, rows), rows), :]
            out.append(pltpu.make_async_copy(src, buf.at[slot], sem.at[r, slot]))
        return out

    def request(target):
        @pl.when(target < n_steps)
        def _():
            for copy in copies(target):
                copy.start()

    @pl.when(step == 0)
    def _():
        for ahead in range(RIDER_RING - 1):
            request(step + ahead)

    request(step + RIDER_RING - 1)
    for copy in copies(step):
        copy.wait()
    slot = lax.rem(step, RIDER_RING)
    for buf, out in zip(bufs, out_refs):
        out[...] = buf[slot].astype(out.dtype)


def _memattn_kernel(*refs, heads, dim, n_riders):
    q_ref, k_ref, v_ref, qw_ref, kw_ref = refs[:5]
    o_ref = refs[5 + n_riders]
    n_t = pl.num_programs(1)
    _cast_riders_ringed(refs[5:5 + n_riders], refs[6 + n_riders:6 + 2 * n_riders],
                        refs[6 + 2 * n_riders:6 + 3 * n_riders], refs[-1],
                        pl.program_id(0) * n_t + pl.program_id(1), pl.num_programs(0) * n_t)
    for h in range(heads):
        cols = slice(h * dim, (h + 1) * dim)
        qn = _rms(q_ref[:, cols], qw_ref[...]).astype(BF16)
        kn = _rms(k_ref[:, cols], kw_ref[...]).astype(BF16)
        lg = _dot_nt(qn, kn) * (dim ** -0.5)
        e = jnp.exp(lg - jnp.max(lg, axis=-1, keepdims=True))
        den = jnp.sum(e, axis=-1, keepdims=True)
        pv = _dot(e.astype(BF16), v_ref[:, cols].astype(BF16))
        o_ref[:, cols] = (pv / den).astype(o_ref.dtype)


def _memattn(proj3, mkv3, q_norm_w, k_norm_w, riders, heads, dim, col_q, tq):
    b, s, _ = proj3.shape
    m = mkv3.shape[1]
    w = heads * dim
    nt = s // tq
    _, rider_out, rider_shapes = _cast_rider_specs(riders, b * nt, lambda i, t: i * nt + t)
    slabs = [(r.shape[0] // (b * nt), r.shape[1]) for r in riders]
    pipelined = (_nbytes((tq, w), F32) + 2 * _nbytes((m, w), F32) + _nbytes((tq, w), BF16)
                 + sum(_nbytes(slab, BF16) for slab in slabs))
    ring_bytes = RIDER_RING * sum(_nbytes(slab, F32) for slab in slabs)
    return pl.pallas_call(
        functools.partial(_memattn_kernel, heads=heads, dim=dim, n_riders=len(riders)),
        grid=(b, nt),
        in_specs=[
            pl.BlockSpec((pl.Element(tq), pl.Element(w)), lambda i, t: (pl.multiple_of(i * s + t * tq, tq), col_q)),
            pl.BlockSpec((None, m, w), lambda i, t: (i, 0, 0)),
            pl.BlockSpec((None, m, w), lambda i, t: (i, 0, 1)),
            pl.BlockSpec((1, dim), lambda i, t: (0, 0)),
            pl.BlockSpec((1, dim), lambda i, t: (0, 0)),
            *([pl.BlockSpec(memory_space=pl.ANY)] * len(riders)),
        ],
        out_specs=[pl.BlockSpec((None, tq, w), lambda i, t: (i, t, 0)), *rider_out],
        out_shape=[jax.ShapeDtypeStruct((b, s, w), BF16), *rider_shapes],
        scratch_shapes=[*[pltpu.VMEM((RIDER_RING, *slab), F32) for slab in slabs],
                        pltpu.SemaphoreType.DMA((len(riders), RIDER_RING))],
        compiler_params=pltpu.CompilerParams(
            dimension_semantics=("arbitrary", "arbitrary"),
            vmem_limit_bytes=_vmem_limit(pipelined, 2 * heads * _nbytes((tq, m), F32) + ring_bytes,
                                         claim_all=True)),
        name="memattn",
    )(proj3.reshape(b * s, -1), mkv3, mkv3, q_norm_w, k_norm_w, *riders)


def _merge_kernel(od_ref, os_ref, om_ref, pd_ref, ps_ref, pm_ref, gd_ref, gs_ref, gm_ref, o_ref):
    merged = (gd_ref[...].astype(F32) * _dot(od_ref[...], pd_ref[...])
              + gs_ref[...].astype(F32) * _dot(os_ref[...], ps_ref[...])
              + gm_ref[...].astype(F32) * _dot(om_ref[...], pm_ref[...]))
    o_ref[...] = merged.astype(o_ref.dtype)


def _merge(o_dn, o_swa, o_mem, p_dn, p_swa, p_mem, gates, tm, tn):
    t = o_dn.shape[0]
    d = p_dn.shape[1]
    wd, ws, wm = o_dn.shape[1], o_swa.shape[1], o_mem.shape[1]
    per_branch = d // tn
    pipelined = ((_nbytes((tm, wd), BF16) + _nbytes((tm, ws), BF16) + _nbytes((tm, wm), BF16))
                 + (_nbytes((wd, tn), BF16) + _nbytes((ws, tn), BF16) + _nbytes((wm, tn), BF16))
                 + 4 * _nbytes((tm, tn), BF16))
    return pl.pallas_call(
        _merge_kernel,
        grid=(d // tn, t // tm),
        in_specs=[
            pl.BlockSpec((tm, wd), lambda j, i: (i, 0)),
            pl.BlockSpec((tm, ws), lambda j, i: (i, 0)),
            pl.BlockSpec((tm, wm), lambda j, i: (i, 0)),
            pl.BlockSpec((wd, tn), lambda j, i: (0, j)),
            pl.BlockSpec((ws, tn), lambda j, i: (0, j)),
            pl.BlockSpec((wm, tn), lambda j, i: (0, j)),
            pl.BlockSpec((tm, tn), lambda j, i: (i, j)),
            pl.BlockSpec((tm, tn), lambda j, i: (i, per_branch + j)),
            pl.BlockSpec((tm, tn), lambda j, i: (i, 2 * per_branch + j)),
        ],
        out_specs=pl.BlockSpec((tm, tn), lambda j, i: (i, j)),
        out_shape=jax.ShapeDtypeStruct((t, d), BF16),
        compiler_params=pltpu.CompilerParams(
            dimension_semantics=("parallel", "parallel"),
            vmem_limit_bytes=_vmem_limit(pipelined, 3 * _nbytes((tm, tn), F32))),
        name="merge",
    )(o_dn, o_swa, o_mem, p_dn, p_swa, p_mem, gates, gates, gates)


def _outproj_kernel(x_ref, m_ref, w_ref, nw_ref, x1_ref, h_ref):
    x1 = x_ref[...] + _dot(m_ref[...], w_ref[...])
    x1_ref[...] = x1
    h_ref[...] = _rms(x1, nw_ref[...]).astype(h_ref.dtype)


def _outproj(x, merged, w_out, norm_w, tm):
    t, d = x.shape
    pipelined = 2 * _nbytes((tm, d), F32) + 2 * _nbytes((tm, d), BF16) + _nbytes((d, d), BF16)
    return pl.pallas_call(
        _outproj_kernel,
        grid=(t // tm,),
        in_specs=[
            pl.BlockSpec((tm, d), lambda i: (i, 0)),
            pl.BlockSpec((tm, d), lambda i: (i, 0)),
            pl.BlockSpec((d, d), lambda i: (0, 0)),
            pl.BlockSpec((1, d), lambda i: (0, 0)),
        ],
        out_specs=[pl.BlockSpec((tm, d), lambda i: (i, 0)), pl.BlockSpec((tm, d), lambda i: (i, 0))],
        out_shape=[jax.ShapeDtypeStruct((t, d), F32), jax.ShapeDtypeStruct((t, d), BF16)],
        compiler_params=pltpu.CompilerParams(
            dimension_semantics=("parallel",),
            vmem_limit_bytes=_vmem_limit(pipelined, 2 * _nbytes((tm, d), F32))),
        name="outproj",
    )(x, merged, w_out, norm_w)


MLP_RING = 3


def _mlp_kernel(h_ref, x1_ref, wu_hbm, wd_hbm, o_ref, ubuf, dbuf, sem, *, n_k):
    tf = dbuf.shape[1]
    base = pl.program_id(0) * n_k
    n_tiles = pl.num_programs(0) * n_k

    def tile_copies(t):
        slot = lax.rem(t, MLP_RING)
        start = pl.multiple_of(lax.rem(t, n_k) * tf, tf)
        return (pltpu.make_async_copy(wu_hbm.at[:, pl.ds(start, tf)], ubuf.at[slot], sem.at[0, slot]),
                pltpu.make_async_copy(wd_hbm.at[pl.ds(start, tf), :], dbuf.at[slot], sem.at[1, slot]))

    def request(t):
        @pl.when(t < n_tiles)
        def _():
            for copy in tile_copies(t):
                copy.start()

    def wait(t):
        for copy in tile_copies(t):
            copy.wait()

    def up(t):
        a = jnp.maximum(_dot(h_ref[...], ubuf[lax.rem(t, MLP_RING)]), 0.0)
        return (a * a).astype(BF16)

    @pl.when(base == 0)
    def _():
        for t in range(MLP_RING):
            request(base + t)

    o_ref[...] = x1_ref[...]
    wait(base)
    act = up(base)
    for k in range(n_k):
        if k + 1 < n_k:
            wait(base + k + 1)
        o_ref[...] += _dot(act, dbuf[lax.rem(base + k, MLP_RING)])
        if k + 1 < n_k:
            act = up(base + k + 1)
        request(base + k + MLP_RING)


def _mlp(h2, x1, w_up, w_down, tm, tf):
    t, d = x1.shape
    f = w_up.shape[1]
    pipelined = _nbytes((tm, d), BF16) + 2 * _nbytes((tm, d), F32)
    resident = 2 * _nbytes((tm, tf), F32) + MLP_RING * (_nbytes((d, tf), BF16) + _nbytes((tf, d), BF16))
    return pl.pallas_call(
        functools.partial(_mlp_kernel, n_k=f // tf),
        grid=(t // tm,),
        in_specs=[
            pl.BlockSpec((tm, d), lambda i: (i, 0)),
            pl.BlockSpec((tm, d), lambda i: (i, 0)),
            pl.BlockSpec(memory_space=pl.ANY),
            pl.BlockSpec(memory_space=pl.ANY),
        ],
        out_specs=pl.BlockSpec((tm, d), lambda i: (i, 0)),
        out_shape=jax.ShapeDtypeStruct((t, d), F32),
        scratch_shapes=[pltpu.VMEM((MLP_RING, d, tf), BF16), pltpu.VMEM((MLP_RING, tf, d), BF16),
                        pltpu.SemaphoreType.DMA((2, MLP_RING))],
        compiler_params=pltpu.CompilerParams(
            dimension_semantics=("arbitrary",),
            vmem_limit_bytes=_vmem_limit(pipelined, resident)),
        name="mlp",
    )(h2, x1, w_up, w_down)


def _layer(x, mem, attn_norm_w, w_in, dn_conv_w, dn_a_log, dn_dt_bias, dn_out_norm_w, swa_q_norm_w,
           swa_k_norm_w, swa_sinks, rel_bias, mem_norm_w, w_mem_kv, xq_norm_w, xk_norm_w,
           p_dn, p_swa, p_mem, w_out, mlp_norm_w, w_mlp_up, w_mlp_down):
    b, s, d = x.shape
    m = mem.shape[1]
    t = b * s

    dn_heads = dn_a_log.shape[0]
    dn_dim = dn_out_norm_w.shape[0]
    dn_w = dn_heads * dn_dim
    swa_heads = swa_sinks.shape[0]
    swa_dim = swa_q_norm_w.shape[0]
    swa_w = swa_heads * swa_dim
    mem_dim = xq_norm_w.shape[0]
    mem_w = p_mem.shape[0]
    mem_heads = mem_w // mem_dim
    swa_kv_w = (w_in.shape[1] - 4 * dn_w - 2 * dn_heads - swa_w - mem_w - 3 * d) // 2
    swa_kv = swa_kv_w // swa_dim

    n_ba = 2 * dn_heads
    src_ba = 4 * dn_w
    src_swa = src_ba + n_ba
    w_t = w_in.T.astype(BF16)
    w_b = w_t[src_swa:]
    w_c = jnp.concatenate([w_t[src_ba:src_swa], jnp.zeros((LANES - n_ba, d), BF16)], axis=0)
    col = {"qkv": 0, "z": 3 * dn_w, "sq": src_ba}
    col["sk"] = col["sq"] + swa_w
    col["sv"] = col["sk"] + swa_kv_w
    col["mq"] = col["sv"] + swa_kv_w

    proj, gates, ba = _inproj(x.reshape(t, d), attn_norm_w.reshape(1, d), w_t, src_ba, w_b, w_c, 3 * d,
                              tm=1024, tn=1024)
    proj3 = proj.reshape(b, s, -1)
    mkv = _norm_matmul(mem.reshape(b * m, d), mem_norm_w.reshape(1, d), w_mem_kv.astype(BF16), tm=512, tn=512)
    mkv3 = mkv.reshape(b, m, -1)

    gate_params = jnp.zeros((SUBLANES, LANES), F32)
    gate_params = gate_params.at[0, dn_heads:2 * dn_heads].set(dn_a_log)
    gate_params = gate_params.at[1, dn_heads:2 * dn_heads].set(dn_dt_bias)
    o_dn, w_up_bf, w_down_bf = _deltanet(proj3, ba.reshape(b, s, LANES), dn_conv_w, gate_params,
                                         dn_out_norm_w.reshape(1, dn_dim), [w_mlp_up, w_mlp_down],
                                         dn_heads, dn_dim, col["z"])
    o_swa = _swa(proj3, rel_bias.reshape(-1), swa_sinks, swa_q_norm_w.reshape(1, swa_dim),
                 swa_k_norm_w.reshape(1, swa_dim), swa_heads, swa_kv, swa_dim, col["sq"], col["sk"], col["sv"])
    o_mem, w_out_bf, p_dn_bf, p_swa_bf, p_mem_bf = _memattn(
        proj3, mkv3, xq_norm_w.reshape(1, mem_dim), xk_norm_w.reshape(1, mem_dim),
        [w_out, p_dn, p_swa, p_mem], mem_heads, mem_dim, col["mq"], tq=1024)

    merged = _merge(o_dn.reshape(t, dn_w), o_swa.reshape(t, swa_w), o_mem.reshape(t, mem_w),
                    p_dn_bf, p_swa_bf, p_mem_bf, gates, tm=1024, tn=1024)
    x1, h2 = _outproj(x.reshape(t, d), merged, w_out_bf, mlp_norm_w.reshape(1, d), tm=512)
    out = _mlp(h2, x1, w_up_bf, w_down_bf, tm=512, tf=1024)
    return out.reshape(b, s, d)


def kernel(x, mem, attn_norm_w, w_in, dn_conv_w, dn_A_log, dn_dt_bias, dn_out_norm_w, swa_q_norm_w,
           swa_k_norm_w, swa_sinks, rel_bias, mem_norm_w, w_mem_kv, xq_norm_w, xk_norm_w, p_dn, p_swa,
           p_mem, w_out, mlp_norm_w, w_mlp_up, w_mlp_down):
    depth = w_in.shape[0]
    for l in range(depth):
        x = _layer(x, mem, attn_norm_w[l], w_in[l], dn_conv_w[l], dn_A_log[l], dn_dt_bias[l],
                   dn_out_norm_w[l], swa_q_norm_w[l], swa_k_norm_w[l], swa_sinks[l], rel_bias,
                   mem_norm_w[l], w_mem_kv[l], xq_norm_w[l], xk_norm_w[l], p_dn[l], p_swa[l], p_mem[l],
                   w_out[l], mlp_norm_w[l], w_mlp_up[l], w_mlp_down[l])
    return x
```

```python
import functools
import math

import jax
import jax.numpy as jnp
from jax import lax
from jax.experimental import pallas as pl
from jax.experimental.pallas import tpu as pltpu

F32 = jnp.float32
BF16 = jnp.bfloat16

EPS = 1e-6
LANES = 128
SUBLANES = 8
V7X_VMEM_BYTES = 64 * 1024 * 1024

DN_CHUNK = 64
DN_GROUP = 256
DN_BASE = 8
SWA_WINDOW = 128
SWA_BLOCK = 128
SWA_STEP_BLOCKS = 4
N_BUCKETS = 32
MAX_DISTANCE = 128

_NT = (((1,), (1,)), ((), ()))
_TN = (((0,), (0,)), ((), ()))


COMPILER_SCRATCH_BYTES = 4 * 1024 * 1024
VMEM_RESERVE_BYTES = 8 * 1024 * 1024


def _vmem_limit(pipelined_bytes, resident_bytes, claim_all=False):
    cap = V7X_VMEM_BYTES - VMEM_RESERVE_BYTES
    want = 2 * pipelined_bytes + resident_bytes + COMPILER_SCRATCH_BYTES
    return int(cap if claim_all else min(want, cap))


def _nbytes(shape, dtype):
    return math.prod(shape) * jnp.dtype(dtype).itemsize


def _sigmoid(v):
    return 0.5 * jnp.tanh(0.5 * v) + 0.5


def _silu(v):
    return v * _sigmoid(v)


def _rms(v, w):
    return (v * lax.rsqrt(jnp.mean(v * v, axis=-1, keepdims=True) + EPS)) * w


def _dot(a, b):
    return jnp.dot(a, b, preferred_element_type=F32)


def _dot_nt(a, b):
    return lax.dot_general(a, b, _NT, preferred_element_type=F32)


def _cast_rider_specs(weights, n_steps, step_of):
    in_specs, out_specs, out_shapes = [], [], []
    for w in weights:
        rows = w.shape[0] // n_steps
        assert rows * n_steps == w.shape[0] and rows % (2 * SUBLANES) == 0
        for specs in (in_specs, out_specs):
            specs.append(pl.BlockSpec((rows, w.shape[1]), lambda *ids: (step_of(*ids), 0)))
        out_shapes.append(jax.ShapeDtypeStruct(w.shape, BF16))
    return in_specs, out_specs, out_shapes


def _cast_riders(in_refs, out_refs):
    for i_ref, o_ref in zip(in_refs, out_refs):
        o_ref[...] = i_ref[...].astype(o_ref.dtype)


IN_RING = 4


def _inproj_kernel(x_ref, nw_ref, wa_hbm, wb_hbm, wc_ref, o_ref, g_ref, ba_ref, h_ref, wbuf, sem,
                   *, n_a, n_p, tail, rows_b1):
    i, j = pl.program_id(0), pl.program_id(1)
    n_j = pl.num_programs(1)
    tn = wbuf.shape[1]
    step = i * n_j + j
    n_steps = pl.num_programs(0) * n_j

    def tile_copy(col_step, slot, from_b):
        if from_b:
            row = jnp.where(col_step < n_p, (col_step - n_a) * tn, rows_b1 + (col_step - n_p) * tn)
            src = wb_hbm.at[pl.ds(pl.multiple_of(row, 2 * SUBLANES), tn), :]
        else:
            src = wa_hbm.at[pl.ds(pl.multiple_of(col_step * tn, tn), tn), :]
        return pltpu.make_async_copy(src, wbuf.at[slot], sem.at[slot])

    def request(ahead):
        target = step + ahead
        col_step = lax.rem(target, n_j)
        slot = lax.rem(target, IN_RING)

        @pl.when((target < n_steps) & (col_step < n_a))
        def _():
            tile_copy(col_step, slot, from_b=False).start()

        @pl.when((target < n_steps) & (col_step >= n_a))
        def _():
            tile_copy(col_step, slot, from_b=True).start()

    @pl.when(step == 0)
    def _():
        for ahead in range(IN_RING - 1):
            request(ahead)

    request(IN_RING - 1)
    slot = lax.rem(step, IN_RING)
    pltpu.make_async_copy(wb_hbm.at[pl.ds(0, tn), :], wbuf.at[slot], sem.at[slot]).wait()
    w_ref = wbuf.at[slot]

    @pl.when(j == 0)
    def _():
        h_ref[...] = _rms(x_ref[...], nw_ref[...]).astype(BF16)
        ba_ref[...] = _dot_nt(h_ref[...], wc_ref[...])

    @pl.when(j < n_p - 1)
    def _():
        o_ref[...] = _dot_nt(h_ref[...], w_ref[...])

    @pl.when(j == n_p - 1)
    def _():
        o_ref[:, :tail] = _dot_nt(h_ref[...], w_ref[:tail, :])

    @pl.when(j >= n_p)
    def _():
        g_ref[...] = _sigmoid(_dot_nt(h_ref[...], w_ref[...])).astype(g_ref.dtype)


def _inproj(x, nw, w_a, rows_a, w_b, w_c, n_gates, tm, tn):
    m, k = x.shape
    rows_b1 = w_b.shape[0] - n_gates
    n = rows_a + rows_b1
    n_a = rows_a // tn
    n_p = pl.cdiv(n, tn)
    assert rows_a % tn == 0 and n_gates % tn == 0 and rows_b1 % (2 * SUBLANES) == 0

    pipelined = (_nbytes((tm, k), F32) + _nbytes((tm, tn), F32) + _nbytes((tm, tn), BF16)
                 + _nbytes((tm, LANES), F32))
    resident = (_nbytes((tm, k), BF16) + _nbytes((tm, k), F32) + _nbytes((LANES, k), BF16)
                + IN_RING * _nbytes((tn, k), BF16))
    return pl.pallas_call(
        functools.partial(_inproj_kernel, n_a=n_a, n_p=n_p, tail=n - (n_p - 1) * tn, rows_b1=rows_b1),
        grid=(m // tm, n_p + n_gates // tn),
        in_specs=[
            pl.BlockSpec((tm, k), lambda i, j: (i, 0)),
            pl.BlockSpec((1, k), lambda i, j: (0, 0)),
            pl.BlockSpec(memory_space=pl.ANY),
            pl.BlockSpec(memory_space=pl.ANY),
            pl.BlockSpec((LANES, k), lambda i, j: (0, 0)),
        ],
        out_specs=[pl.BlockSpec((tm, tn), lambda i, j: (i, jnp.minimum(j, n_p - 1))),
                   pl.BlockSpec((tm, tn), lambda i, j: (i, jnp.maximum(j - n_p, 0))),
                   pl.BlockSpec((tm, LANES), lambda i, j: (i, 0))],
        out_shape=[jax.ShapeDtypeStruct((m, n), F32), jax.ShapeDtypeStruct((m, n_gates), BF16),
                   jax.ShapeDtypeStruct((m, LANES), F32)],
        scratch_shapes=[pltpu.VMEM((tm, k), BF16), pltpu.VMEM((IN_RING, tn, k), BF16),
                        pltpu.SemaphoreType.DMA((IN_RING,))],
        compiler_params=pltpu.CompilerParams(
            dimension_semantics=("arbitrary", "arbitrary"),
            vmem_limit_bytes=_vmem_limit(pipelined, resident)),
        name="inproj",
    )(x, nw, w_a, w_b, w_c)


def _norm_matmul_kernel(x_ref, nw_ref, w_ref, o_ref, h_ref):
    @pl.when(pl.program_id(1) == 0)
    def _():
        h_ref[...] = _rms(x_ref[...], nw_ref[...]).astype(BF16)

    o_ref[...] = _dot(h_ref[...], w_ref[...]).astype(o_ref.dtype)


def _norm_matmul(x, nw, w, tm, tn):
    m, k = x.shape
    n = w.shape[1]
    pipelined = _nbytes((tm, k), F32) + _nbytes((k, tn), BF16) + _nbytes((tm, tn), F32)
    resident = _nbytes((tm, k), BF16) + _nbytes((tm, k), F32)
    return pl.pallas_call(
        _norm_matmul_kernel,
        grid=(m // tm, n // tn),
        in_specs=[
            pl.BlockSpec((tm, k), lambda i, j: (i, 0)),
            pl.BlockSpec((1, k), lambda i, j: (0, 0)),
            pl.BlockSpec((k, tn), lambda i, j: (0, j)),
        ],
        out_specs=pl.BlockSpec((tm, tn), lambda i, j: (i, j)),
        out_shape=jax.ShapeDtypeStruct((m, n), F32),
        scratch_shapes=[pltpu.VMEM((tm, k), BF16)],
        compiler_params=pltpu.CompilerParams(
            dimension_semantics=("parallel", "arbitrary"),
            vmem_limit_bytes=_vmem_limit(pipelined, resident)),
        name="norm_matmul",
    )(x, nw, w)


def _dn_kernel(*refs, heads, dim, n_riders):
    qkv_ref, z_ref, ba_ref, cw_ref, gp_ref, onw_ref = refs[:6]
    o_ref = refs[6 + n_riders]
    ext_ref, state_ref = refs[-2:]
    _cast_riders(refs[6:6 + n_riders], refs[7 + n_riders:7 + 2 * n_riders])

    gt = DN_GROUP
    c = DN_CHUNK
    hd = heads * dim
    hs = range(heads)
    chunks = range(gt // c)

    @pl.when(pl.program_id(1) == 0)
    def _():
        ext_ref[0:SUBLANES, :] = jnp.zeros((SUBLANES, 3 * hd), F32)
        state_ref[...] = jnp.zeros_like(state_ref)

    ext_ref[SUBLANES:SUBLANES + gt, :] = qkv_ref[...]
    xe = ext_ref[...]
    conv = cw_ref[3:4, :] * xe[SUBLANES:]
    for s in (1, 2, 3):
        conv = conv + cw_ref[3 - s:4 - s, :] * pltpu.roll(xe, s, 0)[SUBLANES:]
    ext_ref[0:SUBLANES, :] = xe[gt:gt + SUBLANES]
    act = _silu(conv)

    ba = ba_ref[...]
    beta_all = _sigmoid(ba)
    xa = ba + gp_ref[1:2, :]
    softplus = jnp.maximum(xa, 0.0) + jnp.log1p(jnp.exp(-jnp.abs(xa)))
    g_all = -jnp.exp(gp_ref[0:1, :]) * softplus
    row_in_chunk = lax.broadcasted_iota(jnp.int32, (gt, LANES), 0) & (c - 1)
    gcum = g_all
    s = 1
    while s < c:
        gcum = gcum + jnp.where(row_in_chunk >= s, pltpu.roll(gcum, s, 0), 0.0)
        s *= 2
    exp_g = jnp.exp(gcum)
    kdec_parts, gc_rows = [], []
    for j in chunks:
        g_last = gcum[c * j + c - 1:c * j + c, :]
        kdec_parts.append(jnp.exp(g_last - gcum[c * j:c * (j + 1), :]))
        gc_rows.append(jnp.exp(g_last))
    kdec = jnp.concatenate(kdec_parts, axis=0)
    gcum_t = gcum.T

    ri = lax.broadcasted_iota(jnp.int32, (gt, gt), 0)
    ci = lax.broadcasted_iota(jnp.int32, (gt, gt), 1)
    same_chunk = (ri // c) == (ci // c)
    strict = same_chunk & (ri > ci)
    incl = same_chunk & (ri >= ci)

    qn, kn, kb, beta, eg = [], [], [], [], []
    for h in hs:
        qh = act[:, h * dim:(h + 1) * dim]
        kh = act[:, hd + h * dim:hd + (h + 1) * dim]
        qn.append(qh * lax.rsqrt(jnp.sum(qh * qh, axis=-1, keepdims=True) + EPS) * (dim ** -0.5))
        kn.append(kh * lax.rsqrt(jnp.sum(kh * kh, axis=-1, keepdims=True) + EPS))
        kb.append(kn[h].astype(BF16))
        beta.append(beta_all[:, h:h + 1])
        eg.append(exp_g[:, heads + h:heads + h + 1])

    decay = [jnp.exp(jnp.where(incl, gcum[:, heads + h:heads + h + 1] - gcum_t[heads + h:heads + h + 1, :],
                               -jnp.inf)) for h in hs]
    kk = [_dot_nt(kb[h], kb[h]) for h in hs]
    nmat = [jnp.where(strict, (beta[h] * kk[h]) * decay[h], 0.0) for h in hs]

    def same_block(size):
        return (ri // size) == (ci // size)

    eye = jnp.where(ri == ci, 1.0, 0.0)
    base = same_block(DN_BASE)
    pw = [jnp.where(base, nmat[h], 0.0) for h in hs]
    tinv = [eye - pw[h] for h in hs]
    pw = [pw[h].astype(BF16) for h in hs]
    order = 2
    while order < DN_BASE:
        pw = [_dot(pw[h], pw[h]).astype(BF16) for h in hs]
        tinv = [tinv[h] + _dot(tinv[h].astype(BF16), pw[h]) for h in hs]
        order *= 2
    size = DN_BASE
    while size < c:
        level = same_block(2 * size) & jnp.logical_not(same_block(size))
        off = [jnp.where(level, nmat[h], 0.0).astype(BF16) for h in hs]
        ct = [_dot(off[h], tinv[h].astype(BF16)).astype(BF16) for h in hs]
        tinv = [tinv[h] - _dot(tinv[h].astype(BF16), ct[h]) for h in hs]
        size *= 2

    sol = [jnp.concatenate([act[:, 2 * hd + h * dim:2 * hd + (h + 1) * dim] * beta[h],
                            kn[h] * (beta[h] * eg[h])], axis=1) for h in hs]
    sol = [sol[h] + _dot((tinv[h] - eye).astype(BF16), sol[h].astype(BF16)) for h in hs]

    pmat = [(_dot_nt(qn[h].astype(BF16), kb[h]) * decay[h]).astype(BF16) for h in hs]
    wq = [[jnp.concatenate([sol[h][c * j:c * (j + 1), dim:], (qn[h] * eg[h])[c * j:c * (j + 1)]],
                           axis=0).astype(BF16) for j in chunks] for h in hs]
    kd = [(kn[h] * kdec[:, heads + h:heads + h + 1]).astype(BF16) for h in hs]

    state = [state_ref[h] for h in hs]
    o_parts = [[] for _ in hs]
    for j in chunks:
        r = slice(c * j, c * (j + 1))
        ws = [_dot(wq[h][j], state[h].astype(BF16)) for h in hs]
        db = [(sol[h][r, :dim] - ws[h][:c]).astype(BF16) for h in hs]
        for h in hs:
            o_parts[h].append(ws[h][c:] + _dot(pmat[h][r, r], db[h]))
        state = [gc_rows[j][:, heads + h:heads + h + 1] * state[h]
                 + lax.dot_general(kd[h][r], db[h], _TN, preferred_element_type=F32) for h in hs]
    for h in hs:
        state_ref[h] = state[h]

    for h in hs:
        o = jnp.concatenate(o_parts[h], axis=0)
        zh = z_ref[:, h * dim:(h + 1) * dim]
        o_ref[:, h * dim:(h + 1) * dim] = (_rms(o, onw_ref[...]) * _silu(zh)).astype(o_ref.dtype)


def _deltanet(proj3, ba3, conv_w, gate_params, out_norm_w, riders, heads, dim, col_z):
    b, s, _ = proj3.shape
    hd = heads * dim
    gt = DN_GROUP
    ng = s // gt
    rider_in, rider_out, rider_shapes = _cast_rider_specs(riders, b * ng, lambda i, g: i * ng + g)
    pipelined = (_nbytes((gt, 3 * hd), F32) + _nbytes((gt, hd), F32) + _nbytes((gt, LANES), F32)
                 + _nbytes((gt, hd), BF16) + sum(_nbytes(w.shape, F32) * 3 // 2 for w in riders) // (b * ng))
    resident = (_nbytes((gt + SUBLANES, 3 * hd), F32) * 4 + _nbytes((heads, dim, dim), F32)
                + heads * 6 * _nbytes((gt, gt), F32))
    return pl.pallas_call(
        functools.partial(_dn_kernel, heads=heads, dim=dim, n_riders=len(riders)),
        grid=(b, ng),
        in_specs=[
            pl.BlockSpec((None, gt, 3 * hd), lambda i, g: (i, g, 0)),
            pl.BlockSpec((None, gt, hd), lambda i, g: (i, g, col_z // hd)),
            pl.BlockSpec((None, gt, LANES), lambda i, g: (i, g, 0)),
            pl.BlockSpec((4, 3 * hd), lambda i, g: (0, 0)),
            pl.BlockSpec((SUBLANES, LANES), lambda i, g: (0, 0)),
            pl.BlockSpec((1, dim), lambda i, g: (0, 0)),
            *rider_in,
        ],
        out_specs=[pl.BlockSpec((None, gt, hd), lambda i, g: (i, g, 0)), *rider_out],
        out_shape=[jax.ShapeDtypeStruct((b, s, hd), BF16), *rider_shapes],
        scratch_shapes=[pltpu.VMEM((gt + SUBLANES, 3 * hd), F32), pltpu.VMEM((heads, dim, dim), F32)],
        compiler_params=pltpu.CompilerParams(
            dimension_semantics=("parallel", "arbitrary"),
            vmem_limit_bytes=_vmem_limit(pipelined, resident, claim_all=True)),
        name="deltanet",
    )(proj3, proj3, ba3, conv_w, gate_params, out_norm_w, *riders)


def _swa_kernel(rb_ref, sink_ref, q_ref, kc_ref, kp_ref, vc_ref, vp_ref, qw_ref, kw_ref, o_ref, bias_ref,
                *, q_heads, kv_heads, dim):
    blk = SWA_BLOCK
    n = pl.program_id(1)
    qi = lax.broadcasted_iota(jnp.int32, (blk, 2 * blk), 0)
    kj = lax.broadcasted_iota(jnp.int32, (blk, 2 * blk), 1)
    dist = qi - kj + blk

    @pl.when((pl.program_id(0) == 0) & (n == 0))
    def _():
        max_exact = N_BUCKETS // 2
        nn = jnp.maximum(dist, 0)
        nf = jnp.maximum(nn, 1).astype(F32)
        large = max_exact + (jnp.log(nf / max_exact) / math.log(MAX_DISTANCE / max_exact)
                             * (N_BUCKETS - max_exact)).astype(jnp.int32)
        bucket = jnp.where(nn < max_exact, nn, jnp.minimum(large, N_BUCKETS - 1))
        for h in range(q_heads):
            acc = jnp.zeros((blk, 2 * blk), F32)
            for bk in range(N_BUCKETS):
                acc = jnp.where(bucket == bk, rb_ref[bk * q_heads + h], acc)
            acc = jnp.where((dist >= 0) & (dist < SWA_WINDOW), acc, -jnp.inf)
            bias_ref[h] = acc
            bias_ref[q_heads + h] = jnp.where(kj >= blk, acc, -jnp.inf)

    first = jnp.where(n == 0, q_heads, 0)
    group = q_heads // kv_heads
    for j in range(kv_heads):
        cols = slice(j * dim, (j + 1) * dim)
        k_all = _rms(jnp.concatenate([kp_ref[:, cols], kc_ref[:, cols]], axis=0), kw_ref[...]).astype(BF16)
        v_all = jnp.concatenate([vp_ref[:, cols], vc_ref[:, cols]], axis=0).astype(BF16)
        for u in range(SWA_STEP_BLOCKS):
            rows = slice(u * blk, (u + 1) * blk)
            kwin, vwin = k_all[u * blk:(u + 2) * blk], v_all[u * blk:(u + 2) * blk]
            qcat = jnp.concatenate(
                [_rms(q_ref[rows, (j * group + i) * dim:(j * group + i + 1) * dim], qw_ref[...])
                 for i in range(group)], axis=0).astype(BF16)
            logits = _dot_nt(qcat, kwin) * (dim ** -0.5)
            for i in range(group):
                h = j * group + i
                lg = logits[i * blk:(i + 1) * blk] + bias_ref[(first if u == 0 else 0) + h]
                sink = sink_ref[h]
                mx = jnp.maximum(jnp.max(lg, axis=-1, keepdims=True), sink)
                e = jnp.exp(lg - mx)
                den = jnp.sum(e, axis=-1, keepdims=True) + jnp.exp(sink - mx)
                pv = _dot(e.astype(BF16), vwin)
                o_ref[rows, h * dim:(h + 1) * dim] = (pv / den).astype(o_ref.dtype)


def _swa(proj3, rel_bias_flat, sinks, q_norm_w, k_norm_w, q_heads, kv_heads, dim, col_q, col_k, col_v):
    b, s, _ = proj3.shape
    blk = SWA_BLOCK
    step = SWA_STEP_BLOCKS * blk
    qw, kvw = q_heads * dim, kv_heads * dim
    smem = pl.BlockSpec(memory_space=pltpu.SMEM)
    pipelined = (_nbytes((step, qw), F32) + 2 * _nbytes((step + blk, kvw), F32) + _nbytes((step, qw), BF16))
    resident = (_nbytes((2 * q_heads, blk, 2 * blk), F32) + 2 * pipelined
                + 2 * SWA_STEP_BLOCKS * _nbytes((q_heads // kv_heads * blk, 2 * blk), F32))

    def prev(n):
        return jnp.maximum(SWA_STEP_BLOCKS * n - 1, 0)

    return pl.pallas_call(
        functools.partial(_swa_kernel, q_heads=q_heads, kv_heads=kv_heads, dim=dim),
        grid=(b, s // step),
        in_specs=[
            smem, smem,
            pl.BlockSpec((None, step, qw), lambda i, n: (i, n, col_q // qw)),
            pl.BlockSpec((None, step, kvw), lambda i, n: (i, n, col_k // kvw)),
            pl.BlockSpec((None, blk, kvw), lambda i, n: (i, prev(n), col_k // kvw)),
            pl.BlockSpec((None, step, kvw), lambda i, n: (i, n, col_v // kvw)),
            pl.BlockSpec((None, blk, kvw), lambda i, n: (i, prev(n), col_v // kvw)),
            pl.BlockSpec((1, dim), lambda i, n: (0, 0)),
            pl.BlockSpec((1, dim), lambda i, n: (0, 0)),
        ],
        out_specs=pl.BlockSpec((None, step, qw), lambda i, n: (i, n, 0)),
        out_shape=jax.ShapeDtypeStruct((b, s, qw), BF16),
        scratch_shapes=[pltpu.VMEM((2 * q_heads, blk, 2 * blk), F32)],
        compiler_params=pltpu.CompilerParams(
            dimension_semantics=("arbitrary", "arbitrary"),
            vmem_limit_bytes=_vmem_limit(pipelined, resident, claim_all=True)),
        name="swa",
    )(rel_bias_flat, sinks, proj3, proj3, proj3, proj3, proj3, q_norm_w, k_norm_w)


ROW_RING = 3


def _row_ring(hbm_refs, bufs, sem, step, n_steps):
    def copies(target):
        slot = lax.rem(target, ROW_RING)
        out = []
        for r, (hbm, buf) in enumerate(zip(hbm_refs, bufs)):
            rows = buf.shape[1]
            src = hbm.at[pl.ds(pl.multiple_of(target * rows, rows), rows), :]
            out.append(pltpu.make_async_copy(src, buf.at[slot], sem.at[r, slot]))
        return out

    def request(target):
        @pl.when(target < n_steps)
        def _():
            for copy in copies(target):
                copy.start()

    @pl.when(step == 0)
    def _():
        for ahead in range(ROW_RING - 1):
            request(step + ahead)

    request(step + ROW_RING - 1)
    for copy in copies(step):
        copy.wait()
    return lax.rem(step, ROW_RING)


def _cast_riders_ringed(hbm_refs, out_refs, bufs, sem, step, n_steps):
    slot = _row_ring(hbm_refs, bufs, sem, step, n_steps)
    for buf, out in zip(bufs, out_refs):
        out[...] = buf[slot].astype(out.dtype)


def _memattn_kernel(*refs, heads, dim, n_riders):
    q_ref, k_ref, v_ref, qw_ref, kw_ref = refs[:5]
    o_ref = refs[5 + n_riders]
    n_t = pl.num_programs(1)
    _cast_riders_ringed(refs[5:5 + n_riders], refs[6 + n_riders:6 + 2 * n_riders],
                        refs[6 + 2 * n_riders:6 + 3 * n_riders], refs[-1],
                        pl.program_id(0) * n_t + pl.program_id(1), pl.num_programs(0) * n_t)
    for h in range(heads):
        cols = slice(h * dim, (h + 1) * dim)
        qn = _rms(q_ref[:, cols], qw_ref[...]).astype(BF16)
        kn = _rms(k_ref[:, cols], kw_ref[...]).astype(BF16)
        lg = _dot_nt(qn, kn) * (dim ** -0.5)
        e = jnp.exp(lg - jnp.max(lg, axis=-1, keepdims=True))
        den = jnp.sum(e, axis=-1, keepdims=True)
        pv = _dot(e.astype(BF16), v_ref[:, cols].astype(BF16))
        o_ref[:, cols] = (pv / den).astype(o_ref.dtype)


def _memattn(proj3, mkv3, q_norm_w, k_norm_w, riders, heads, dim, col_q, tq):
    b, s, _ = proj3.shape
    m = mkv3.shape[1]
    w = heads * dim
    nt = s // tq
    _, rider_out, rider_shapes = _cast_rider_specs(riders, b * nt, lambda i, t: i * nt + t)
    slabs = [(r.shape[0] // (b * nt), r.shape[1]) for r in riders]
    pipelined = (_nbytes((tq, w), F32) + 2 * _nbytes((m, w), F32) + _nbytes((tq, w), BF16)
                 + sum(_nbytes(slab, BF16) for slab in slabs))
    ring_bytes = ROW_RING * sum(_nbytes(slab, F32) for slab in slabs)
    return pl.pallas_call(
        functools.partial(_memattn_kernel, heads=heads, dim=dim, n_riders=len(riders)),
        grid=(b, nt),
        in_specs=[
            pl.BlockSpec((pl.Element(tq), pl.Element(w)), lambda i, t: (pl.multiple_of(i * s + t * tq, tq), col_q)),
            pl.BlockSpec((None, m, w), lambda i, t: (i, 0, 0)),
            pl.BlockSpec((None, m, w), lambda i, t: (i, 0, 1)),
            pl.BlockSpec((1, dim), lambda i, t: (0, 0)),
            pl.BlockSpec((1, dim), lambda i, t: (0, 0)),
            *([pl.BlockSpec(memory_space=pl.ANY)] * len(riders)),
        ],
        out_specs=[pl.BlockSpec((None, tq, w), lambda i, t: (i, t, 0)), *rider_out],
        out_shape=[jax.ShapeDtypeStruct((b, s, w), BF16), *rider_shapes],
        scratch_shapes=[*[pltpu.VMEM((ROW_RING, *slab), F32) for slab in slabs],
                        pltpu.SemaphoreType.DMA((len(riders), ROW_RING))],
        compiler_params=pltpu.CompilerParams(
            dimension_semantics=("arbitrary", "arbitrary"),
            vmem_limit_bytes=_vmem_limit(pipelined, 2 * heads * _nbytes((tq, m), F32) + ring_bytes,
                                         claim_all=True)),
        name="memattn",
    )(proj3.reshape(b * s, -1), mkv3, mkv3, q_norm_w, k_norm_w, *riders)


def _merge_kernel(od_ref, os_ref, om_ref, pd_ref, ps_ref, pm_ref, gd_ref, gs_ref, gm_ref, o_ref):
    merged = (gd_ref[...].astype(F32) * _dot(od_ref[...], pd_ref[...])
              + gs_ref[...].astype(F32) * _dot(os_ref[...], ps_ref[...])
              + gm_ref[...].astype(F32) * _dot(om_ref[...], pm_ref[...]))
    o_ref[...] = merged.astype(o_ref.dtype)


def _merge(o_dn, o_swa, o_mem, p_dn, p_swa, p_mem, gates, tm, tn):
    t = o_dn.shape[0]
    d = p_dn.shape[1]
    wd, ws, wm = o_dn.shape[1], o_swa.shape[1], o_mem.shape[1]
    per_branch = d // tn
    pipelined = ((_nbytes((tm, wd), BF16) + _nbytes((tm, ws), BF16) + _nbytes((tm, wm), BF16))
                 + (_nbytes((wd, tn), BF16) + _nbytes((ws, tn), BF16) + _nbytes((wm, tn), BF16))
                 + 4 * _nbytes((tm, tn), BF16))
    return pl.pallas_call(
        _merge_kernel,
        grid=(d // tn, t // tm),
        in_specs=[
            pl.BlockSpec((tm, wd), lambda j, i: (i, 0)),
            pl.BlockSpec((tm, ws), lambda j, i: (i, 0)),
            pl.BlockSpec((tm, wm), lambda j, i: (i, 0)),
            pl.BlockSpec((wd, tn), lambda j, i: (0, j)),
            pl.BlockSpec((ws, tn), lambda j, i: (0, j)),
            pl.BlockSpec((wm, tn), lambda j, i: (0, j)),
            pl.BlockSpec((tm, tn), lambda j, i: (i, j)),
            pl.BlockSpec((tm, tn), lambda j, i: (i, per_branch + j)),
            pl.BlockSpec((tm, tn), lambda j, i: (i, 2 * per_branch + j)),
        ],
        out_specs=pl.BlockSpec((tm, tn), lambda j, i: (i, j)),
        out_shape=jax.ShapeDtypeStruct((t, d), BF16),
        compiler_params=pltpu.CompilerParams(
            dimension_semantics=("parallel", "parallel"),
            vmem_limit_bytes=_vmem_limit(pipelined, 3 * _nbytes((tm, tn), F32))),
        name="merge",
    )(o_dn, o_swa, o_mem, p_dn, p_swa, p_mem, gates, gates, gates)


def _outproj_kernel(x_hbm, m_hbm, w_ref, nw_ref, x1_ref, h_ref, xbuf, mbuf, sem):
    slot = _row_ring((x_hbm, m_hbm), (xbuf, mbuf), sem, pl.program_id(0), pl.num_programs(0))
    x1 = xbuf[slot] + _dot(mbuf[slot], w_ref[...])
    x1_ref[...] = x1
    h_ref[...] = _rms(x1, nw_ref[...]).astype(h_ref.dtype)


def _outproj(x, merged, w_out, norm_w, tm):
    t, d = x.shape
    pipelined = _nbytes((tm, d), F32) + _nbytes((tm, d), BF16) + _nbytes((d, d), BF16)
    resident = 2 * _nbytes((tm, d), F32) + ROW_RING * (_nbytes((tm, d), F32) + _nbytes((tm, d), BF16))
    raw = pl.BlockSpec(memory_space=pl.ANY)
    return pl.pallas_call(
        _outproj_kernel,
        grid=(t // tm,),
        in_specs=[
            raw,
            raw,
            pl.BlockSpec((d, d), lambda i: (0, 0)),
            pl.BlockSpec((1, d), lambda i: (0, 0)),
        ],
        out_specs=[pl.BlockSpec((tm, d), lambda i: (i, 0)), pl.BlockSpec((tm, d), lambda i: (i, 0))],
        out_shape=[jax.ShapeDtypeStruct((t, d), F32), jax.ShapeDtypeStruct((t, d), BF16)],
        scratch_shapes=[pltpu.VMEM((ROW_RING, tm, d), F32), pltpu.VMEM((ROW_RING, tm, d), BF16),
                        pltpu.SemaphoreType.DMA((2, ROW_RING))],
        compiler_params=pltpu.CompilerParams(
            dimension_semantics=("arbitrary",),
            vmem_limit_bytes=_vmem_limit(pipelined, resident)),
        name="outproj",
    )(x, merged, w_out, norm_w)


MLP_RING = 3


def _mlp_kernel(h_ref, x1_ref, wu_hbm, wd_hbm, o_ref, ubuf, dbuf, sem, *, n_k):
    tf = dbuf.shape[1]
    base = pl.program_id(0) * n_k
    n_tiles = pl.num_programs(0) * n_k

    def tile_copies(t):
        slot = lax.rem(t, MLP_RING)
        start = pl.multiple_of(lax.rem(t, n_k) * tf, tf)
        return (pltpu.make_async_copy(wu_hbm.at[:, pl.ds(start, tf)], ubuf.at[slot], sem.at[0, slot]),
                pltpu.make_async_copy(wd_hbm.at[pl.ds(start, tf), :], dbuf.at[slot], sem.at[1, slot]))

    def request(t):
        @pl.when(t < n_tiles)
        def _():
            for copy in tile_copies(t):
                copy.start()

    def wait(t):
        for copy in tile_copies(t):
            copy.wait()

    def up(t):
        a = jnp.maximum(_dot(h_ref[...], ubuf[lax.rem(t, MLP_RING)]), 0.0)
        return (a * a).astype(BF16)

    @pl.when(base == 0)
    def _():
        for t in range(MLP_RING):
            request(base + t)

    o_ref[...] = x1_ref[...]
    wait(base)
    act = up(base)
    for k in range(n_k):
        if k + 1 < n_k:
            wait(base + k + 1)
        o_ref[...] += _dot(act, dbuf[lax.rem(base + k, MLP_RING)])
        if k + 1 < n_k:
            act = up(base + k + 1)
        request(base + k + MLP_RING)


def _mlp(h2, x1, w_up, w_down, tm, tf):
    t, d = x1.shape
    f = w_up.shape[1]
    pipelined = _nbytes((tm, d), BF16) + 2 * _nbytes((tm, d), F32)
    resident = 2 * _nbytes((tm, tf), F32) + MLP_RING * (_nbytes((d, tf), BF16) + _nbytes((tf, d), BF16))
    return pl.pallas_call(
        functools.partial(_mlp_kernel, n_k=f // tf),
        grid=(t // tm,),
        in_specs=[
            pl.BlockSpec((tm, d), lambda i: (i, 0)),
            pl.BlockSpec((tm, d), lambda i: (i, 0)),
            pl.BlockSpec(memory_space=pl.ANY),
            pl.BlockSpec(memory_space=pl.ANY),
        ],
        out_specs=pl.BlockSpec((tm, d), lambda i: (i, 0)),
        out_shape=jax.ShapeDtypeStruct((t, d), F32),
        scratch_shapes=[pltpu.VMEM((MLP_RING, d, tf), BF16), pltpu.VMEM((MLP_RING, tf, d), BF16),
                        pltpu.SemaphoreType.DMA((2, MLP_RING))],
        compiler_params=pltpu.CompilerParams(
            dimension_semantics=("arbitrary",),
            vmem_limit_bytes=_vmem_limit(pipelined, resident)),
        name="mlp",
    )(h2, x1, w_up, w_down)


def _layer(x, mem, attn_norm_w, w_in, dn_conv_w, dn_a_log, dn_dt_bias, dn_out_norm_w, swa_q_norm_w,
           swa_k_norm_w, swa_sinks, rel_bias, mem_norm_w, w_mem_kv, xq_norm_w, xk_norm_w,
           p_dn, p_swa, p_mem, w_out, mlp_norm_w, w_mlp_up, w_mlp_down):
    b, s, d = x.shape
    m = mem.shape[1]
    t = b * s

    dn_heads = dn_a_log.shape[0]
    dn_dim = dn_out_norm_w.shape[0]
    dn_w = dn_heads * dn_dim
    swa_heads = swa_sinks.shape[0]
    swa_dim = swa_q_norm_w.shape[0]
    swa_w = swa_heads * swa_dim
    mem_dim = xq_norm_w.shape[0]
    mem_w = p_mem.shape[0]
    mem_heads = mem_w // mem_dim
    swa_kv_w = (w_in.shape[1] - 4 * dn_w - 2 * dn_heads - swa_w - mem_w - 3 * d) // 2
    swa_kv = swa_kv_w // swa_dim

    n_ba = 2 * dn_heads
    src_ba = 4 * dn_w
    src_swa = src_ba + n_ba
    w_t = w_in.T.astype(BF16)
    w_b = w_t[src_swa:]
    w_c = jnp.concatenate([w_t[src_ba:src_swa], jnp.zeros((LANES - n_ba, d), BF16)], axis=0)
    col = {"qkv": 0, "z": 3 * dn_w, "sq": src_ba}
    col["sk"] = col["sq"] + swa_w
    col["sv"] = col["sk"] + swa_kv_w
    col["mq"] = col["sv"] + swa_kv_w

    proj, gates, ba = _inproj(x.reshape(t, d), attn_norm_w.reshape(1, d), w_t, src_ba, w_b, w_c, 3 * d,
                              tm=1024, tn=1024)
    proj3 = proj.reshape(b, s, -1)
    mkv = _norm_matmul(mem.reshape(b * m, d), mem_norm_w.reshape(1, d), w_mem_kv.astype(BF16), tm=512, tn=512)
    mkv3 = mkv.reshape(b, m, -1)

    gate_params = jnp.zeros((SUBLANES, LANES), F32)
    gate_params = gate_params.at[0, dn_heads:2 * dn_heads].set(dn_a_log)
    gate_params = gate_params.at[1, dn_heads:2 * dn_heads].set(dn_dt_bias)
    o_dn, w_up_bf, w_down_bf = _deltanet(proj3, ba.reshape(b, s, LANES), dn_conv_w, gate_params,
                                         dn_out_norm_w.reshape(1, dn_dim), [w_mlp_up, w_mlp_down],
                                         dn_heads, dn_dim, col["z"])
    o_swa = _swa(proj3, rel_bias.reshape(-1), swa_sinks, swa_q_norm_w.reshape(1, swa_dim),
                 swa_k_norm_w.reshape(1, swa_dim), swa_heads, swa_kv, swa_dim, col["sq"], col["sk"], col["sv"])
    o_mem, w_out_bf, p_dn_bf, p_swa_bf, p_mem_bf = _memattn(
        proj3, mkv3, xq_norm_w.reshape(1, mem_dim), xk_norm_w.reshape(1, mem_dim),
        [w_out, p_dn, p_swa, p_mem], mem_heads, mem_dim, col["mq"], tq=1024)

    merged = _merge(o_dn.reshape(t, dn_w), o_swa.reshape(t, swa_w), o_mem.reshape(t, mem_w),
                    p_dn_bf, p_swa_bf, p_mem_bf, gates, tm=1024, tn=1024)
    x1, h2 = _outproj(x.reshape(t, d), merged, w_out_bf, mlp_norm_w.reshape(1, d), tm=512)
    out = _mlp(h2, x1, w_up_bf, w_down_bf, tm=512, tf=1024)
    return out.reshape(b, s, d)


def kernel(x, mem, attn_norm_w, w_in, dn_conv_w, dn_A_log, dn_dt_bias, dn_out_norm_w, swa_q_norm_w,
           swa_k_norm_w, swa_sinks, rel_bias, mem_norm_w, w_mem_kv, xq_norm_w, xk_norm_w, p_dn, p_swa,
           p_mem, w_out, mlp_norm_w, w_mlp_up, w_mlp_down):
    depth = w_in.shape[0]
    for l in range(depth):
        x = _layer(x, mem, attn_norm_w[l], w_in[l], dn_conv_w[l], dn_A_log[l], dn_dt_bias[l],
                   dn_out_norm_w[l], swa_q_norm_w[l], swa_k_norm_w[l], swa_sinks[l], rel_bias,
                   mem_norm_w[l], w_mem_kv[l], xq_norm_w[l], xk_norm_w[l], p_dn[l], p_swa[l], p_mem[l],
                   w_out[l], mlp_norm_w[l], w_mlp_up[l], w_mlp_down[l])
    return x
```

```python
import functools
import math

import jax
import jax.numpy as jnp
from jax import lax
from jax.experimental import pallas as pl
from jax.experimental.pallas import tpu as pltpu

F32 = jnp.float32
BF16 = jnp.bfloat16

EPS = 1e-6
LANES = 128
SUBLANES = 8
V7X_VMEM_BYTES = 64 * 1024 * 1024

DN_CHUNK = 64
DN_GROUP = 256
DN_BASE = 8
SWA_WINDOW = 128
SWA_BLOCK = 128
SWA_STEP_BLOCKS = 4
N_BUCKETS = 32
MAX_DISTANCE = 128

_NT = (((1,), (1,)), ((), ()))
_TN = (((0,), (0,)), ((), ()))


COMPILER_SCRATCH_BYTES = 4 * 1024 * 1024
VMEM_RESERVE_BYTES = 8 * 1024 * 1024


def _vmem_limit(pipelined_bytes, resident_bytes, claim_all=False):
    cap = V7X_VMEM_BYTES - VMEM_RESERVE_BYTES
    want = 2 * pipelined_bytes + resident_bytes + COMPILER_SCRATCH_BYTES
    return int(cap if claim_all else min(want, cap))


def _nbytes(shape, dtype):
    return math.prod(shape) * jnp.dtype(dtype).itemsize


def _sigmoid(v):
    return 0.5 * jnp.tanh(0.5 * v) + 0.5


def _silu(v):
    return v * _sigmoid(v)


def _rms(v, w):
    return (v * lax.rsqrt(jnp.mean(v * v, axis=-1, keepdims=True) + EPS)) * w


def _dot(a, b):
    return jnp.dot(a, b, preferred_element_type=F32)


def _dot_nt(a, b):
    return lax.dot_general(a, b, _NT, preferred_element_type=F32)


def _cast_rider_specs(weights, n_steps, step_of):
    in_specs, out_specs, out_shapes = [], [], []
    for w in weights:
        rows = w.shape[0] // n_steps
        assert rows * n_steps == w.shape[0] and rows % (2 * SUBLANES) == 0
        for specs in (in_specs, out_specs):
            specs.append(pl.BlockSpec((rows, w.shape[1]), lambda *ids: (step_of(*ids), 0)))
        out_shapes.append(jax.ShapeDtypeStruct(w.shape, BF16))
    return in_specs, out_specs, out_shapes


def _cast_riders(in_refs, out_refs):
    for i_ref, o_ref in zip(in_refs, out_refs):
        o_ref[...] = i_ref[...].astype(o_ref.dtype)


IN_RING = 4


def _inproj_kernel(x_ref, nw_ref, wa_hbm, wb_hbm, wc_ref, o_ref, g_ref, ba_ref, h_ref, wbuf, sem,
                   *, n_a, n_p, tail, rows_b1):
    i, j = pl.program_id(0), pl.program_id(1)
    n_j = pl.num_programs(1)
    tn = wbuf.shape[1]
    step = i * n_j + j
    n_steps = pl.num_programs(0) * n_j

    def tile_copy(col_step, slot, from_b):
        if from_b:
            row = jnp.where(col_step < n_p, (col_step - n_a) * tn, rows_b1 + (col_step - n_p) * tn)
            src = wb_hbm.at[pl.ds(pl.multiple_of(row, 2 * SUBLANES), tn), :]
        else:
            src = wa_hbm.at[pl.ds(pl.multiple_of(col_step * tn, tn), tn), :]
        return pltpu.make_async_copy(src, wbuf.at[slot], sem.at[slot])

    def request(ahead):
        target = step + ahead
        col_step = lax.rem(target, n_j)
        slot = lax.rem(target, IN_RING)

        @pl.when((target < n_steps) & (col_step < n_a))
        def _():
            tile_copy(col_step, slot, from_b=False).start()

        @pl.when((target < n_steps) & (col_step >= n_a))
        def _():
            tile_copy(col_step, slot, from_b=True).start()

    @pl.when(step == 0)
    def _():
        for ahead in range(IN_RING - 1):
            request(ahead)

    request(IN_RING - 1)
    slot = lax.rem(step, IN_RING)
    pltpu.make_async_copy(wb_hbm.at[pl.ds(0, tn), :], wbuf.at[slot], sem.at[slot]).wait()
    w_ref = wbuf.at[slot]

    @pl.when(j == 0)
    def _():
        h_ref[...] = _rms(x_ref[...], nw_ref[...]).astype(BF16)
        ba_ref[...] = _dot_nt(h_ref[...], wc_ref[...])

    @pl.when(j < n_p - 1)
    def _():
        o_ref[...] = _dot_nt(h_ref[...], w_ref[...])

    @pl.when(j == n_p - 1)
    def _():
        o_ref[:, :tail] = _dot_nt(h_ref[...], w_ref[:tail, :])

    @pl.when(j >= n_p)
    def _():
        g_ref[...] = _sigmoid(_dot_nt(h_ref[...], w_ref[...])).astype(g_ref.dtype)


def _inproj(x, nw, w_a, rows_a, w_b, w_c, n_gates, tm, tn):
    m, k = x.shape
    rows_b1 = w_b.shape[0] - n_gates
    n = rows_a + rows_b1
    n_a = rows_a // tn
    n_p = pl.cdiv(n, tn)
    assert rows_a % tn == 0 and n_gates % tn == 0 and rows_b1 % (2 * SUBLANES) == 0

    pipelined = (_nbytes((tm, k), F32) + _nbytes((tm, tn), F32) + _nbytes((tm, tn), BF16)
                 + _nbytes((tm, LANES), F32))
    resident = (_nbytes((tm, k), BF16) + _nbytes((tm, k), F32) + _nbytes((LANES, k), BF16)
                + IN_RING * _nbytes((tn, k), BF16))
    return pl.pallas_call(
        functools.partial(_inproj_kernel, n_a=n_a, n_p=n_p, tail=n - (n_p - 1) * tn, rows_b1=rows_b1),
        grid=(m // tm, n_p + n_gates // tn),
        in_specs=[
            pl.BlockSpec((tm, k), lambda i, j: (i, 0)),
            pl.BlockSpec((1, k), lambda i, j: (0, 0)),
            pl.BlockSpec(memory_space=pl.ANY),
            pl.BlockSpec(memory_space=pl.ANY),
            pl.BlockSpec((LANES, k), lambda i, j: (0, 0)),
        ],
        out_specs=[pl.BlockSpec((tm, tn), lambda i, j: (i, jnp.minimum(j, n_p - 1))),
                   pl.BlockSpec((tm, tn), lambda i, j: (i, jnp.maximum(j - n_p, 0))),
                   pl.BlockSpec((tm, LANES), lambda i, j: (i, 0))],
        out_shape=[jax.ShapeDtypeStruct((m, n), F32), jax.ShapeDtypeStruct((m, n_gates), BF16),
                   jax.ShapeDtypeStruct((m, LANES), F32)],
        scratch_shapes=[pltpu.VMEM((tm, k), BF16), pltpu.VMEM((IN_RING, tn, k), BF16),
                        pltpu.SemaphoreType.DMA((IN_RING,))],
        compiler_params=pltpu.CompilerParams(
            dimension_semantics=("arbitrary", "arbitrary"),
            vmem_limit_bytes=_vmem_limit(pipelined, resident)),
        name="inproj",
    )(x, nw, w_a, w_b, w_c)


def _norm_matmul_kernel(x_ref, nw_ref, w_ref, o_ref, h_ref):
    @pl.when(pl.program_id(1) == 0)
    def _():
        h_ref[...] = _rms(x_ref[...], nw_ref[...]).astype(BF16)

    o_ref[...] = _dot(h_ref[...], w_ref[...]).astype(o_ref.dtype)


def _norm_matmul(x, nw, w, tm, tn):
    m, k = x.shape
    n = w.shape[1]
    pipelined = _nbytes((tm, k), F32) + _nbytes((k, tn), BF16) + _nbytes((tm, tn), F32)
    resident = _nbytes((tm, k), BF16) + _nbytes((tm, k), F32)
    return pl.pallas_call(
        _norm_matmul_kernel,
        grid=(m // tm, n // tn),
        in_specs=[
            pl.BlockSpec((tm, k), lambda i, j: (i, 0)),
            pl.BlockSpec((1, k), lambda i, j: (0, 0)),
            pl.BlockSpec((k, tn), lambda i, j: (0, j)),
        ],
        out_specs=pl.BlockSpec((tm, tn), lambda i, j: (i, j)),
        out_shape=jax.ShapeDtypeStruct((m, n), F32),
        scratch_shapes=[pltpu.VMEM((tm, k), BF16)],
        compiler_params=pltpu.CompilerParams(
            dimension_semantics=("parallel", "arbitrary"),
            vmem_limit_bytes=_vmem_limit(pipelined, resident)),
        name="norm_matmul",
    )(x, nw, w)


def _dn_kernel(*refs, heads, dim, n_riders):
    qkv_ref, z_ref, ba_ref, cw_ref, gp_ref, onw_ref = refs[:6]
    o_ref = refs[6 + n_riders]
    ext_ref, state_ref = refs[-2:]
    _cast_riders(refs[6:6 + n_riders], refs[7 + n_riders:7 + 2 * n_riders])

    gt = DN_GROUP
    c = DN_CHUNK
    hd = heads * dim
    hs = range(heads)
    chunks = range(gt // c)

    @pl.when(pl.program_id(1) == 0)
    def _():
        ext_ref[0:SUBLANES, :] = jnp.zeros((SUBLANES, 3 * hd), F32)
        state_ref[...] = jnp.zeros_like(state_ref)

    ext_ref[SUBLANES:SUBLANES + gt, :] = qkv_ref[...]
    xe = ext_ref[...]
    conv = cw_ref[3:4, :] * xe[SUBLANES:]
    for s in (1, 2, 3):
        conv = conv + cw_ref[3 - s:4 - s, :] * pltpu.roll(xe, s, 0)[SUBLANES:]
    ext_ref[0:SUBLANES, :] = xe[gt:gt + SUBLANES]
    act = _silu(conv)

    ba = ba_ref[...]
    beta_all = _sigmoid(ba)
    xa = ba + gp_ref[1:2, :]
    softplus = jnp.maximum(xa, 0.0) + jnp.log1p(jnp.exp(-jnp.abs(xa)))
    g_all = -jnp.exp(gp_ref[0:1, :]) * softplus
    row_in_chunk = lax.broadcasted_iota(jnp.int32, (gt, LANES), 0) & (c - 1)
    gcum = g_all
    s = 1
    while s < c:
        gcum = gcum + jnp.where(row_in_chunk >= s, pltpu.roll(gcum, s, 0), 0.0)
        s *= 2
    exp_g = jnp.exp(gcum)
    kdec_parts, gc_rows = [], []
    for j in chunks:
        g_last = gcum[c * j + c - 1:c * j + c, :]
        kdec_parts.append(jnp.exp(g_last - gcum[c * j:c * (j + 1), :]))
        gc_rows.append(jnp.exp(g_last))
    kdec = jnp.concatenate(kdec_parts, axis=0)
    gcum_t = gcum.T

    ri = lax.broadcasted_iota(jnp.int32, (gt, gt), 0)
    ci = lax.broadcasted_iota(jnp.int32, (gt, gt), 1)
    same_chunk = (ri // c) == (ci // c)
    strict = same_chunk & (ri > ci)
    incl = same_chunk & (ri >= ci)

    qn, kn, kb, beta, eg = [], [], [], [], []
    for h in hs:
        qh = act[:, h * dim:(h + 1) * dim]
        kh = act[:, hd + h * dim:hd + (h + 1) * dim]
        qn.append(qh * lax.rsqrt(jnp.sum(qh * qh, axis=-1, keepdims=True) + EPS) * (dim ** -0.5))
        kn.append(kh * lax.rsqrt(jnp.sum(kh * kh, axis=-1, keepdims=True) + EPS))
        kb.append(kn[h].astype(BF16))
        beta.append(beta_all[:, h:h + 1])
        eg.append(exp_g[:, heads + h:heads + h + 1])

    decay = [jnp.exp(jnp.where(incl, gcum[:, heads + h:heads + h + 1] - gcum_t[heads + h:heads + h + 1, :],
                               -jnp.inf)) for h in hs]
    kk = [_dot_nt(kb[h], kb[h]) for h in hs]
    nmat = [jnp.where(strict, (beta[h] * kk[h]) * decay[h], 0.0) for h in hs]

    def same_block(size):
        return (ri // size) == (ci // size)

    eye = jnp.where(ri == ci, 1.0, 0.0)
    base = same_block(DN_BASE)
    pw = [jnp.where(base, nmat[h], 0.0) for h in hs]
    tinv = [eye - pw[h] for h in hs]
    pw = [pw[h].astype(BF16) for h in hs]
    order = 2
    while order < DN_BASE:
        pw = [_dot(pw[h], pw[h]).astype(BF16) for h in hs]
        tinv = [tinv[h] + _dot(tinv[h].astype(BF16), pw[h]) for h in hs]
        order *= 2
    size = DN_BASE
    while size < c:
        level = same_block(2 * size) & jnp.logical_not(same_block(size))
        off = [jnp.where(level, nmat[h], 0.0).astype(BF16) for h in hs]
        ct = [_dot(off[h], tinv[h].astype(BF16)).astype(BF16) for h in hs]
        tinv = [tinv[h] - _dot(tinv[h].astype(BF16), ct[h]) for h in hs]
        size *= 2

    sol = [jnp.concatenate([act[:, 2 * hd + h * dim:2 * hd + (h + 1) * dim] * beta[h],
                            kn[h] * (beta[h] * eg[h])], axis=1) for h in hs]
    sol = [sol[h] + _dot((tinv[h] - eye).astype(BF16), sol[h].astype(BF16)) for h in hs]

    pmat = [(_dot_nt(qn[h].astype(BF16), kb[h]) * decay[h]).astype(BF16) for h in hs]
    wq = [[jnp.concatenate([sol[h][c * j:c * (j + 1), dim:], (qn[h] * eg[h])[c * j:c * (j + 1)]],
                           axis=0).astype(BF16) for j in chunks] for h in hs]
    kd = [(kn[h] * kdec[:, heads + h:heads + h + 1]).astype(BF16) for h in hs]

    state = [state_ref[h] for h in hs]
    o_parts = [[] for _ in hs]
    for j in chunks:
        r = slice(c * j, c * (j + 1))
        ws = [_dot(wq[h][j], state[h].astype(BF16)) for h in hs]
        db = [(sol[h][r, :dim] - ws[h][:c]).astype(BF16) for h in hs]
        for h in hs:
            o_parts[h].append(ws[h][c:] + _dot(pmat[h][r, r], db[h]))
        state = [gc_rows[j][:, heads + h:heads + h + 1] * state[h]
                 + lax.dot_general(kd[h][r], db[h], _TN, preferred_element_type=F32) for h in hs]
    for h in hs:
        state_ref[h] = state[h]

    for h in hs:
        o = jnp.concatenate(o_parts[h], axis=0)
        zh = z_ref[:, h * dim:(h + 1) * dim]
        o_ref[:, h * dim:(h + 1) * dim] = (_rms(o, onw_ref[...]) * _silu(zh)).astype(o_ref.dtype)


def _deltanet(proj3, ba3, conv_w, gate_params, out_norm_w, riders, heads, dim, col_z):
    b, s, _ = proj3.shape
    hd = heads * dim
    gt = DN_GROUP
    ng = s // gt
    rider_in, rider_out, rider_shapes = _cast_rider_specs(riders, b * ng, lambda i, g: i * ng + g)
    pipelined = (_nbytes((gt, 3 * hd), F32) + _nbytes((gt, hd), F32) + _nbytes((gt, LANES), F32)
                 + _nbytes((gt, hd), BF16) + sum(_nbytes(w.shape, F32) * 3 // 2 for w in riders) // (b * ng))
    resident = (_nbytes((gt + SUBLANES, 3 * hd), F32) * 4 + _nbytes((heads, dim, dim), F32)
                + heads * 6 * _nbytes((gt, gt), F32))
    return pl.pallas_call(
        functools.partial(_dn_kernel, heads=heads, dim=dim, n_riders=len(riders)),
        grid=(b, ng),
        in_specs=[
            pl.BlockSpec((None, gt, 3 * hd), lambda i, g: (i, g, 0)),
            pl.BlockSpec((None, gt, hd), lambda i, g: (i, g, col_z // hd)),
            pl.BlockSpec((None, gt, LANES), lambda i, g: (i, g, 0)),
            pl.BlockSpec((4, 3 * hd), lambda i, g: (0, 0)),
            pl.BlockSpec((SUBLANES, LANES), lambda i, g: (0, 0)),
            pl.BlockSpec((1, dim), lambda i, g: (0, 0)),
            *rider_in,
        ],
        out_specs=[pl.BlockSpec((None, gt, hd), lambda i, g: (i, g, 0)), *rider_out],
        out_shape=[jax.ShapeDtypeStruct((b, s, hd), BF16), *rider_shapes],
        scratch_shapes=[pltpu.VMEM((gt + SUBLANES, 3 * hd), F32), pltpu.VMEM((heads, dim, dim), F32)],
        compiler_params=pltpu.CompilerParams(
            dimension_semantics=("parallel", "arbitrary"),
            vmem_limit_bytes=_vmem_limit(pipelined, resident, claim_all=True)),
        name="deltanet",
    )(proj3, proj3, ba3, conv_w, gate_params, out_norm_w, *riders)


def _swa_kernel(rb_ref, sink_ref, q_ref, kc_ref, kp_ref, vc_ref, vp_ref, qw_ref, kw_ref, o_ref, bias_ref,
                *, q_heads, kv_heads, dim):
    blk = SWA_BLOCK
    n = pl.program_id(1)
    qi = lax.broadcasted_iota(jnp.int32, (blk, 2 * blk), 0)
    kj = lax.broadcasted_iota(jnp.int32, (blk, 2 * blk), 1)
    dist = qi - kj + blk

    @pl.when((pl.program_id(0) == 0) & (n == 0))
    def _():
        max_exact = N_BUCKETS // 2
        nn = jnp.maximum(dist, 0)
        nf = jnp.maximum(nn, 1).astype(F32)
        large = max_exact + (jnp.log(nf / max_exact) / math.log(MAX_DISTANCE / max_exact)
                             * (N_BUCKETS - max_exact)).astype(jnp.int32)
        bucket = jnp.where(nn < max_exact, nn, jnp.minimum(large, N_BUCKETS - 1))
        for h in range(q_heads):
            acc = jnp.zeros((blk, 2 * blk), F32)
            for bk in range(N_BUCKETS):
                acc = jnp.where(bucket == bk, rb_ref[bk * q_heads + h], acc)
            acc = jnp.where((dist >= 0) & (dist < SWA_WINDOW), acc, -jnp.inf)
            bias_ref[h] = acc
            bias_ref[q_heads + h] = jnp.where(kj >= blk, acc, -jnp.inf)

    first = jnp.where(n == 0, q_heads, 0)
    group = q_heads // kv_heads
    for j in range(kv_heads):
        cols = slice(j * dim, (j + 1) * dim)
        k_all = _rms(jnp.concatenate([kp_ref[:, cols], kc_ref[:, cols]], axis=0), kw_ref[...]).astype(BF16)
        v_all = jnp.concatenate([vp_ref[:, cols], vc_ref[:, cols]], axis=0).astype(BF16)
        for u in range(SWA_STEP_BLOCKS):
            rows = slice(u * blk, (u + 1) * blk)
            kwin, vwin = k_all[u * blk:(u + 2) * blk], v_all[u * blk:(u + 2) * blk]
            qcat = jnp.concatenate(
                [_rms(q_ref[rows, (j * group + i) * dim:(j * group + i + 1) * dim], qw_ref[...])
                 for i in range(group)], axis=0).astype(BF16)
            logits = _dot_nt(qcat, kwin) * (dim ** -0.5)
            for i in range(group):
                h = j * group + i
                lg = logits[i * blk:(i + 1) * blk] + bias_ref[(first if u == 0 else 0) + h]
                sink = sink_ref[h]
                mx = jnp.maximum(jnp.max(lg, axis=-1, keepdims=True), sink)
                e = jnp.exp(lg - mx)
                den = jnp.sum(e, axis=-1, keepdims=True) + jnp.exp(sink - mx)
                pv = _dot(e.astype(BF16), vwin)
                o_ref[rows, h * dim:(h + 1) * dim] = (pv / den).astype(o_ref.dtype)


def _swa(proj3, rel_bias_flat, sinks, q_norm_w, k_norm_w, q_heads, kv_heads, dim, col_q, col_k, col_v):
    b, s, _ = proj3.shape
    blk = SWA_BLOCK
    step = SWA_STEP_BLOCKS * blk
    qw, kvw = q_heads * dim, kv_heads * dim
    smem = pl.BlockSpec(memory_space=pltpu.SMEM)
    pipelined = (_nbytes((step, qw), F32) + 2 * _nbytes((step + blk, kvw), F32) + _nbytes((step, qw), BF16))
    resident = (_nbytes((2 * q_heads, blk, 2 * blk), F32) + 2 * pipelined
                + 2 * SWA_STEP_BLOCKS * _nbytes((q_heads // kv_heads * blk, 2 * blk), F32))

    def prev(n):
        return jnp.maximum(SWA_STEP_BLOCKS * n - 1, 0)

    return pl.pallas_call(
        functools.partial(_swa_kernel, q_heads=q_heads, kv_heads=kv_heads, dim=dim),
        grid=(b, s // step),
        in_specs=[
            smem, smem,
            pl.BlockSpec((None, step, qw), lambda i, n: (i, n, col_q // qw)),
            pl.BlockSpec((None, step, kvw), lambda i, n: (i, n, col_k // kvw)),
            pl.BlockSpec((None, blk, kvw), lambda i, n: (i, prev(n), col_k // kvw)),
            pl.BlockSpec((None, step, kvw), lambda i, n: (i, n, col_v // kvw)),
            pl.BlockSpec((None, blk, kvw), lambda i, n: (i, prev(n), col_v // kvw)),
            pl.BlockSpec((1, dim), lambda i, n: (0, 0)),
            pl.BlockSpec((1, dim), lambda i, n: (0, 0)),
        ],
        out_specs=pl.BlockSpec((None, step, qw), lambda i, n: (i, n, 0)),
        out_shape=jax.ShapeDtypeStruct((b, s, qw), BF16),
        scratch_shapes=[pltpu.VMEM((2 * q_heads, blk, 2 * blk), F32)],
        compiler_params=pltpu.CompilerParams(
            dimension_semantics=("arbitrary", "arbitrary"),
            vmem_limit_bytes=_vmem_limit(pipelined, resident, claim_all=True)),
        name="swa",
    )(rel_bias_flat, sinks, proj3, proj3, proj3, proj3, proj3, q_norm_w, k_norm_w)


RIDER_RING = 3


def _cast_riders_ringed(hbm_refs, out_refs, bufs, sem, step, n_steps):
    def copies(target):
        slot = lax.rem(target, RIDER_RING)
        out = []
        for r, (hbm, buf) in enumerate(zip(hbm_refs, bufs)):
            rows = buf.shape[1]
            src = hbm.at[pl.ds(pl.multiple_of(target * rows, rows), rows), :]
            out.append(pltpu.make_async_copy(src, buf.at[slot], sem.at[r, slot]))
        return out

    def request(target):
        @pl.when(target < n_steps)
        def _():
            for copy in copies(target):
                copy.start()

    @pl.when(step == 0)
    def _():
        for ahead in range(RIDER_RING - 1):
            request(step + ahead)

    request(step + RIDER_RING - 1)
    for copy in copies(step):
        copy.wait()
    slot = lax.rem(step, RIDER_RING)
    for buf, out in zip(bufs, out_refs):
        out[...] = buf[slot].astype(out.dtype)


def _memattn_kernel(*refs, heads, dim, n_riders):
    q_ref, k_ref, v_ref, qw_ref, kw_ref = refs[:5]
    o_ref = refs[5 + n_riders]
    n_t = pl.num_programs(1)
    _cast_riders_ringed(refs[5:5 + n_riders], refs[6 + n_riders:6 + 2 * n_riders],
                        refs[6 + 2 * n_riders:6 + 3 * n_riders], refs[-1],
                        pl.program_id(0) * n_t + pl.program_id(1), pl.num_programs(0) * n_t)
    for h in range(heads):
        cols = slice(h * dim, (h + 1) * dim)
        qn = _rms(q_ref[:, cols], qw_ref[...]).astype(BF16)
        kn = _rms(k_ref[:, cols], kw_ref[...]).astype(BF16)
        lg = _dot_nt(qn, kn) * (dim ** -0.5)
        e = jnp.exp(lg - jnp.max(lg, axis=-1, keepdims=True))
        den = jnp.sum(e, axis=-1, keepdims=True)
        pv = _dot(e.astype(BF16), v_ref[:, cols].astype(BF16))
        o_ref[:, cols] = (pv / den).astype(o_ref.dtype)


def _memattn(proj3, mkv3, q_norm_w, k_norm_w, riders, heads, dim, col_q, tq):
    b, s, _ = proj3.shape
    m = mkv3.shape[1]
    w = heads * dim
    nt = s // tq
    _, rider_out, rider_shapes = _cast_rider_specs(riders, b * nt, lambda i, t: i * nt + t)
    slabs = [(r.shape[0] // (b * nt), r.shape[1]) for r in riders]
    pipelined = (_nbytes((tq, w), F32) + 2 * _nbytes((m, w), F32) + _nbytes((tq, w), BF16)
                 + sum(_nbytes(slab, BF16) for slab in slabs))
    ring_bytes = RIDER_RING * sum(_nbytes(slab, F32) for slab in slabs)
    return pl.pallas_call(
        functools.partial(_memattn_kernel, heads=heads, dim=dim, n_riders=len(riders)),
        grid=(b, nt),
        in_specs=[
            pl.BlockSpec((pl.Element(tq), pl.Element(w)), lambda i, t: (pl.multiple_of(i * s + t * tq, tq), col_q)),
            pl.BlockSpec((None, m, w), lambda i, t: (i, 0, 0)),
            pl.BlockSpec((None, m, w), lambda i, t: (i, 0, 1)),
            pl.BlockSpec((1, dim), lambda i, t: (0, 0)),
            pl.BlockSpec((1, dim), lambda i, t: (0, 0)),
            *([pl.BlockSpec(memory_space=pl.ANY)] * len(riders)),
        ],
        out_specs=[pl.BlockSpec((None, tq, w), lambda i, t: (i, t, 0)), *rider_out],
        out_shape=[jax.ShapeDtypeStruct((b, s, w), BF16), *rider_shapes],
        scratch_shapes=[*[pltpu.VMEM((RIDER_RING, *slab), F32) for slab in slabs],
                        pltpu.SemaphoreType.DMA((len(riders), RIDER_RING))],
        compiler_params=pltpu.CompilerParams(
            dimension_semantics=("arbitrary", "arbitrary"),
            vmem_limit_bytes=_vmem_limit(pipelined, 2 * heads * _nbytes((tq, m), F32) + ring_bytes,
                                         claim_all=True)),
        name="memattn",
    )(proj3.reshape(b * s, -1), mkv3, mkv3, q_norm_w, k_norm_w, *riders)


def _merge_outproj_kernel(x_ref, od_ref, os_ref, om_ref, pd_ref, ps_ref, pm_ref, gd_ref, gs_ref, gm_ref,
                          w_ref, nw_ref, x1_ref, h_ref):
    j = pl.program_id(1)
    merged = (gd_ref[...].astype(F32) * _dot(od_ref[...], pd_ref[...])
              + gs_ref[...].astype(F32) * _dot(os_ref[...], ps_ref[...])
              + gm_ref[...].astype(F32) * _dot(om_ref[...], pm_ref[...])).astype(BF16)
    part = _dot(merged, w_ref[...])

    @pl.when(j == 0)
    def _():
        x1_ref[...] = x_ref[...] + part

    @pl.when(j > 0)
    def _():
        x1_ref[...] += part

    @pl.when(j == pl.num_programs(1) - 1)
    def _():
        h_ref[...] = _rms(x1_ref[...], nw_ref[...]).astype(h_ref.dtype)


def _merge_outproj(x, o_dn, o_swa, o_mem, p_dn, p_swa, p_mem, gates, w_out, norm_w, tm, tn):
    t, d = x.shape
    wd, ws, wm = o_dn.shape[1], o_swa.shape[1], o_mem.shape[1]
    per_branch = d // tn
    pipelined = (_nbytes((tm, d), F32)
                 + (_nbytes((tm, wd), BF16) + _nbytes((tm, ws), BF16) + _nbytes((tm, wm), BF16))
                 + (_nbytes((wd, tn), BF16) + _nbytes((ws, tn), BF16) + _nbytes((wm, tn), BF16))
                 + 3 * _nbytes((tm, tn), BF16) + _nbytes((tn, d), BF16)
                 + _nbytes((tm, d), F32) + _nbytes((tm, d), BF16))
    return pl.pallas_call(
        _merge_outproj_kernel,
        grid=(t // tm, d // tn),
        in_specs=[
            pl.BlockSpec((tm, d), lambda i, j: (i, 0)),
            pl.BlockSpec((tm, wd), lambda i, j: (i, 0)),
            pl.BlockSpec((tm, ws), lambda i, j: (i, 0)),
            pl.BlockSpec((tm, wm), lambda i, j: (i, 0)),
            pl.BlockSpec((wd, tn), lambda i, j: (0, j)),
            pl.BlockSpec((ws, tn), lambda i, j: (0, j)),
            pl.BlockSpec((wm, tn), lambda i, j: (0, j)),
            pl.BlockSpec((tm, tn), lambda i, j: (i, j)),
            pl.BlockSpec((tm, tn), lambda i, j: (i, per_branch + j)),
            pl.BlockSpec((tm, tn), lambda i, j: (i, 2 * per_branch + j)),
            pl.BlockSpec((tn, d), lambda i, j: (j, 0)),
            pl.BlockSpec((1, d), lambda i, j: (0, 0)),
        ],
        out_specs=[pl.BlockSpec((tm, d), lambda i, j: (i, 0)), pl.BlockSpec((tm, d), lambda i, j: (i, 0))],
        out_shape=[jax.ShapeDtypeStruct((t, d), F32), jax.ShapeDtypeStruct((t, d), BF16)],
        compiler_params=pltpu.CompilerParams(
            dimension_semantics=("parallel", "arbitrary"),
            vmem_limit_bytes=_vmem_limit(pipelined, 3 * _nbytes((tm, tn), F32) + 2 * _nbytes((tm, d), F32))),
        name="merge_outproj",
    )(x, o_dn, o_swa, o_mem, p_dn, p_swa, p_mem, gates, gates, gates, w_out, norm_w)


MLP_RING = 3


def _mlp_kernel(h_ref, x1_ref, wu_hbm, wd_hbm, o_ref, ubuf, dbuf, sem, *, n_k):
    tf = dbuf.shape[1]
    base = pl.program_id(0) * n_k
    n_tiles = pl.num_programs(0) * n_k

    def tile_copies(t):
        slot = lax.rem(t, MLP_RING)
        start = pl.multiple_of(lax.rem(t, n_k) * tf, tf)
        return (pltpu.make_async_copy(wu_hbm.at[:, pl.ds(start, tf)], ubuf.at[slot], sem.at[0, slot]),
                pltpu.make_async_copy(wd_hbm.at[pl.ds(start, tf), :], dbuf.at[slot], sem.at[1, slot]))

    def request(t):
        @pl.when(t < n_tiles)
        def _():
            for copy in tile_copies(t):
                copy.start()

    def wait(t):
        for copy in tile_copies(t):
            copy.wait()

    def up(t):
        a = jnp.maximum(_dot(h_ref[...], ubuf[lax.rem(t, MLP_RING)]), 0.0)
        return (a * a).astype(BF16)

    @pl.when(base == 0)
    def _():
        for t in range(MLP_RING):
            request(base + t)

    o_ref[...] = x1_ref[...]
    wait(base)
    act = up(base)
    for k in range(n_k):
        if k + 1 < n_k:
            wait(base + k + 1)
        o_ref[...] += _dot(act, dbuf[lax.rem(base + k, MLP_RING)])
        if k + 1 < n_k:
            act = up(base + k + 1)
        request(base + k + MLP_RING)


def _mlp(h2, x1, w_up, w_down, tm, tf):
    t, d = x1.shape
    f = w_up.shape[1]
    pipelined = _nbytes((tm, d), BF16) + 2 * _nbytes((tm, d), F32)
    resident = 2 * _nbytes((tm, tf), F32) + MLP_RING * (_nbytes((d, tf), BF16) + _nbytes((tf, d), BF16))
    return pl.pallas_call(
        functools.partial(_mlp_kernel, n_k=f // tf),
        grid=(t // tm,),
        in_specs=[
            pl.BlockSpec((tm, d), lambda i: (i, 0)),
            pl.BlockSpec((tm, d), lambda i: (i, 0)),
            pl.BlockSpec(memory_space=pl.ANY),
            pl.BlockSpec(memory_space=pl.ANY),
        ],
        out_specs=pl.BlockSpec((tm, d), lambda i: (i, 0)),
        out_shape=jax.ShapeDtypeStruct((t, d), F32),
        scratch_shapes=[pltpu.VMEM((MLP_RING, d, tf), BF16), pltpu.VMEM((MLP_RING, tf, d), BF16),
                        pltpu.SemaphoreType.DMA((2, MLP_RING))],
        compiler_params=pltpu.CompilerParams(
            dimension_semantics=("arbitrary",),
            vmem_limit_bytes=_vmem_limit(pipelined, resident)),
        name="mlp",
    )(h2, x1, w_up, w_down)


def _layer(x, mem, attn_norm_w, w_in, dn_conv_w, dn_a_log, dn_dt_bias, dn_out_norm_w, swa_q_norm_w,
           swa_k_norm_w, swa_sinks, rel_bias, mem_norm_w, w_mem_kv, xq_norm_w, xk_norm_w,
           p_dn, p_swa, p_mem, w_out, mlp_norm_w, w_mlp_up, w_mlp_down):
    b, s, d = x.shape
    m = mem.shape[1]
    t = b * s

    dn_heads = dn_a_log.shape[0]
    dn_dim = dn_out_norm_w.shape[0]
    dn_w = dn_heads * dn_dim
    swa_heads = swa_sinks.shape[0]
    swa_dim = swa_q_norm_w.shape[0]
    swa_w = swa_heads * swa_dim
    mem_dim = xq_norm_w.shape[0]
    mem_w = p_mem.shape[0]
    mem_heads = mem_w // mem_dim
    swa_kv_w = (w_in.shape[1] - 4 * dn_w - 2 * dn_heads - swa_w - mem_w - 3 * d) // 2
    swa_kv = swa_kv_w // swa_dim

    n_ba = 2 * dn_heads
    src_ba = 4 * dn_w
    src_swa = src_ba + n_ba
    w_t = w_in.T.astype(BF16)
    w_b = w_t[src_swa:]
    w_c = jnp.concatenate([w_t[src_ba:src_swa], jnp.zeros((LANES - n_ba, d), BF16)], axis=0)
    col = {"qkv": 0, "z": 3 * dn_w, "sq": src_ba}
    col["sk"] = col["sq"] + swa_w
    col["sv"] = col["sk"] + swa_kv_w
    col["mq"] = col["sv"] + swa_kv_w

    proj, gates, ba = _inproj(x.reshape(t, d), attn_norm_w.reshape(1, d), w_t, src_ba, w_b, w_c, 3 * d,
                              tm=1024, tn=1024)
    proj3 = proj.reshape(b, s, -1)
    mkv = _norm_matmul(mem.reshape(b * m, d), mem_norm_w.reshape(1, d), w_mem_kv.astype(BF16), tm=512, tn=512)
    mkv3 = mkv.reshape(b, m, -1)

    gate_params = jnp.zeros((SUBLANES, LANES), F32)
    gate_params = gate_params.at[0, dn_heads:2 * dn_heads].set(dn_a_log)
    gate_params = gate_params.at[1, dn_heads:2 * dn_heads].set(dn_dt_bias)
    o_dn, w_up_bf, w_down_bf = _deltanet(proj3, ba.reshape(b, s, LANES), dn_conv_w, gate_params,
                                         dn_out_norm_w.reshape(1, dn_dim), [w_mlp_up, w_mlp_down],
                                         dn_heads, dn_dim, col["z"])
    o_swa = _swa(proj3, rel_bias.reshape(-1), swa_sinks, swa_q_norm_w.reshape(1, swa_dim),
                 swa_k_norm_w.reshape(1, swa_dim), swa_heads, swa_kv, swa_dim, col["sq"], col["sk"], col["sv"])
    o_mem, w_out_bf, p_dn_bf, p_swa_bf, p_mem_bf = _memattn(
        proj3, mkv3, xq_norm_w.reshape(1, mem_dim), xk_norm_w.reshape(1, mem_dim),
        [w_out, p_dn, p_swa, p_mem], mem_heads, mem_dim, col["mq"], tq=1024)

    x1, h2 = _merge_outproj(x.reshape(t, d), o_dn.reshape(t, dn_w), o_swa.reshape(t, swa_w),
                            o_mem.reshape(t, mem_w), p_dn_bf, p_swa_bf, p_mem_bf, gates, w_out_bf,
                            mlp_norm_w.reshape(1, d), tm=512, tn=512)
    out = _mlp(h2, x1, w_up_bf, w_down_bf, tm=512, tf=1024)
    return out.reshape(b, s, d)


def kernel(x, mem, attn_norm_w, w_in, dn_conv_w, dn_A_log, dn_dt_bias, dn_out_norm_w, swa_q_norm_w,
           swa_k_norm_w, swa_sinks, rel_bias, mem_norm_w, w_mem_kv, xq_norm_w, xk_norm_w, p_dn, p_swa,
           p_mem, w_out, mlp_norm_w, w_mlp_up, w_mlp_down):
    depth = w_in.shape[0]
    for l in range(depth):
        x = _layer(x, mem, attn_norm_w[l], w_in[l], dn_conv_w[l], dn_A_log[l], dn_dt_bias[l],
                   dn_out_norm_w[l], swa_q_norm_w[l], swa_k_norm_w[l], swa_sinks[l], rel_bias,
                   mem_norm_w[l], w_mem_kv[l], xq_norm_w[l], xk_norm_w[l], p_dn[l], p_swa[l], p_mem[l],
                   w_out[l], mlp_norm_w[l], w_mlp_up[l], w_mlp_down[l])
    return x
```

```python
import functools
import math

import jax
import jax.numpy as jnp
from jax import lax
from jax.experimental import pallas as pl
from jax.experimental.pallas import tpu as pltpu

F32 = jnp.float32
BF16 = jnp.bfloat16

EPS = 1e-6
LANES = 128
SUBLANES = 8
V7X_VMEM_BYTES = 64 * 1024 * 1024

DN_CHUNK = 64
DN_GROUP = 256
DN_BASE = 8
SWA_WINDOW = 128
SWA_BLOCK = 128
SWA_STEP_BLOCKS = 4
N_BUCKETS = 32
MAX_DISTANCE = 128

_NT = (((1,), (1,)), ((), ()))
_TN = (((0,), (0,)), ((), ()))


COMPILER_SCRATCH_BYTES = 4 * 1024 * 1024
VMEM_RESERVE_BYTES = 8 * 1024 * 1024


def _vmem_limit(pipelined_bytes, resident_bytes, claim_all=False):
    cap = V7X_VMEM_BYTES - VMEM_RESERVE_BYTES
    want = 2 * pipelined_bytes + resident_bytes + COMPILER_SCRATCH_BYTES
    return int(cap if claim_all else min(want, cap))


def _nbytes(shape, dtype):
    return math.prod(shape) * jnp.dtype(dtype).itemsize


def _sigmoid(v):
    return 0.5 * jnp.tanh(0.5 * v) + 0.5


def _silu(v):
    return v * _sigmoid(v)


def _rms(v, w):
    return (v * lax.rsqrt(jnp.mean(v * v, axis=-1, keepdims=True) + EPS)) * w


def _dot(a, b):
    return jnp.dot(a, b, preferred_element_type=F32)


def _dot_nt(a, b):
    return lax.dot_general(a, b, _NT, preferred_element_type=F32)


def _cast_rider_specs(weights, n_steps, step_of):
    in_specs, out_specs, out_shapes = [], [], []
    for w in weights:
        rows = w.shape[0] // n_steps
        assert rows * n_steps == w.shape[0] and rows % (2 * SUBLANES) == 0
        for specs in (in_specs, out_specs):
            specs.append(pl.BlockSpec((rows, w.shape[1]), lambda *ids: (step_of(*ids), 0)))
        out_shapes.append(jax.ShapeDtypeStruct(w.shape, BF16))
    return in_specs, out_specs, out_shapes


def _cast_riders(in_refs, out_refs):
    for i_ref, o_ref in zip(in_refs, out_refs):
        o_ref[...] = i_ref[...].astype(o_ref.dtype)


IN_RING = 4


def _inproj_kernel(x_hbm, nw_ref, wa_hbm, wb_hbm, wc_ref, o_ref, g_ref, ba_ref, h_ref, xbuf, wbuf, sem, xsem,
                   *, n_a, n_p, tail, rows_b1):
    i, j = pl.program_id(0), pl.program_id(1)
    n_j = pl.num_programs(1)
    tm = xbuf.shape[0]
    tn = wbuf.shape[1]
    step = i * n_j + j
    n_steps = pl.num_programs(0) * n_j

    def x_copy(row_tile):
        return pltpu.make_async_copy(x_hbm.at[pl.ds(pl.multiple_of(row_tile * tm, tm), tm), :], xbuf, xsem.at[0])

    @pl.when(step == 0)
    def _():
        x_copy(0).start()

    @pl.when((j == 1) & (i + 1 < pl.num_programs(0)))
    def _():
        x_copy(i + 1).start()

    def tile_copy(col_step, slot, from_b):
        if from_b:
            row = jnp.where(col_step < n_p, (col_step - n_a) * tn, rows_b1 + (col_step - n_p) * tn)
            src = wb_hbm.at[pl.ds(pl.multiple_of(row, 2 * SUBLANES), tn), :]
        else:
            src = wa_hbm.at[pl.ds(pl.multiple_of(col_step * tn, tn), tn), :]
        return pltpu.make_async_copy(src, wbuf.at[slot], sem.at[slot])

    def request(ahead):
        target = step + ahead
        col_step = lax.rem(target, n_j)
        slot = lax.rem(target, IN_RING)

        @pl.when((target < n_steps) & (col_step < n_a))
        def _():
            tile_copy(col_step, slot, from_b=False).start()

        @pl.when((target < n_steps) & (col_step >= n_a))
        def _():
            tile_copy(col_step, slot, from_b=True).start()

    @pl.when(step == 0)
    def _():
        for ahead in range(IN_RING - 1):
            request(ahead)

    request(IN_RING - 1)
    slot = lax.rem(step, IN_RING)
    pltpu.make_async_copy(wb_hbm.at[pl.ds(0, tn), :], wbuf.at[slot], sem.at[slot]).wait()
    w_ref = wbuf.at[slot]

    @pl.when(j == 0)
    def _():
        x_copy(i).wait()
        h_ref[...] = _rms(xbuf[...], nw_ref[...]).astype(BF16)
        ba_ref[...] = _dot_nt(h_ref[...], wc_ref[...])

    @pl.when(j < n_p - 1)
    def _():
        o_ref[...] = _dot_nt(h_ref[...], w_ref[...])

    @pl.when(j == n_p - 1)
    def _():
        o_ref[:, :tail] = _dot_nt(h_ref[...], w_ref[:tail, :])

    @pl.when(j >= n_p)
    def _():
        g_ref[...] = _sigmoid(_dot_nt(h_ref[...], w_ref[...])).astype(g_ref.dtype)


def _inproj(x, nw, w_a, rows_a, w_b, w_c, n_gates, tm, tn):
    m, k = x.shape
    rows_b1 = w_b.shape[0] - n_gates
    n = rows_a + rows_b1
    n_a = rows_a // tn
    n_p = pl.cdiv(n, tn)
    assert rows_a % tn == 0 and n_gates % tn == 0 and rows_b1 % (2 * SUBLANES) == 0

    pipelined = _nbytes((tm, tn), F32) + _nbytes((tm, tn), BF16) + _nbytes((tm, LANES), F32)
    resident = (_nbytes((tm, k), BF16) + _nbytes((tm, k), F32) + _nbytes((tm, tn), F32)
                + _nbytes((LANES, k), BF16) + IN_RING * _nbytes((tn, k), BF16))
    return pl.pallas_call(
        functools.partial(_inproj_kernel, n_a=n_a, n_p=n_p, tail=n - (n_p - 1) * tn, rows_b1=rows_b1),
        grid=(m // tm, n_p + n_gates // tn),
        in_specs=[
            pl.BlockSpec(memory_space=pl.ANY),
            pl.BlockSpec((1, k), lambda i, j: (0, 0)),
            pl.BlockSpec(memory_space=pl.ANY),
            pl.BlockSpec(memory_space=pl.ANY),
            pl.BlockSpec((LANES, k), lambda i, j: (0, 0)),
        ],
        out_specs=[pl.BlockSpec((tm, tn), lambda i, j: (i, jnp.minimum(j, n_p - 1))),
                   pl.BlockSpec((tm, tn), lambda i, j: (i, jnp.maximum(j - n_p, 0))),
                   pl.BlockSpec((tm, LANES), lambda i, j: (i, 0))],
        out_shape=[jax.ShapeDtypeStruct((m, n), F32), jax.ShapeDtypeStruct((m, n_gates), BF16),
                   jax.ShapeDtypeStruct((m, LANES), F32)],
        scratch_shapes=[pltpu.VMEM((tm, k), BF16), pltpu.VMEM((tm, k), F32), pltpu.VMEM((IN_RING, tn, k), BF16),
                        pltpu.SemaphoreType.DMA((IN_RING,)), pltpu.SemaphoreType.DMA((1,))],
        compiler_params=pltpu.CompilerParams(
            dimension_semantics=("arbitrary", "arbitrary"),
            vmem_limit_bytes=_vmem_limit(pipelined, resident)),
        name="inproj",
    )(x, nw, w_a, w_b, w_c)


def _norm_matmul_kernel(x_ref, nw_ref, w_ref, o_ref, h_ref):
    @pl.when(pl.program_id(1) == 0)
    def _():
        h_ref[...] = _rms(x_ref[...], nw_ref[...]).astype(BF16)

    o_ref[...] = _dot(h_ref[...], w_ref[...]).astype(o_ref.dtype)


def _norm_matmul(x, nw, w, tm, tn):
    m, k = x.shape
    n = w.shape[1]
    pipelined = _nbytes((tm, k), F32) + _nbytes((k, tn), BF16) + _nbytes((tm, tn), F32)
    resident = _nbytes((tm, k), BF16) + _nbytes((tm, k), F32)
    return pl.pallas_call(
        _norm_matmul_kernel,
        grid=(m // tm, n // tn),
        in_specs=[
            pl.BlockSpec((tm, k), lambda i, j: (i, 0)),
            pl.BlockSpec((1, k), lambda i, j: (0, 0)),
            pl.BlockSpec((k, tn), lambda i, j: (0, j)),
        ],
        out_specs=pl.BlockSpec((tm, tn), lambda i, j: (i, j)),
        out_shape=jax.ShapeDtypeStruct((m, n), F32),
        scratch_shapes=[pltpu.VMEM((tm, k), BF16)],
        compiler_params=pltpu.CompilerParams(
            dimension_semantics=("parallel", "arbitrary"),
            vmem_limit_bytes=_vmem_limit(pipelined, resident)),
        name="norm_matmul",
    )(x, nw, w)


def _dn_kernel(*refs, heads, dim, n_riders):
    qkv_ref, z_ref, ba_ref, cw_ref, gp_ref, onw_ref = refs[:6]
    o_ref = refs[6 + n_riders]
    ext_ref, state_ref = refs[-2:]
    _cast_riders(refs[6:6 + n_riders], refs[7 + n_riders:7 + 2 * n_riders])

    gt = DN_GROUP
    c = DN_CHUNK
    hd = heads * dim
    hs = range(heads)
    chunks = range(gt // c)

    @pl.when(pl.program_id(1) == 0)
    def _():
        ext_ref[0:SUBLANES, :] = jnp.zeros((SUBLANES, 3 * hd), F32)
        state_ref[...] = jnp.zeros_like(state_ref)

    ext_ref[SUBLANES:SUBLANES + gt, :] = qkv_ref[...]
    xe = ext_ref[...]
    conv = cw_ref[3:4, :] * xe[SUBLANES:]
    for s in (1, 2, 3):
        conv = conv + cw_ref[3 - s:4 - s, :] * pltpu.roll(xe, s, 0)[SUBLANES:]
    ext_ref[0:SUBLANES, :] = xe[gt:gt + SUBLANES]
    act = _silu(conv)

    ba = ba_ref[...]
    beta_all = _sigmoid(ba)
    xa = ba + gp_ref[1:2, :]
    softplus = jnp.maximum(xa, 0.0) + jnp.log1p(jnp.exp(-jnp.abs(xa)))
    g_all = -jnp.exp(gp_ref[0:1, :]) * softplus
    row_in_chunk = lax.broadcasted_iota(jnp.int32, (gt, LANES), 0) & (c - 1)
    gcum = g_all
    s = 1
    while s < c:
        gcum = gcum + jnp.where(row_in_chunk >= s, pltpu.roll(gcum, s, 0), 0.0)
        s *= 2
    exp_g = jnp.exp(gcum)
    kdec_parts, gc_rows = [], []
    for j in chunks:
        g_last = gcum[c * j + c - 1:c * j + c, :]
        kdec_parts.append(jnp.exp(g_last - gcum[c * j:c * (j + 1), :]))
        gc_rows.append(jnp.exp(g_last))
    kdec = jnp.concatenate(kdec_parts, axis=0)
    gcum_t = gcum.T

    ri = lax.broadcasted_iota(jnp.int32, (gt, gt), 0)
    ci = lax.broadcasted_iota(jnp.int32, (gt, gt), 1)
    same_chunk = (ri // c) == (ci // c)
    strict = same_chunk & (ri > ci)
    incl = same_chunk & (ri >= ci)

    qn, kn, kb, beta, eg = [], [], [], [], []
    for h in hs:
        qh = act[:, h * dim:(h + 1) * dim]
        kh = act[:, hd + h * dim:hd + (h + 1) * dim]
        qn.append(qh * lax.rsqrt(jnp.sum(qh * qh, axis=-1, keepdims=True) + EPS) * (dim ** -0.5))
        kn.append(kh * lax.rsqrt(jnp.sum(kh * kh, axis=-1, keepdims=True) + EPS))
        kb.append(kn[h].astype(BF16))
        beta.append(beta_all[:, h:h + 1])
        eg.append(exp_g[:, heads + h:heads + h + 1])

    decay = [jnp.exp(jnp.where(incl, gcum[:, heads + h:heads + h + 1] - gcum_t[heads + h:heads + h + 1, :],
                               -jnp.inf)) for h in hs]
    kk = [_dot_nt(kb[h], kb[h]) for h in hs]
    nmat = [jnp.where(strict, (beta[h] * kk[h]) * decay[h], 0.0) for h in hs]

    def same_block(size):
        return (ri // size) == (ci // size)

    eye = jnp.where(ri == ci, 1.0, 0.0)
    base = same_block(DN_BASE)
    pw = [jnp.where(base, nmat[h], 0.0) for h in hs]
    tinv = [eye - pw[h] for h in hs]
    pw = [pw[h].astype(BF16) for h in hs]
    order = 2
    while order < DN_BASE:
        pw = [_dot(pw[h], pw[h]).astype(BF16) for h in hs]
        tinv = [tinv[h] + _dot(tinv[h].astype(BF16), pw[h]) for h in hs]
        order *= 2
    size = DN_BASE
    while size < c:
        level = same_block(2 * size) & jnp.logical_not(same_block(size))
        off = [jnp.where(level, nmat[h], 0.0).astype(BF16) for h in hs]
        ct = [_dot(off[h], tinv[h].astype(BF16)).astype(BF16) for h in hs]
        tinv = [tinv[h] - _dot(tinv[h].astype(BF16), ct[h]) for h in hs]
        size *= 2

    sol = [jnp.concatenate([act[:, 2 * hd + h * dim:2 * hd + (h + 1) * dim] * beta[h],
                            kn[h] * (beta[h] * eg[h])], axis=1) for h in hs]
    sol = [sol[h] + _dot((tinv[h] - eye).astype(BF16), sol[h].astype(BF16)) for h in hs]

    pmat = [(_dot_nt(qn[h].astype(BF16), kb[h]) * decay[h]).astype(BF16) for h in hs]
    wq = [[jnp.concatenate([sol[h][c * j:c * (j + 1), dim:], (qn[h] * eg[h])[c * j:c * (j + 1)]],
                           axis=0).astype(BF16) for j in chunks] for h in hs]
    kd = [(kn[h] * kdec[:, heads + h:heads + h + 1]).astype(BF16) for h in hs]

    state = [state_ref[h] for h in hs]
    o_parts = [[] for _ in hs]
    for j in chunks:
        r = slice(c * j, c * (j + 1))
        ws = [_dot(wq[h][j], state[h].astype(BF16)) for h in hs]
        db = [(sol[h][r, :dim] - ws[h][:c]).astype(BF16) for h in hs]
        for h in hs:
            o_parts[h].append(ws[h][c:] + _dot(pmat[h][r, r], db[h]))
        state = [gc_rows[j][:, heads + h:heads + h + 1] * state[h]
                 + lax.dot_general(kd[h][r], db[h], _TN, preferred_element_type=F32) for h in hs]
    for h in hs:
        state_ref[h] = state[h]

    for h in hs:
        o = jnp.concatenate(o_parts[h], axis=0)
        zh = z_ref[:, h * dim:(h + 1) * dim]
        o_ref[:, h * dim:(h + 1) * dim] = (_rms(o, onw_ref[...]) * _silu(zh)).astype(o_ref.dtype)


def _deltanet(proj3, ba3, conv_w, gate_params, out_norm_w, riders, heads, dim, col_z):
    b, s, _ = proj3.shape
    hd = heads * dim
    gt = DN_GROUP
    ng = s // gt
    rider_in, rider_out, rider_shapes = _cast_rider_specs(riders, b * ng, lambda i, g: i * ng + g)
    pipelined = (_nbytes((gt, 3 * hd), F32) + _nbytes((gt, hd), F32) + _nbytes((gt, LANES), F32)
                 + _nbytes((gt, hd), BF16) + sum(_nbytes(w.shape, F32) * 3 // 2 for w in riders) // (b * ng))
    resident = (_nbytes((gt + SUBLANES, 3 * hd), F32) * 4 + _nbytes((heads, dim, dim), F32)
                + heads * 6 * _nbytes((gt, gt), F32))
    return pl.pallas_call(
        functools.partial(_dn_kernel, heads=heads, dim=dim, n_riders=len(riders)),
        grid=(b, ng),
        in_specs=[
            pl.BlockSpec((None, gt, 3 * hd), lambda i, g: (i, g, 0)),
            pl.BlockSpec((None, gt, hd), lambda i, g: (i, g, col_z // hd)),
            pl.BlockSpec((None, gt, LANES), lambda i, g: (i, g, 0)),
            pl.BlockSpec((4, 3 * hd), lambda i, g: (0, 0)),
            pl.BlockSpec((SUBLANES, LANES), lambda i, g: (0, 0)),
            pl.BlockSpec((1, dim), lambda i, g: (0, 0)),
            *rider_in,
        ],
        out_specs=[pl.BlockSpec((None, gt, hd), lambda i, g: (i, g, 0)), *rider_out],
        out_shape=[jax.ShapeDtypeStruct((b, s, hd), BF16), *rider_shapes],
        scratch_shapes=[pltpu.VMEM((gt + SUBLANES, 3 * hd), F32), pltpu.VMEM((heads, dim, dim), F32)],
        compiler_params=pltpu.CompilerParams(
            dimension_semantics=("parallel", "arbitrary"),
            vmem_limit_bytes=_vmem_limit(pipelined, resident, claim_all=True)),
        name="deltanet",
    )(proj3, proj3, ba3, conv_w, gate_params, out_norm_w, *riders)


def _swa_kernel(rb_ref, sink_ref, q_ref, kc_ref, kp_ref, vc_ref, vp_ref, qw_ref, kw_ref, o_ref, bias_ref,
                *, q_heads, kv_heads, dim):
    blk = SWA_BLOCK
    n = pl.program_id(1)
    qi = lax.broadcasted_iota(jnp.int32, (blk, 2 * blk), 0)
    kj = lax.broadcasted_iota(jnp.int32, (blk, 2 * blk), 1)
    dist = qi - kj + blk

    @pl.when((pl.program_id(0) == 0) & (n == 0))
    def _():
        max_exact = N_BUCKETS // 2
        nn = jnp.maximum(dist, 0)
        nf = jnp.maximum(nn, 1).astype(F32)
        large = max_exact + (jnp.log(nf / max_exact) / math.log(MAX_DISTANCE / max_exact)
                             * (N_BUCKETS - max_exact)).astype(jnp.int32)
        bucket = jnp.where(nn < max_exact, nn, jnp.minimum(large, N_BUCKETS - 1))
        for h in range(q_heads):
            acc = jnp.zeros((blk, 2 * blk), F32)
            for bk in range(N_BUCKETS):
                acc = jnp.where(bucket == bk, rb_ref[bk * q_heads + h], acc)
            acc = jnp.where((dist >= 0) & (dist < SWA_WINDOW), acc, -jnp.inf)
            bias_ref[h] = acc
            bias_ref[q_heads + h] = jnp.where(kj >= blk, acc, -jnp.inf)

    first = jnp.where(n == 0, q_heads, 0)
    group = q_heads // kv_heads
    for j in range(kv_heads):
        cols = slice(j * dim, (j + 1) * dim)
        k_all = _rms(jnp.concatenate([kp_ref[:, cols], kc_ref[:, cols]], axis=0), kw_ref[...]).astype(BF16)
        v_all = jnp.concatenate([vp_ref[:, cols], vc_ref[:, cols]], axis=0).astype(BF16)
        for u in range(SWA_STEP_BLOCKS):
            rows = slice(u * blk, (u + 1) * blk)
            kwin, vwin = k_all[u * blk:(u + 2) * blk], v_all[u * blk:(u + 2) * blk]
            qcat = jnp.concatenate(
                [_rms(q_ref[rows, (j * group + i) * dim:(j * group + i + 1) * dim], qw_ref[...])
                 for i in range(group)], axis=0).astype(BF16)
            logits = _dot_nt(qcat, kwin) * (dim ** -0.5)
            for i in range(group):
                h = j * group + i
                lg = logits[i * blk:(i + 1) * blk] + bias_ref[(first if u == 0 else 0) + h]
                sink = sink_ref[h]
                mx = jnp.maximum(jnp.max(lg, axis=-1, keepdims=True), sink)
                e = jnp.exp(lg - mx)
                den = jnp.sum(e, axis=-1, keepdims=True) + jnp.exp(sink - mx)
                pv = _dot(e.astype(BF16), vwin)
                o_ref[rows, h * dim:(h + 1) * dim] = (pv / den).astype(o_ref.dtype)


def _swa(proj3, rel_bias_flat, sinks, q_norm_w, k_norm_w, q_heads, kv_heads, dim, col_q, col_k, col_v):
    b, s, _ = proj3.shape
    blk = SWA_BLOCK
    step = SWA_STEP_BLOCKS * blk
    qw, kvw = q_heads * dim, kv_heads * dim
    smem = pl.BlockSpec(memory_space=pltpu.SMEM)
    pipelined = (_nbytes((step, qw), F32) + 2 * _nbytes((step + blk, kvw), F32) + _nbytes((step, qw), BF16))
    resident = (_nbytes((2 * q_heads, blk, 2 * blk), F32) + 2 * pipelined
                + 2 * SWA_STEP_BLOCKS * _nbytes((q_heads // kv_heads * blk, 2 * blk), F32))

    def prev(n):
        return jnp.maximum(SWA_STEP_BLOCKS * n - 1, 0)

    return pl.pallas_call(
        functools.partial(_swa_kernel, q_heads=q_heads, kv_heads=kv_heads, dim=dim),
        grid=(b, s // step),
        in_specs=[
            smem, smem,
            pl.BlockSpec((None, step, qw), lambda i, n: (i, n, col_q // qw)),
            pl.BlockSpec((None, step, kvw), lambda i, n: (i, n, col_k // kvw)),
            pl.BlockSpec((None, blk, kvw), lambda i, n: (i, prev(n), col_k // kvw)),
            pl.BlockSpec((None, step, kvw), lambda i, n: (i, n, col_v // kvw)),
            pl.BlockSpec((None, blk, kvw), lambda i, n: (i, prev(n), col_v // kvw)),
            pl.BlockSpec((1, dim), lambda i, n: (0, 0)),
            pl.BlockSpec((1, dim), lambda i, n: (0, 0)),
        ],
        out_specs=pl.BlockSpec((None, step, qw), lambda i, n: (i, n, 0)),
        out_shape=jax.ShapeDtypeStruct((b, s, qw), BF16),
        scratch_shapes=[pltpu.VMEM((2 * q_heads, blk, 2 * blk), F32)],
        compiler_params=pltpu.CompilerParams(
            dimension_semantics=("arbitrary", "arbitrary"),
            vmem_limit_bytes=_vmem_limit(pipelined, resident, claim_all=True)),
        name="swa",
    )(rel_bias_flat, sinks, proj3, proj3, proj3, proj3, proj3, q_norm_w, k_norm_w)


RIDER_RING = 3


def _cast_riders_ringed(hbm_refs, out_refs, bufs, sem, step, n_steps):
    def copies(target):
        slot = lax.rem(target, RIDER_RING)
        out = []
        for r, (hbm, buf) in enumerate(zip(hbm_refs, bufs)):
            rows = buf.shape[1]
            src = hbm.at[pl.ds(pl.multiple_of(target * rows, rows), rows), :]
            out.append(pltpu.make_async_copy(src, buf.at[slot], sem.at[r, slot]))
        return out

    def request(target):
        @pl.when(target < n_steps)
        def _():
            for copy in copies(target):
                copy.start()

    @pl.when(step == 0)
    def _():
        for ahead in range(RIDER_RING - 1):
            request(step + ahead)

    request(step + RIDER_RING - 1)
    for copy in copies(step):
        copy.wait()
    slot = lax.rem(step, RIDER_RING)
    for buf, out in zip(bufs, out_refs):
        out[...] = buf[slot].astype(out.dtype)


def _memattn_kernel(*refs, heads, dim, n_riders):
    q_ref, k_ref, v_ref, qw_ref, kw_ref = refs[:5]
    o_ref = refs[5 + n_riders]
    n_t = pl.num_programs(1)
    _cast_riders_ringed(refs[5:5 + n_riders], refs[6 + n_riders:6 + 2 * n_riders],
                        refs[6 + 2 * n_riders:6 + 3 * n_riders], refs[-1],
                        pl.program_id(0) * n_t + pl.program_id(1), pl.num_programs(0) * n_t)
    for h in range(heads):
        cols = slice(h * dim, (h + 1) * dim)
        qn = _rms(q_ref[:, cols], qw_ref[...]).astype(BF16)
        kn = _rms(k_ref[:, cols], kw_ref[...]).astype(BF16)
        lg = _dot_nt(qn, kn) * (dim ** -0.5)
        e = jnp.exp(lg - jnp.max(lg, axis=-1, keepdims=True))
        den = jnp.sum(e, axis=-1, keepdims=True)
        pv = _dot(e.astype(BF16), v_ref[:, cols].astype(BF16))
        o_ref[:, cols] = (pv / den).astype(o_ref.dtype)


def _memattn(proj3, mkv3, q_norm_w, k_norm_w, riders, heads, dim, col_q, tq):
    b, s, _ = proj3.shape
    m = mkv3.shape[1]
    w = heads * dim
    nt = s // tq
    _, rider_out, rider_shapes = _cast_rider_specs(riders, b * nt, lambda i, t: i * nt + t)
    slabs = [(r.shape[0] // (b * nt), r.shape[1]) for r in riders]
    pipelined = (_nbytes((tq, w), F32) + 2 * _nbytes((m, w), F32) + _nbytes((tq, w), BF16)
                 + sum(_nbytes(slab, BF16) for slab in slabs))
    ring_bytes = RIDER_RING * sum(_nbytes(slab, F32) for slab in slabs)
    return pl.pallas_call(
        functools.partial(_memattn_kernel, heads=heads, dim=dim, n_riders=len(riders)),
        grid=(b, nt),
        in_specs=[
            pl.BlockSpec((pl.Element(tq), pl.Element(w)), lambda i, t: (pl.multiple_of(i * s + t * tq, tq), col_q)),
            pl.BlockSpec((None, m, w), lambda i, t: (i, 0, 0)),
            pl.BlockSpec((None, m, w), lambda i, t: (i, 0, 1)),
            pl.BlockSpec((1, dim), lambda i, t: (0, 0)),
            pl.BlockSpec((1, dim), lambda i, t: (0, 0)),
            *([pl.BlockSpec(memory_space=pl.ANY)] * len(riders)),
        ],
        out_specs=[pl.BlockSpec((None, tq, w), lambda i, t: (i, t, 0)), *rider_out],
        out_shape=[jax.ShapeDtypeStruct((b, s, w), BF16), *rider_shapes],
        scratch_shapes=[*[pltpu.VMEM((RIDER_RING, *slab), F32) for slab in slabs],
                        pltpu.SemaphoreType.DMA((len(riders), RIDER_RING))],
        compiler_params=pltpu.CompilerParams(
            dimension_semantics=("arbitrary", "arbitrary"),
            vmem_limit_bytes=_vmem_limit(pipelined, 2 * heads * _nbytes((tq, m), F32) + ring_bytes,
                                         claim_all=True)),
        name="memattn",
    )(proj3.reshape(b * s, -1), mkv3, mkv3, q_norm_w, k_norm_w, *riders)


def _merge_kernel(od_ref, os_ref, om_ref, pd_ref, ps_ref, pm_ref, gd_ref, gs_ref, gm_ref, o_ref):
    merged = (gd_ref[...].astype(F32) * _dot(od_ref[...], pd_ref[...])
              + gs_ref[...].astype(F32) * _dot(os_ref[...], ps_ref[...])
              + gm_ref[...].astype(F32) * _dot(om_ref[...], pm_ref[...]))
    o_ref[...] = merged.astype(o_ref.dtype)


def _merge(o_dn, o_swa, o_mem, p_dn, p_swa, p_mem, gates, tm, tn):
    t = o_dn.shape[0]
    d = p_dn.shape[1]
    wd, ws, wm = o_dn.shape[1], o_swa.shape[1], o_mem.shape[1]
    per_branch = d // tn
    pipelined = ((_nbytes((tm, wd), BF16) + _nbytes((tm, ws), BF16) + _nbytes((tm, wm), BF16))
                 + (_nbytes((wd, tn), BF16) + _nbytes((ws, tn), BF16) + _nbytes((wm, tn), BF16))
                 + 4 * _nbytes((tm, tn), BF16))
    return pl.pallas_call(
        _merge_kernel,
        grid=(d // tn, t // tm),
        in_specs=[
            pl.BlockSpec((tm, wd), lambda j, i: (i, 0)),
            pl.BlockSpec((tm, ws), lambda j, i: (i, 0)),
            pl.BlockSpec((tm, wm), lambda j, i: (i, 0)),
            pl.BlockSpec((wd, tn), lambda j, i: (0, j)),
            pl.BlockSpec((ws, tn), lambda j, i: (0, j)),
            pl.BlockSpec((wm, tn), lambda j, i: (0, j)),
            pl.BlockSpec((tm, tn), lambda j, i: (i, j)),
            pl.BlockSpec((tm, tn), lambda j, i: (i, per_branch + j)),
            pl.BlockSpec((tm, tn), lambda j, i: (i, 2 * per_branch + j)),
        ],
        out_specs=pl.BlockSpec((tm, tn), lambda j, i: (i, j)),
        out_shape=jax.ShapeDtypeStruct((t, d), BF16),
        compiler_params=pltpu.CompilerParams(
            dimension_semantics=("parallel", "parallel"),
            vmem_limit_bytes=_vmem_limit(pipelined, 3 * _nbytes((tm, tn), F32))),
        name="merge",
    )(o_dn, o_swa, o_mem, p_dn, p_swa, p_mem, gates, gates, gates)


def _outproj_kernel(x_ref, m_ref, w_ref, nw_ref, x1_ref, h_ref):
    x1 = x_ref[...] + _dot(m_ref[...], w_ref[...])
    x1_ref[...] = x1
    h_ref[...] = _rms(x1, nw_ref[...]).astype(h_ref.dtype)


def _outproj(x, merged, w_out, norm_w, tm):
    t, d = x.shape
    pipelined = 2 * _nbytes((tm, d), F32) + 2 * _nbytes((tm, d), BF16) + _nbytes((d, d), BF16)
    return pl.pallas_call(
        _outproj_kernel,
        grid=(t // tm,),
        in_specs=[
            pl.BlockSpec((tm, d), lambda i: (i, 0)),
            pl.BlockSpec((tm, d), lambda i: (i, 0)),
            pl.BlockSpec((d, d), lambda i: (0, 0)),
            pl.BlockSpec((1, d), lambda i: (0, 0)),
        ],
        out_specs=[pl.BlockSpec((tm, d), lambda i: (i, 0)), pl.BlockSpec((tm, d), lambda i: (i, 0))],
        out_shape=[jax.ShapeDtypeStruct((t, d), F32), jax.ShapeDtypeStruct((t, d), BF16)],
        compiler_params=pltpu.CompilerParams(
            dimension_semantics=("parallel",),
            vmem_limit_bytes=_vmem_limit(pipelined, 2 * _nbytes((tm, d), F32))),
        name="outproj",
    )(x, merged, w_out, norm_w)


MLP_RING = 3


def _mlp_kernel(h_ref, x1_ref, wu_hbm, wd_hbm, o_ref, ubuf, dbuf, sem, *, n_k):
    tf = dbuf.shape[1]
    base = pl.program_id(0) * n_k
    n_tiles = pl.num_programs(0) * n_k

    def tile_copies(t):
        slot = lax.rem(t, MLP_RING)
        start = pl.multiple_of(lax.rem(t, n_k) * tf, tf)
        return (pltpu.make_async_copy(wu_hbm.at[:, pl.ds(start, tf)], ubuf.at[slot], sem.at[0, slot]),
                pltpu.make_async_copy(wd_hbm.at[pl.ds(start, tf), :], dbuf.at[slot], sem.at[1, slot]))

    def request(t):
        @pl.when(t < n_tiles)
        def _():
            for copy in tile_copies(t):
                copy.start()

    def wait(t):
        for copy in tile_copies(t):
            copy.wait()

    def up(t):
        a = jnp.maximum(_dot(h_ref[...], ubuf[lax.rem(t, MLP_RING)]), 0.0)
        return (a * a).astype(BF16)

    @pl.when(base == 0)
    def _():
        for t in range(MLP_RING):
            request(base + t)

    o_ref[...] = x1_ref[...]
    wait(base)
    act = up(base)
    for k in range(n_k):
        if k + 1 < n_k:
            wait(base + k + 1)
        o_ref[...] += _dot(act, dbuf[lax.rem(base + k, MLP_RING)])
        if k + 1 < n_k:
            act = up(base + k + 1)
        request(base + k + MLP_RING)


def _mlp(h2, x1, w_up, w_down, tm, tf):
    t, d = x1.shape
    f = w_up.shape[1]
    pipelined = _nbytes((tm, d), BF16) + 2 * _nbytes((tm, d), F32)
    resident = 2 * _nbytes((tm, tf), F32) + MLP_RING * (_nbytes((d, tf), BF16) + _nbytes((tf, d), BF16))
    return pl.pallas_call(
        functools.partial(_mlp_kernel, n_k=f // tf),
        grid=(t // tm,),
        in_specs=[
            pl.BlockSpec((tm, d), lambda i: (i, 0)),
            pl.BlockSpec((tm, d), lambda i: (i, 0)),
            pl.BlockSpec(memory_space=pl.ANY),
            pl.BlockSpec(memory_space=pl.ANY),
        ],
        out_specs=pl.BlockSpec((tm, d), lambda i: (i, 0)),
        out_shape=jax.ShapeDtypeStruct((t, d), F32),
        scratch_shapes=[pltpu.VMEM((MLP_RING, d, tf), BF16), pltpu.VMEM((MLP_RING, tf, d), BF16),
                        pltpu.SemaphoreType.DMA((2, MLP_RING))],
        compiler_params=pltpu.CompilerParams(
            dimension_semantics=("arbitrary",),
            vmem_limit_bytes=_vmem_limit(pipelined, resident)),
        name="mlp",
    )(h2, x1, w_up, w_down)


def _layer(x, mem, attn_norm_w, w_in, dn_conv_w, dn_a_log, dn_dt_bias, dn_out_norm_w, swa_q_norm_w,
           swa_k_norm_w, swa_sinks, rel_bias, mem_norm_w, w_mem_kv, xq_norm_w, xk_norm_w,
           p_dn, p_swa, p_mem, w_out, mlp_norm_w, w_mlp_up, w_mlp_down):
    b, s, d = x.shape
    m = mem.shape[1]
    t = b * s

    dn_heads = dn_a_log.shape[0]
    dn_dim = dn_out_norm_w.shape[0]
    dn_w = dn_heads * dn_dim
    swa_heads = swa_sinks.shape[0]
    swa_dim = swa_q_norm_w.shape[0]
    swa_w = swa_heads * swa_dim
    mem_dim = xq_norm_w.shape[0]
    mem_w = p_mem.shape[0]
    mem_heads = mem_w // mem_dim
    swa_kv_w = (w_in.shape[1] - 4 * dn_w - 2 * dn_heads - swa_w - mem_w - 3 * d) // 2
    swa_kv = swa_kv_w // swa_dim

    n_ba = 2 * dn_heads
    src_ba = 4 * dn_w
    src_swa = src_ba + n_ba
    w_t = w_in.T.astype(BF16)
    w_b = w_t[src_swa:]
    w_c = jnp.concatenate([w_t[src_ba:src_swa], jnp.zeros((LANES - n_ba, d), BF16)], axis=0)
    col = {"qkv": 0, "z": 3 * dn_w, "sq": src_ba}
    col["sk"] = col["sq"] + swa_w
    col["sv"] = col["sk"] + swa_kv_w
    col["mq"] = col["sv"] + swa_kv_w

    proj, gates, ba = _inproj(x.reshape(t, d), attn_norm_w.reshape(1, d), w_t, src_ba, w_b, w_c, 3 * d,
                              tm=2048, tn=512)
    proj3 = proj.reshape(b, s, -1)
    mkv = _norm_matmul(mem.reshape(b * m, d), mem_norm_w.reshape(1, d), w_mem_kv.astype(BF16), tm=512, tn=512)
    mkv3 = mkv.reshape(b, m, -1)

    gate_params = jnp.zeros((SUBLANES, LANES), F32)
    gate_params = gate_params.at[0, dn_heads:2 * dn_heads].set(dn_a_log)
    gate_params = gate_params.at[1, dn_heads:2 * dn_heads].set(dn_dt_bias)
    o_dn, w_up_bf, w_down_bf = _deltanet(proj3, ba.reshape(b, s, LANES), dn_conv_w, gate_params,
                                         dn_out_norm_w.reshape(1, dn_dim), [w_mlp_up, w_mlp_down],
                                         dn_heads, dn_dim, col["z"])
    o_swa = _swa(proj3, rel_bias.reshape(-1), swa_sinks, swa_q_norm_w.reshape(1, swa_dim),
                 swa_k_norm_w.reshape(1, swa_dim), swa_heads, swa_kv, swa_dim, col["sq"], col["sk"], col["sv"])
    o_mem, w_out_bf, p_dn_bf, p_swa_bf, p_mem_bf = _memattn(
        proj3, mkv3, xq_norm_w.reshape(1, mem_dim), xk_norm_w.reshape(1, mem_dim),
        [w_out, p_dn, p_swa, p_mem], mem_heads, mem_dim, col["mq"], tq=1024)

    merged = _merge(o_dn.reshape(t, dn_w), o_swa.reshape(t, swa_w), o_mem.reshape(t, mem_w),
                    p_dn_bf, p_swa_bf, p_mem_bf, gates, tm=1024, tn=1024)
    x1, h2 = _outproj(x.reshape(t, d), merged, w_out_bf, mlp_norm_w.reshape(1, d), tm=512)
    out = _mlp(h2, x1, w_up_bf, w_down_bf, tm=512, tf=1024)
    return out.reshape(b, s, d)


def kernel(x, mem, attn_norm_w, w_in, dn_conv_w, dn_A_log, dn_dt_bias, dn_out_norm_w, swa_q_norm_w,
           swa_k_norm_w, swa_sinks, rel_bias, mem_norm_w, w_mem_kv, xq_norm_w, xk_norm_w, p_dn, p_swa,
           p_mem, w_out, mlp_norm_w, w_mlp_up, w_mlp_down):
    depth = w_in.shape[0]
    for l in range(depth):
        x = _layer(x, mem, attn_norm_w[l], w_in[l], dn_conv_w[l], dn_A_log[l], dn_dt_bias[l],
                   dn_out_norm_w[l], swa_q_norm_w[l], swa_k_norm_w[l], swa_sinks[l], rel_bias,
                   mem_norm_w[l], w_mem_kv[l], xq_norm_w[l], xk_norm_w[l], p_dn[l], p_swa[l], p_mem[l],
                   w_out[l], mlp_norm_w[l], w_mlp_up[l], w_mlp_down[l])
    return x
```

```python
import functools
import math

import jax
import jax.numpy as jnp
from jax import lax
from jax.experimental import pallas as pl
from jax.experimental.pallas import tpu as pltpu

F32 = jnp.float32
BF16 = jnp.bfloat16

EPS = 1e-6
LANES = 128
SUBLANES = 8
V7X_VMEM_BYTES = 64 * 1024 * 1024

DN_CHUNK = 64
DN_GROUP = 256
DN_BASE = 8
SWA_WINDOW = 128
SWA_BLOCK = 128
SWA_STEP_BLOCKS = 4
N_BUCKETS = 32
MAX_DISTANCE = 128

_NT = (((1,), (1,)), ((), ()))
_TN = (((0,), (0,)), ((), ()))


COMPILER_SCRATCH_BYTES = 4 * 1024 * 1024
VMEM_RESERVE_BYTES = 8 * 1024 * 1024


def _vmem_limit(pipelined_bytes, resident_bytes, claim_all=False):
    cap = V7X_VMEM_BYTES - VMEM_RESERVE_BYTES
    want = 2 * pipelined_bytes + resident_bytes + COMPILER_SCRATCH_BYTES
    return int(cap if claim_all else min(want, cap))


def _nbytes(shape, dtype):
    return math.prod(shape) * jnp.dtype(dtype).itemsize


def _sigmoid(v):
    return 0.5 * jnp.tanh(0.5 * v) + 0.5


def _silu(v):
    return v * _sigmoid(v)


def _rms(v, w):
    return (v * lax.rsqrt(jnp.mean(v * v, axis=-1, keepdims=True) + EPS)) * w


def _dot(a, b):
    return jnp.dot(a, b, preferred_element_type=F32)


def _dot_nt(a, b):
    return lax.dot_general(a, b, _NT, preferred_element_type=F32)


def _cast_rider_specs(weights, n_steps, step_of):
    in_specs, out_specs, out_shapes = [], [], []
    for w in weights:
        rows = w.shape[0] // n_steps
        assert rows * n_steps == w.shape[0] and rows % (2 * SUBLANES) == 0
        for specs in (in_specs, out_specs):
            specs.append(pl.BlockSpec((rows, w.shape[1]), lambda *ids: (step_of(*ids), 0)))
        out_shapes.append(jax.ShapeDtypeStruct(w.shape, BF16))
    return in_specs, out_specs, out_shapes


def _cast_riders(in_refs, out_refs):
    for i_ref, o_ref in zip(in_refs, out_refs):
        o_ref[...] = i_ref[...].astype(o_ref.dtype)


IN_RING = 4


def _inproj_kernel(x_ref, nw_ref, wa_hbm, wb_hbm, wc_ref, o_ref, g_ref, ba_ref, h_ref, wbuf, sem,
                   *, n_a, n_p, tail, rows_b1):
    i, j = pl.program_id(0), pl.program_id(1)
    n_j = pl.num_programs(1)
    tn = wbuf.shape[1]
    step = i * n_j + j
    n_steps = pl.num_programs(0) * n_j

    def tile_copy(col_step, slot, from_b):
        if from_b:
            row = jnp.where(col_step < n_p, (col_step - n_a) * tn, rows_b1 + (col_step - n_p) * tn)
            src = wb_hbm.at[pl.ds(pl.multiple_of(row, 2 * SUBLANES), tn), :]
        else:
            src = wa_hbm.at[pl.ds(pl.multiple_of(col_step * tn, tn), tn), :]
        return pltpu.make_async_copy(src, wbuf.at[slot], sem.at[slot])

    def request(ahead):
        target = step + ahead
        col_step = lax.rem(target, n_j)
        slot = lax.rem(target, IN_RING)

        @pl.when((target < n_steps) & (col_step < n_a))
        def _():
            tile_copy(col_step, slot, from_b=False).start()

        @pl.when((target < n_steps) & (col_step >= n_a))
        def _():
            tile_copy(col_step, slot, from_b=True).start()

    @pl.when(step == 0)
    def _():
        for ahead in range(IN_RING - 1):
            request(ahead)

    request(IN_RING - 1)
    slot = lax.rem(step, IN_RING)
    pltpu.make_async_copy(wb_hbm.at[pl.ds(0, tn), :], wbuf.at[slot], sem.at[slot]).wait()
    w_ref = wbuf.at[slot]

    @pl.when(j == 0)
    def _():
        h_ref[...] = _rms(x_ref[...], nw_ref[...]).astype(BF16)
        ba_ref[...] = _dot_nt(h_ref[...], wc_ref[...])

    @pl.when(j < n_p - 1)
    def _():
        o_ref[...] = _dot_nt(h_ref[...], w_ref[...])

    @pl.when(j == n_p - 1)
    def _():
        o_ref[:, :tail] = _dot_nt(h_ref[...], w_ref[:tail, :])

    @pl.when(j >= n_p)
    def _():
        g_ref[...] = _sigmoid(_dot_nt(h_ref[...], w_ref[...])).astype(g_ref.dtype)


def _inproj(x, nw, w_a, rows_a, w_b, w_c, n_gates, tm, tn):
    m, k = x.shape
    rows_b1 = w_b.shape[0] - n_gates
    n = rows_a + rows_b1
    n_a = rows_a // tn
    n_p = pl.cdiv(n, tn)
    assert rows_a % tn == 0 and n_gates % tn == 0 and rows_b1 % (2 * SUBLANES) == 0

    pipelined = (_nbytes((tm, k), F32) + _nbytes((tm, tn), F32) + _nbytes((tm, tn), BF16)
                 + _nbytes((tm, LANES), F32))
    resident = (_nbytes((tm, k), BF16) + _nbytes((tm, k), F32) + _nbytes((LANES, k), BF16)
                + IN_RING * _nbytes((tn, k), BF16))
    return pl.pallas_call(
        functools.partial(_inproj_kernel, n_a=n_a, n_p=n_p, tail=n - (n_p - 1) * tn, rows_b1=rows_b1),
        grid=(m // tm, n_p + n_gates // tn),
        in_specs=[
            pl.BlockSpec((tm, k), lambda i, j: (i, 0)),
            pl.BlockSpec((1, k), lambda i, j: (0, 0)),
            pl.BlockSpec(memory_space=pl.ANY),
            pl.BlockSpec(memory_space=pl.ANY),
            pl.BlockSpec((LANES, k), lambda i, j: (0, 0)),
        ],
        out_specs=[pl.BlockSpec((tm, tn), lambda i, j: (i, jnp.minimum(j, n_p - 1))),
                   pl.BlockSpec((tm, tn), lambda i, j: (i, jnp.maximum(j - n_p, 0))),
                   pl.BlockSpec((tm, LANES), lambda i, j: (i, 0))],
        out_shape=[jax.ShapeDtypeStruct((m, n), F32), jax.ShapeDtypeStruct((m, n_gates), BF16),
                   jax.ShapeDtypeStruct((m, LANES), F32)],
        scratch_shapes=[pltpu.VMEM((tm, k), BF16), pltpu.VMEM((IN_RING, tn, k), BF16),
                        pltpu.SemaphoreType.DMA((IN_RING,))],
        compiler_params=pltpu.CompilerParams(
            dimension_semantics=("arbitrary", "arbitrary"),
            vmem_limit_bytes=_vmem_limit(pipelined, resident)),
        name="inproj",
    )(x, nw, w_a, w_b, w_c)


def _norm_matmul_kernel(x_ref, nw_ref, w_ref, o_ref, h_ref):
    @pl.when(pl.program_id(1) == 0)
    def _():
        h_ref[...] = _rms(x_ref[...], nw_ref[...]).astype(BF16)

    o_ref[...] = _dot(h_ref[...], w_ref[...]).astype(o_ref.dtype)


def _norm_matmul(x, nw, w, tm, tn):
    m, k = x.shape
    n = w.shape[1]
    pipelined = _nbytes((tm, k), F32) + _nbytes((k, tn), BF16) + _nbytes((tm, tn), F32)
    resident = _nbytes((tm, k), BF16) + _nbytes((tm, k), F32)
    return pl.pallas_call(
        _norm_matmul_kernel,
        grid=(m // tm, n // tn),
        in_specs=[
            pl.BlockSpec((tm, k), lambda i, j: (i, 0)),
            pl.BlockSpec((1, k), lambda i, j: (0, 0)),
            pl.BlockSpec((k, tn), lambda i, j: (0, j)),
        ],
        out_specs=pl.BlockSpec((tm, tn), lambda i, j: (i, j)),
        out_shape=jax.ShapeDtypeStruct((m, n), F32),
        scratch_shapes=[pltpu.VMEM((tm, k), BF16)],
        compiler_params=pltpu.CompilerParams(
            dimension_semantics=("parallel", "arbitrary"),
            vmem_limit_bytes=_vmem_limit(pipelined, resident)),
        name="norm_matmul",
    )(x, nw, w)


def _dn_kernel(*refs, heads, dim, n_riders):
    qkv_ref, z_ref, ba_ref, cw_ref, gp_ref, onw_ref = refs[:6]
    o_ref = refs[6 + n_riders]
    ext_ref, state_ref = refs[-2:]
    _cast_riders(refs[6:6 + n_riders], refs[7 + n_riders:7 + 2 * n_riders])

    gt = DN_GROUP
    c = DN_CHUNK
    hd = heads * dim
    hs = range(heads)
    chunks = range(gt // c)

    @pl.when(pl.program_id(1) == 0)
    def _():
        ext_ref[0:SUBLANES, :] = jnp.zeros((SUBLANES, 3 * hd), F32)
        state_ref[...] = jnp.zeros_like(state_ref)

    ext_ref[SUBLANES:SUBLANES + gt, :] = qkv_ref[...]
    xe = ext_ref[...]
    conv = cw_ref[3:4, :] * xe[SUBLANES:]
    for s in (1, 2, 3):
        conv = conv + cw_ref[3 - s:4 - s, :] * pltpu.roll(xe, s, 0)[SUBLANES:]
    ext_ref[0:SUBLANES, :] = xe[gt:gt + SUBLANES]
    act = _silu(conv)

    ba = ba_ref[...]
    beta_all = _sigmoid(ba)
    xa = ba + gp_ref[1:2, :]
    softplus = jnp.maximum(xa, 0.0) + jnp.log1p(jnp.exp(-jnp.abs(xa)))
    g_all = -jnp.exp(gp_ref[0:1, :]) * softplus
    row_in_chunk = lax.broadcasted_iota(jnp.int32, (gt, LANES), 0) & (c - 1)
    gcum = g_all
    s = 1
    while s < c:
        gcum = gcum + jnp.where(row_in_chunk >= s, pltpu.roll(gcum, s, 0), 0.0)
        s *= 2
    exp_g = jnp.exp(gcum)
    kdec_parts, gc_rows = [], []
    for j in chunks:
        g_last = gcum[c * j + c - 1:c * j + c, :]
        kdec_parts.append(jnp.exp(g_last - gcum[c * j:c * (j + 1), :]))
        gc_rows.append(jnp.exp(g_last))
    kdec = jnp.concatenate(kdec_parts, axis=0)
    gcum_t = gcum.T

    ri = lax.broadcasted_iota(jnp.int32, (gt, gt), 0)
    ci = lax.broadcasted_iota(jnp.int32, (gt, gt), 1)
    same_chunk = (ri // c) == (ci // c)
    strict = same_chunk & (ri > ci)
    incl = same_chunk & (ri >= ci)

    qn, kn, kb, beta, eg = [], [], [], [], []
    for h in hs:
        qh = act[:, h * dim:(h + 1) * dim]
        kh = act[:, hd + h * dim:hd + (h + 1) * dim]
        qn.append(qh * lax.rsqrt(jnp.sum(qh * qh, axis=-1, keepdims=True) + EPS) * (dim ** -0.5))
        kn.append(kh * lax.rsqrt(jnp.sum(kh * kh, axis=-1, keepdims=True) + EPS))
        kb.append(kn[h].astype(BF16))
        beta.append(beta_all[:, h:h + 1])
        eg.append(exp_g[:, heads + h:heads + h + 1])

    decay = [jnp.exp(jnp.where(incl, gcum[:, heads + h:heads + h + 1] - gcum_t[heads + h:heads + h + 1, :],
                               -jnp.inf)) for h in hs]
    kk = [_dot_nt(kb[h], kb[h]) for h in hs]
    nmat = [jnp.where(strict, (beta[h] * kk[h]) * decay[h], 0.0) for h in hs]

    def same_block(size):
        return (ri // size) == (ci // size)

    eye = jnp.where(ri == ci, 1.0, 0.0)
    base = same_block(DN_BASE)
    pw = [jnp.where(base, nmat[h], 0.0) for h in hs]
    tinv = [eye - pw[h] for h in hs]
    pw = [pw[h].astype(BF16) for h in hs]
    order = 2
    while order < DN_BASE:
        pw = [_dot(pw[h], pw[h]).astype(BF16) for h in hs]
        tinv = [tinv[h] + _dot(tinv[h].astype(BF16), pw[h]) for h in hs]
        order *= 2
    size = DN_BASE
    while size < c:
        level = same_block(2 * size) & jnp.logical_not(same_block(size))
        off = [jnp.where(level, nmat[h], 0.0).astype(BF16) for h in hs]
        ct = [_dot(off[h], tinv[h].astype(BF16)).astype(BF16) for h in hs]
        tinv = [tinv[h] - _dot(tinv[h].astype(BF16), ct[h]) for h in hs]
        size *= 2

    sol = [jnp.concatenate([act[:, 2 * hd + h * dim:2 * hd + (h + 1) * dim] * beta[h],
                            kn[h] * (beta[h] * eg[h])], axis=1) for h in hs]
    sol = [sol[h] + _dot((tinv[h] - eye).astype(BF16), sol[h].astype(BF16)) for h in hs]

    pmat = [(_dot_nt(qn[h].astype(BF16), kb[h]) * decay[h]).astype(BF16) for h in hs]
    wq = [[jnp.concatenate([sol[h][c * j:c * (j + 1), dim:], (qn[h] * eg[h])[c * j:c * (j + 1)]],
                           axis=0).astype(BF16) for j in chunks] for h in hs]
    kd = [(kn[h] * kdec[:, heads + h:heads + h + 1]).astype(BF16) for h in hs]

    state = [state_ref[h] for h in hs]
    o_parts = [[] for _ in hs]
    for j in chunks:
        r = slice(c * j, c * (j + 1))
        ws = [_dot(wq[h][j], state[h].astype(BF16)) for h in hs]
        db = [(sol[h][r, :dim] - ws[h][:c]).astype(BF16) for h in hs]
        for h in hs:
            o_parts[h].append(ws[h][c:] + _dot(pmat[h][r, r], db[h]))
        state = [gc_rows[j][:, heads + h:heads + h + 1] * state[h]
                 + lax.dot_general(kd[h][r], db[h], _TN, preferred_element_type=F32) for h in hs]
    for h in hs:
        state_ref[h] = state[h]

    for h in hs:
        o = jnp.concatenate(o_parts[h], axis=0)
        zh = z_ref[:, h * dim:(h + 1) * dim]
        o_ref[:, h * dim:(h + 1) * dim] = (_rms(o, onw_ref[...]) * _silu(zh)).astype(o_ref.dtype)


def _deltanet(proj3, ba3, conv_w, gate_params, out_norm_w, riders, heads, dim, col_z):
    b, s, _ = proj3.shape
    hd = heads * dim
    gt = DN_GROUP
    ng = s // gt
    rider_in, rider_out, rider_shapes = _cast_rider_specs(riders, b * ng, lambda i, g: i * ng + g)
    pipelined = (_nbytes((gt, 3 * hd), F32) + _nbytes((gt, hd), F32) + _nbytes((gt, LANES), F32)
                 + _nbytes((gt, hd), BF16) + sum(_nbytes(w.shape, F32) * 3 // 2 for w in riders) // (b * ng))
    resident = (_nbytes((gt + SUBLANES, 3 * hd), F32) * 4 + _nbytes((heads, dim, dim), F32)
                + heads * 6 * _nbytes((gt, gt), F32))
    return pl.pallas_call(
        functools.partial(_dn_kernel, heads=heads, dim=dim, n_riders=len(riders)),
        grid=(b, ng),
        in_specs=[
            pl.BlockSpec((None, gt, 3 * hd), lambda i, g: (i, g, 0)),
            pl.BlockSpec((None, gt, hd), lambda i, g: (i, g, col_z // hd)),
            pl.BlockSpec((None, gt, LANES), lambda i, g: (i, g, 0)),
            pl.BlockSpec((4, 3 * hd), lambda i, g: (0, 0)),
            pl.BlockSpec((SUBLANES, LANES), lambda i, g: (0, 0)),
            pl.BlockSpec((1, dim), lambda i, g: (0, 0)),
            *rider_in,
        ],
        out_specs=[pl.BlockSpec((None, gt, hd), lambda i, g: (i, g, 0)), *rider_out],
        out_shape=[jax.ShapeDtypeStruct((b, s, hd), BF16), *rider_shapes],
        scratch_shapes=[pltpu.VMEM((gt + SUBLANES, 3 * hd), F32), pltpu.VMEM((heads, dim, dim), F32)],
        compiler_params=pltpu.CompilerParams(
            dimension_semantics=("parallel", "arbitrary"),
            vmem_limit_bytes=_vmem_limit(pipelined, resident, claim_all=True)),
        name="deltanet",
    )(proj3, proj3, ba3, conv_w, gate_params, out_norm_w, *riders)


def _swa_kernel(rb_ref, sink_ref, q_ref, kc_ref, kp_ref, vc_ref, vp_ref, qw_ref, kw_ref, o_ref, bias_ref,
                *, q_heads, kv_heads, dim):
    blk = SWA_BLOCK
    n = pl.program_id(1)
    qi = lax.broadcasted_iota(jnp.int32, (blk, 2 * blk), 0)
    kj = lax.broadcasted_iota(jnp.int32, (blk, 2 * blk), 1)
    dist = qi - kj + blk

    @pl.when((pl.program_id(0) == 0) & (n == 0))
    def _():
        max_exact = N_BUCKETS // 2
        nn = jnp.maximum(dist, 0)
        nf = jnp.maximum(nn, 1).astype(F32)
        large = max_exact + (jnp.log(nf / max_exact) / math.log(MAX_DISTANCE / max_exact)
                             * (N_BUCKETS - max_exact)).astype(jnp.int32)
        bucket = jnp.where(nn < max_exact, nn, jnp.minimum(large, N_BUCKETS - 1))
        for h in range(q_heads):
            acc = jnp.zeros((blk, 2 * blk), F32)
            for bk in range(N_BUCKETS):
                acc = jnp.where(bucket == bk, rb_ref[bk * q_heads + h], acc)
            acc = jnp.where((dist >= 0) & (dist < SWA_WINDOW), acc, -jnp.inf)
            bias_ref[h] = acc
            bias_ref[q_heads + h] = jnp.where(kj >= blk, acc, -jnp.inf)

    first = jnp.where(n == 0, q_heads, 0)
    group = q_heads // kv_heads
    for j in range(kv_heads):
        cols = slice(j * dim, (j + 1) * dim)
        k_all = _rms(jnp.concatenate([kp_ref[:, cols], kc_ref[:, cols]], axis=0), kw_ref[...]).astype(BF16)
        v_all = jnp.concatenate([vp_ref[:, cols], vc_ref[:, cols]], axis=0).astype(BF16)
        for u in range(SWA_STEP_BLOCKS):
            rows = slice(u * blk, (u + 1) * blk)
            kwin, vwin = k_all[u * blk:(u + 2) * blk], v_all[u * blk:(u + 2) * blk]
            qcat = jnp.concatenate(
                [_rms(q_ref[rows, (j * group + i) * dim:(j * group + i + 1) * dim], qw_ref[...])
                 for i in range(group)], axis=0).astype(BF16)
            logits = _dot_nt(qcat, kwin) * (dim ** -0.5)
            for i in range(group):
                h = j * group + i
                lg = logits[i * blk:(i + 1) * blk] + bias_ref[(first if u == 0 else 0) + h]
                sink = sink_ref[h]
                mx = jnp.maximum(jnp.max(lg, axis=-1, keepdims=True), sink)
                e = jnp.exp(lg - mx)
                den = jnp.sum(e, axis=-1, keepdims=True) + jnp.exp(sink - mx)
                pv = _dot(e.astype(BF16), vwin)
                o_ref[rows, h * dim:(h + 1) * dim] = (pv / den).astype(o_ref.dtype)


def _swa(proj3, rel_bias_flat, sinks, q_norm_w, k_norm_w, q_heads, kv_heads, dim, col_q, col_k, col_v):
    b, s, _ = proj3.shape
    blk = SWA_BLOCK
    step = SWA_STEP_BLOCKS * blk
    qw, kvw = q_heads * dim, kv_heads * dim
    smem = pl.BlockSpec(memory_space=pltpu.SMEM)
    pipelined = (_nbytes((step, qw), F32) + 2 * _nbytes((step + blk, kvw), F32) + _nbytes((step, qw), BF16))
    resident = (_nbytes((2 * q_heads, blk, 2 * blk), F32) + 2 * pipelined
                + 2 * SWA_STEP_BLOCKS * _nbytes((q_heads // kv_heads * blk, 2 * blk), F32))

    def prev(n):
        return jnp.maximum(SWA_STEP_BLOCKS * n - 1, 0)

    return pl.pallas_call(
        functools.partial(_swa_kernel, q_heads=q_heads, kv_heads=kv_heads, dim=dim),
        grid=(b, s // step),
        in_specs=[
            smem, smem,
            pl.BlockSpec((None, step, qw), lambda i, n: (i, n, col_q // qw)),
            pl.BlockSpec((None, step, kvw), lambda i, n: (i, n, col_k // kvw)),
            pl.BlockSpec((None, blk, kvw), lambda i, n: (i, prev(n), col_k // kvw)),
            pl.BlockSpec((None, step, kvw), lambda i, n: (i, n, col_v // kvw)),
            pl.BlockSpec((None, blk, kvw), lambda i, n: (i, prev(n), col_v // kvw)),
            pl.BlockSpec((1, dim), lambda i, n: (0, 0)),
            pl.BlockSpec((1, dim), lambda i, n: (0, 0)),
        ],
        out_specs=pl.BlockSpec((None, step, qw), lambda i, n: (i, n, 0)),
        out_shape=jax.ShapeDtypeStruct((b, s, qw), BF16),
        scratch_shapes=[pltpu.VMEM((2 * q_heads, blk, 2 * blk), F32)],
        compiler_params=pltpu.CompilerParams(
            dimension_semantics=("arbitrary", "arbitrary"),
            vmem_limit_bytes=_vmem_limit(pipelined, resident, claim_all=True)),
        name="swa",
    )(rel_bias_flat, sinks, proj3, proj3, proj3, proj3, proj3, q_norm_w, k_norm_w)


RIDER_RING = 3


def _cast_riders_ringed(hbm_refs, out_refs, bufs, sem, step, n_steps):
    def copies(target):
        slot = lax.rem(target, RIDER_RING)
        out = []
        for r, (hbm, buf) in enumerate(zip(hbm_refs, bufs)):
            rows = buf.shape[1]
            src = hbm.at[pl.ds(pl.multiple_of(target * rows, rows), rows), :]
            out.append(pltpu.make_async_copy(src, buf.at[slot], sem.at[r, slot]))
        return out

    def request(target):
        @pl.when(target < n_steps)
        def _():
            for copy in copies(target):
                copy.start()

    @pl.when(step == 0)
    def _():
        for ahead in range(RIDER_RING - 1):
            request(step + ahead)

    request(step + RIDER_RING - 1)
    for copy in copies(step):
        copy.wait()
    slot = lax.rem(step, RIDER_RING)
    for buf, out in zip(bufs, out_refs):
        out[...] = buf[slot].astype(out.dtype)


def _memattn_kernel(*refs, heads, dim, n_riders):
    q_ref, k_ref, v_ref, qw_ref, kw_ref = refs[:5]
    o_ref = refs[5 + n_riders]
    n_t = pl.num_programs(1)
    _cast_riders_ringed(refs[5:5 + n_riders], refs[6 + n_riders:6 + 2 * n_riders],
                        refs[6 + 2 * n_riders:6 + 3 * n_riders], refs[-1],
                        pl.program_id(0) * n_t + pl.program_id(1), pl.num_programs(0) * n_t)
    for h in range(heads):
        cols = slice(h * dim, (h + 1) * dim)
        qn = _rms(q_ref[:, cols], qw_ref[...]).astype(BF16)
        kn = _rms(k_ref[:, cols], kw_ref[...]).astype(BF16)
        lg = _dot_nt(qn, kn) * (dim ** -0.5)
        e = jnp.exp(lg - jnp.max(lg, axis=-1, keepdims=True))
        den = jnp.sum(e, axis=-1, keepdims=True)
        pv = _dot(e.astype(BF16), v_ref[:, cols].astype(BF16))
        o_ref[:, cols] = (pv / den).astype(o_ref.dtype)


def _memattn(proj3, mkv3, q_norm_w, k_norm_w, riders, heads, dim, col_q, tq):
    b, s, _ = proj3.shape
    m = mkv3.shape[1]
    w = heads * dim
    nt = s // tq
    _, rider_out, rider_shapes = _cast_rider_specs(riders, b * nt, lambda i, t: i * nt + t)
    slabs = [(r.shape[0] // (b * nt), r.shape[1]) for r in riders]
    pipelined = (_nbytes((tq, w), F32) + 2 * _nbytes((m, w), F32) + _nbytes((tq, w), BF16)
                 + sum(_nbytes(slab, BF16) for slab in slabs))
    ring_bytes = RIDER_RING * sum(_nbytes(slab, F32) for slab in slabs)
    return pl.pallas_call(
        functools.partial(_memattn_kernel, heads=heads, dim=dim, n_riders=len(riders)),
        grid=(b, nt),
        in_specs=[
            pl.BlockSpec((pl.Element(tq), pl.Element(w)), lambda i, t: (pl.multiple_of(i * s + t * tq, tq), col_q)),
            pl.BlockSpec((None, m, w), lambda i, t: (i, 0, 0)),
            pl.BlockSpec((None, m, w), lambda i, t: (i, 0, 1)),
            pl.BlockSpec((1, dim), lambda i, t: (0, 0)),
            pl.BlockSpec((1, dim), lambda i, t: (0, 0)),
            *([pl.BlockSpec(memory_space=pl.ANY)] * len(riders)),
        ],
        out_specs=[pl.BlockSpec((None, tq, w), lambda i, t: (i, t, 0)), *rider_out],
        out_shape=[jax.ShapeDtypeStruct((b, s, w), BF16), *rider_shapes],
        scratch_shapes=[*[pltpu.VMEM((RIDER_RING, *slab), F32) for slab in slabs],
                        pltpu.SemaphoreType.DMA((len(riders), RIDER_RING))],
        compiler_params=pltpu.CompilerParams(
            dimension_semantics=("arbitrary", "arbitrary"),
            vmem_limit_bytes=_vmem_limit(pipelined, 2 * heads * _nbytes((tq, m), F32) + ring_bytes,
                                         claim_all=True)),
        name="memattn",
    )(proj3.reshape(b * s, -1), mkv3, mkv3, q_norm_w, k_norm_w, *riders)


def _merge_kernel(od_ref, os_ref, om_ref, pd_ref, ps_ref, pm_ref, gd_ref, gs_ref, gm_ref, o_ref):
    merged = (gd_ref[...].astype(F32) * _dot(od_ref[...], pd_ref[...])
              + gs_ref[...].astype(F32) * _dot(os_ref[...], ps_ref[...])
              + gm_ref[...].astype(F32) * _dot(om_ref[...], pm_ref[...]))
    o_ref[...] = merged.astype(o_ref.dtype)


def _merge(o_dn, o_swa, o_mem, p_dn, p_swa, p_mem, gates, tm, tn):
    t = o_dn.shape[0]
    d = p_dn.shape[1]
    wd, ws, wm = o_dn.shape[1], o_swa.shape[1], o_mem.shape[1]
    per_branch = d // tn
    pipelined = ((_nbytes((tm, wd), BF16) + _nbytes((tm, ws), BF16) + _nbytes((tm, wm), BF16))
                 + (_nbytes((wd, tn), BF16) + _nbytes((ws, tn), BF16) + _nbytes((wm, tn), BF16))
                 + 4 * _nbytes((tm, tn), BF16))
    return pl.pallas_call(
        _merge_kernel,
        grid=(d // tn, t // tm),
        in_specs=[
            pl.BlockSpec((tm, wd), lambda j, i: (i, 0)),
            pl.BlockSpec((tm, ws), lambda j, i: (i, 0)),
            pl.BlockSpec((tm, wm), lambda j, i: (i, 0)),
            pl.BlockSpec((wd, tn), lambda j, i: (0, j)),
            pl.BlockSpec((ws, tn), lambda j, i: (0, j)),
            pl.BlockSpec((wm, tn), lambda j, i: (0, j)),
            pl.BlockSpec((tm, tn), lambda j, i: (i, j)),
            pl.BlockSpec((tm, tn), lambda j, i: (i, per_branch + j)),
            pl.BlockSpec((tm, tn), lambda j, i: (i, 2 * per_branch + j)),
        ],
        out_specs=pl.BlockSpec((tm, tn), lambda j, i: (i, j)),
        out_shape=jax.ShapeDtypeStruct((t, d), BF16),
        compiler_params=pltpu.CompilerParams(
            dimension_semantics=("parallel", "parallel"),
            vmem_limit_bytes=_vmem_limit(pipelined, 3 * _nbytes((tm, tn), F32))),
        name="merge",
    )(o_dn, o_swa, o_mem, p_dn, p_swa, p_mem, gates, gates, gates)


def _outproj_kernel(x_ref, m_ref, w_ref, nw_ref, x1_ref, h_ref):
    x1 = x_ref[...] + _dot(m_ref[...], w_ref[...])
    x1_ref[...] = x1
    h_ref[...] = _rms(x1, nw_ref[...]).astype(h_ref.dtype)


def _outproj(x, merged, w_out, norm_w, tm):
    t, d = x.shape
    pipelined = 2 * _nbytes((tm, d), F32) + 2 * _nbytes((tm, d), BF16) + _nbytes((d, d), BF16)
    return pl.pallas_call(
        _outproj_kernel,
        grid=(t // tm,),
        in_specs=[
            pl.BlockSpec((tm, d), lambda i: (i, 0)),
            pl.BlockSpec((tm, d), lambda i: (i, 0)),
            pl.BlockSpec((d, d), lambda i: (0, 0)),
            pl.BlockSpec((1, d), lambda i: (0, 0)),
        ],
        out_specs=[pl.BlockSpec((tm, d), lambda i: (i, 0)), pl.BlockSpec((tm, d), lambda i: (i, 0))],
        out_shape=[jax.ShapeDtypeStruct((t, d), F32), jax.ShapeDtypeStruct((t, d), BF16)],
        compiler_params=pltpu.CompilerParams(
            dimension_semantics=("parallel",),
            vmem_limit_bytes=_vmem_limit(pipelined, 2 * _nbytes((tm, d), F32))),
        name="outproj",
    )(x, merged, w_out, norm_w)


MLP_RING = 3


def _mlp_kernel(h_ref, x1_ref, wu_hbm, wd_hbm, o_ref, ubuf, dbuf, sem, *, n_k):
    tf = dbuf.shape[1]
    base = pl.program_id(0) * n_k
    n_tiles = pl.num_programs(0) * n_k

    def tile_copies(t):
        slot = lax.rem(t, MLP_RING)
        start = pl.multiple_of(lax.rem(t, n_k) * tf, tf)
        return (pltpu.make_async_copy(wu_hbm.at[:, pl.ds(start, tf)], ubuf.at[slot], sem.at[0, slot]),
                pltpu.make_async_copy(wd_hbm.at[pl.ds(start, tf), :], dbuf.at[slot], sem.at[1, slot]))

    def request(t):
        @pl.when(t < n_tiles)
        def _():
            for copy in tile_copies(t):
                copy.start()

    def wait(t):
        for copy in tile_copies(t):
            copy.wait()

    def up(t):
        a = jnp.maximum(_dot(h_ref[...], ubuf[lax.rem(t, MLP_RING)]), 0.0)
        return (a * a).astype(BF16)

    @pl.when(base == 0)
    def _():
        for t in range(MLP_RING):
            request(base + t)

    o_ref[...] = x1_ref[...]
    wait(base)
    act = up(base)
    for k in range(n_k):
        if k + 1 < n_k:
            wait(base + k + 1)
        o_ref[...] += _dot(act, dbuf[lax.rem(base + k, MLP_RING)])
        if k + 1 < n_k:
            act = up(base + k + 1)
        request(base + k + MLP_RING)


def _mlp(h2, x1, w_up, w_down, tm, tf):
    t, d = x1.shape
    f = w_up.shape[1]
    pipelined = _nbytes((tm, d), BF16) + 2 * _nbytes((tm, d), F32)
    resident = 2 * _nbytes((tm, tf), F32) + MLP_RING * (_nbytes((d, tf), BF16) + _nbytes((tf, d), BF16))
    return pl.pallas_call(
        functools.partial(_mlp_kernel, n_k=f // tf),
        grid=(t // tm,),
        in_specs=[
            pl.BlockSpec((tm, d), lambda i: (i, 0)),
            pl.BlockSpec((tm, d), lambda i: (i, 0)),
            pl.BlockSpec(memory_space=pl.ANY),
            pl.BlockSpec(memory_space=pl.ANY),
        ],
        out_specs=pl.BlockSpec((tm, d), lambda i: (i, 0)),
        out_shape=jax.ShapeDtypeStruct((t, d), F32),
        scratch_shapes=[pltpu.VMEM((MLP_RING, d, tf), BF16), pltpu.VMEM((MLP_RING, tf, d), BF16),
                        pltpu.SemaphoreType.DMA((2, MLP_RING))],
        compiler_params=pltpu.CompilerParams(
            dimension_semantics=("arbitrary",),
            vmem_limit_bytes=_vmem_limit(pipelined, resident)),
        name="mlp",
    )(h2, x1, w_up, w_down)


def _layer(x, mem, attn_norm_w, w_in, dn_conv_w, dn_a_log, dn_dt_bias, dn_out_norm_w, swa_q_norm_w,
           swa_k_norm_w, swa_sinks, rel_bias, mem_norm_w, w_mem_kv, xq_norm_w, xk_norm_w,
           p_dn, p_swa, p_mem, w_out, mlp_norm_w, w_mlp_up, w_mlp_down):
    b, s, d = x.shape
    m = mem.shape[1]
    t = b * s

    dn_heads = dn_a_log.shape[0]
    dn_dim = dn_out_norm_w.shape[0]
    dn_w = dn_heads * dn_dim
    swa_heads = swa_sinks.shape[0]
    swa_dim = swa_q_norm_w.shape[0]
    swa_w = swa_heads * swa_dim
    mem_dim = xq_norm_w.shape[0]
    mem_w = p_mem.shape[0]
    mem_heads = mem_w // mem_dim
    swa_kv_w = (w_in.shape[1] - 4 * dn_w - 2 * dn_heads - swa_w - mem_w - 3 * d) // 2
    swa_kv = swa_kv_w // swa_dim

    n_ba = 2 * dn_heads
    src_ba = 4 * dn_w
    src_swa = src_ba + n_ba
    start_a, start_b = lax.optimization_barrier((jnp.int32(0), jnp.int32(src_swa)))
    w_t = lax.dynamic_slice_in_dim(w_in, start_a, src_swa, axis=1).T.astype(BF16)
    w_b = lax.dynamic_slice_in_dim(w_in, start_b, w_in.shape[1] - src_swa, axis=1).T.astype(BF16)
    w_c = jnp.concatenate([w_t[src_ba:src_swa], jnp.zeros((LANES - n_ba, d), BF16)], axis=0)
    col = {"qkv": 0, "z": 3 * dn_w, "sq": src_ba}
    col["sk"] = col["sq"] + swa_w
    col["sv"] = col["sk"] + swa_kv_w
    col["mq"] = col["sv"] + swa_kv_w

    proj, gates, ba = _inproj(x.reshape(t, d), attn_norm_w.reshape(1, d), w_t, src_ba, w_b, w_c, 3 * d,
                              tm=1024, tn=1024)
    proj3 = proj.reshape(b, s, -1)
    mkv = _norm_matmul(mem.reshape(b * m, d), mem_norm_w.reshape(1, d), w_mem_kv.astype(BF16), tm=512, tn=512)
    mkv3 = mkv.reshape(b, m, -1)

    gate_params = jnp.zeros((SUBLANES, LANES), F32)
    gate_params = gate_params.at[0, dn_heads:2 * dn_heads].set(dn_a_log)
    gate_params = gate_params.at[1, dn_heads:2 * dn_heads].set(dn_dt_bias)
    o_dn, w_up_bf, w_down_bf = _deltanet(proj3, ba.reshape(b, s, LANES), dn_conv_w, gate_params,
                                         dn_out_norm_w.reshape(1, dn_dim), [w_mlp_up, w_mlp_down],
                                         dn_heads, dn_dim, col["z"])
    o_swa = _swa(proj3, rel_bias.reshape(-1), swa_sinks, swa_q_norm_w.reshape(1, swa_dim),
                 swa_k_norm_w.reshape(1, swa_dim), swa_heads, swa_kv, swa_dim, col["sq"], col["sk"], col["sv"])
    o_mem, w_out_bf, p_dn_bf, p_swa_bf, p_mem_bf = _memattn(
        proj3, mkv3, xq_norm_w.reshape(1, mem_dim), xk_norm_w.reshape(1, mem_dim),
        [w_out, p_dn, p_swa, p_mem], mem_heads, mem_dim, col["mq"], tq=1024)

    merged = _merge(o_dn.reshape(t, dn_w), o_swa.reshape(t, swa_w), o_mem.reshape(t, mem_w),
                    p_dn_bf, p_swa_bf, p_mem_bf, gates, tm=1024, tn=1024)
    x1, h2 = _outproj(x.reshape(t, d), merged, w_out_bf, mlp_norm_w.reshape(1, d), tm=512)
    out = _mlp(h2, x1, w_up_bf, w_down_bf, tm=512, tf=1024)
    return out.reshape(b, s, d)


def kernel(x, mem, attn_norm_w, w_in, dn_conv_w, dn_A_log, dn_dt_bias, dn_out_norm_w, swa_q_norm_w,
           swa_k_norm_w, swa_sinks, rel_bias, mem_norm_w, w_mem_kv, xq_norm_w, xk_norm_w, p_dn, p_swa,
           p_mem, w_out, mlp_norm_w, w_mlp_up, w_mlp_down):
    depth = w_in.shape[0]
    for l in range(depth):
        x = _layer(x, mem, attn_norm_w[l], w_in[l], dn_conv_w[l], dn_A_log[l], dn_dt_bias[l],
                   dn_out_norm_w[l], swa_q_norm_w[l], swa_k_norm_w[l], swa_sinks[l], rel_bias,
                   mem_norm_w[l], w_mem_kv[l], xq_norm_w[l], xk_norm_w[l], p_dn[l], p_swa[l], p_mem[l],
                   w_out[l], mlp_norm_w[l], w_mlp_up[l], w_mlp_down[l])
    return x
```

```python
import functools
import math

import jax
import jax.numpy as jnp
from jax import lax
from jax.experimental import pallas as pl
from jax.experimental.pallas import tpu as pltpu

F32 = jnp.float32
BF16 = jnp.bfloat16

EPS = 1e-6
LANES = 128
SUBLANES = 8
V7X_VMEM_BYTES = 64 * 1024 * 1024

DN_CHUNK = 64
DN_GROUP = 256
DN_BASE = 8
SWA_WINDOW = 128
SWA_BLOCK = 128
SWA_STEP_BLOCKS = 4
N_BUCKETS = 32
MAX_DISTANCE = 128

_NT = (((1,), (1,)), ((), ()))
_TN = (((0,), (0,)), ((), ()))


COMPILER_SCRATCH_BYTES = 4 * 1024 * 1024
VMEM_RESERVE_BYTES = 8 * 1024 * 1024


def _vmem_limit(pipelined_bytes, resident_bytes, claim_all=False):
    cap = V7X_VMEM_BYTES - VMEM_RESERVE_BYTES
    want = 2 * pipelined_bytes + resident_bytes + COMPILER_SCRATCH_BYTES
    return int(cap if claim_all else min(want, cap))


def _nbytes(shape, dtype):
    return math.prod(shape) * jnp.dtype(dtype).itemsize


def _sigmoid(v):
    return 0.5 * jnp.tanh(0.5 * v) + 0.5


def _silu(v):
    return v * _sigmoid(v)


def _rms(v, w):
    return (v * lax.rsqrt(jnp.mean(v * v, axis=-1, keepdims=True) + EPS)) * w


def _dot(a, b):
    return jnp.dot(a, b, preferred_element_type=F32)


def _dot_nt(a, b):
    return lax.dot_general(a, b, _NT, preferred_element_type=F32)


def _cast_rider_specs(weights, n_steps, step_of):
    in_specs, out_specs, out_shapes = [], [], []
    for w in weights:
        rows = w.shape[0] // n_steps
        assert rows * n_steps == w.shape[0] and rows % (2 * SUBLANES) == 0
        for specs in (in_specs, out_specs):
            specs.append(pl.BlockSpec((rows, w.shape[1]), lambda *ids: (step_of(*ids), 0)))
        out_shapes.append(jax.ShapeDtypeStruct(w.shape, BF16))
    return in_specs, out_specs, out_shapes


def _cast_riders(in_refs, out_refs):
    for i_ref, o_ref in zip(in_refs, out_refs):
        o_ref[...] = i_ref[...].astype(o_ref.dtype)


IN_RING = 4


def _inproj_kernel(x_ref, nw_ref, wa_hbm, wb_hbm, wc_ref, o_ref, g_ref, ba_ref, h_ref, wbuf, sem,
                   *, n_a, n_p, tail, rows_b1):
    i, j = pl.program_id(0), pl.program_id(1)
    n_j = pl.num_programs(1)
    tn = wbuf.shape[1]
    step = i * n_j + j
    n_steps = pl.num_programs(0) * n_j

    def tile_copy(col_step, slot, from_b):
        if from_b:
            row = jnp.where(col_step < n_p, (col_step - n_a) * tn, rows_b1 + (col_step - n_p) * tn)
            src = wb_hbm.at[pl.ds(pl.multiple_of(row, 2 * SUBLANES), tn), :]
        else:
            src = wa_hbm.at[pl.ds(pl.multiple_of(col_step * tn, tn), tn), :]
        return pltpu.make_async_copy(src, wbuf.at[slot], sem.at[slot])

    def request(ahead):
        target = step + ahead
        col_step = lax.rem(target, n_j)
        slot = lax.rem(target, IN_RING)

        @pl.when((target < n_steps) & (col_step < n_a))
        def _():
            tile_copy(col_step, slot, from_b=False).start()

        @pl.when((target < n_steps) & (col_step >= n_a))
        def _():
            tile_copy(col_step, slot, from_b=True).start()

    @pl.when(step == 0)
    def _():
        for ahead in range(IN_RING - 1):
            request(ahead)

    request(IN_RING - 1)
    slot = lax.rem(step, IN_RING)
    pltpu.make_async_copy(wb_hbm.at[pl.ds(0, tn), :], wbuf.at[slot], sem.at[slot]).wait()
    w_ref = wbuf.at[slot]

    @pl.when(j == 0)
    def _():
        h_ref[...] = _rms(x_ref[...], nw_ref[...]).astype(BF16)
        ba_ref[...] = _dot_nt(h_ref[...], wc_ref[...])

    @pl.when(j < n_p - 1)
    def _():
        o_ref[...] = _dot_nt(h_ref[...], w_ref[...])

    @pl.when(j == n_p - 1)
    def _():
        o_ref[:, :tail] = _dot_nt(h_ref[...], w_ref[:tail, :])

    @pl.when(j >= n_p)
    def _():
        g_ref[...] = _sigmoid(_dot_nt(h_ref[...], w_ref[...])).astype(g_ref.dtype)


def _inproj(x, nw, w_a, rows_a, w_b, w_c, n_gates, tm, tn):
    m, k = x.shape
    rows_b1 = w_b.shape[0] - n_gates
    n = rows_a + rows_b1
    n_a = rows_a // tn
    n_p = pl.cdiv(n, tn)
    assert rows_a % tn == 0 and n_gates % tn == 0 and rows_b1 % (2 * SUBLANES) == 0

    pipelined = (_nbytes((tm, k), F32) + _nbytes((tm, tn), F32) + _nbytes((tm, tn), BF16)
                 + _nbytes((tm, LANES), F32))
    resident = (_nbytes((tm, k), BF16) + _nbytes((tm, k), F32) + _nbytes((LANES, k), BF16)
                + IN_RING * _nbytes((tn, k), BF16))
    return pl.pallas_call(
        functools.partial(_inproj_kernel, n_a=n_a, n_p=n_p, tail=n - (n_p - 1) * tn, rows_b1=rows_b1),
        grid=(m // tm, n_p + n_gates // tn),
        in_specs=[
            pl.BlockSpec((tm, k), lambda i, j: (i, 0)),
            pl.BlockSpec((1, k), lambda i, j: (0, 0)),
            pl.BlockSpec(memory_space=pl.ANY),
            pl.BlockSpec(memory_space=pl.ANY),
            pl.BlockSpec((LANES, k), lambda i, j: (0, 0)),
        ],
        out_specs=[pl.BlockSpec((tm, tn), lambda i, j: (i, jnp.minimum(j, n_p - 1))),
                   pl.BlockSpec((tm, tn), lambda i, j: (i, jnp.maximum(j - n_p, 0))),
                   pl.BlockSpec((tm, LANES), lambda i, j: (i, 0))],
        out_shape=[jax.ShapeDtypeStruct((m, n), F32), jax.ShapeDtypeStruct((m, n_gates), BF16),
                   jax.ShapeDtypeStruct((m, LANES), F32)],
        scratch_shapes=[pltpu.VMEM((tm, k), BF16), pltpu.VMEM((IN_RING, tn, k), BF16),
                        pltpu.SemaphoreType.DMA((IN_RING,))],
        compiler_params=pltpu.CompilerParams(
            dimension_semantics=("arbitrary", "arbitrary"),
            vmem_limit_bytes=_vmem_limit(pipelined, resident)),
        name="inproj",
    )(x, nw, w_a, w_b, w_c)


def _norm_matmul_kernel(x_ref, nw_ref, w_ref, o_ref, h_ref):
    @pl.when(pl.program_id(1) == 0)
    def _():
        h_ref[...] = _rms(x_ref[...], nw_ref[...]).astype(BF16)

    o_ref[...] = _dot(h_ref[...], w_ref[...]).astype(o_ref.dtype)


def _norm_matmul(x, nw, w, tm, tn):
    m, k = x.shape
    n = w.shape[1]
    pipelined = _nbytes((tm, k), F32) + _nbytes((k, tn), BF16) + _nbytes((tm, tn), F32)
    resident = _nbytes((tm, k), BF16) + _nbytes((tm, k), F32)
    return pl.pallas_call(
        _norm_matmul_kernel,
        grid=(m // tm, n // tn),
        in_specs=[
            pl.BlockSpec((tm, k), lambda i, j: (i, 0)),
            pl.BlockSpec((1, k), lambda i, j: (0, 0)),
            pl.BlockSpec((k, tn), lambda i, j: (0, j)),
        ],
        out_specs=pl.BlockSpec((tm, tn), lambda i, j: (i, j)),
        out_shape=jax.ShapeDtypeStruct((m, n), F32),
        scratch_shapes=[pltpu.VMEM((tm, k), BF16)],
        compiler_params=pltpu.CompilerParams(
            dimension_semantics=("parallel", "arbitrary"),
            vmem_limit_bytes=_vmem_limit(pipelined, resident)),
        name="norm_matmul",
    )(x, nw, w)


def _dn_kernel(*refs, heads, dim, n_riders):
    qkv_ref, z_ref, ba_ref, cw_ref, gp_ref, onw_ref = refs[:6]
    o_ref = refs[6 + n_riders]
    ext_ref, state_ref = refs[-2:]
    _cast_riders(refs[6:6 + n_riders], refs[7 + n_riders:7 + 2 * n_riders])

    gt = DN_GROUP
    c = DN_CHUNK
    hd = heads * dim
    hs = range(heads)
    chunks = range(gt // c)

    @pl.when(pl.program_id(1) == 0)
    def _():
        ext_ref[0:SUBLANES, :] = jnp.zeros((SUBLANES, 3 * hd), F32)
        state_ref[...] = jnp.zeros_like(state_ref)

    ext_ref[SUBLANES:SUBLANES + gt, :] = qkv_ref[...]
    xe = ext_ref[...]
    conv = cw_ref[3:4, :] * xe[SUBLANES:]
    for s in (1, 2, 3):
        conv = conv + cw_ref[3 - s:4 - s, :] * pltpu.roll(xe, s, 0)[SUBLANES:]
    ext_ref[0:SUBLANES, :] = xe[gt:gt + SUBLANES]
    act = _silu(conv)

    ba = ba_ref[...]
    beta_all = _sigmoid(ba)
    xa = ba + gp_ref[1:2, :]
    softplus = jnp.maximum(xa, 0.0) + jnp.log1p(jnp.exp(-jnp.abs(xa)))
    g_all = -jnp.exp(gp_ref[0:1, :]) * softplus
    row_in_chunk = lax.broadcasted_iota(jnp.int32, (gt, LANES), 0) & (c - 1)
    gcum = g_all
    s = 1
    while s < c:
        gcum = gcum + jnp.where(row_in_chunk >= s, pltpu.roll(gcum, s, 0), 0.0)
        s *= 2
    exp_g = jnp.exp(gcum)
    kdec_parts, gc_rows = [], []
    for j in chunks:
        g_last = gcum[c * j + c - 1:c * j + c, :]
        kdec_parts.append(jnp.exp(g_last - gcum[c * j:c * (j + 1), :]))
        gc_rows.append(jnp.exp(g_last))
    kdec = jnp.concatenate(kdec_parts, axis=0)
    gcum_t = gcum.T

    ri = lax.broadcasted_iota(jnp.int32, (gt, gt), 0)
    ci = lax.broadcasted_iota(jnp.int32, (gt, gt), 1)
    same_chunk = (ri // c) == (ci // c)
    strict = same_chunk & (ri > ci)
    incl = same_chunk & (ri >= ci)

    qn, kn, kb, beta, eg = [], [], [], [], []
    for h in hs:
        qh = act[:, h * dim:(h + 1) * dim]
        kh = act[:, hd + h * dim:hd + (h + 1) * dim]
        qn.append(qh * lax.rsqrt(jnp.sum(qh * qh, axis=-1, keepdims=True) + EPS) * (dim ** -0.5))
        kn.append(kh * lax.rsqrt(jnp.sum(kh * kh, axis=-1, keepdims=True) + EPS))
        kb.append(kn[h].astype(BF16))
        beta.append(beta_all[:, h:h + 1])
        eg.append(exp_g[:, heads + h:heads + h + 1])

    decay = [jnp.exp(jnp.where(incl, gcum[:, heads + h:heads + h + 1] - gcum_t[heads + h:heads + h + 1, :],
                               -jnp.inf)) for h in hs]
    kk = [_dot_nt(kb[h], kb[h]) for h in hs]
    nmat = [jnp.where(strict, (beta[h] * kk[h]) * decay[h], 0.0) for h in hs]

    def same_block(size):
        return (ri // size) == (ci // size)

    eye = jnp.where(ri == ci, 1.0, 0.0)
    base = same_block(DN_BASE)
    pw = [jnp.where(base, nmat[h], 0.0) for h in hs]
    tinv = [eye - pw[h] for h in hs]
    pw = [pw[h].astype(BF16) for h in hs]
    order = 2
    while order < DN_BASE:
        pw = [_dot(pw[h], pw[h]).astype(BF16) for h in hs]
        tinv = [tinv[h] + _dot(tinv[h].astype(BF16), pw[h]) for h in hs]
        order *= 2
    size = DN_BASE
    while size < c:
        level = same_block(2 * size) & jnp.logical_not(same_block(size))
        off = [jnp.where(level, nmat[h], 0.0).astype(BF16) for h in hs]
        ct = [_dot(off[h], tinv[h].astype(BF16)).astype(BF16) for h in hs]
        tinv = [tinv[h] - _dot(tinv[h].astype(BF16), ct[h]) for h in hs]
        size *= 2

    sol = [jnp.concatenate([act[:, 2 * hd + h * dim:2 * hd + (h + 1) * dim] * beta[h],
                            kn[h] * (beta[h] * eg[h])], axis=1) for h in hs]
    sol = [sol[h] + _dot((tinv[h] - eye).astype(BF16), sol[h].astype(BF16)) for h in hs]

    pmat = [(_dot_nt(qn[h].astype(BF16), kb[h]) * decay[h]).astype(BF16) for h in hs]
    wq = [[jnp.concatenate([sol[h][c * j:c * (j + 1), dim:], (qn[h] * eg[h])[c * j:c * (j + 1)]],
                           axis=0).astype(BF16) for j in chunks] for h in hs]
    kd = [(kn[h] * kdec[:, heads + h:heads + h + 1]).astype(BF16) for h in hs]

    state = [state_ref[h] for h in hs]
    o_parts = [[] for _ in hs]
    for j in chunks:
        r = slice(c * j, c * (j + 1))
        ws = [_dot(wq[h][j], state[h].astype(BF16)) for h in hs]
        db = [(sol[h][r, :dim] - ws[h][:c]).astype(BF16) for h in hs]
        for h in hs:
            o_parts[h].append(ws[h][c:] + _dot(pmat[h][r, r], db[h]))
        state = [gc_rows[j][:, heads + h:heads + h + 1] * state[h]
                 + lax.dot_general(kd[h][r], db[h], _TN, preferred_element_type=F32) for h in hs]
    for h in hs:
        state_ref[h] = state[h]

    for h in hs:
        o = jnp.concatenate(o_parts[h], axis=0)
        zh = z_ref[:, h * dim:(h + 1) * dim]
        o_ref[:, h * dim:(h + 1) * dim] = (_rms(o, onw_ref[...]) * _silu(zh)).astype(o_ref.dtype)


def _deltanet(proj3, ba3, conv_w, gate_params, out_norm_w, riders, heads, dim, col_z):
    b, s, _ = proj3.shape
    hd = heads * dim
    gt = DN_GROUP
    ng = s // gt
    rider_in, rider_out, rider_shapes = _cast_rider_specs(riders, b * ng, lambda i, g: i * ng + g)
    pipelined = (_nbytes((gt, 3 * hd), F32) + _nbytes((gt, hd), F32) + _nbytes((gt, LANES), F32)
                 + _nbytes((gt, hd), BF16) + sum(_nbytes(w.shape, F32) * 3 // 2 for w in riders) // (b * ng))
    resident = (_nbytes((gt + SUBLANES, 3 * hd), F32) * 4 + _nbytes((heads, dim, dim), F32)
                + heads * 6 * _nbytes((gt, gt), F32))
    return pl.pallas_call(
        functools.partial(_dn_kernel, heads=heads, dim=dim, n_riders=len(riders)),
        grid=(b, ng),
        in_specs=[
            pl.BlockSpec((None, gt, 3 * hd), lambda i, g: (i, g, 0)),
            pl.BlockSpec((None, gt, hd), lambda i, g: (i, g, col_z // hd)),
            pl.BlockSpec((None, gt, LANES), lambda i, g: (i, g, 0)),
            pl.BlockSpec((4, 3 * hd), lambda i, g: (0, 0)),
            pl.BlockSpec((SUBLANES, LANES), lambda i, g: (0, 0)),
            pl.BlockSpec((1, dim), lambda i, g: (0, 0)),
            *rider_in,
        ],
        out_specs=[pl.BlockSpec((None, gt, hd), lambda i, g: (i, g, 0)), *rider_out],
        out_shape=[jax.ShapeDtypeStruct((b, s, hd), BF16), *rider_shapes],
        scratch_shapes=[pltpu.VMEM((gt + SUBLANES, 3 * hd), F32), pltpu.VMEM((heads, dim, dim), F32)],
        compiler_params=pltpu.CompilerParams(
            dimension_semantics=("parallel", "arbitrary"),
            vmem_limit_bytes=_vmem_limit(pipelined, resident, claim_all=True)),
        name="deltanet",
    )(proj3, proj3, ba3, conv_w, gate_params, out_norm_w, *riders)


def _swa_kernel(rb_ref, sink_ref, q_ref, kc_ref, kp_ref, vc_ref, vp_ref, qw_ref, kw_ref, o_ref, bias_ref,
                *, q_heads, kv_heads, dim):
    blk = SWA_BLOCK
    n = pl.program_id(1)
    qi = lax.broadcasted_iota(jnp.int32, (blk, 2 * blk), 0)
    kj = lax.broadcasted_iota(jnp.int32, (blk, 2 * blk), 1)
    dist = qi - kj + blk

    @pl.when((pl.program_id(0) == 0) & (n == 0))
    def _():
        max_exact = N_BUCKETS // 2
        nn = jnp.maximum(dist, 0)
        nf = jnp.maximum(nn, 1).astype(F32)
        large = max_exact + (jnp.log(nf / max_exact) / math.log(MAX_DISTANCE / max_exact)
                             * (N_BUCKETS - max_exact)).astype(jnp.int32)
        bucket = jnp.where(nn < max_exact, nn, jnp.minimum(large, N_BUCKETS - 1))
        for h in range(q_heads):
            acc = jnp.zeros((blk, 2 * blk), F32)
            for bk in range(N_BUCKETS):
                acc = jnp.where(bucket == bk, rb_ref[bk * q_heads + h], acc)
            acc = jnp.where((dist >= 0) & (dist < SWA_WINDOW), acc, -jnp.inf)
            bias_ref[h] = acc
            bias_ref[q_heads + h] = jnp.where(kj >= blk, acc, -jnp.inf)

    first = jnp.where(n == 0, q_heads, 0)
    group = q_heads // kv_heads
    for j in range(kv_heads):
        cols = slice(j * dim, (j + 1) * dim)
        k_all = _rms(jnp.concatenate([kp_ref[:, cols], kc_ref[:, cols]], axis=0), kw_ref[...]).astype(BF16)
        v_all = jnp.concatenate([vp_ref[:, cols], vc_ref[:, cols]], axis=0).astype(BF16)
        for u in range(SWA_STEP_BLOCKS):
            rows = slice(u * blk, (u + 1) * blk)
            kwin, vwin = k_all[u * blk:(u + 2) * blk], v_all[u * blk:(u + 2) * blk]
            qcat = jnp.concatenate(
                [_rms(q_ref[rows, (j * group + i) * dim:(j * group + i + 1) * dim], qw_ref[...])
                 for i in range(group)], axis=0).astype(BF16)
            logits = _dot_nt(qcat, kwin) * (dim ** -0.5)
            for i in range(group):
                h = j * group + i
                lg = logits[i * blk:(i + 1) * blk] + bias_ref[(first if u == 0 else 0) + h]
                sink = sink_ref[h]
                mx = jnp.maximum(jnp.max(lg, axis=-1, keepdims=True), sink)
                e = jnp.exp(lg - mx)
                den = jnp.sum(e, axis=-1, keepdims=True) + jnp.exp(sink - mx)
                pv = _dot(e.astype(BF16), vwin)
                o_ref[rows, h * dim:(h + 1) * dim] = (pv / den).astype(o_ref.dtype)


def _swa(proj3, rel_bias_flat, sinks, q_norm_w, k_norm_w, q_heads, kv_heads, dim, col_q, col_k, col_v):
    b, s, _ = proj3.shape
    blk = SWA_BLOCK
    step = SWA_STEP_BLOCKS * blk
    qw, kvw = q_heads * dim, kv_heads * dim
    smem = pl.BlockSpec(memory_space=pltpu.SMEM)
    pipelined = (_nbytes((step, qw), F32) + 2 * _nbytes((step + blk, kvw), F32) + _nbytes((step, qw), BF16))
    resident = (_nbytes((2 * q_heads, blk, 2 * blk), F32) + 2 * pipelined
                + 2 * SWA_STEP_BLOCKS * _nbytes((q_heads // kv_heads * blk, 2 * blk), F32))

    def prev(n):
        return jnp.maximum(SWA_STEP_BLOCKS * n - 1, 0)

    return pl.pallas_call(
        functools.partial(_swa_kernel, q_heads=q_heads, kv_heads=kv_heads, dim=dim),
        grid=(b, s // step),
        in_specs=[
            smem, smem,
            pl.BlockSpec((None, step, qw), lambda i, n: (i, n, col_q // qw)),
            pl.BlockSpec((None, step, kvw), lambda i, n: (i, n, col_k // kvw)),
            pl.BlockSpec((None, blk, kvw), lambda i, n: (i, prev(n), col_k // kvw)),
            pl.BlockSpec((None, step, kvw), lambda i, n: (i, n, col_v // kvw)),
            pl.BlockSpec((None, blk, kvw), lambda i, n: (i, prev(n), col_v // kvw)),
            pl.BlockSpec((1, dim), lambda i, n: (0, 0)),
            pl.BlockSpec((1, dim), lambda i, n: (0, 0)),
        ],
        out_specs=pl.BlockSpec((None, step, qw), lambda i, n: (i, n, 0)),
        out_shape=jax.ShapeDtypeStruct((b, s, qw), BF16),
        scratch_shapes=[pltpu.VMEM((2 * q_heads, blk, 2 * blk), F32)],
        compiler_params=pltpu.CompilerParams(
            dimension_semantics=("arbitrary", "arbitrary"),
            vmem_limit_bytes=_vmem_limit(pipelined, resident, claim_all=True)),
        name="swa",
    )(rel_bias_flat, sinks, proj3, proj3, proj3, proj3, proj3, q_norm_w, k_norm_w)


RIDER_RING = 3


def _cast_riders_ringed(hbm_refs, out_refs, bufs, sem, step, n_steps):
    def copies(target):
        slot = lax.rem(target, RIDER_RING)
        out = []
        for r, (hbm, buf) in enumerate(zip(hbm_refs, bufs)):
            rows = buf.shape[1]
            src = hbm.at[pl.ds(pl.multiple_of(target * rows, rows), rows), :]
            out.append(pltpu.make_async_copy(src, buf.at[slot], sem.at[r, slot]))
        return out

    def request(target):
        @pl.when(target < n_steps)
        def _():
            for copy in copies(target):
                copy.start()

    @pl.when(step == 0)
    def _():
        for ahead in range(RIDER_RING - 1):
            request(step + ahead)

    request(step + RIDER_RING - 1)
    for copy in copies(step):
        copy.wait()
    slot = lax.rem(step, RIDER_RING)
    for buf, out in zip(bufs, out_refs):
        out[...] = buf[slot].astype(out.dtype)


def _memattn_kernel(*refs, heads, dim, n_riders):
    q_ref, k_ref, v_ref, qw_ref, kw_ref = refs[:5]
    o_ref = refs[5 + n_riders]
    n_t = pl.num_programs(1)
    _cast_riders_ringed(refs[5:5 + n_riders], refs[6 + n_riders:6 + 2 * n_riders],
                        refs[6 + 2 * n_riders:6 + 3 * n_riders], refs[-1],
                        pl.program_id(0) * n_t + pl.program_id(1), pl.num_programs(0) * n_t)
    for h in range(heads):
        cols = slice(h * dim, (h + 1) * dim)
        qn = _rms(q_ref[:, cols], qw_ref[...]).astype(BF16)
        kn = _rms(k_ref[:, cols], kw_ref[...]).astype(BF16)
        lg = _dot_nt(qn, kn) * (dim ** -0.5)
        e = jnp.exp(lg - jnp.max(lg, axis=-1, keepdims=True))
        den = jnp.sum(e, axis=-1, keepdims=True)
        pv = _dot(e.astype(BF16), v_ref[:, cols].astype(BF16))
        o_ref[:, cols] = (pv / den).astype(o_ref.dtype)


def _memattn(proj3, mkv3, q_norm_w, k_norm_w, riders, heads, dim, col_q, tq):
    b, s, _ = proj3.shape
    m = mkv3.shape[1]
    w = heads * dim
    nt = s // tq
    _, rider_out, rider_shapes = _cast_rider_specs(riders, b * nt, lambda i, t: i * nt + t)
    slabs = [(r.shape[0] // (b * nt), r.shape[1]) for r in riders]
    pipelined = (_nbytes((tq, w), F32) + 2 * _nbytes((m, w), F32) + _nbytes((tq, w), BF16)
                 + sum(_nbytes(slab, BF16) for slab in slabs))
    ring_bytes = RIDER_RING * sum(_nbytes(slab, F32) for slab in slabs)
    return pl.pallas_call(
        functools.partial(_memattn_kernel, heads=heads, dim=dim, n_riders=len(riders)),
        grid=(b, nt),
        in_specs=[
            pl.BlockSpec((pl.Element(tq), pl.Element(w)), lambda i, t: (pl.multiple_of(i * s + t * tq, tq), col_q)),
            pl.BlockSpec((None, m, w), lambda i, t: (i, 0, 0)),
            pl.BlockSpec((None, m, w), lambda i, t: (i, 0, 1)),
            pl.BlockSpec((1, dim), lambda i, t: (0, 0)),
            pl.BlockSpec((1, dim), lambda i, t: (0, 0)),
            *([pl.BlockSpec(memory_space=pl.ANY)] * len(riders)),
        ],
        out_specs=[pl.BlockSpec((None, tq, w), lambda i, t: (i, t, 0)), *rider_out],
        out_shape=[jax.ShapeDtypeStruct((b, s, w), BF16), *rider_shapes],
        scratch_shapes=[*[pltpu.VMEM((RIDER_RING, *slab), F32) for slab in slabs],
                        pltpu.SemaphoreType.DMA((len(riders), RIDER_RING))],
        compiler_params=pltpu.CompilerParams(
            dimension_semantics=("arbitrary", "arbitrary"),
            vmem_limit_bytes=_vmem_limit(pipelined, 2 * heads * _nbytes((tq, m), F32) + ring_bytes,
                                         claim_all=True)),
        name="memattn",
    )(proj3.reshape(b * s, -1), mkv3, mkv3, q_norm_w, k_norm_w, *riders)


def _merge_kernel(od_ref, os_ref, om_ref, pd_ref, ps_ref, pm_ref, gd_ref, gs_ref, gm_ref, o_ref):
    merged = (gd_ref[...].astype(F32) * _dot(od_ref[...], pd_ref[...])
              + gs_ref[...].astype(F32) * _dot(os_ref[...], ps_ref[...])
              + gm_ref[...].astype(F32) * _dot(om_ref[...], pm_ref[...]))
    o_ref[...] = merged.astype(o_ref.dtype)


def _merge(o_dn, o_swa, o_mem, p_dn, p_swa, p_mem, gates, tm, tn):
    t = o_dn.shape[0]
    d = p_dn.shape[1]
    wd, ws, wm = o_dn.shape[1], o_swa.shape[1], o_mem.shape[1]
    per_branch = d // tn
    pipelined = ((_nbytes((tm, wd), BF16) + _nbytes((tm, ws), BF16) + _nbytes((tm, wm), BF16))
                 + (_nbytes((wd, tn), BF16) + _nbytes((ws, tn), BF16) + _nbytes((wm, tn), BF16))
                 + 4 * _nbytes((tm, tn), BF16))
    return pl.pallas_call(
        _merge_kernel,
        grid=(d // tn, t // tm),
        in_specs=[
            pl.BlockSpec((tm, wd), lambda j, i: (i, 0)),
            pl.BlockSpec((tm, ws), lambda j, i: (i, 0)),
            pl.BlockSpec((tm, wm), lambda j, i: (i, 0)),
            pl.BlockSpec((wd, tn), lambda j, i: (0, j)),
            pl.BlockSpec((ws, tn), lambda j, i: (0, j)),
            pl.BlockSpec((wm, tn), lambda j, i: (0, j)),
            pl.BlockSpec((tm, tn), lambda j, i: (i, j)),
            pl.BlockSpec((tm, tn), lambda j, i: (i, per_branch + j)),
            pl.BlockSpec((tm, tn), lambda j, i: (i, 2 * per_branch + j)),
        ],
        out_specs=pl.BlockSpec((tm, tn), lambda j, i: (i, j)),
        out_shape=jax.ShapeDtypeStruct((t, d), BF16),
        compiler_params=pltpu.CompilerParams(
            dimension_semantics=("parallel", "parallel"),
            vmem_limit_bytes=_vmem_limit(pipelined, 3 * _nbytes((tm, tn), F32))),
        name="merge",
    )(o_dn, o_swa, o_mem, p_dn, p_swa, p_mem, gates, gates, gates)


def _outproj_kernel(x_ref, m_ref, w_ref, nw_ref, x1_ref, h_ref):
    x1 = x_ref[...] + _dot(m_ref[...], w_ref[...])
    x1_ref[...] = x1
    h_ref[...] = _rms(x1, nw_ref[...]).astype(h_ref.dtype)


def _outproj(x, merged, w_out, norm_w, tm):
    t, d = x.shape
    pipelined = 2 * _nbytes((tm, d), F32) + 2 * _nbytes((tm, d), BF16) + _nbytes((d, d), BF16)
    return pl.pallas_call(
        _outproj_kernel,
        grid=(t // tm,),
        in_specs=[
            pl.BlockSpec((tm, d), lambda i: (i, 0)),
            pl.BlockSpec((tm, d), lambda i: (i, 0)),
            pl.BlockSpec((d, d), lambda i: (0, 0)),
            pl.BlockSpec((1, d), lambda i: (0, 0)),
        ],
        out_specs=[pl.BlockSpec((tm, d), lambda i: (i, 0)), pl.BlockSpec((tm, d), lambda i: (i, 0))],
        out_shape=[jax.ShapeDtypeStruct((t, d), F32), jax.ShapeDtypeStruct((t, d), BF16)],
        compiler_params=pltpu.CompilerParams(
            dimension_semantics=("parallel",),
            vmem_limit_bytes=_vmem_limit(pipelined, 2 * _nbytes((tm, d), F32))),
        name="outproj",
    )(x, merged, w_out, norm_w)


MLP_RING = 3


def _mlp_kernel(h_ref, x1_ref, wu_hbm, wd_hbm, o_ref, ubuf, dbuf, sem, *, n_k):
    tf = dbuf.shape[1]
    base = pl.program_id(0) * n_k
    n_tiles = pl.num_programs(0) * n_k

    def tile_copies(t):
        slot = lax.rem(t, MLP_RING)
        start = pl.multiple_of(lax.rem(t, n_k) * tf, tf)
        return (pltpu.make_async_copy(wu_hbm.at[:, pl.ds(start, tf)], ubuf.at[slot], sem.at[0, slot]),
                pltpu.make_async_copy(wd_hbm.at[pl.ds(start, tf), :], dbuf.at[slot], sem.at[1, slot]))

    def request(t):
        @pl.when(t < n_tiles)
        def _():
            for copy in tile_copies(t):
                copy.start()

    def wait(t):
        for copy in tile_copies(t):
            copy.wait()

    def up(t):
        a = jnp.maximum(_dot(h_ref[...], ubuf[lax.rem(t, MLP_RING)]), 0.0)
        return (a * a).astype(BF16)

    @pl.when(base == 0)
    def _():
        for t in range(MLP_RING):
            request(base + t)

    o_ref[...] = x1_ref[...]
    wait(base)
    act = up(base)
    for k in range(n_k):
        if k + 1 < n_k:
            wait(base + k + 1)
        o_ref[...] += _dot(act, dbuf[lax.rem(base + k, MLP_RING)])
        if k + 1 < n_k:
            act = up(base + k + 1)
        request(base + k + MLP_RING)


def _mlp(h2, x1, w_up, w_down, tm, tf):
    t, d = x1.shape
    f = w_up.shape[1]
    pipelined = _nbytes((tm, d), BF16) + 2 * _nbytes((tm, d), F32)
    resident = 2 * _nbytes((tm, tf), F32) + MLP_RING * (_nbytes((d, tf), BF16) + _nbytes((tf, d), BF16))
    return pl.pallas_call(
        functools.partial(_mlp_kernel, n_k=f // tf),
        grid=(t // tm,),
        in_specs=[
            pl.BlockSpec((tm, d), lambda i: (i, 0)),
            pl.BlockSpec((tm, d), lambda i: (i, 0)),
            pl.BlockSpec(memory_space=pl.ANY),
            pl.BlockSpec(memory_space=pl.ANY),
        ],
        out_specs=pl.BlockSpec((tm, d), lambda i: (i, 0)),
        out_shape=jax.ShapeDtypeStruct((t, d), F32),
        scratch_shapes=[pltpu.VMEM((MLP_RING, d, tf), BF16), pltpu.VMEM((MLP_RING, tf, d), BF16),
                        pltpu.SemaphoreType.DMA((2, MLP_RING))],
        compiler_params=pltpu.CompilerParams(
            dimension_semantics=("arbitrary",),
            vmem_limit_bytes=_vmem_limit(pipelined, resident)),
        name="mlp",
    )(h2, x1, w_up, w_down)


def _layer(x, mem, attn_norm_w, w_in, dn_conv_w, dn_a_log, dn_dt_bias, dn_out_norm_w, swa_q_norm_w,
           swa_k_norm_w, swa_sinks, rel_bias, mem_norm_w, w_mem_kv, xq_norm_w, xk_norm_w,
           p_dn, p_swa, p_mem, w_out, mlp_norm_w, w_mlp_up, w_mlp_down):
    b, s, d = x.shape
    m = mem.shape[1]
    t = b * s

    dn_heads = dn_a_log.shape[0]
    dn_dim = dn_out_norm_w.shape[0]
    dn_w = dn_heads * dn_dim
    swa_heads = swa_sinks.shape[0]
    swa_dim = swa_q_norm_w.shape[0]
    swa_w = swa_heads * swa_dim
    mem_dim = xq_norm_w.shape[0]
    mem_w = p_mem.shape[0]
    mem_heads = mem_w // mem_dim
    swa_kv_w = (w_in.shape[1] - 4 * dn_w - 2 * dn_heads - swa_w - mem_w - 3 * d) // 2
    swa_kv = swa_kv_w // swa_dim

    n_ba = 2 * dn_heads
    src_ba = 4 * dn_w
    src_swa = src_ba + n_ba
    w_t = w_in.T.astype(BF16)
    w_b = w_t[src_swa:]
    w_c = jnp.concatenate([w_t[src_ba:src_swa], jnp.zeros((LANES - n_ba, d), BF16)], axis=0)
    col = {"qkv": 0, "z": 3 * dn_w, "sq": src_ba}
    col["sk"] = col["sq"] + swa_w
    col["sv"] = col["sk"] + swa_kv_w
    col["mq"] = col["sv"] + swa_kv_w

    proj, gates, ba = _inproj(x.reshape(t, d), attn_norm_w.reshape(1, d), w_t, src_ba, w_b, w_c, 3 * d,
                              tm=1024, tn=1024)
    proj3 = proj.reshape(b, s, -1)
    mkv = _norm_matmul(mem.reshape(b * m, d), mem_norm_w.reshape(1, d), w_mem_kv.astype(BF16), tm=512, tn=512)
    mkv3 = mkv.reshape(b, m, -1)

    gate_params = jnp.zeros((SUBLANES, LANES), F32)
    gate_params = gate_params.at[0, dn_heads:2 * dn_heads].set(dn_a_log)
    gate_params = gate_params.at[1, dn_heads:2 * dn_heads].set(dn_dt_bias)
    o_dn, w_up_bf, w_down_bf = _deltanet(proj3, ba.reshape(b, s, LANES), dn_conv_w, gate_params,
                                         dn_out_norm_w.reshape(1, dn_dim), [w_mlp_up, w_mlp_down],
                                         dn_heads, dn_dim, col["z"])
    o_swa = _swa(proj3, rel_bias.reshape(-1), swa_sinks, swa_q_norm_w.reshape(1, swa_dim),
                 swa_k_norm_w.reshape(1, swa_dim), swa_heads, swa_kv, swa_dim, col["sq"], col["sk"], col["sv"])
    o_mem, w_out_bf, p_dn_bf, p_swa_bf, p_mem_bf = _memattn(
        proj3, mkv3, xq_norm_w.reshape(1, mem_dim), xk_norm_w.reshape(1, mem_dim),
        [w_out, p_dn, p_swa, p_mem], mem_heads, mem_dim, col["mq"], tq=1024)

    merged = _merge(o_dn.reshape(t, dn_w), o_swa.reshape(t, swa_w), o_mem.reshape(t, mem_w),
                    p_dn_bf, p_swa_bf, p_mem_bf, gates, tm=512, tn=2048)
    x1, h2 = _outproj(x.reshape(t, d), merged, w_out_bf, mlp_norm_w.reshape(1, d), tm=512)
    out = _mlp(h2, x1, w_up_bf, w_down_bf, tm=512, tf=1024)
    return out.reshape(b, s, d)


def kernel(x, mem, attn_norm_w, w_in, dn_conv_w, dn_A_log, dn_dt_bias, dn_out_norm_w, swa_q_norm_w,
           swa_k_norm_w, swa_sinks, rel_bias, mem_norm_w, w_mem_kv, xq_norm_w, xk_norm_w, p_dn, p_swa,
           p_mem, w_out, mlp_norm_w, w_mlp_up, w_mlp_down):
    depth = w_in.shape[0]
    for l in range(depth):
        x = _layer(x, mem, attn_norm_w[l], w_in[l], dn_conv_w[l], dn_A_log[l], dn_dt_bias[l],
                   dn_out_norm_w[l], swa_q_norm_w[l], swa_k_norm_w[l], swa_sinks[l], rel_bias,
                   mem_norm_w[l], w_mem_kv[l], xq_norm_w[l], xk_norm_w[l], p_dn[l], p_swa[l], p_mem[l],
                   w_out[l], mlp_norm_w[l], w_mlp_up[l], w_mlp_down[l])
    return x
```

```python
import functools
import math

import jax
import jax.numpy as jnp
from jax import lax
from jax.experimental import pallas as pl
from jax.experimental.pallas import tpu as pltpu

F32 = jnp.float32
BF16 = jnp.bfloat16

EPS = 1e-6
LANES = 128
SUBLANES = 8
V7X_VMEM_BYTES = 64 * 1024 * 1024

DN_CHUNK = 64
DN_GROUP = 256
DN_BASE = 8
SWA_WINDOW = 128
SWA_BLOCK = 128
SWA_STEP_BLOCKS = 8
N_BUCKETS = 32
MAX_DISTANCE = 128

_NT = (((1,), (1,)), ((), ()))
_TN = (((0,), (0,)), ((), ()))


COMPILER_SCRATCH_BYTES = 4 * 1024 * 1024
VMEM_RESERVE_BYTES = 8 * 1024 * 1024


def _vmem_limit(pipelined_bytes, resident_bytes, claim_all=False):
    cap = V7X_VMEM_BYTES - VMEM_RESERVE_BYTES
    want = 2 * pipelined_bytes + resident_bytes + COMPILER_SCRATCH_BYTES
    return int(cap if claim_all else min(want, cap))


def _nbytes(shape, dtype):
    return math.prod(shape) * jnp.dtype(dtype).itemsize


def _sigmoid(v):
    return 0.5 * jnp.tanh(0.5 * v) + 0.5


def _silu(v):
    return v * _sigmoid(v)


def _rms(v, w):
    return (v * lax.rsqrt(jnp.mean(v * v, axis=-1, keepdims=True) + EPS)) * w


def _dot(a, b):
    return jnp.dot(a, b, preferred_element_type=F32)


def _dot_nt(a, b):
    return lax.dot_general(a, b, _NT, preferred_element_type=F32)


def _cast_rider_specs(weights, n_steps, step_of):
    in_specs, out_specs, out_shapes = [], [], []
    for w in weights:
        rows = w.shape[0] // n_steps
        assert rows * n_steps == w.shape[0] and rows % (2 * SUBLANES) == 0
        for specs in (in_specs, out_specs):
            specs.append(pl.BlockSpec((rows, w.shape[1]), lambda *ids: (step_of(*ids), 0)))
        out_shapes.append(jax.ShapeDtypeStruct(w.shape, BF16))
    return in_specs, out_specs, out_shapes


def _cast_riders(in_refs, out_refs):
    for i_ref, o_ref in zip(in_refs, out_refs):
        o_ref[...] = i_ref[...].astype(o_ref.dtype)


IN_RING = 4


def _inproj_kernel(x_ref, nw_ref, wa_hbm, wb_hbm, wc_ref, o_ref, g_ref, ba_ref, h_ref, wbuf, sem,
                   *, n_a, n_p, tail, rows_b1):
    i, j = pl.program_id(0), pl.program_id(1)
    n_j = pl.num_programs(1)
    tn = wbuf.shape[1]
    step = i * n_j + j
    n_steps = pl.num_programs(0) * n_j

    def tile_copy(col_step, slot, from_b):
        if from_b:
            row = jnp.where(col_step < n_p, (col_step - n_a) * tn, rows_b1 + (col_step - n_p) * tn)
            src = wb_hbm.at[pl.ds(pl.multiple_of(row, 2 * SUBLANES), tn), :]
        else:
            src = wa_hbm.at[pl.ds(pl.multiple_of(col_step * tn, tn), tn), :]
        return pltpu.make_async_copy(src, wbuf.at[slot], sem.at[slot])

    def request(ahead):
        target = step + ahead
        col_step = lax.rem(target, n_j)
        slot = lax.rem(target, IN_RING)

        @pl.when((target < n_steps) & (col_step < n_a))
        def _():
            tile_copy(col_step, slot, from_b=False).start()

        @pl.when((target < n_steps) & (col_step >= n_a))
        def _():
            tile_copy(col_step, slot, from_b=True).start()

    @pl.when(step == 0)
    def _():
        for ahead in range(IN_RING - 1):
            request(ahead)

    request(IN_RING - 1)
    slot = lax.rem(step, IN_RING)
    pltpu.make_async_copy(wb_hbm.at[pl.ds(0, tn), :], wbuf.at[slot], sem.at[slot]).wait()
    w_ref = wbuf.at[slot]

    @pl.when(j == 0)
    def _():
        h_ref[...] = _rms(x_ref[...], nw_ref[...]).astype(BF16)
        ba_ref[...] = _dot_nt(h_ref[...], wc_ref[...])

    @pl.when(j < n_p - 1)
    def _():
        o_ref[...] = _dot_nt(h_ref[...], w_ref[...])

    @pl.when(j == n_p - 1)
    def _():
        o_ref[:, :tail] = _dot_nt(h_ref[...], w_ref[:tail, :])

    @pl.when(j >= n_p)
    def _():
        g_ref[...] = _sigmoid(_dot_nt(h_ref[...], w_ref[...])).astype(g_ref.dtype)


def _inproj(x, nw, w_a, rows_a, w_b, w_c, n_gates, tm, tn):
    m, k = x.shape
    rows_b1 = w_b.shape[0] - n_gates
    n = rows_a + rows_b1
    n_a = rows_a // tn
    n_p = pl.cdiv(n, tn)
    assert rows_a % tn == 0 and n_gates % tn == 0 and rows_b1 % (2 * SUBLANES) == 0

    pipelined = (_nbytes((tm, k), F32) + _nbytes((tm, tn), F32) + _nbytes((tm, tn), BF16)
                 + _nbytes((tm, LANES), F32))
    resident = (_nbytes((tm, k), BF16) + _nbytes((tm, k), F32) + _nbytes((LANES, k), BF16)
                + IN_RING * _nbytes((tn, k), BF16))
    return pl.pallas_call(
        functools.partial(_inproj_kernel, n_a=n_a, n_p=n_p, tail=n - (n_p - 1) * tn, rows_b1=rows_b1),
        grid=(m // tm, n_p + n_gates // tn),
        in_specs=[
            pl.BlockSpec((tm, k), lambda i, j: (i, 0)),
            pl.BlockSpec((1, k), lambda i, j: (0, 0)),
            pl.BlockSpec(memory_space=pl.ANY),
            pl.BlockSpec(memory_space=pl.ANY),
            pl.BlockSpec((LANES, k), lambda i, j: (0, 0)),
        ],
        out_specs=[pl.BlockSpec((tm, tn), lambda i, j: (i, jnp.minimum(j, n_p - 1))),
                   pl.BlockSpec((tm, tn), lambda i, j: (i, jnp.maximum(j - n_p, 0))),
                   pl.BlockSpec((tm, LANES), lambda i, j: (i, 0))],
        out_shape=[jax.ShapeDtypeStruct((m, n), F32), jax.ShapeDtypeStruct((m, n_gates), BF16),
                   jax.ShapeDtypeStruct((m, LANES), F32)],
        scratch_shapes=[pltpu.VMEM((tm, k), BF16), pltpu.VMEM((IN_RING, tn, k), BF16),
                        pltpu.SemaphoreType.DMA((IN_RING,))],
        compiler_params=pltpu.CompilerParams(
            dimension_semantics=("arbitrary", "arbitrary"),
            vmem_limit_bytes=_vmem_limit(pipelined, resident)),
        name="inproj",
    )(x, nw, w_a, w_b, w_c)


def _norm_matmul_kernel(x_ref, nw_ref, w_ref, o_ref, h_ref):
    @pl.when(pl.program_id(1) == 0)
    def _():
        h_ref[...] = _rms(x_ref[...], nw_ref[...]).astype(BF16)

    o_ref[...] = _dot(h_ref[...], w_ref[...]).astype(o_ref.dtype)


def _norm_matmul(x, nw, w, tm, tn):
    m, k = x.shape
    n = w.shape[1]
    pipelined = _nbytes((tm, k), F32) + _nbytes((k, tn), BF16) + _nbytes((tm, tn), F32)
    resident = _nbytes((tm, k), BF16) + _nbytes((tm, k), F32)
    return pl.pallas_call(
        _norm_matmul_kernel,
        grid=(m // tm, n // tn),
        in_specs=[
            pl.BlockSpec((tm, k), lambda i, j: (i, 0)),
            pl.BlockSpec((1, k), lambda i, j: (0, 0)),
            pl.BlockSpec((k, tn), lambda i, j: (0, j)),
        ],
        out_specs=pl.BlockSpec((tm, tn), lambda i, j: (i, j)),
        out_shape=jax.ShapeDtypeStruct((m, n), F32),
        scratch_shapes=[pltpu.VMEM((tm, k), BF16)],
        compiler_params=pltpu.CompilerParams(
            dimension_semantics=("parallel", "arbitrary"),
            vmem_limit_bytes=_vmem_limit(pipelined, resident)),
        name="norm_matmul",
    )(x, nw, w)


def _dn_kernel(*refs, heads, dim, n_riders):
    qkv_ref, z_ref, ba_ref, cw_ref, gp_ref, onw_ref = refs[:6]
    o_ref = refs[6 + n_riders]
    ext_ref, state_ref = refs[-2:]
    _cast_riders(refs[6:6 + n_riders], refs[7 + n_riders:7 + 2 * n_riders])

    gt = DN_GROUP
    c = DN_CHUNK
    hd = heads * dim
    hs = range(heads)
    chunks = range(gt // c)

    @pl.when(pl.program_id(1) == 0)
    def _():
        ext_ref[0:SUBLANES, :] = jnp.zeros((SUBLANES, 3 * hd), F32)
        state_ref[...] = jnp.zeros_like(state_ref)

    ext_ref[SUBLANES:SUBLANES + gt, :] = qkv_ref[...]
    xe = ext_ref[...]
    conv = cw_ref[3:4, :] * xe[SUBLANES:]
    for s in (1, 2, 3):
        conv = conv + cw_ref[3 - s:4 - s, :] * pltpu.roll(xe, s, 0)[SUBLANES:]
    ext_ref[0:SUBLANES, :] = xe[gt:gt + SUBLANES]
    act = _silu(conv)

    ba = ba_ref[...]
    beta_all = _sigmoid(ba)
    xa = ba + gp_ref[1:2, :]
    softplus = jnp.maximum(xa, 0.0) + jnp.log1p(jnp.exp(-jnp.abs(xa)))
    g_all = -jnp.exp(gp_ref[0:1, :]) * softplus
    row_in_chunk = lax.broadcasted_iota(jnp.int32, (gt, LANES), 0) & (c - 1)
    gcum = g_all
    s = 1
    while s < c:
        gcum = gcum + jnp.where(row_in_chunk >= s, pltpu.roll(gcum, s, 0), 0.0)
        s *= 2
    exp_g = jnp.exp(gcum)
    kdec_parts, gc_rows = [], []
    for j in chunks:
        g_last = gcum[c * j + c - 1:c * j + c, :]
        kdec_parts.append(jnp.exp(g_last - gcum[c * j:c * (j + 1), :]))
        gc_rows.append(jnp.exp(g_last))
    kdec = jnp.concatenate(kdec_parts, axis=0)
    gcum_t = gcum.T

    ri = lax.broadcasted_iota(jnp.int32, (gt, gt), 0)
    ci = lax.broadcasted_iota(jnp.int32, (gt, gt), 1)
    same_chunk = (ri // c) == (ci // c)
    strict = same_chunk & (ri > ci)
    incl = same_chunk & (ri >= ci)

    qn, kn, kb, beta, eg = [], [], [], [], []
    for h in hs:
        qh = act[:, h * dim:(h + 1) * dim]
        kh = act[:, hd + h * dim:hd + (h + 1) * dim]
        qn.append(qh * lax.rsqrt(jnp.sum(qh * qh, axis=-1, keepdims=True) + EPS) * (dim ** -0.5))
        kn.append(kh * lax.rsqrt(jnp.sum(kh * kh, axis=-1, keepdims=True) + EPS))
        kb.append(kn[h].astype(BF16))
        beta.append(beta_all[:, h:h + 1])
        eg.append(exp_g[:, heads + h:heads + h + 1])

    decay = [jnp.exp(jnp.where(incl, gcum[:, heads + h:heads + h + 1] - gcum_t[heads + h:heads + h + 1, :],
                               -jnp.inf)) for h in hs]
    kk = [_dot_nt(kb[h], kb[h]) for h in hs]
    nmat = [jnp.where(strict, (beta[h] * kk[h]) * decay[h], 0.0) for h in hs]

    def same_block(size):
        return (ri // size) == (ci // size)

    eye = jnp.where(ri == ci, 1.0, 0.0)
    base = same_block(DN_BASE)
    pw = [jnp.where(base, nmat[h], 0.0) for h in hs]
    tinv = [eye - pw[h] for h in hs]
    pw = [pw[h].astype(BF16) for h in hs]
    order = 2
    while order < DN_BASE:
        pw = [_dot(pw[h], pw[h]).astype(BF16) for h in hs]
        tinv = [tinv[h] + _dot(tinv[h].astype(BF16), pw[h]) for h in hs]
        order *= 2
    size = DN_BASE
    while size < c:
        level = same_block(2 * size) & jnp.logical_not(same_block(size))
        off = [jnp.where(level, nmat[h], 0.0).astype(BF16) for h in hs]
        ct = [_dot(off[h], tinv[h].astype(BF16)).astype(BF16) for h in hs]
        tinv = [tinv[h] - _dot(tinv[h].astype(BF16), ct[h]) for h in hs]
        size *= 2

    sol = [jnp.concatenate([act[:, 2 * hd + h * dim:2 * hd + (h + 1) * dim] * beta[h],
                            kn[h] * (beta[h] * eg[h])], axis=1) for h in hs]
    sol = [sol[h] + _dot((tinv[h] - eye).astype(BF16), sol[h].astype(BF16)) for h in hs]

    pmat = [(_dot_nt(qn[h].astype(BF16), kb[h]) * decay[h]).astype(BF16) for h in hs]
    wq = [[jnp.concatenate([sol[h][c * j:c * (j + 1), dim:], (qn[h] * eg[h])[c * j:c * (j + 1)]],
                           axis=0).astype(BF16) for j in chunks] for h in hs]
    kd = [(kn[h] * kdec[:, heads + h:heads + h + 1]).astype(BF16) for h in hs]

    state = [state_ref[h] for h in hs]
    o_parts = [[] for _ in hs]
    for j in chunks:
        r = slice(c * j, c * (j + 1))
        ws = [_dot(wq[h][j], state[h].astype(BF16)) for h in hs]
        db = [(sol[h][r, :dim] - ws[h][:c]).astype(BF16) for h in hs]
        for h in hs:
            o_parts[h].append(ws[h][c:] + _dot(pmat[h][r, r], db[h]))
        state = [gc_rows[j][:, heads + h:heads + h + 1] * state[h]
                 + lax.dot_general(kd[h][r], db[h], _TN, preferred_element_type=F32) for h in hs]
    for h in hs:
        state_ref[h] = state[h]

    for h in hs:
        o = jnp.concatenate(o_parts[h], axis=0)
        zh = z_ref[:, h * dim:(h + 1) * dim]
        o_ref[:, h * dim:(h + 1) * dim] = (_rms(o, onw_ref[...]) * _silu(zh)).astype(o_ref.dtype)


def _deltanet(proj3, ba3, conv_w, gate_params, out_norm_w, riders, heads, dim, col_z):
    b, s, _ = proj3.shape
    hd = heads * dim
    gt = DN_GROUP
    ng = s // gt
    rider_in, rider_out, rider_shapes = _cast_rider_specs(riders, b * ng, lambda i, g: i * ng + g)
    pipelined = (_nbytes((gt, 3 * hd), F32) + _nbytes((gt, hd), F32) + _nbytes((gt, LANES), F32)
                 + _nbytes((gt, hd), BF16) + sum(_nbytes(w.shape, F32) * 3 // 2 for w in riders) // (b * ng))
    resident = (_nbytes((gt + SUBLANES, 3 * hd), F32) * 4 + _nbytes((heads, dim, dim), F32)
                + heads * 6 * _nbytes((gt, gt), F32))
    return pl.pallas_call(
        functools.partial(_dn_kernel, heads=heads, dim=dim, n_riders=len(riders)),
        grid=(b, ng),
        in_specs=[
            pl.BlockSpec((None, gt, 3 * hd), lambda i, g: (i, g, 0)),
            pl.BlockSpec((None, gt, hd), lambda i, g: (i, g, col_z // hd)),
            pl.BlockSpec((None, gt, LANES), lambda i, g: (i, g, 0)),
            pl.BlockSpec((4, 3 * hd), lambda i, g: (0, 0)),
            pl.BlockSpec((SUBLANES, LANES), lambda i, g: (0, 0)),
            pl.BlockSpec((1, dim), lambda i, g: (0, 0)),
            *rider_in,
        ],
        out_specs=[pl.BlockSpec((None, gt, hd), lambda i, g: (i, g, 0)), *rider_out],
        out_shape=[jax.ShapeDtypeStruct((b, s, hd), BF16), *rider_shapes],
        scratch_shapes=[pltpu.VMEM((gt + SUBLANES, 3 * hd), F32), pltpu.VMEM((heads, dim, dim), F32)],
        compiler_params=pltpu.CompilerParams(
            dimension_semantics=("parallel", "arbitrary"),
            vmem_limit_bytes=_vmem_limit(pipelined, resident, claim_all=True)),
        name="deltanet",
    )(proj3, proj3, ba3, conv_w, gate_params, out_norm_w, *riders)


def _swa_kernel(rb_ref, sink_ref, q_ref, kc_ref, kp_ref, vc_ref, vp_ref, qw_ref, kw_ref, o_ref, bias_ref,
                *, q_heads, kv_heads, dim):
    blk = SWA_BLOCK
    n = pl.program_id(1)
    qi = lax.broadcasted_iota(jnp.int32, (blk, 2 * blk), 0)
    kj = lax.broadcasted_iota(jnp.int32, (blk, 2 * blk), 1)
    dist = qi - kj + blk

    @pl.when((pl.program_id(0) == 0) & (n == 0))
    def _():
        max_exact = N_BUCKETS // 2
        nn = jnp.maximum(dist, 0)
        nf = jnp.maximum(nn, 1).astype(F32)
        large = max_exact + (jnp.log(nf / max_exact) / math.log(MAX_DISTANCE / max_exact)
                             * (N_BUCKETS - max_exact)).astype(jnp.int32)
        bucket = jnp.where(nn < max_exact, nn, jnp.minimum(large, N_BUCKETS - 1))
        for h in range(q_heads):
            acc = jnp.zeros((blk, 2 * blk), F32)
            for bk in range(N_BUCKETS):
                acc = jnp.where(bucket == bk, rb_ref[bk * q_heads + h], acc)
            acc = jnp.where((dist >= 0) & (dist < SWA_WINDOW), acc, -jnp.inf)
            bias_ref[h] = acc
            bias_ref[q_heads + h] = jnp.where(kj >= blk, acc, -jnp.inf)

    first = jnp.where(n == 0, q_heads, 0)
    group = q_heads // kv_heads
    for j in range(kv_heads):
        cols = slice(j * dim, (j + 1) * dim)
        k_all = _rms(jnp.concatenate([kp_ref[:, cols], kc_ref[:, cols]], axis=0), kw_ref[...]).astype(BF16)
        v_all = jnp.concatenate([vp_ref[:, cols], vc_ref[:, cols]], axis=0).astype(BF16)
        for u in range(SWA_STEP_BLOCKS):
            rows = slice(u * blk, (u + 1) * blk)
            kwin, vwin = k_all[u * blk:(u + 2) * blk], v_all[u * blk:(u + 2) * blk]
            qcat = jnp.concatenate(
                [_rms(q_ref[rows, (j * group + i) * dim:(j * group + i + 1) * dim], qw_ref[...])
                 for i in range(group)], axis=0).astype(BF16)
            logits = _dot_nt(qcat, kwin) * (dim ** -0.5)
            for i in range(group):
                h = j * group + i
                lg = logits[i * blk:(i + 1) * blk] + bias_ref[(first if u == 0 else 0) + h]
                sink = sink_ref[h]
                mx = jnp.maximum(jnp.max(lg, axis=-1, keepdims=True), sink)
                e = jnp.exp(lg - mx)
                den = jnp.sum(e, axis=-1, keepdims=True) + jnp.exp(sink - mx)
                pv = _dot(e.astype(BF16), vwin)
                o_ref[rows, h * dim:(h + 1) * dim] = (pv / den).astype(o_ref.dtype)


def _swa(proj3, rel_bias_flat, sinks, q_norm_w, k_norm_w, q_heads, kv_heads, dim, col_q, col_k, col_v):
    b, s, _ = proj3.shape
    blk = SWA_BLOCK
    step = SWA_STEP_BLOCKS * blk
    qw, kvw = q_heads * dim, kv_heads * dim
    smem = pl.BlockSpec(memory_space=pltpu.SMEM)
    pipelined = (_nbytes((step, qw), F32) + 2 * _nbytes((step + blk, kvw), F32) + _nbytes((step, qw), BF16))
    resident = (_nbytes((2 * q_heads, blk, 2 * blk), F32) + 2 * pipelined
                + 2 * SWA_STEP_BLOCKS * _nbytes((q_heads // kv_heads * blk, 2 * blk), F32))

    def prev(n):
        return jnp.maximum(SWA_STEP_BLOCKS * n - 1, 0)

    return pl.pallas_call(
        functools.partial(_swa_kernel, q_heads=q_heads, kv_heads=kv_heads, dim=dim),
        grid=(b, s // step),
        in_specs=[
            smem, smem,
            pl.BlockSpec((None, step, qw), lambda i, n: (i, n, col_q // qw)),
            pl.BlockSpec((None, step, kvw), lambda i, n: (i, n, col_k // kvw)),
            pl.BlockSpec((None, blk, kvw), lambda i, n: (i, prev(n), col_k // kvw)),
            pl.BlockSpec((None, step, kvw), lambda i, n: (i, n, col_v // kvw)),
            pl.BlockSpec((None, blk, kvw), lambda i, n: (i, prev(n), col_v // kvw)),
            pl.BlockSpec((1, dim), lambda i, n: (0, 0)),
            pl.BlockSpec((1, dim), lambda i, n: (0, 0)),
        ],
        out_specs=pl.BlockSpec((None, step, qw), lambda i, n: (i, n, 0)),
        out_shape=jax.ShapeDtypeStruct((b, s, qw), BF16),
        scratch_shapes=[pltpu.VMEM((2 * q_heads, blk, 2 * blk), F32)],
        compiler_params=pltpu.CompilerParams(
            dimension_semantics=("arbitrary", "arbitrary"),
            vmem_limit_bytes=_vmem_limit(pipelined, resident, claim_all=True)),
        name="swa",
    )(rel_bias_flat, sinks, proj3, proj3, proj3, proj3, proj3, q_norm_w, k_norm_w)


RIDER_RING = 3


def _cast_riders_ringed(hbm_refs, out_refs, bufs, sem, step, n_steps):
    def copies(target):
        slot = lax.rem(target, RIDER_RING)
        out = []
        for r, (hbm, buf) in enumerate(zip(hbm_refs, bufs)):
            rows = buf.shape[1]
            src = hbm.at[pl.ds(pl.multiple_of(target * rows, rows), rows), :]
            out.append(pltpu.make_async_copy(src, buf.at[slot], sem.at[r, slot]))
        return out

    def request(target):
        @pl.when(target < n_steps)
        def _():
            for copy in copies(target):
                copy.start()

    @pl.when(step == 0)
    def _():
        for ahead in range(RIDER_RING - 1):
            request(step + ahead)

    request(step + RIDER_RING - 1)
    for copy in copies(step):
        copy.wait()
    slot = lax.rem(step, RIDER_RING)
    for buf, out in zip(bufs, out_refs):
        out[...] = buf[slot].astype(out.dtype)


def _memattn_kernel(*refs, heads, dim, n_riders):
    q_ref, k_ref, v_ref, qw_ref, kw_ref = refs[:5]
    o_ref = refs[5 + n_riders]
    n_t = pl.num_programs(1)
    _cast_riders_ringed(refs[5:5 + n_riders], refs[6 + n_riders:6 + 2 * n_riders],
                        refs[6 + 2 * n_riders:6 + 3 * n_riders], refs[-1],
                        pl.program_id(0) * n_t + pl.program_id(1), pl.num_programs(0) * n_t)
    for h in range(heads):
        cols = slice(h * dim, (h + 1) * dim)
        qn = _rms(q_ref[:, cols], qw_ref[...]).astype(BF16)
        kn = _rms(k_ref[:, cols], kw_ref[...]).astype(BF16)
        lg = _dot_nt(qn, kn) * (dim ** -0.5)
        e = jnp.exp(lg - jnp.max(lg, axis=-1, keepdims=True))
        den = jnp.sum(e, axis=-1, keepdims=True)
        pv = _dot(e.astype(BF16), v_ref[:, cols].astype(BF16))
        o_ref[:, cols] = (pv / den).astype(o_ref.dtype)


def _memattn(proj3, mkv3, q_norm_w, k_norm_w, riders, heads, dim, col_q, tq):
    b, s, _ = proj3.shape
    m = mkv3.shape[1]
    w = heads * dim
    nt = s // tq
    _, rider_out, rider_shapes = _cast_rider_specs(riders, b * nt, lambda i, t: i * nt + t)
    slabs = [(r.shape[0] // (b * nt), r.shape[1]) for r in riders]
    pipelined = (_nbytes((tq, w), F32) + 2 * _nbytes((m, w), F32) + _nbytes((tq, w), BF16)
                 + sum(_nbytes(slab, BF16) for slab in slabs))
    ring_bytes = RIDER_RING * sum(_nbytes(slab, F32) for slab in slabs)
    return pl.pallas_call(
        functools.partial(_memattn_kernel, heads=heads, dim=dim, n_riders=len(riders)),
        grid=(b, nt),
        in_specs=[
            pl.BlockSpec((pl.Element(tq), pl.Element(w)), lambda i, t: (pl.multiple_of(i * s + t * tq, tq), col_q)),
            pl.BlockSpec((None, m, w), lambda i, t: (i, 0, 0)),
            pl.BlockSpec((None, m, w), lambda i, t: (i, 0, 1)),
            pl.BlockSpec((1, dim), lambda i, t: (0, 0)),
            pl.BlockSpec((1, dim), lambda i, t: (0, 0)),
            *([pl.BlockSpec(memory_space=pl.ANY)] * len(riders)),
        ],
        out_specs=[pl.BlockSpec((None, tq, w), lambda i, t: (i, t, 0)), *rider_out],
        out_shape=[jax.ShapeDtypeStruct((b, s, w), BF16), *rider_shapes],
        scratch_shapes=[*[pltpu.VMEM((RIDER_RING, *slab), F32) for slab in slabs],
                        pltpu.SemaphoreType.DMA((len(riders), RIDER_RING))],
        compiler_params=pltpu.CompilerParams(
            dimension_semantics=("arbitrary", "arbitrary"),
            vmem_limit_bytes=_vmem_limit(pipelined, 2 * heads * _nbytes((tq, m), F32) + ring_bytes,
                                         claim_all=True)),
        name="memattn",
    )(proj3.reshape(b * s, -1), mkv3, mkv3, q_norm_w, k_norm_w, *riders)


def _merge_kernel(od_ref, os_ref, om_ref, pd_ref, ps_ref, pm_ref, gd_ref, gs_ref, gm_ref, o_ref):
    merged = (gd_ref[...].astype(F32) * _dot(od_ref[...], pd_ref[...])
              + gs_ref[...].astype(F32) * _dot(os_ref[...], ps_ref[...])
              + gm_ref[...].astype(F32) * _dot(om_ref[...], pm_ref[...]))
    o_ref[...] = merged.astype(o_ref.dtype)


def _merge(o_dn, o_swa, o_mem, p_dn, p_swa, p_mem, gates, tm, tn):
    t = o_dn.shape[0]
    d = p_dn.shape[1]
    wd, ws, wm = o_dn.shape[1], o_swa.shape[1], o_mem.shape[1]
    per_branch = d // tn
    pipelined = ((_nbytes((tm, wd), BF16) + _nbytes((tm, ws), BF16) + _nbytes((tm, wm), BF16))
                 + (_nbytes((wd, tn), BF16) + _nbytes((ws, tn), BF16) + _nbytes((wm, tn), BF16))
                 + 4 * _nbytes((tm, tn), BF16))
    return pl.pallas_call(
        _merge_kernel,
        grid=(d // tn, t // tm),
        in_specs=[
            pl.BlockSpec((tm, wd), lambda j, i: (i, 0)),
            pl.BlockSpec((tm, ws), lambda j, i: (i, 0)),
            pl.BlockSpec((tm, wm), lambda j, i: (i, 0)),
            pl.BlockSpec((wd, tn), lambda j, i: (0, j)),
            pl.BlockSpec((ws, tn), lambda j, i: (0, j)),
            pl.BlockSpec((wm, tn), lambda j, i: (0, j)),
            pl.BlockSpec((tm, tn), lambda j, i: (i, j)),
            pl.BlockSpec((tm, tn), lambda j, i: (i, per_branch + j)),
            pl.BlockSpec((tm, tn), lambda j, i: (i, 2 * per_branch + j)),
        ],
        out_specs=pl.BlockSpec((tm, tn), lambda j, i: (i, j)),
        out_shape=jax.ShapeDtypeStruct((t, d), BF16),
        compiler_params=pltpu.CompilerParams(
            dimension_semantics=("parallel", "parallel"),
            vmem_limit_bytes=_vmem_limit(pipelined, 3 * _nbytes((tm, tn), F32))),
        name="merge",
    )(o_dn, o_swa, o_mem, p_dn, p_swa, p_mem, gates, gates, gates)


def _outproj_kernel(x_ref, m_ref, w_ref, nw_ref, x1_ref, h_ref):
    x1 = x_ref[...] + _dot(m_ref[...], w_ref[...])
    x1_ref[...] = x1
    h_ref[...] = _rms(x1, nw_ref[...]).astype(h_ref.dtype)


def _outproj(x, merged, w_out, norm_w, tm):
    t, d = x.shape
    pipelined = 2 * _nbytes((tm, d), F32) + 2 * _nbytes((tm, d), BF16) + _nbytes((d, d), BF16)
    return pl.pallas_call(
        _outproj_kernel,
        grid=(t // tm,),
        in_specs=[
            pl.BlockSpec((tm, d), lambda i: (i, 0)),
            pl.BlockSpec((tm, d), lambda i: (i, 0)),
            pl.BlockSpec((d, d), lambda i: (0, 0)),
            pl.BlockSpec((1, d), lambda i: (0, 0)),
        ],
        out_specs=[pl.BlockSpec((tm, d), lambda i: (i, 0)), pl.BlockSpec((tm, d), lambda i: (i, 0))],
        out_shape=[jax.ShapeDtypeStruct((t, d), F32), jax.ShapeDtypeStruct((t, d), BF16)],
        compiler_params=pltpu.CompilerParams(
            dimension_semantics=("parallel",),
            vmem_limit_bytes=_vmem_limit(pipelined, 2 * _nbytes((tm, d), F32))),
        name="outproj",
    )(x, merged, w_out, norm_w)


MLP_RING = 3


def _mlp_kernel(h_ref, x1_ref, wu_hbm, wd_hbm, o_ref, ubuf, dbuf, sem, *, n_k):
    tf = dbuf.shape[1]
    base = pl.program_id(0) * n_k
    n_tiles = pl.num_programs(0) * n_k

    def tile_copies(t):
        slot = lax.rem(t, MLP_RING)
        start = pl.multiple_of(lax.rem(t, n_k) * tf, tf)
        return (pltpu.make_async_copy(wu_hbm.at[:, pl.ds(start, tf)], ubuf.at[slot], sem.at[0, slot]),
                pltpu.make_async_copy(wd_hbm.at[pl.ds(start, tf), :], dbuf.at[slot], sem.at[1, slot]))

    def request(t):
        @pl.when(t < n_tiles)
        def _():
            for copy in tile_copies(t):
                copy.start()

    def wait(t):
        for copy in tile_copies(t):
            copy.wait()

    def up(t):
        a = jnp.maximum(_dot(h_ref[...], ubuf[lax.rem(t, MLP_RING)]), 0.0)
        return (a * a).astype(BF16)

    @pl.when(base == 0)
    def _():
        for t in range(MLP_RING):
            request(base + t)

    o_ref[...] = x1_ref[...]
    wait(base)
    act = up(base)
    for k in range(n_k):
        if k + 1 < n_k:
            wait(base + k + 1)
        o_ref[...] += _dot(act, dbuf[lax.rem(base + k, MLP_RING)])
        if k + 1 < n_k:
            act = up(base + k + 1)
        request(base + k + MLP_RING)


def _mlp(h2, x1, w_up, w_down, tm, tf):
    t, d = x1.shape
    f = w_up.shape[1]
    pipelined = _nbytes((tm, d), BF16) + 2 * _nbytes((tm, d), F32)
    resident = 2 * _nbytes((tm, tf), F32) + MLP_RING * (_nbytes((d, tf), BF16) + _nbytes((tf, d), BF16))
    return pl.pallas_call(
        functools.partial(_mlp_kernel, n_k=f // tf),
        grid=(t // tm,),
        in_specs=[
            pl.BlockSpec((tm, d), lambda i: (i, 0)),
            pl.BlockSpec((tm, d), lambda i: (i, 0)),
            pl.BlockSpec(memory_space=pl.ANY),
            pl.BlockSpec(memory_space=pl.ANY),
        ],
        out_specs=pl.BlockSpec((tm, d), lambda i: (i, 0)),
        out_shape=jax.ShapeDtypeStruct((t, d), F32),
        scratch_shapes=[pltpu.VMEM((MLP_RING, d, tf), BF16), pltpu.VMEM((MLP_RING, tf, d), BF16),
                        pltpu.SemaphoreType.DMA((2, MLP_RING))],
        compiler_params=pltpu.CompilerParams(
            dimension_semantics=("arbitrary",),
            vmem_limit_bytes=_vmem_limit(pipelined, resident)),
        name="mlp",
    )(h2, x1, w_up, w_down)


def _layer(x, mem, attn_norm_w, w_in, dn_conv_w, dn_a_log, dn_dt_bias, dn_out_norm_w, swa_q_norm_w,
           swa_k_norm_w, swa_sinks, rel_bias, mem_norm_w, w_mem_kv, xq_norm_w, xk_norm_w,
           p_dn, p_swa, p_mem, w_out, mlp_norm_w, w_mlp_up, w_mlp_down):
    b, s, d = x.shape
    m = mem.shape[1]
    t = b * s

    dn_heads = dn_a_log.shape[0]
    dn_dim = dn_out_norm_w.shape[0]
    dn_w = dn_heads * dn_dim
    swa_heads = swa_sinks.shape[0]
    swa_dim = swa_q_norm_w.shape[0]
    swa_w = swa_heads * swa_dim
    mem_dim = xq_norm_w.shape[0]
    mem_w = p_mem.shape[0]
    mem_heads = mem_w // mem_dim
    swa_kv_w = (w_in.shape[1] - 4 * dn_w - 2 * dn_heads - swa_w - mem_w - 3 * d) // 2
    swa_kv = swa_kv_w // swa_dim

    n_ba = 2 * dn_heads
    src_ba = 4 * dn_w
    src_swa = src_ba + n_ba
    w_t = w_in.T.astype(BF16)
    w_b = w_t[src_swa:]
    w_c = jnp.concatenate([w_t[src_ba:src_swa], jnp.zeros((LANES - n_ba, d), BF16)], axis=0)
    col = {"qkv": 0, "z": 3 * dn_w, "sq": src_ba}
    col["sk"] = col["sq"] + swa_w
    col["sv"] = col["sk"] + swa_kv_w
    col["mq"] = col["sv"] + swa_kv_w

    proj, gates, ba = _inproj(x.reshape(t, d), attn_norm_w.reshape(1, d), w_t, src_ba, w_b, w_c, 3 * d,
                              tm=1024, tn=1024)
    proj3 = proj.reshape(b, s, -1)
    mkv = _norm_matmul(mem.reshape(b * m, d), mem_norm_w.reshape(1, d), w_mem_kv.astype(BF16), tm=512, tn=512)
    mkv3 = mkv.reshape(b, m, -1)

    gate_params = jnp.zeros((SUBLANES, LANES), F32)
    gate_params = gate_params.at[0, dn_heads:2 * dn_heads].set(dn_a_log)
    gate_params = gate_params.at[1, dn_heads:2 * dn_heads].set(dn_dt_bias)
    o_dn, w_up_bf, w_down_bf = _deltanet(proj3, ba.reshape(b, s, LANES), dn_conv_w, gate_params,
                                         dn_out_norm_w.reshape(1, dn_dim), [w_mlp_up, w_mlp_down],
                                         dn_heads, dn_dim, col["z"])
    o_swa = _swa(proj3, rel_bias.reshape(-1), swa_sinks, swa_q_norm_w.reshape(1, swa_dim),
                 swa_k_norm_w.reshape(1, swa_dim), swa_heads, swa_kv, swa_dim, col["sq"], col["sk"], col["sv"])
    o_mem, w_out_bf, p_dn_bf, p_swa_bf, p_mem_bf = _memattn(
        proj3, mkv3, xq_norm_w.reshape(1, mem_dim), xk_norm_w.reshape(1, mem_dim),
        [w_out, p_dn, p_swa, p_mem], mem_heads, mem_dim, col["mq"], tq=1024)

    merged = _merge(o_dn.reshape(t, dn_w), o_swa.reshape(t, swa_w), o_mem.reshape(t, mem_w),
                    p_dn_bf, p_swa_bf, p_mem_bf, gates, tm=1024, tn=1024)
    x1, h2 = _outproj(x.reshape(t, d), merged, w_out_bf, mlp_norm_w.reshape(1, d), tm=512)
    out = _mlp(h2, x1, w_up_bf, w_down_bf, tm=512, tf=1024)
    return out.reshape(b, s, d)


def kernel(x, mem, attn_norm_w, w_in, dn_conv_w, dn_A_log, dn_dt_bias, dn_out_norm_w, swa_q_norm_w,
           swa_k_norm_w, swa_sinks, rel_bias, mem_norm_w, w_mem_kv, xq_norm_w, xk_norm_w, p_dn, p_swa,
           p_mem, w_out, mlp_norm_w, w_mlp_up, w_mlp_down):
    depth = w_in.shape[0]
    for l in range(depth):
        x = _layer(x, mem, attn_norm_w[l], w_in[l], dn_conv_w[l], dn_A_log[l], dn_dt_bias[l],
                   dn_out_norm_w[l], swa_q_norm_w[l], swa_k_norm_w[l], swa_sinks[l], rel_bias,
                   mem_norm_w[l], w_mem_kv[l], xq_norm_w[l], xk_norm_w[l], p_dn[l], p_swa[l], p_mem[l],
                   w_out[l], mlp_norm_w[l], w_mlp_up[l], w_mlp_down[l])
    return x
```
